```python
import math
import jax
import jax.numpy as jnp
from jax import lax
import numpy as np

D_MODEL = 1024
BATCH = 4
SEQ = 4096
DEPTH = 1
DEC_BATCH = 32
DEC_SEQ = 8
PAST_LEN = 8192
PAGE_SIZE = 128

MIX_WIDTH = D_MODEL
A_WIDTH = MIX_WIDTH // 2
A_HEAD = 64
A_GROUPS = A_WIDTH // A_HEAD
CHUNK = 128
B_WIDTH = MIX_WIDTH - A_WIDTH
B_HEAD = 64
B_VDIM = 2 * B_HEAD
B_HEADS = B_WIDTH // B_VDIM
B_QK = 2 * B_HEAD
IN_WIDTH = 2 * A_WIDTH + 3 * B_WIDTH
NUM_BUCKETS = 32
MAX_DISTANCE = 128
Q_BLOCK = 128
N_EXPERTS = 32
TOP_K = 4
D_EXPERT = D_MODEL
SWIGLU_LIMIT = 7.0
SWIGLU_ALPHA = 1.702
MOE_BLOCK = 128
NORM_EPS = 1e-6

kernel_name = 'hybrid_gmlp_diffattn_moe_step'


def _rmsnorm(x, g):
    xf = x.astype(jnp.float32)
    y = xf * lax.rsqrt(jnp.mean(xf * xf, axis=-1, keepdims=True) + NORM_EPS)
    return (y * g.astype(jnp.float32)).astype(x.dtype)


def _layernorm(x, g, b):
    xf = x.astype(jnp.float32)
    xc = xf - jnp.mean(xf, axis=-1, keepdims=True)
    y = xc * lax.rsqrt(jnp.mean(xc * xc, axis=-1, keepdims=True) + NORM_EPS)
    return (y * g.astype(jnp.float32) + b.astype(jnp.float32)).astype(x.dtype)


def _t5_bucket(dist):
    n = jnp.maximum(dist, 0)
    max_exact = NUM_BUCKETS // 2
    nf = jnp.maximum(n, 1).astype(jnp.float32)
    large = max_exact + (jnp.log(nf / max_exact) / math.log(MAX_DISTANCE / max_exact)
                         * (NUM_BUCKETS - max_exact)).astype(jnp.int32)
    large = jnp.minimum(large, NUM_BUCKETS - 1)
    return jnp.where(n < max_exact, n, large)


def _chunk_spatial_gate(u, v, w_s, b_s):
    bsz, t = u.shape[:2]
    L = min(CHUNK, t)
    nc = t // L
    w = jnp.tril(w_s[:, :L, :L]).astype(v.dtype)
    vb = v.reshape(bsz, nc, L, A_GROUPS, A_HEAD)
    s = jnp.einsum('gts,bnsgc->bntgc', w, vb)
    s = s + b_s[:, :L].T.astype(v.dtype)[None, None, :, :, None]
    return u * s.reshape(bsz, t, A_GROUPS, A_HEAD)


def _diff_attention(q, k, v, q_pos, k_pos, rel_bias, lam, subln_g, lam_init):
    bsz, tq = q.shape[:2]
    blk = min(Q_BLOCK, tq)
    nb = tq // blk
    qb = jnp.moveaxis(q.reshape(bsz, nb, blk, B_HEADS, 2, B_HEAD), 1, 0)
    pb = q_pos.reshape(nb, blk)
    scale = 1.0 / math.sqrt(B_HEAD)

    def block(args):
        qi, pi = args
        s = jnp.einsum('bqhmd,bkhmd->bhmqk', qi, k).astype(jnp.float32) * scale
        dist = pi[:, None] - k_pos[None, :]
        bias = rel_bias[_t5_bucket(dist)].astype(jnp.float32)
        s = s + jnp.transpose(bias, (2, 3, 0, 1))[None]
        s = jnp.where(dist >= 0, s, -jnp.inf)
        p = jax.nn.softmax(s, axis=-1)
        a = p[:, :, 0] - lam * p[:, :, 1]
        return jnp.einsum('bhqk,bkhd->bqhd', a.astype(v.dtype), v)

    o = lax.map(block, (qb, pb))
    o = jnp.moveaxis(o, 0, 1).reshape(bsz, tq, B_HEADS, B_VDIM)
    return _rmsnorm(o, subln_g) * (1.0 - lam_init)


def _moe(x2, w_router, b_router, w_gu, b_gu, w_down, b_down):
    t, d = x2.shape
    logits = (x2 @ w_router + b_router).astype(jnp.float32)
    top_v, top_i = lax.top_k(logits, TOP_K)
    top_w = jax.nn.softmax(top_v, axis=-1)
    n_assign = t * TOP_K
    flat_e = top_i.reshape(-1)
    flat_tok = jnp.arange(n_assign, dtype=jnp.int32) // TOP_K
    order = jnp.argsort(flat_e)
    e_sorted = flat_e[order]
    tok_sorted = flat_tok[order]
    w_sorted = top_w.reshape(-1)[order]
    counts = jnp.bincount(flat_e, length=N_EXPERTS)
    padded = (counts + MOE_BLOCK - 1) // MOE_BLOCK * MOE_BLOCK
    pad_end = jnp.cumsum(padded)
    pad_start = pad_end - padded
    grp_start = jnp.cumsum(counts) - counts
    dest = pad_start[e_sorted] + jnp.arange(n_assign, dtype=jnp.int32) - grp_start[e_sorted]
    n_blocks = -(-n_assign // MOE_BLOCK) + N_EXPERTS
    buf = jnp.zeros((n_blocks * MOE_BLOCK, d), x2.dtype).at[dest].set(x2[tok_sorted])
    block_e = jnp.minimum(jnp.searchsorted(pad_end, jnp.arange(n_blocks, dtype=jnp.int32) * MOE_BLOCK,
                                           side='right'), N_EXPERTS - 1)

    def expert_block(args):
        xb, e = args
        gu = xb @ w_gu[e] + b_gu[e]
        glu = jnp.minimum(gu[:, :D_EXPERT], SWIGLU_LIMIT)
        lin = jnp.clip(gu[:, D_EXPERT:], -SWIGLU_LIMIT, SWIGLU_LIMIT)
        hid = glu * jax.nn.sigmoid(SWIGLU_ALPHA * glu) * (lin + 1.0)
        return hid @ w_down[e] + b_down[e]

    out = lax.map(expert_block, (buf.reshape(n_blocks, MOE_BLOCK, d), block_e))
    rows = out.reshape(n_blocks * MOE_BLOCK, d)[dest] * w_sorted[:, None].astype(x2.dtype)
    return jax.ops.segment_sum(rows, tok_sorted, num_segments=t)


def _layer(x, c, q_pos, k_pos, past_k, past_v, p, lam_init):
    bsz, t, d = x.shape
    mod = jax.nn.silu(c) @ p['w_ada'] + p['b_ada']
    sh1, sc1, g1, sh2, sc2, g2 = [m[:, None, :] for m in jnp.split(mod, 6, axis=-1)]
    h = _rmsnorm(x, p['norm_g'][0]) * (1 + sc1) + sh1
    z = h @ p['w_in']
    zu, zv, zq, zk, zval = jnp.split(
        z, [A_WIDTH, 2 * A_WIDTH, 2 * A_WIDTH + B_WIDTH, 2 * A_WIDTH + 2 * B_WIDTH], axis=-1)
    u = jax.nn.gelu(zu).reshape(bsz, t, A_GROUPS, A_HEAD)
    va = _layernorm(jax.nn.gelu(zv).reshape(bsz, t, A_GROUPS, A_HEAD), p['ln_v_g'], p['ln_v_b'])
    a_out = _chunk_spatial_gate(u, va, p['w_spatial'], p['b_spatial'])
    q = zq.reshape(bsz, t, B_HEADS, 2, B_HEAD)
    k = zk.reshape(bsz, t, B_HEADS, 2, B_HEAD)
    v = zval.reshape(bsz, t, B_HEADS, B_VDIM)
    if past_k is None:
        k_all, v_all = k, v
    else:
        k_all = jnp.concatenate([past_k.astype(k.dtype), k], axis=1)
        v_all = jnp.concatenate([past_v.astype(v.dtype), v], axis=1)
    lp = p['lam_params'].astype(jnp.float32)
    lam = jnp.exp(jnp.sum(lp[0] * lp[1])) - jnp.exp(jnp.sum(lp[2] * lp[3])) + lam_init
    b_out = _diff_attention(q, k_all, v_all, q_pos, k_pos, p['rel_bias'], lam, p['subln_g'], lam_init)
    mix = jnp.concatenate([a_out.reshape(bsz, t, A_WIDTH), b_out.reshape(bsz, t, B_WIDTH)], axis=-1) @ p['w_out']
    x = x + g1 * _rmsnorm(mix, p['norm_g'][1])
    h2 = _rmsnorm(x, p['norm_g'][2]) * (1 + sc2) + sh2
    f = _moe(h2.reshape(bsz * t, d), p['w_router'], p['b_router'], p['w_gu'], p['b_gu'],
             p['w_down'], p['b_down']).reshape(bsz, t, d)
    x = x + g2 * _rmsnorm(f, p['norm_g'][3])
    return x, k.reshape(bsz, t, B_HEADS, B_QK), v, va.reshape(bsz, t, A_WIDTH)


def setup_inputs(seed: int = 0) -> dict:
    key = jax.random.key(seed)
    ks = jax.random.split(key, 26)
    f32 = jnp.float32
    n_pages = PAST_LEN // PAGE_SIZE
    n_used = DEC_BATCH * n_pages
    n_phys = n_used + n_used // 4

    def nrm(k, shape, s):
        return jax.random.normal(k, shape, f32) * s

    page_table = jax.random.permutation(ks[4], n_phys)[:n_used].reshape(DEC_BATCH, n_pages).astype(jnp.int32)
    return {
        'x_prompt': nrm(ks[0], (BATCH, SEQ, D_MODEL), 1.0),
        'x_sample': nrm(ks[1], (DEC_BATCH, DEC_SEQ, D_MODEL), 1.0),
        'cache_k': nrm(ks[2], (n_phys, PAGE_SIZE, DEPTH, B_HEADS, B_QK), 1.0),
        'cache_v': nrm(ks[3], (n_phys, PAGE_SIZE, DEPTH, B_HEADS, B_VDIM), 1.0),
        'page_table': page_table,
        'c_prompt': nrm(ks[5], (BATCH, D_MODEL), 1.0),
        'c_sample': nrm(ks[6], (DEC_BATCH, D_MODEL), 1.0),
        'w_ada': nrm(ks[7], (DEPTH, D_MODEL, 6 * D_MODEL), 0.5 * D_MODEL ** -0.5),
        'b_ada': nrm(ks[8], (DEPTH, 6 * D_MODEL), 0.02),
        'norm_g': 1.0 + nrm(ks[9], (DEPTH, 4, D_MODEL), 0.05),
        'w_in': nrm(ks[10], (DEPTH, D_MODEL, IN_WIDTH), D_MODEL ** -0.5),
        'w_out': nrm(ks[11], (DEPTH, MIX_WIDTH, D_MODEL), MIX_WIDTH ** -0.5),
        'ln_v_g': 1.0 + nrm(ks[12], (DEPTH, A_GROUPS, A_HEAD), 0.05),
        'ln_v_b': nrm(ks[13], (DEPTH, A_GROUPS, A_HEAD), 0.02),
        'w_spatial': nrm(ks[14], (DEPTH, A_GROUPS, CHUNK, CHUNK), CHUNK ** -0.5),
        'b_spatial': 1.0 + nrm(ks[15], (DEPTH, A_GROUPS, CHUNK), 0.02),
        'lam_params': nrm(ks[16], (DEPTH, 4, B_HEAD), 0.1),
        'subln_g': 1.0 + nrm(ks[17], (DEPTH, B_VDIM), 0.05),
        'rel_bias': nrm(ks[18], (NUM_BUCKETS, B_HEADS, 2), 0.5),
        'w_router': nrm(ks[19], (DEPTH, D_MODEL, N_EXPERTS), D_MODEL ** -0.5),
        'b_router': nrm(ks[20], (DEPTH, N_EXPERTS), 0.01),
        'w_gu': nrm(ks[21], (DEPTH, N_EXPERTS, D_MODEL, 2 * D_EXPERT), D_MODEL ** -0.5),
        'b_gu': nrm(ks[22], (DEPTH, N_EXPERTS, 2 * D_EXPERT), 0.01),
        'w_down': nrm(ks[23], (DEPTH, N_EXPERTS, D_EXPERT, D_MODEL), D_EXPERT ** -0.5),
        'b_down': nrm(ks[24], (DEPTH, N_EXPERTS, D_MODEL), 0.01),
    }


def reference(x_prompt, x_sample, cache_k, cache_v, page_table, c_prompt, c_sample,
              w_ada, b_ada, norm_g, w_in, w_out, ln_v_g, ln_v_b, w_spatial, b_spatial,
              lam_params, subln_g, rel_bias, w_router, b_router, w_gu, b_gu, w_down, b_down):
    seq = x_prompt.shape[1]
    dec_b, dec_s = x_sample.shape[:2]
    past = page_table.shape[1] * PAGE_SIZE
    pos_p = jnp.arange(seq, dtype=jnp.int32)
    q_pos_s = past + jnp.arange(dec_s, dtype=jnp.int32)
    k_pos_s = jnp.arange(past + dec_s, dtype=jnp.int32)
    hp, hs = x_prompt, x_sample
    kp_l, vp_l, ks_l, vs_l, cp_l, cs_l = [], [], [], [], [], []
    for l in range(DEPTH):
        lam_init = 0.8 - 0.6 * math.exp(-0.3 * l)
        p = {'w_ada': w_ada[l], 'b_ada': b_ada[l], 'norm_g': norm_g[l], 'w_in': w_in[l],
             'w_out': w_out[l], 'ln_v_g': ln_v_g[l], 'ln_v_b': ln_v_b[l],
             'w_spatial': w_spatial[l], 'b_spatial': b_spatial[l], 'lam_params': lam_params[l],
             'subln_g': subln_g[l], 'rel_bias': rel_bias, 'w_router': w_router[l],
             'b_router': b_router[l], 'w_gu': w_gu[l], 'b_gu': b_gu[l], 'w_down': w_down[l],
             'b_down': b_down[l]}
        past_k = jnp.take(cache_k[:, :, l], page_table, axis=0).reshape(dec_b, past, B_HEADS, 2, B_HEAD)
        past_v = jnp.take(cache_v[:, :, l], page_table, axis=0).reshape(dec_b, past, B_HEADS, B_VDIM)
        hp, kp, vp, cp = _layer(hp, c_prompt, pos_p, pos_p, None, None, p, lam_init)
        hs, ks_, vs_, cs_ = _layer(hs, c_sample, q_pos_s, k_pos_s, past_k, past_v, p, lam_init)
        kp_l.append(kp)
        vp_l.append(vp)
        ks_l.append(ks_)
        vs_l.append(vs_)
        cp_l.append(cp[:, -CHUNK:])
        cs_l.append(cs_)
    new_k_prompt = jnp.stack(kp_l, axis=2)
    new_v_prompt = jnp.stack(vp_l, axis=2)
    new_k_sample = jnp.stack(ks_l, axis=2)
    new_v_sample = jnp.stack(vs_l, axis=2)
    chunk_v_prompt = jnp.stack(cp_l, axis=2)
    chunk_v_sample = jnp.stack(cs_l, axis=2)
    return (hp, hs, new_k_prompt, new_v_prompt, new_k_sample, new_v_sample, chunk_v_prompt, chunk_v_sample)
```

```python
import functools
import math

import numpy as np
import jax
import jax.numpy as jnp
from jax import lax
from jax.experimental import pallas as pl
from jax.experimental.pallas import tpu as pltpu

F32 = jnp.float32
BF16 = jnp.bfloat16

D_MODEL = 1024
A_WIDTH = 512
A_HEAD = 64
A_GROUPS = 8
CHUNK = 128
B_WIDTH = 512
B_HEAD = 64
B_VDIM = 128
B_HEADS = 4
IN_WIDTH = 2 * A_WIDTH + 3 * B_WIDTH
NUM_BUCKETS = 32
MAX_DISTANCE = 128
PAGE_SIZE = 128
N_EXPERTS = 32
TOP_K = 4
D_EXPERT = 1024
SWIGLU_LIMIT = 7.0
SWIGLU_ALPHA = 1.702
NORM_EPS = 1e-6
LAM_INIT = 0.8 - 0.6 * math.exp(-0.3 * 0)

ROW_TILE = 256
ATTN_TILE = 256
PAGES_PER_STEP = 8
EXPERT_BLOCK = 256
VMEM_LIMIT = 56 * 1024 * 1024


def _cparams(*sem):
    return pltpu.CompilerParams(dimension_semantics=sem, vmem_limit_bytes=VMEM_LIMIT)


def _rms(x, g):
    return x * lax.rsqrt(jnp.mean(x * x, axis=-1, keepdims=True) + NORM_EPS) * g


def _ada_kernel(c_ref, w_ref, b_ref, o_ref):
    s = jax.nn.silu(c_ref[...]).astype(BF16)
    o_ref[...] = jnp.dot(s, w_ref[...].astype(BF16), preferred_element_type=F32) + b_ref[...]


def _ada(c_all, w_ada, b_ada):
    n = c_all.shape[0]
    tn = 1024
    return pl.pallas_call(
        _ada_kernel,
        grid=(6 * D_MODEL // tn,),
        in_specs=[pl.BlockSpec((n, D_MODEL), lambda j: (0, 0)),
                  pl.BlockSpec((D_MODEL, tn), lambda j: (0, j)),
                  pl.BlockSpec((1, tn), lambda j: (0, j))],
        out_specs=pl.BlockSpec((n, tn), lambda j: (0, j)),
        out_shape=jax.ShapeDtypeStruct((n, 6 * D_MODEL), F32),
        compiler_params=_cparams("arbitrary"),
        name="ada",
    )(c_all, w_ada, b_ada.reshape(1, -1))


def _inproj_kernel(x_ref, sh_ref, sc_ref, ng_ref, w_ref, lng_ref, lnb_ref, avg_ref, ws_ref, bs_ref,
                   a_ref, va_ref, q_ref, k_ref, v_ref, kb_ref, vb_ref, *, chunk, q_scale):
    tm = x_ref.shape[0]
    h = (_rms(x_ref[...], ng_ref[...]) * (1.0 + sc_ref[0]) + sh_ref[0]).astype(BF16)

    u = jax.nn.gelu(jnp.dot(h, w_ref[:, 0:A_WIDTH], preferred_element_type=F32))
    gv = jax.nn.gelu(jnp.dot(h, w_ref[:, A_WIDTH:2 * A_WIDTH], preferred_element_type=F32))

    avg = avg_ref[...]

    def group_mean(t):
        hi = t.astype(BF16)
        lo = (t - hi.astype(F32)).astype(BF16)
        return (jnp.dot(hi, avg, preferred_element_type=F32)
                + jnp.dot(lo, avg, preferred_element_type=F32))

    xc = gv - group_mean(gv)
    va = xc * lax.rsqrt(group_mean(xc * xc) + NORM_EPS) * lng_ref[...] + lnb_ref[...]
    va_ref[...] = va

    vab = va.astype(BF16)
    lane = lax.broadcasted_iota(jnp.int32, (chunk, 128), 1)
    for c in range(tm // chunk):
        r0 = c * chunk
        for pair in range(A_GROUPS // 2):
            c0 = pair * 128
            vp = vab[r0:r0 + chunk, c0:c0 + 128]
            lo_half = jnp.where(lane < A_HEAD, vp, jnp.zeros_like(vp))
            hi_half = jnp.where(lane >= A_HEAD, vp, jnp.zeros_like(vp))
            s = (jnp.dot(ws_ref[2 * pair], lo_half, preferred_element_type=F32)
                 + jnp.dot(ws_ref[2 * pair + 1], hi_half, preferred_element_type=F32))
            a = u[r0:r0 + chunk, c0:c0 + 128] * (s + bs_ref[:, c0:c0 + 128])
            a_ref[r0:r0 + chunk, c0:c0 + 128] = a.astype(a_ref.dtype)

    q = jnp.dot(h, w_ref[:, 2 * A_WIDTH:2 * A_WIDTH + B_WIDTH], preferred_element_type=F32)
    q_ref[...] = (q * q_scale).astype(q_ref.dtype)
    k = jnp.dot(h, w_ref[:, 2 * A_WIDTH + B_WIDTH:2 * A_WIDTH + 2 * B_WIDTH], preferred_element_type=F32)
    k_ref[...] = k
    kb_ref[...] = k.astype(BF16)
    v = jnp.dot(h, w_ref[:, 2 * A_WIDTH + 2 * B_WIDTH:IN_WIDTH], preferred_element_type=F32)
    v_ref[...] = v
    vb_ref[...] = v.astype(BF16)


def _inproj(x2, sh, sc, ng, w_in_b, lng, lnb, avg, ws, bs, *, tiles_per_mod, chunk, q_dtype, q_scale):
    rows = x2.shape[0]
    tm = ROW_TILE
    mod_rows = sh.shape[1]
    row = lambda i: (i, 0)
    const2 = lambda i: (0, 0)
    mod_map = lambda i: (i // tiles_per_mod, 0, 0)
    out_w = lambda w, dt: jax.ShapeDtypeStruct((rows, w), dt)
    return pl.pallas_call(
        functools.partial(_inproj_kernel, chunk=chunk, q_scale=q_scale),
        grid=(rows // tm,),
        in_specs=[pl.BlockSpec((tm, D_MODEL), row),
                  pl.BlockSpec((1, mod_rows, D_MODEL), mod_map),
                  pl.BlockSpec((1, mod_rows, D_MODEL), mod_map),
                  pl.BlockSpec((1, D_MODEL), const2),
                  pl.BlockSpec((D_MODEL, IN_WIDTH), const2),
                  pl.BlockSpec((1, A_WIDTH), const2),
                  pl.BlockSpec((1, A_WIDTH), const2),
                  pl.BlockSpec((A_WIDTH, A_WIDTH), const2),
                  pl.BlockSpec((A_GROUPS, chunk, chunk), lambda i: (0, 0, 0)),
                  pl.BlockSpec((chunk, A_WIDTH), const2)],
        out_specs=[pl.BlockSpec((tm, A_WIDTH), row)] * 7,
        out_shape=[out_w(A_WIDTH, BF16), out_w(A_WIDTH, F32), out_w(B_WIDTH, q_dtype),
                   out_w(B_WIDTH, F32), out_w(B_WIDTH, F32), out_w(B_WIDTH, BF16), out_w(B_WIDTH, BF16)],
        compiler_params=_cparams("arbitrary"),
        name="inproj",
    )(x2, sh, sc, ng, w_in_b, lng, lnb, avg, ws, bs)


def _stack_halves(q):
    lane = lax.broadcasted_iota(jnp.int32, q.shape, 1)
    zero = jnp.zeros_like(q)
    return jnp.concatenate([jnp.where(lane < B_HEAD, q, zero), jnp.where(lane >= B_HEAD, q, zero)], axis=0)


def _lambda(lam_ref):
    lp = lam_ref[...]
    return (jnp.exp(jnp.sum(lp[0:1] * lp[1:2], axis=-1, keepdims=True))
            - jnp.exp(jnp.sum(lp[2:3] * lp[3:4], axis=-1, keepdims=True)) + LAM_INIT)


def _diff_finish(acc, l, n, lam, g):
    o = acc[:n] / l[:n] - lam * (acc[n:] / l[n:])
    return _rms(o, g) * (1.0 - LAM_INIT)


def _attn_p_kernel(q_ref, k_ref, v_ref, bias_ref, lam_ref, g_ref, o_ref, acc_ref, m_ref, l_ref, *, tq, tk):
    qi = pl.program_id(2)
    q2 = _stack_halves(q_ref[...])
    m_ref[...] = jnp.full(m_ref.shape, -jnp.inf, F32)
    l_ref[...] = jnp.zeros(l_ref.shape, F32)
    acc_ref[...] = jnp.zeros(acc_ref.shape, F32)

    def step(j, bias):
        off = pl.multiple_of(j * tk, tk)
        k = k_ref[pl.ds(off, tk), :]
        v = v_ref[pl.ds(off, tk), :]
        s = lax.dot_general(q2, k, (((1,), (1,)), ((), ())), preferred_element_type=F32)
        if bias is not None:
            s = s + bias
        m_old = m_ref[...]
        m_new = jnp.maximum(m_old, jnp.max(s, axis=-1, keepdims=True))
        alpha = jnp.exp(m_old - m_new)
        p = jnp.exp(s - m_new)
        l_ref[...] = alpha * l_ref[...] + jnp.sum(p, axis=-1, keepdims=True)
        acc_ref[...] = alpha * acc_ref[...] + jnp.dot(p.astype(BF16), v, preferred_element_type=F32)
        m_ref[...] = m_new

    def far_step(j, carry):
        step(j, None)
        return carry

    lax.fori_loop(0, qi - 1, far_step, 0)

    @pl.when(qi >= 1)
    def _():
        step(qi - 1, bias_ref[0, 1])

    step(qi, bias_ref[0, 0])
    o_ref[...] = _diff_finish(acc_ref[...], l_ref[...], tq, _lambda(lam_ref), g_ref[...]).astype(o_ref.dtype)


def _attn_p(qb, kb, vb, bias, lam_params, subln_g, *, batch, seq):
    tq = tk = ATTN_TILE
    nq = seq // tq
    return pl.pallas_call(
        functools.partial(_attn_p_kernel, tq=tq, tk=tk),
        grid=(batch, B_HEADS, nq),
        in_specs=[pl.BlockSpec((tq, B_VDIM), lambda b, h, i: (b * nq + i, h)),
                  pl.BlockSpec((seq, B_VDIM), lambda b, h, i: (b, h)),
                  pl.BlockSpec((seq, B_VDIM), lambda b, h, i: (b, h)),
                  pl.BlockSpec((1, 2, 2 * tq, tk), lambda b, h, i: (h, 0, 0, 0)),
                  pl.BlockSpec((4, B_HEAD), lambda b, h, i: (0, 0)),
                  pl.BlockSpec((1, B_VDIM), lambda b, h, i: (0, 0))],
        out_specs=pl.BlockSpec((tq, B_VDIM), lambda b, h, i: (b * nq + i, h)),
        out_shape=jax.ShapeDtypeStruct((batch * seq, B_WIDTH), BF16),
        scratch_shapes=[pltpu.VMEM((2 * tq, B_VDIM), F32),
                        pltpu.VMEM((2 * tq, 1), F32),
                        pltpu.VMEM((2 * tq, 1), F32)],
        compiler_params=_cparams("arbitrary", "arbitrary", "arbitrary"),
        name="attn_prompt",
    )(qb, kb, vb, bias, lam_params, subln_g)


def _attn_s_kernel(pt_ref, q_ref, kn_ref, vn_ref, bias_ref, biasn_ref, lam_ref, g_ref, *rest, npages, ds):
    del pt_ref
    k_refs = rest[:npages]
    v_refs = rest[npages:2 * npages]
    o_ref = rest[2 * npages]
    acc_ref, m_ref, l_ref, kpad_ref, vpad_ref = rest[2 * npages + 1:]
    g = pl.program_id(1)

    @pl.when(g == 0)
    def _():
        m_ref[...] = jnp.full(m_ref.shape, -jnp.inf, F32)
        l_ref[...] = jnp.zeros(l_ref.shape, F32)
        acc_ref[...] = jnp.zeros(acc_ref.shape, F32)

    q = q_ref[...] * (1.0 / math.sqrt(B_HEAD))

    def process(k_get, v_get, nblk, bias_get):
        for h in range(B_HEADS):
            c0 = h * B_VDIM
            q2 = _stack_halves(q[:, c0:c0 + B_VDIM]).astype(BF16)
            s = jnp.concatenate(
                [lax.dot_general(q2, k_get(p, c0).astype(BF16),
                                 (((1,), (1,)), ((), ())), preferred_element_type=F32)
                 for p in range(nblk)], axis=1) + bias_get(h)
            m_old = m_ref[h]
            m_new = jnp.maximum(m_old, jnp.max(s, axis=-1, keepdims=True))
            alpha = jnp.exp(m_old - m_new)
            pr = jnp.exp(s - m_new)
            l_ref[h] = alpha * l_ref[h] + jnp.sum(pr, axis=-1, keepdims=True)
            prb = pr.astype(BF16)
            pv = jnp.dot(prb[:, 0:PAGE_SIZE], v_get(0, c0).astype(BF16), preferred_element_type=F32)
            for p in range(1, nblk):
                pv = pv + jnp.dot(prb[:, p * PAGE_SIZE:(p + 1) * PAGE_SIZE], v_get(p, c0).astype(BF16),
                                  preferred_element_type=F32)
            acc_ref[h] = alpha * acc_ref[h] + pv
            m_ref[h] = m_new

    process(lambda p, c0: k_refs[p][0, :, c0:c0 + B_VDIM], lambda p, c0: v_refs[p][0, :, c0:c0 + B_VDIM],
            npages, lambda h: bias_ref[h])

    @pl.when(g == pl.num_programs(1) - 1)
    def _():
        kpad_ref[...] = jnp.zeros(kpad_ref.shape, F32)
        vpad_ref[...] = jnp.zeros(vpad_ref.shape, F32)
        kpad_ref[0:ds, :] = kn_ref[...]
        vpad_ref[0:ds, :] = vn_ref[...]
        process(lambda p, c0: kpad_ref[:, c0:c0 + B_VDIM], lambda p, c0: vpad_ref[:, c0:c0 + B_VDIM],
                1, lambda h: biasn_ref[h])
        lam = _lambda(lam_ref)
        for h in range(B_HEADS):
            o_ref[:, h * B_VDIM:(h + 1) * B_VDIM] = _diff_finish(acc_ref[h], l_ref[h], ds, lam, g_ref[...])


def _attn_s(page_table, qs, kn, vn, bias, bias_new, lam_params, subln_g, cache_k3, cache_v3, *, ds):
    nb, n_pages = page_table.shape
    npg = PAGES_PER_STEP
    steps = n_pages // npg

    def page_spec(p):
        return pl.BlockSpec((1, PAGE_SIZE, B_WIDTH), lambda b, g, pt, p=p: (pt[b, g * npg + p], 0, 0))

    row = lambda b, g, pt: (b, 0)
    grid_spec = pltpu.PrefetchScalarGridSpec(
        num_scalar_prefetch=1,
        grid=(nb, steps),
        in_specs=[pl.BlockSpec((ds, B_WIDTH), row),
                  pl.BlockSpec((ds, B_WIDTH), row),
                  pl.BlockSpec((ds, B_WIDTH), row),
                  pl.BlockSpec((B_HEADS, 2 * ds, npg * PAGE_SIZE), lambda b, g, pt: (0, 0, g)),
                  pl.BlockSpec((B_HEADS, 2 * ds, PAGE_SIZE), lambda b, g, pt: (0, 0, 0)),
                  pl.BlockSpec((4, B_HEAD), lambda b, g, pt: (0, 0)),
                  pl.BlockSpec((1, B_VDIM), lambda b, g, pt: (0, 0))]
                 + [page_spec(p) for p in range(npg)] + [page_spec(p) for p in range(npg)],
        out_specs=pl.BlockSpec((ds, B_WIDTH), row),
        scratch_shapes=[pltpu.VMEM((B_HEADS, 2 * ds, B_VDIM), F32),
                        pltpu.VMEM((B_HEADS, 2 * ds, 1), F32),
                        pltpu.VMEM((B_HEADS, 2 * ds, 1), F32),
                        pltpu.VMEM((PAGE_SIZE, B_WIDTH), F32),
                        pltpu.VMEM((PAGE_SIZE, B_WIDTH), F32)])
    return pl.pallas_call(
        functools.partial(_attn_s_kernel, npages=npg, ds=ds),
        grid_spec=grid_spec,
        out_shape=jax.ShapeDtypeStruct((nb * ds, B_WIDTH), F32),
        compiler_params=_cparams("arbitrary", "arbitrary"),
        name="attn_sample",
    )(page_table, qs, kn, vn, bias, bias_new, lam_params, subln_g,
      *([cache_k3] * npg), *([cache_v3] * npg))


def _outproj_kernel(ap_ref, as_ref, bp_ref, bs_ref, xp_ref, xs_ref, mp_ref, ms_ref, ng_ref, wo_ref,
                    wr_ref, br_ref, x1_ref, h2_ref, ti_ref, tr_ref, tw_ref, cnt_ref, run_ref, *, n_prompt_tiles):
    i = pl.program_id(0)
    tm = xp_ref.shape[0]
    is_p = i < n_prompt_tiles

    @pl.when(i == 0)
    def _():
        run_ref[...] = jnp.zeros(run_ref.shape, F32)

    a = jnp.where(is_p, ap_ref[...], as_ref[...])
    b = jnp.where(is_p, bp_ref[...], bs_ref[...].astype(BF16))
    x = jnp.where(is_p, xp_ref[...], xs_ref[...])
    g1 = jnp.where(is_p, mp_ref[0, 2:3, :], ms_ref[2])
    sh2 = jnp.where(is_p, mp_ref[0, 3:4, :], ms_ref[3])
    sc2 = jnp.where(is_p, mp_ref[0, 4:5, :], ms_ref[4])

    mix = (jnp.dot(a, wo_ref[0:A_WIDTH, :], preferred_element_type=F32)
           + jnp.dot(b, wo_ref[A_WIDTH:, :], preferred_element_type=F32))
    x1 = x + g1 * _rms(mix, ng_ref[1:2, :])
    x1_ref[...] = x1
    h2 = _rms(x1, ng_ref[2:3, :]) * (1.0 + sc2) + sh2
    h2_ref[...] = h2

    logits = jnp.dot(h2, wr_ref[...], preferred_element_type=F32, precision=lax.Precision.HIGHEST) + br_ref[...]
    lane = lax.broadcasted_iota(jnp.int32, logits.shape, 1)
    work = logits
    vals, idxs = [], []
    for _ in range(TOP_K):
        mx = jnp.max(work, axis=-1, keepdims=True)
        ix = jnp.min(jnp.where(work == mx, lane, N_EXPERTS), axis=-1, keepdims=True)
        vals.append(mx)
        idxs.append(ix)
        work = jnp.where(lane == ix, -jnp.inf, work)
    exps = [jnp.exp(v - vals[0]) for v in vals]
    den = exps[0] + exps[1] + exps[2] + exps[3]

    sel = (work == -jnp.inf)
    r_i = lax.broadcasted_iota(jnp.int32, (tm, tm), 0)
    c_i = lax.broadcasted_iota(jnp.int32, (tm, tm), 1)
    tri = jnp.where(c_i < r_i, 1.0, 0.0).astype(BF16)
    before = jnp.dot(tri, jnp.where(sel, 1.0, 0.0).astype(BF16), preferred_element_type=F32) + run_ref[...]
    run_ref[...] = run_ref[...] + jnp.sum(jnp.where(sel, 1.0, 0.0), axis=0, keepdims=True)
    cnt_ref[...] = run_ref[...]

    lane4 = lax.broadcasted_iota(jnp.int32, (tm, TOP_K), 1)
    ti = jnp.zeros((tm, TOP_K), jnp.int32)
    tr = jnp.zeros((tm, TOP_K), F32)
    tw = jnp.zeros((tm, TOP_K), F32)
    for j in range(TOP_K):
        rank_j = jnp.sum(jnp.where(lane == idxs[j], before, 0.0), axis=-1, keepdims=True)
        ti = jnp.where(lane4 == j, idxs[j], ti)
        tr = jnp.where(lane4 == j, rank_j, tr)
        tw = jnp.where(lane4 == j, exps[j] / den, tw)
    ti_ref[...] = ti
    tr_ref[...] = tr.astype(jnp.int32)
    tw_ref[...] = tw


def _outproj(a_p, a_s, b_p, b_s, x_p, x_s, mod_p, mod_s, norm_g, w_out_b, w_router, b_router):
    tm = ROW_TILE
    npt = x_p.shape[0] // tm
    nst = x_s.shape[0] // tm
    assert nst == 1
    n_tiles = npt + nst
    rows = n_tiles * tm
    tiles_per_batch = npt // mod_p.shape[0]
    last_p = npt - 1
    prow = lambda i: (jnp.minimum(i, last_p), 0)
    srow = lambda i: (0, 0)
    row = lambda i: (i, 0)
    const2 = lambda i: (0, 0)
    return pl.pallas_call(
        functools.partial(_outproj_kernel, n_prompt_tiles=npt),
        grid=(n_tiles,),
        in_specs=[pl.BlockSpec((tm, A_WIDTH), prow), pl.BlockSpec((tm, A_WIDTH), srow),
                  pl.BlockSpec((tm, B_WIDTH), prow), pl.BlockSpec((tm, B_WIDTH), srow),
                  pl.BlockSpec((tm, D_MODEL), prow), pl.BlockSpec((tm, D_MODEL), srow),
                  pl.BlockSpec((1, 6, D_MODEL), lambda i: (jnp.minimum(i, last_p) // tiles_per_batch, 0, 0)),
                  pl.BlockSpec((6, tm, D_MODEL), lambda i: (0, 0, 0)),
                  pl.BlockSpec((4, D_MODEL), const2),
                  pl.BlockSpec((D_MODEL, D_MODEL), const2),
                  pl.BlockSpec((D_MODEL, N_EXPERTS), const2),
                  pl.BlockSpec((1, N_EXPERTS), const2)],
        out_specs=[pl.BlockSpec((tm, D_MODEL), row), pl.BlockSpec((tm, D_MODEL), row),
                   pl.BlockSpec((tm, TOP_K), row), pl.BlockSpec((tm, TOP_K), row),
                   pl.BlockSpec((tm, TOP_K), row), pl.BlockSpec((1, N_EXPERTS), const2)],
        out_shape=[jax.ShapeDtypeStruct((rows, D_MODEL), F32), jax.ShapeDtypeStruct((rows, D_MODEL), F32),
                   jax.ShapeDtypeStruct((rows, TOP_K), jnp.int32), jax.ShapeDtypeStruct((rows, TOP_K), jnp.int32),
                   jax.ShapeDtypeStruct((rows, TOP_K), F32), jax.ShapeDtypeStruct((1, N_EXPERTS), F32)],
        scratch_shapes=[pltpu.VMEM((1, N_EXPERTS), F32)],
        compiler_params=_cparams("arbitrary"),
        name="outproj_router",
    )(a_p, a_s, b_p, b_s, x_p, x_s, mod_p, mod_s, norm_g, w_out_b, w_router, b_router.reshape(1, -1))


def _dispatch_kernel(dest_ref, h_ref, xg_in_ref, xg_ref, sem):
    del xg_in_ref
    tm = h_ref.shape[0]

    def body(t, carry):
        for j in range(TOP_K):
            d = dest_ref[t * TOP_K + j]
            pltpu.make_async_copy(h_ref.at[pl.ds(t, 1), :], xg_ref.at[pl.ds(d, 1), :], sem).start()
        return carry

    lax.fori_loop(0, tm, body, 0)
    n = tm * TOP_K
    pltpu.make_async_copy(xg_ref.at[pl.ds(0, n), :], xg_ref.at[pl.ds(0, n), :], sem).wait()


def _dispatch(dest_flat, h2, xg_zero):
    tm = ROW_TILE
    rows = h2.shape[0]
    return pl.pallas_call(
        _dispatch_kernel,
        grid=(rows // tm,),
        in_specs=[pl.BlockSpec((tm * TOP_K,), lambda i: (i,), memory_space=pltpu.SMEM),
                  pl.BlockSpec((tm, D_MODEL), lambda i: (i, 0)),
                  pl.BlockSpec(memory_space=pl.ANY)],
        out_specs=pl.BlockSpec(memory_space=pl.ANY),
        out_shape=jax.ShapeDtypeStruct(xg_zero.shape, xg_zero.dtype),
        scratch_shapes=[pltpu.SemaphoreType.DMA(())],
        input_output_aliases={2: 0},
        compiler_params=_cparams("arbitrary"),
        name="moe_dispatch",
    )(dest_flat, h2, xg_zero)


def _expert_kernel(be_ref, nu_ref, x_ref, wgu_ref, bgu_ref, wd_ref, bd_ref, y_ref):
    del be_ref
    used = pl.program_id(0) < nu_ref[0]

    @pl.when(jnp.logical_not(used))
    def _():
        y_ref[...] = jnp.zeros(y_ref.shape, F32)

    @pl.when(used)
    def _():
        x = x_ref[...].astype(BF16)
        gu = jnp.dot(x, wgu_ref[0], preferred_element_type=F32) + bgu_ref[0]
        glu = jnp.minimum(gu[:, :D_EXPERT], SWIGLU_LIMIT)
        lin = jnp.clip(gu[:, D_EXPERT:], -SWIGLU_LIMIT, SWIGLU_LIMIT)
        hid = glu * jax.nn.sigmoid(SWIGLU_ALPHA * glu) * (lin + 1.0)
        y_ref[...] = jnp.dot(hid.astype(BF16), wd_ref[0], preferred_element_type=F32) + bd_ref[0]


def _experts(block_e, n_used, xg, w_gu_b, b_gu, w_down_b, b_down):
    bm = EXPERT_BLOCK
    n_blocks = xg.shape[0] // bm
    grid_spec = pltpu.PrefetchScalarGridSpec(
        num_scalar_prefetch=2,
        grid=(n_blocks,),
        in_specs=[pl.BlockSpec((bm, D_MODEL), lambda i, be, nu: (i, 0)),
                  pl.BlockSpec((1, D_MODEL, 2 * D_EXPERT), lambda i, be, nu: (be[i], 0, 0)),
                  pl.BlockSpec((1, 1, 2 * D_EXPERT), lambda i, be, nu: (be[i], 0, 0)),
                  pl.BlockSpec((1, D_EXPERT, D_MODEL), lambda i, be, nu: (be[i], 0, 0)),
                  pl.BlockSpec((1, 1, D_MODEL), lambda i, be, nu: (be[i], 0, 0))],
        out_specs=pl.BlockSpec((bm, D_MODEL), lambda i, be, nu: (i, 0)))
    return pl.pallas_call(
        _expert_kernel,
        grid_spec=grid_spec,
        out_shape=jax.ShapeDtypeStruct(xg.shape, F32),
        compiler_params=_cparams("arbitrary"),
        name="moe_experts",
    )(block_e, n_used, xg, w_gu_b, b_gu.reshape(N_EXPERTS, 1, -1), w_down_b, b_down.reshape(N_EXPERTS, 1, -1))


def _combine_kernel(dest_ref, tw_ref, x1_ref, mp_ref, ms_ref, ng_ref, yg_ref, yp_ref, ys_ref, rows_ref, sem,
                    *, n_prompt_tiles):
    i = pl.program_id(0)
    tm = x1_ref.shape[0]
    is_p = i < n_prompt_tiles

    def body(t, carry):
        for j in range(TOP_K):
            d = dest_ref[t * TOP_K + j]
            pltpu.make_async_copy(yg_ref.at[pl.ds(d, 1), :], rows_ref.at[pl.ds(j * tm + t, 1), :], sem).start()
        return carry

    lax.fori_loop(0, tm, body, 0)
    n = tm * TOP_K
    pltpu.make_async_copy(yg_ref.at[pl.ds(0, n), :], rows_ref, sem).wait()

    tw = tw_ref[...]
    f = tw[:, 0:1] * rows_ref[0:tm, :]
    for j in range(1, TOP_K):
        f = f + tw[:, j:j + 1] * rows_ref[j * tm:(j + 1) * tm, :]
    g2 = jnp.where(is_p, mp_ref[0, 5:6, :], ms_ref[5])
    y = x1_ref[...] + g2 * _rms(f, ng_ref[3:4, :])

    @pl.when(is_p)
    def _():
        yp_ref[...] = y

    @pl.when(jnp.logical_not(is_p))
    def _():
        ys_ref[...] = y


def _combine(dest_flat, tw, x1, mod_p, mod_s, norm_g, yg, *, n_prompt_rows, n_sample_rows):
    tm = ROW_TILE
    npt = n_prompt_rows // tm
    assert n_sample_rows == tm
    n_tiles = npt + 1
    tiles_per_batch = npt // mod_p.shape[0]
    last_p = npt - 1
    return pl.pallas_call(
        functools.partial(_combine_kernel, n_prompt_tiles=npt),
        grid=(n_tiles,),
        in_specs=[pl.BlockSpec((tm * TOP_K,), lambda i: (i,), memory_space=pltpu.SMEM),
                  pl.BlockSpec((tm, TOP_K), lambda i: (i, 0)),
                  pl.BlockSpec((tm, D_MODEL), lambda i: (i, 0)),
                  pl.BlockSpec((1, 6, D_MODEL), lambda i: (jnp.minimum(i, last_p) // tiles_per_batch, 0, 0)),
                  pl.BlockSpec((6, tm, D_MODEL), lambda i: (0, 0, 0)),
                  pl.BlockSpec((4, D_MODEL), lambda i: (0, 0)),
                  pl.BlockSpec(memory_space=pl.ANY)],
        out_specs=[pl.BlockSpec((tm, D_MODEL), lambda i: (jnp.minimum(i, last_p), 0)),
                   pl.BlockSpec((tm, D_MODEL), lambda i: (0, 0))],
        out_shape=[jax.ShapeDtypeStruct((n_prompt_rows, D_MODEL), F32),
                   jax.ShapeDtypeStruct((n_sample_rows, D_MODEL), F32)],
        scratch_shapes=[pltpu.VMEM((tm * TOP_K, D_MODEL), F32), pltpu.SemaphoreType.DMA(())],
        compiler_params=_cparams("arbitrary"),
        name="moe_combine",
    )(dest_flat, tw, x1, mod_p, mod_s, norm_g, yg)


def _t5_bucket_np(dist):
    n = np.maximum(dist, 0)
    max_exact = NUM_BUCKETS // 2
    nf = np.maximum(n, 1).astype(np.float64)
    large = max_exact + (np.log(nf / max_exact) / math.log(MAX_DISTANCE / max_exact)
                         * (NUM_BUCKETS - max_exact)).astype(np.int32)
    large = np.minimum(large, NUM_BUCKETS - 1)
    return np.where(n < max_exact, n, large).astype(np.int32)


def _bias_table(rel_bias, dist):
    bucket = _t5_bucket_np(dist)
    t = rel_bias[bucket] - rel_bias[NUM_BUCKETS - 1]
    t = jnp.where(jnp.asarray(dist >= 0)[..., None, None], t, -jnp.inf)
    nd = dist.ndim
    perm = (nd, *range(nd - 2), nd + 1, nd - 2, nd - 1)
    t = jnp.transpose(t, perm)
    return t.reshape(*t.shape[:-3], 2 * dist.shape[-2], dist.shape[-1]).astype(F32)


def kernel(x_prompt, x_sample, cache_k, cache_v, page_table, c_prompt, c_sample, w_ada, b_ada, norm_g, w_in,
           w_out, ln_v_g, ln_v_b, w_spatial, b_spatial, lam_params, subln_g, rel_bias, w_router, b_router,
           w_gu, b_gu, w_down, b_down):
    batch, seq, d = x_prompt.shape
    nb, ds = x_sample.shape[:2]
    n_pages = page_table.shape[1]
    past = n_pages * PAGE_SIZE
    rows_p = batch * seq
    rows_s = nb * ds
    assert rows_s == ROW_TILE and d == D_MODEL

    mod = _ada(jnp.concatenate([c_prompt, c_sample], axis=0), w_ada[0], b_ada[0])
    mod_p = mod[:batch].reshape(batch, 6, D_MODEL)
    mod_s = jnp.transpose(jnp.repeat(mod[batch:], ds, axis=0).reshape(rows_s, 6, D_MODEL), (1, 0, 2))

    ng = norm_g[0]
    w_in_b = w_in[0].astype(BF16)
    w_out_b = w_out[0].astype(BF16)
    lng = ln_v_g[0].reshape(1, A_WIDTH)
    lnb = ln_v_b[0].reshape(1, A_WIDTH)
    grp = np.arange(A_WIDTH) // A_HEAD
    avg = jnp.asarray((grp[:, None] == grp[None, :]).astype(np.float32) / A_HEAD, BF16)
    ws_p = jnp.tril(w_spatial[0]).astype(BF16)
    bs_p = jnp.repeat(b_spatial[0].T, A_HEAD, axis=1)
    w_small = jnp.tril(w_spatial[0][:, :ds, :ds])
    ws_s = jnp.einsum('ab,gts->gatbs', jnp.eye(nb, dtype=F32), w_small).reshape(A_GROUPS, rows_s, rows_s).astype(BF16)
    bs_s = jnp.tile(jnp.repeat(b_spatial[0][:, :ds].T, A_HEAD, axis=1), (nb, 1))

    xp2 = x_prompt.reshape(rows_p, D_MODEL)
    xs2 = x_sample.reshape(rows_s, D_MODEL)
    a_p, va_p, q_p, k_p, v_p, kb_p, vb_p = _inproj(
        xp2, mod_p[:, 0:1], mod_p[:, 1:2], ng[0:1], w_in_b, lng, lnb, avg, ws_p, bs_p,
        tiles_per_mod=seq // ROW_TILE, chunk=CHUNK, q_dtype=BF16, q_scale=1.0 / math.sqrt(B_HEAD))
    a_s, va_s, q_s, k_s, v_s, _, _ = _inproj(
        xs2, mod_s[0:1], mod_s[1:2], ng[0:1], w_in_b, lng, lnb, avg, ws_s, bs_s,
        tiles_per_mod=1, chunk=rows_s, q_dtype=F32, q_scale=1.0)

    ti = np.arange(ATTN_TILE)
    dist_p = np.stack([ti[:, None] - ti[None, :], ti[:, None] - ti[None, :] + ATTN_TILE])
    bias_p = _bias_table(rel_bias, dist_p)
    b_p = _attn_p(q_p, kb_p, vb_p, bias_p, lam_params[0], subln_g, batch=batch, seq=seq)

    qpos = past + np.arange(ds)
    dist_s = qpos[:, None] - np.arange(past)[None, :]
    knew = np.arange(PAGE_SIZE)
    dist_n = np.where(knew[None, :] < ds, qpos[:, None] - (past + knew[None, :]), -1)
    bias_s = _bias_table(rel_bias, dist_s)
    bias_n = _bias_table(rel_bias, dist_n)
    n_phys = cache_k.shape[0]
    b_s = _attn_s(page_table, q_s, k_s, v_s, bias_s, bias_n, lam_params[0], subln_g,
                  cache_k.reshape(n_phys, PAGE_SIZE, B_WIDTH), cache_v.reshape(n_phys, PAGE_SIZE, B_WIDTH), ds=ds)

    x1, h2, top_i, top_r, top_w, counts = _outproj(
        a_p, a_s, b_p, b_s, xp2, xs2, mod_p, mod_s, ng, w_out_b, w_router[0], b_router[0])

    bm = EXPERT_BLOCK
    rows = rows_p + rows_s
    n_blocks = rows * TOP_K // bm + N_EXPERTS
    cnt = counts[0].astype(jnp.int32)
    padded = (cnt + bm - 1) // bm * bm
    pad_end = jnp.cumsum(padded)
    pad_start = pad_end - padded
    dest = (pad_start[top_i] + top_r).reshape(-1)
    blk_start = jnp.arange(n_blocks, dtype=jnp.int32) * bm
    block_e = jnp.minimum(jnp.searchsorted(pad_end, blk_start, side='right'), N_EXPERTS - 1).astype(jnp.int32)
    n_used = (pad_end[-1:] // bm).astype(jnp.int32)

    xg = _dispatch(dest, h2, jnp.zeros((n_blocks * bm, D_MODEL), F32))
    yg = _experts(block_e, n_used, xg, w_gu[0].astype(BF16), b_gu[0], w_down[0].astype(BF16), b_down[0])
    y_p, y_s = _combine(dest, top_w, x1, mod_p, mod_s, ng, yg, n_prompt_rows=rows_p, n_sample_rows=rows_s)

    return (y_p.reshape(batch, seq, D_MODEL),
            y_s.reshape(nb, ds, D_MODEL),
            k_p.reshape(batch, seq, 1, B_HEADS, B_VDIM),
            v_p.reshape(batch, seq, 1, B_HEADS, B_VDIM),
            k_s.reshape(nb, ds, 1, B_HEADS, B_VDIM),
            v_s.reshape(nb, ds, 1, B_HEADS, B_VDIM),
            va_p.reshape(batch, seq, A_WIDTH)[:, -CHUNK:].reshape(batch, CHUNK, 1, A_WIDTH),
            va_s.reshape(nb, ds, 1, A_WIDTH))
```

```python
import functools
import math

import numpy as np
import jax
import jax.numpy as jnp
from jax import lax
from jax.experimental import pallas as pl
from jax.experimental.pallas import tpu as pltpu

F32 = jnp.float32
BF16 = jnp.bfloat16

D_MODEL = 1024
A_WIDTH = 512
A_HEAD = 64
A_GROUPS = 8
CHUNK = 128
B_WIDTH = 512
B_HEAD = 64
B_VDIM = 128
B_HEADS = 4
IN_WIDTH = 2 * A_WIDTH + 3 * B_WIDTH
NUM_BUCKETS = 32
MAX_DISTANCE = 128
PAGE_SIZE = 128
N_EXPERTS = 32
TOP_K = 4
D_EXPERT = 1024
SWIGLU_LIMIT = 7.0
SWIGLU_ALPHA = 1.702
NORM_EPS = 1e-6
LAM_INIT = 0.8 - 0.6 * math.exp(-0.3 * 0)

ROW_TILE = 256
ATTN_TILE = 256
PAGES_PER_STEP = 8
EXPERT_BLOCK = 256
VMEM_LIMIT = 56 * 1024 * 1024


def _cparams(*sem):
    return pltpu.CompilerParams(dimension_semantics=sem, vmem_limit_bytes=VMEM_LIMIT)


def _rms(x, g):
    return x * lax.rsqrt(jnp.mean(x * x, axis=-1, keepdims=True) + NORM_EPS) * g


def _ada_kernel(c_ref, w_ref, b_ref, o_ref):
    s = jax.nn.silu(c_ref[...]).astype(BF16)
    o_ref[...] = jnp.dot(s, w_ref[...].astype(BF16), preferred_element_type=F32) + b_ref[...]


def _ada(c_all, w_ada, b_ada):
    n = c_all.shape[0]
    tn = 1024
    return pl.pallas_call(
        _ada_kernel,
        grid=(6 * D_MODEL // tn,),
        in_specs=[pl.BlockSpec((n, D_MODEL), lambda j: (0, 0)),
                  pl.BlockSpec((D_MODEL, tn), lambda j: (0, j)),
                  pl.BlockSpec((1, tn), lambda j: (0, j))],
        out_specs=pl.BlockSpec((n, tn), lambda j: (0, j)),
        out_shape=jax.ShapeDtypeStruct((n, 6 * D_MODEL), F32),
        compiler_params=_cparams("arbitrary"),
        name="ada",
    )(c_all, w_ada, b_ada.reshape(1, -1))


def _inproj_kernel(x_ref, sh_ref, sc_ref, ng_ref, w_ref, lng_ref, lnb_ref, avg_ref, ws_ref, bs_ref,
                   a_ref, va_ref, q_ref, k_ref, v_ref, kb_ref, vt_ref, *, chunk, q_scale):
    tm = x_ref.shape[0]
    va_rows = va_ref.shape[0]
    h = (_rms(x_ref[...], ng_ref[...]) * (1.0 + sc_ref[0]) + sh_ref[0]).astype(BF16)

    u = jax.nn.gelu(jnp.dot(h, w_ref[:, 0:A_WIDTH], preferred_element_type=F32))
    gv = jax.nn.gelu(jnp.dot(h, w_ref[:, A_WIDTH:2 * A_WIDTH], preferred_element_type=F32))

    avg = avg_ref[...]

    def group_mean(t):
        hi = t.astype(BF16)
        lo = (t - hi.astype(F32)).astype(BF16)
        return (jnp.dot(hi, avg, preferred_element_type=F32)
                + jnp.dot(lo, avg, preferred_element_type=F32))

    xc = gv - group_mean(gv)
    va = xc * lax.rsqrt(group_mean(xc * xc) + NORM_EPS) * lng_ref[...] + lnb_ref[...]
    va_ref[...] = va[tm - va_rows:, :]

    vab = va.astype(BF16)
    lane = lax.broadcasted_iota(jnp.int32, (chunk, 128), 1)
    for c in range(tm // chunk):
        r0 = c * chunk
        for pair in range(A_GROUPS // 2):
            c0 = pair * 128
            vp = vab[r0:r0 + chunk, c0:c0 + 128]
            lo_half = jnp.where(lane < A_HEAD, vp, jnp.zeros_like(vp))
            hi_half = jnp.where(lane >= A_HEAD, vp, jnp.zeros_like(vp))
            s = (jnp.dot(ws_ref[2 * pair], lo_half, preferred_element_type=F32)
                 + jnp.dot(ws_ref[2 * pair + 1], hi_half, preferred_element_type=F32))
            a = u[r0:r0 + chunk, c0:c0 + 128] * (s + bs_ref[:, c0:c0 + 128])
            a_ref[r0:r0 + chunk, c0:c0 + 128] = a.astype(a_ref.dtype)

    q = jnp.dot(h, w_ref[:, 2 * A_WIDTH:2 * A_WIDTH + B_WIDTH], preferred_element_type=F32)
    q_ref[...] = (q * q_scale).astype(q_ref.dtype)
    k = jnp.dot(h, w_ref[:, 2 * A_WIDTH + B_WIDTH:2 * A_WIDTH + 2 * B_WIDTH], preferred_element_type=F32)
    kb_ref[...] = k.astype(BF16)
    v = jnp.dot(h, w_ref[:, 2 * A_WIDTH + 2 * B_WIDTH:IN_WIDTH], preferred_element_type=F32)
    for hh in range(B_HEADS):
        c0 = hh * B_VDIM
        k_ref[pl.ds(hh, tm, stride=B_HEADS), :] = k[:, c0:c0 + B_VDIM]
        v_ref[pl.ds(hh, tm, stride=B_HEADS), :] = v[:, c0:c0 + B_VDIM]
        vt_ref[0, hh, 0] = v[:, c0:c0 + B_VDIM].T.astype(BF16)


def _inproj(x2, sh, sc, ng, w_in_b, lng, lnb, avg, ws, bs, *, tiles_per_mod, chunk, q_dtype, q_scale, va_rows):
    rows = x2.shape[0]
    tm = ROW_TILE
    n_tiles = rows // tm
    mod_rows = sh.shape[1]
    row = lambda i: (i, 0)
    const2 = lambda i: (0, 0)
    mod_map = lambda i: (i // tiles_per_mod, 0, 0)
    out_w = lambda w, dt: jax.ShapeDtypeStruct((rows, w), dt)
    head_rows = jax.ShapeDtypeStruct((rows * B_HEADS, B_VDIM), F32)
    head_rows_spec = pl.BlockSpec((tm * B_HEADS, B_VDIM), row)
    vt_shape = jax.ShapeDtypeStruct((n_tiles // tiles_per_mod, B_HEADS, tiles_per_mod, B_VDIM, tm), BF16)
    vt_spec = pl.BlockSpec((1, B_HEADS, 1, B_VDIM, tm), lambda i: (i // tiles_per_mod, 0, i % tiles_per_mod, 0, 0))
    return pl.pallas_call(
        functools.partial(_inproj_kernel, chunk=chunk, q_scale=q_scale),
        grid=(rows // tm,),
        in_specs=[pl.BlockSpec((tm, D_MODEL), row),
                  pl.BlockSpec((1, mod_rows, D_MODEL), mod_map),
                  pl.BlockSpec((1, mod_rows, D_MODEL), mod_map),
                  pl.BlockSpec((1, D_MODEL), const2),
                  pl.BlockSpec((D_MODEL, IN_WIDTH), const2),
                  pl.BlockSpec((1, A_WIDTH), const2),
                  pl.BlockSpec((1, A_WIDTH), const2),
                  pl.BlockSpec((A_WIDTH, A_WIDTH), const2),
                  pl.BlockSpec((A_GROUPS, chunk, chunk), lambda i: (0, 0, 0)),
                  pl.BlockSpec((chunk, A_WIDTH), const2)],
        out_specs=[pl.BlockSpec((tm, A_WIDTH), row),
                   pl.BlockSpec((va_rows, A_WIDTH), lambda i: (i // tiles_per_mod, 0)),
                   pl.BlockSpec((tm, B_WIDTH), row), head_rows_spec, head_rows_spec,
                   pl.BlockSpec((tm, B_WIDTH), row), vt_spec],
        out_shape=[out_w(A_WIDTH, BF16),
                   jax.ShapeDtypeStruct((n_tiles // tiles_per_mod * va_rows, A_WIDTH), F32),
                   out_w(B_WIDTH, q_dtype), head_rows, head_rows, out_w(B_WIDTH, BF16), vt_shape],
        compiler_params=_cparams("arbitrary"),
        name="inproj",
    )(x2, sh, sc, ng, w_in_b, lng, lnb, avg, ws, bs)


def _stack_halves(q):
    lane = lax.broadcasted_iota(jnp.int32, q.shape, 1)
    zero = jnp.zeros_like(q)
    return jnp.concatenate([jnp.where(lane < B_HEAD, q, zero), jnp.where(lane >= B_HEAD, q, zero)], axis=0)


def _lambda(lam_ref):
    lp = lam_ref[...]
    return (jnp.exp(jnp.sum(lp[0:1] * lp[1:2], axis=-1, keepdims=True))
            - jnp.exp(jnp.sum(lp[2:3] * lp[3:4], axis=-1, keepdims=True)) + LAM_INIT)


def _diff_finish(acc, l, n, lam, g):
    o = acc[:n] / l[:n] - lam * (acc[n:] / l[n:])
    return _rms(o, g) * (1.0 - LAM_INIT)


def _attn_p_kernel(q_ref, k_ref, vt_ref, bias_ref, lam_ref, g_ref, o_ref, acc_ref, m_ref, l_ref, *, tq, tk):
    qi = pl.program_id(2)
    q2 = _stack_halves(q_ref[...])
    m_ref[...] = jnp.full(m_ref.shape, -jnp.inf, F32)
    l_ref[...] = jnp.zeros(l_ref.shape, F32)
    acc_ref[...] = jnp.zeros(acc_ref.shape, F32)

    def step(j, bias):
        off = pl.multiple_of(j * tk, tk)
        k = k_ref[pl.ds(off, tk), :]
        s = lax.dot_general(k, q2, (((1,), (1,)), ((), ())), preferred_element_type=F32)
        if bias is not None:
            s = s + bias
        m_old = m_ref[...]
        m_new = jnp.maximum(m_old, jnp.max(s, axis=0, keepdims=True))
        alpha = jnp.exp(m_old - m_new)
        p = jnp.exp(s - m_new)
        l_ref[...] = alpha * l_ref[...] + jnp.sum(p, axis=0, keepdims=True)
        acc_ref[...] = alpha * acc_ref[...] + jnp.dot(vt_ref[0, 0, j], p.astype(BF16), preferred_element_type=F32)
        m_ref[...] = m_new

    def far_pair(jj, carry):
        step(2 * jj, None)
        step(2 * jj + 1, None)
        return carry

    n_far = jnp.maximum(qi - 1, 0)
    lax.fori_loop(0, n_far // 2, far_pair, 0)

    @pl.when(n_far % 2 == 1)
    def _():
        step(n_far - 1, None)

    @pl.when(qi >= 1)
    def _():
        step(qi - 1, bias_ref[0, 1])

    step(qi, bias_ref[0, 0])

    acc = acc_ref[...]
    l = l_ref[...]
    o = acc[:, :tq] / l[:, :tq] - _lambda(lam_ref) * (acc[:, tq:] / l[:, tq:])
    y = o * lax.rsqrt(jnp.mean(o * o, axis=0, keepdims=True) + NORM_EPS) * (1.0 - LAM_INIT)
    o_ref[...] = (y.T * g_ref[...]).astype(o_ref.dtype)


def _attn_p(qb, kb, vt, bias, lam_params, subln_g, *, batch, seq):
    tq = tk = ATTN_TILE
    nq = seq // tq
    return pl.pallas_call(
        functools.partial(_attn_p_kernel, tq=tq, tk=tk),
        grid=(batch, B_HEADS, nq),
        in_specs=[pl.BlockSpec((tq, B_VDIM), lambda b, h, i: (b * nq + i, h)),
                  pl.BlockSpec((seq, B_VDIM), lambda b, h, i: (b, h)),
                  pl.BlockSpec((1, 1, seq // tk, B_VDIM, tk), lambda b, h, i: (b, h, 0, 0, 0)),
                  pl.BlockSpec((1, 2, tk, 2 * tq), lambda b, h, i: (h, 0, 0, 0)),
                  pl.BlockSpec((4, B_HEAD), lambda b, h, i: (0, 0)),
                  pl.BlockSpec((1, B_VDIM), lambda b, h, i: (0, 0))],
        out_specs=pl.BlockSpec((tq, B_VDIM), lambda b, h, i: (b * nq + i, h)),
        out_shape=jax.ShapeDtypeStruct((batch * seq, B_WIDTH), BF16),
        scratch_shapes=[pltpu.VMEM((B_VDIM, 2 * tq), F32),
                        pltpu.VMEM((1, 2 * tq), F32),
                        pltpu.VMEM((1, 2 * tq), F32)],
        compiler_params=_cparams("arbitrary", "arbitrary", "arbitrary"),
        name="attn_prompt",
    )(qb, kb, vt, bias, lam_params, subln_g)


def _attn_s_kernel(pt_ref, q_ref, kn_ref, vn_ref, maskb_ref, biasl_ref, biasn_ref, lam_ref, g_ref, *rest,
                   npages, ds):
    del pt_ref
    k_refs = rest[:npages]
    v_refs = rest[npages:2 * npages]
    o_ref = rest[2 * npages]
    acc_ref, m_ref, l_ref = rest[2 * npages + 1:]
    g = pl.program_id(1)
    last = g == pl.num_programs(1) - 1

    @pl.when(g == 0)
    def _():
        m_ref[...] = jnp.full(m_ref.shape, -jnp.inf, F32)
        l_ref[...] = jnp.zeros(l_ref.shape, F32)
        acc_ref[...] = jnp.zeros(acc_ref.shape, F32)

    q = q_ref[...] * (1.0 / math.sqrt(B_HEAD))
    qall = jnp.concatenate([_stack_halves(q[:, h * B_VDIM:(h + 1) * B_VDIM]) for h in range(B_HEADS)],
                           axis=0).astype(BF16)

    def process(k_blocks, v_blocks, biases):
        s = jnp.concatenate(
            [lax.dot_general(qall, kb.astype(BF16), (((1,), (1,)), ((), ())), preferred_element_type=F32) + bb
             for kb, bb in zip(k_blocks, biases)], axis=1)
        m_old = m_ref[...]
        m_new = jnp.maximum(m_old, jnp.max(s, axis=-1, keepdims=True))
        alpha = jnp.exp(m_old - m_new)
        pr = jnp.exp(s - m_new)
        l_ref[...] = alpha * l_ref[...] + jnp.sum(pr, axis=-1, keepdims=True)
        prb = pr.astype(BF16)
        pv = None
        off = 0
        for vb in v_blocks:
            n = vb.shape[0]
            t = jnp.dot(prb[:, off:off + n], vb.astype(BF16), preferred_element_type=F32)
            pv = t if pv is None else pv + t
            off += n
        acc_ref[...] = alpha * acc_ref[...] + pv
        m_ref[...] = m_new

    maskb = maskb_ref[...]
    newest = jnp.where(last, biasl_ref[...], maskb)
    process([k_refs[p][...] for p in range(npages)], [v_refs[p][...] for p in range(npages)],
            [maskb] * (npages - 1) + [newest])

    @pl.when(last)
    def _():
        process([kn_ref[0]], [vn_ref[0]], [biasn_ref[...]])
        lam = _lambda(lam_ref)
        acc = acc_ref[...]
        l = l_ref[...]
        for h in range(B_HEADS):
            r0 = h * 2 * ds
            o_ref[:, h * B_VDIM:(h + 1) * B_VDIM] = _diff_finish(
                acc[r0:r0 + 2 * ds], l[r0:r0 + 2 * ds], ds, lam, g_ref[...])


def _attn_s(page_table, qs, kn, vn, maskb, bias_last, bias_new, lam_params, subln_g, cache_k2, cache_v2, *, ds):
    nb, n_pages = page_table.shape
    npg = PAGES_PER_STEP
    steps = n_pages // npg
    page_rows = PAGE_SIZE * B_HEADS
    nrow = B_HEADS * 2 * ds

    def page_spec(p):
        return pl.BlockSpec((page_rows, B_VDIM), lambda b, g, pt, p=p: (pt[b, g * npg + p], 0))

    const2 = lambda b, g, pt: (0, 0)
    grid_spec = pltpu.PrefetchScalarGridSpec(
        num_scalar_prefetch=1,
        grid=(nb, steps),
        in_specs=[pl.BlockSpec((ds, B_WIDTH), lambda b, g, pt: (b, 0)),
                  pl.BlockSpec((1, PAGE_SIZE, B_VDIM), lambda b, g, pt: (b, 0, 0)),
                  pl.BlockSpec((1, PAGE_SIZE, B_VDIM), lambda b, g, pt: (b, 0, 0)),
                  pl.BlockSpec((nrow, page_rows), const2),
                  pl.BlockSpec((nrow, page_rows), const2),
                  pl.BlockSpec((nrow, PAGE_SIZE), const2),
                  pl.BlockSpec((4, B_HEAD), const2),
                  pl.BlockSpec((1, B_VDIM), const2)]
                 + [page_spec(p) for p in range(npg)] + [page_spec(p) for p in range(npg)],
        out_specs=pl.BlockSpec((ds, B_WIDTH), lambda b, g, pt: (b, 0)),
        scratch_shapes=[pltpu.VMEM((nrow, B_VDIM), F32),
                        pltpu.VMEM((nrow, 1), F32),
                        pltpu.VMEM((nrow, 1), F32)])
    return pl.pallas_call(
        functools.partial(_attn_s_kernel, npages=npg, ds=ds),
        grid_spec=grid_spec,
        out_shape=jax.ShapeDtypeStruct((nb * ds, B_WIDTH), F32),
        compiler_params=_cparams("arbitrary", "arbitrary"),
        name="attn_sample",
    )(page_table, qs, kn, vn, maskb, bias_last, bias_new, lam_params, subln_g,
      *([cache_k2] * npg), *([cache_v2] * npg))


def _outproj_kernel(ap_ref, as_ref, bp_ref, bs_ref, xp_ref, xs_ref, mp_ref, ms_ref, ng_ref, wo_ref,
                    wr_ref, br_ref, x1_ref, h2_ref, ti_ref, tr_ref, tw_ref, cnt_ref, run_ref, *, n_prompt_tiles):
    i = pl.program_id(0)
    tm = xp_ref.shape[0]
    is_p = i < n_prompt_tiles

    @pl.when(i == 0)
    def _():
        run_ref[...] = jnp.zeros(run_ref.shape, F32)

    a = jnp.where(is_p, ap_ref[...], as_ref[...])
    b = jnp.where(is_p, bp_ref[...], bs_ref[...].astype(BF16))
    x = jnp.where(is_p, xp_ref[...], xs_ref[...])
    g1 = jnp.where(is_p, mp_ref[0, 2:3, :], ms_ref[2])
    sh2 = jnp.where(is_p, mp_ref[0, 3:4, :], ms_ref[3])
    sc2 = jnp.where(is_p, mp_ref[0, 4:5, :], ms_ref[4])

    mix = (jnp.dot(a, wo_ref[0:A_WIDTH, :], preferred_element_type=F32)
           + jnp.dot(b, wo_ref[A_WIDTH:, :], preferred_element_type=F32))
    x1 = x + g1 * _rms(mix, ng_ref[1:2, :])
    x1_ref[...] = x1
    h2 = _rms(x1, ng_ref[2:3, :]) * (1.0 + sc2) + sh2
    h2_ref[...] = h2

    logits = jnp.dot(h2, wr_ref[...], preferred_element_type=F32, precision=lax.Precision.HIGHEST) + br_ref[...]
    lane = lax.broadcasted_iota(jnp.int32, logits.shape, 1)
    work = logits
    vals, idxs = [], []
    for _ in range(TOP_K):
        mx = jnp.max(work, axis=-1, keepdims=True)
        ix = jnp.min(jnp.where(work == mx, lane, N_EXPERTS), axis=-1, keepdims=True)
        vals.append(mx)
        idxs.append(ix)
        work = jnp.where(lane == ix, -jnp.inf, work)
    exps = [jnp.exp(v - vals[0]) for v in vals]
    den = exps[0] + exps[1] + exps[2] + exps[3]

    sel = (work == -jnp.inf)
    r_i = lax.broadcasted_iota(jnp.int32, (tm, tm), 0)
    c_i = lax.broadcasted_iota(jnp.int32, (tm, tm), 1)
    tri = jnp.where(c_i < r_i, 1.0, 0.0).astype(BF16)
    before = jnp.dot(tri, jnp.where(sel, 1.0, 0.0).astype(BF16), preferred_element_type=F32) + run_ref[...]
    run_ref[...] = run_ref[...] + jnp.sum(jnp.where(sel, 1.0, 0.0), axis=0, keepdims=True)
    cnt_ref[...] = run_ref[...]

    lane4 = lax.broadcasted_iota(jnp.int32, (tm, TOP_K), 1)
    ti = jnp.zeros((tm, TOP_K), jnp.int32)
    tr = jnp.zeros((tm, TOP_K), F32)
    tw = jnp.zeros((tm, TOP_K), F32)
    for j in range(TOP_K):
        rank_j = jnp.sum(jnp.where(lane == idxs[j], before, 0.0), axis=-1, keepdims=True)
        ti = jnp.where(lane4 == j, idxs[j], ti)
        tr = jnp.where(lane4 == j, rank_j, tr)
        tw = jnp.where(lane4 == j, exps[j] / den, tw)
    ti_ref[...] = ti
    tr_ref[...] = tr.astype(jnp.int32)
    tw_ref[...] = tw


def _outproj(a_p, a_s, b_p, b_s, x_p, x_s, mod_p, mod_s, norm_g, w_out_b, w_router, b_router):
    tm = ROW_TILE
    npt = x_p.shape[0] // tm
    nst = x_s.shape[0] // tm
    assert nst == 1
    n_tiles = npt + nst
    rows = n_tiles * tm
    tiles_per_batch = npt // mod_p.shape[0]
    last_p = npt - 1
    prow = lambda i: (jnp.minimum(i, last_p), 0)
    srow = lambda i: (0, 0)
    row = lambda i: (i, 0)
    const2 = lambda i: (0, 0)
    return pl.pallas_call(
        functools.partial(_outproj_kernel, n_prompt_tiles=npt),
        grid=(n_tiles,),
        in_specs=[pl.BlockSpec((tm, A_WIDTH), prow), pl.BlockSpec((tm, A_WIDTH), srow),
                  pl.BlockSpec((tm, B_WIDTH), prow), pl.BlockSpec((tm, B_WIDTH), srow),
                  pl.BlockSpec((tm, D_MODEL), prow), pl.BlockSpec((tm, D_MODEL), srow),
                  pl.BlockSpec((1, 6, D_MODEL), lambda i: (jnp.minimum(i, last_p) // tiles_per_batch, 0, 0)),
                  pl.BlockSpec((6, tm, D_MODEL), lambda i: (0, 0, 0)),
                  pl.BlockSpec((4, D_MODEL), const2),
                  pl.BlockSpec((D_MODEL, D_MODEL), const2),
                  pl.BlockSpec((D_MODEL, N_EXPERTS), const2),
                  pl.BlockSpec((1, N_EXPERTS), const2)],
        out_specs=[pl.BlockSpec((tm, D_MODEL), row), pl.BlockSpec((tm, D_MODEL), row),
                   pl.BlockSpec((tm, TOP_K), row), pl.BlockSpec((tm, TOP_K), row),
                   pl.BlockSpec((tm, TOP_K), row), pl.BlockSpec((1, N_EXPERTS), const2)],
        out_shape=[jax.ShapeDtypeStruct((rows, D_MODEL), F32), jax.ShapeDtypeStruct((rows, D_MODEL), F32),
                   jax.ShapeDtypeStruct((rows, TOP_K), jnp.int32), jax.ShapeDtypeStruct((rows, TOP_K), jnp.int32),
                   jax.ShapeDtypeStruct((rows, TOP_K), F32), jax.ShapeDtypeStruct((1, N_EXPERTS), F32)],
        scratch_shapes=[pltpu.VMEM((1, N_EXPERTS), F32)],
        compiler_params=_cparams("arbitrary"),
        name="outproj_router",
    )(a_p, a_s, b_p, b_s, x_p, x_s, mod_p, mod_s, norm_g, w_out_b, w_router, b_router.reshape(1, -1))


def _dispatch_kernel(dest_ref, h_ref, xg_in_ref, xg_ref, sem):
    del xg_in_ref
    tm = h_ref.shape[0]

    def body(t, carry):
        for j in range(TOP_K):
            d = dest_ref[t * TOP_K + j]
            pltpu.make_async_copy(h_ref.at[pl.ds(t, 1), :], xg_ref.at[pl.ds(d, 1), :], sem).start()
        return carry

    lax.fori_loop(0, tm, body, 0)
    n = tm * TOP_K
    pltpu.make_async_copy(xg_ref.at[pl.ds(0, n), :], xg_ref.at[pl.ds(0, n), :], sem).wait()


def _dispatch(dest_flat, h2, xg_zero):
    tm = ROW_TILE
    rows = h2.shape[0]
    return pl.pallas_call(
        _dispatch_kernel,
        grid=(rows // tm,),
        in_specs=[pl.BlockSpec((tm * TOP_K,), lambda i: (i,), memory_space=pltpu.SMEM),
                  pl.BlockSpec((tm, D_MODEL), lambda i: (i, 0)),
                  pl.BlockSpec(memory_space=pl.ANY)],
        out_specs=pl.BlockSpec(memory_space=pl.ANY),
        out_shape=jax.ShapeDtypeStruct(xg_zero.shape, xg_zero.dtype),
        scratch_shapes=[pltpu.SemaphoreType.DMA(())],
        input_output_aliases={2: 0},
        compiler_params=_cparams("arbitrary"),
        name="moe_dispatch",
    )(dest_flat, h2, xg_zero)


def _expert_kernel(be_ref, nu_ref, x_ref, wgu_ref, bgu_ref, wd_ref, bd_ref, y_ref, wgu_b, wd_b):
    i = pl.program_id(0)
    used = i < nu_ref[0]
    fresh = jnp.logical_or(i == 0, be_ref[i] != be_ref[jnp.maximum(i - 1, 0)])

    @pl.when(jnp.logical_and(used, fresh))
    def _():
        wgu_b[...] = wgu_ref[0].astype(BF16)
        wd_b[...] = wd_ref[0].astype(BF16)

    @pl.when(jnp.logical_not(used))
    def _():
        y_ref[...] = jnp.zeros(y_ref.shape, F32)

    @pl.when(used)
    def _():
        x = x_ref[...].astype(BF16)
        gu = jnp.dot(x, wgu_b[...], preferred_element_type=F32) + bgu_ref[0]
        glu = jnp.minimum(gu[:, :D_EXPERT], SWIGLU_LIMIT)
        lin = jnp.clip(gu[:, D_EXPERT:], -SWIGLU_LIMIT, SWIGLU_LIMIT)
        hid = glu * jax.nn.sigmoid(SWIGLU_ALPHA * glu) * (lin + 1.0)
        y_ref[...] = jnp.dot(hid.astype(BF16), wd_b[...], preferred_element_type=F32) + bd_ref[0]


def _experts(block_e, n_used, xg, w_gu_b, b_gu, w_down_b, b_down):
    bm = EXPERT_BLOCK
    n_blocks = xg.shape[0] // bm
    grid_spec = pltpu.PrefetchScalarGridSpec(
        num_scalar_prefetch=2,
        grid=(n_blocks,),
        in_specs=[pl.BlockSpec((bm, D_MODEL), lambda i, be, nu: (i, 0)),
                  pl.BlockSpec((1, D_MODEL, 2 * D_EXPERT), lambda i, be, nu: (be[i], 0, 0)),
                  pl.BlockSpec((1, 1, 2 * D_EXPERT), lambda i, be, nu: (be[i], 0, 0)),
                  pl.BlockSpec((1, D_EXPERT, D_MODEL), lambda i, be, nu: (be[i], 0, 0)),
                  pl.BlockSpec((1, 1, D_MODEL), lambda i, be, nu: (be[i], 0, 0))],
        out_specs=pl.BlockSpec((bm, D_MODEL), lambda i, be, nu: (i, 0)),
        scratch_shapes=[pltpu.VMEM((D_MODEL, 2 * D_EXPERT), BF16), pltpu.VMEM((D_EXPERT, D_MODEL), BF16)])
    return pl.pallas_call(
        _expert_kernel,
        grid_spec=grid_spec,
        out_shape=jax.ShapeDtypeStruct(xg.shape, F32),
        compiler_params=_cparams("arbitrary"),
        name="moe_experts",
    )(block_e, n_used, xg, w_gu_b, b_gu.reshape(N_EXPERTS, 1, -1), w_down_b, b_down.reshape(N_EXPERTS, 1, -1))


def _combine_kernel(dest_ref, tw_ref, x1_ref, mp_ref, ms_ref, ng_ref, yg_ref, yp_ref, ys_ref, rows_ref, sem,
                    *, n_prompt_tiles):
    i = pl.program_id(0)
    tm = x1_ref.shape[0]
    is_p = i < n_prompt_tiles

    def body(t, carry):
        for j in range(TOP_K):
            d = dest_ref[t * TOP_K + j]
            pltpu.make_async_copy(yg_ref.at[pl.ds(d, 1), :], rows_ref.at[pl.ds(j * tm + t, 1), :], sem).start()
        return carry

    lax.fori_loop(0, tm, body, 0)
    n = tm * TOP_K
    pltpu.make_async_copy(yg_ref.at[pl.ds(0, n), :], rows_ref, sem).wait()

    tw = tw_ref[...]
    f = tw[:, 0:1] * rows_ref[0:tm, :]
    for j in range(1, TOP_K):
        f = f + tw[:, j:j + 1] * rows_ref[j * tm:(j + 1) * tm, :]
    g2 = jnp.where(is_p, mp_ref[0, 5:6, :], ms_ref[5])
    y = x1_ref[...] + g2 * _rms(f, ng_ref[3:4, :])

    @pl.when(is_p)
    def _():
        yp_ref[...] = y

    @pl.when(jnp.logical_not(is_p))
    def _():
        ys_ref[...] = y


def _combine(dest_flat, tw, x1, mod_p, mod_s, norm_g, yg, *, n_prompt_rows, n_sample_rows):
    tm = ROW_TILE
    npt = n_prompt_rows // tm
    assert n_sample_rows == tm
    n_tiles = npt + 1
    tiles_per_batch = npt // mod_p.shape[0]
    last_p = npt - 1
    return pl.pallas_call(
        functools.partial(_combine_kernel, n_prompt_tiles=npt),
        grid=(n_tiles,),
        in_specs=[pl.BlockSpec((tm * TOP_K,), lambda i: (i,), memory_space=pltpu.SMEM),
                  pl.BlockSpec((tm, TOP_K), lambda i: (i, 0)),
                  pl.BlockSpec((tm, D_MODEL), lambda i: (i, 0)),
                  pl.BlockSpec((1, 6, D_MODEL), lambda i: (jnp.minimum(i, last_p) // tiles_per_batch, 0, 0)),
                  pl.BlockSpec((6, tm, D_MODEL), lambda i: (0, 0, 0)),
                  pl.BlockSpec((4, D_MODEL), lambda i: (0, 0)),
                  pl.BlockSpec(memory_space=pl.ANY)],
        out_specs=[pl.BlockSpec((tm, D_MODEL), lambda i: (jnp.minimum(i, last_p), 0)),
                   pl.BlockSpec((tm, D_MODEL), lambda i: (0, 0))],
        out_shape=[jax.ShapeDtypeStruct((n_prompt_rows, D_MODEL), F32),
                   jax.ShapeDtypeStruct((n_sample_rows, D_MODEL), F32)],
        scratch_shapes=[pltpu.VMEM((tm * TOP_K, D_MODEL), F32), pltpu.SemaphoreType.DMA(())],
        compiler_params=_cparams("arbitrary"),
        name="moe_combine",
    )(dest_flat, tw, x1, mod_p, mod_s, norm_g, yg)


def _t5_bucket_np(dist):
    n = np.maximum(dist, 0)
    max_exact = NUM_BUCKETS // 2
    nf = np.maximum(n, 1).astype(np.float64)
    large = max_exact + (np.log(nf / max_exact) / math.log(MAX_DISTANCE / max_exact)
                         * (NUM_BUCKETS - max_exact)).astype(np.int32)
    large = np.minimum(large, NUM_BUCKETS - 1)
    return np.where(n < max_exact, n, large).astype(np.int32)


def _bias_from_dist(rel_bias, dist):
    onehot = jax.nn.one_hot(_t5_bucket_np(dist).reshape(-1), NUM_BUCKETS, dtype=F32)
    shifted = (rel_bias - rel_bias[NUM_BUCKETS - 1]).reshape(NUM_BUCKETS, -1)
    out = jnp.dot(onehot, shifted, precision=lax.Precision.HIGHEST).reshape(dist.shape + rel_bias.shape[1:])
    return jnp.where(jnp.asarray(dist >= 0)[..., None, None], out, -jnp.inf).astype(F32)


def _head_masked(t):
    h = t.shape[0]
    same = np.eye(h, dtype=bool)[:, None, None, None, :]
    full = jnp.where(same, t[..., None], -jnp.inf)
    return full.reshape(h * t.shape[1] * t.shape[2], t.shape[3] * h)


def kernel(x_prompt, x_sample, cache_k, cache_v, page_table, c_prompt, c_sample, w_ada, b_ada, norm_g, w_in,
           w_out, ln_v_g, ln_v_b, w_spatial, b_spatial, lam_params, subln_g, rel_bias, w_router, b_router,
           w_gu, b_gu, w_down, b_down):
    batch, seq, d = x_prompt.shape
    nb, ds = x_sample.shape[:2]
    n_pages = page_table.shape[1]
    past = n_pages * PAGE_SIZE
    rows_p = batch * seq
    rows_s = nb * ds
    assert rows_s == ROW_TILE and d == D_MODEL

    mod = _ada(jnp.concatenate([c_prompt, c_sample], axis=0), w_ada[0], b_ada[0])
    mod_p = mod[:batch].reshape(batch, 6, D_MODEL)
    mod_s = jnp.transpose(jnp.repeat(mod[batch:], ds, axis=0).reshape(rows_s, 6, D_MODEL), (1, 0, 2))

    ng = norm_g[0]
    w_in_b = w_in[0].astype(BF16)
    w_out_b = w_out[0].astype(BF16)
    lng = ln_v_g[0].reshape(1, A_WIDTH)
    lnb = ln_v_b[0].reshape(1, A_WIDTH)
    grp = np.arange(A_WIDTH) // A_HEAD
    avg = jnp.asarray((grp[:, None] == grp[None, :]).astype(np.float32) / A_HEAD, BF16)
    ws_p = jnp.tril(w_spatial[0]).astype(BF16)
    bs_p = jnp.repeat(b_spatial[0].T, A_HEAD, axis=1)
    w_small = jnp.tril(w_spatial[0][:, :ds, :ds])
    ws_s = jnp.einsum('ab,gts->gatbs', jnp.eye(nb, dtype=F32), w_small).reshape(A_GROUPS, rows_s, rows_s).astype(BF16)
    bs_s = jnp.tile(jnp.repeat(b_spatial[0][:, :ds].T, A_HEAD, axis=1), (nb, 1))

    xp2 = x_prompt.reshape(rows_p, D_MODEL)
    xs2 = x_sample.reshape(rows_s, D_MODEL)
    assert ATTN_TILE == ROW_TILE
    a_p, va_p, q_p, k_p, v_p, kb_p, vt_p = _inproj(
        xp2, mod_p[:, 0:1], mod_p[:, 1:2], ng[0:1], w_in_b, lng, lnb, avg, ws_p, bs_p,
        tiles_per_mod=seq // ROW_TILE, chunk=CHUNK, q_dtype=BF16, q_scale=1.0 / math.sqrt(B_HEAD), va_rows=CHUNK)
    a_s, va_s, q_s, k_s, v_s, _, _ = _inproj(
        xs2, mod_s[0:1], mod_s[1:2], ng[0:1], w_in_b, lng, lnb, avg, ws_s, bs_s,
        tiles_per_mod=1, chunk=rows_s, q_dtype=F32, q_scale=1.0, va_rows=rows_s)

    ti = np.arange(ATTN_TILE)
    dist_p = np.stack([ti[:, None] - ti[None, :], ti[:, None] - ti[None, :] + ATTN_TILE])
    bias_p = jnp.transpose(_bias_from_dist(rel_bias, dist_p), (3, 0, 2, 4, 1))
    bias_p = bias_p.reshape(B_HEADS, 2, ATTN_TILE, 2 * ATTN_TILE)
    b_p = _attn_p(q_p, kb_p, vt_p, bias_p, lam_params[0], subln_g, batch=batch, seq=seq)

    qi = np.arange(ds)
    ki = np.arange(PAGE_SIZE)
    dist_l = PAGE_SIZE + qi[:, None] - ki[None, :]
    kn_i = np.arange(PAGE_SIZE // B_HEADS)
    dist_n = np.where(kn_i[None, :] < ds, qi[:, None] - kn_i[None, :], -1)
    to_rows = lambda t: jnp.transpose(t, (2, 3, 0, 1))
    bias_l = _head_masked(to_rows(_bias_from_dist(rel_bias, dist_l)))
    bias_n = _head_masked(to_rows(_bias_from_dist(rel_bias, dist_n)))
    maskb = _head_masked(jnp.zeros((B_HEADS, 2, ds, PAGE_SIZE), F32))
    pad_keys = lambda t: jnp.pad(t.reshape(nb, ds * B_HEADS, B_VDIM), ((0, 0), (0, PAGE_SIZE - ds * B_HEADS), (0, 0)))
    n_phys = cache_k.shape[0]
    b_s = _attn_s(page_table, q_s, pad_keys(k_s), pad_keys(v_s), maskb, bias_l, bias_n, lam_params[0], subln_g,
                  cache_k.reshape(n_phys * PAGE_SIZE * B_HEADS, B_VDIM),
                  cache_v.reshape(n_phys * PAGE_SIZE * B_HEADS, B_VDIM), ds=ds)

    x1, h2, top_i, top_r, top_w, counts = _outproj(
        a_p, a_s, b_p, b_s, xp2, xs2, mod_p, mod_s, ng, w_out_b, w_router[0], b_router[0])

    bm = EXPERT_BLOCK
    rows = rows_p + rows_s
    n_blocks = rows * TOP_K // bm + N_EXPERTS
    cnt = counts[0].astype(jnp.int32)
    padded = (cnt + bm - 1) // bm * bm
    pad_end = jnp.cumsum(padded)
    pad_start = pad_end - padded
    experts = jnp.arange(N_EXPERTS, dtype=jnp.int32)
    dest = (jnp.sum(jnp.where(top_i[..., None] == experts, pad_start, 0), axis=-1) + top_r).reshape(-1)
    blk_start = jnp.arange(n_blocks, dtype=jnp.int32) * bm
    block_e = jnp.minimum(jnp.sum((pad_end[None, :] <= blk_start[:, None]).astype(jnp.int32), axis=1),
                          N_EXPERTS - 1)
    n_used = (pad_end[-1:] // bm).astype(jnp.int32)

    xg = _dispatch(dest, h2, jnp.zeros((n_blocks * bm, D_MODEL), F32))
    yg = _experts(block_e, n_used, xg, w_gu[0], b_gu[0], w_down[0], b_down[0])
    y_p, y_s = _combine(dest, top_w, x1, mod_p, mod_s, ng, yg, n_prompt_rows=rows_p, n_sample_rows=rows_s)

    return (y_p.reshape(batch, seq, D_MODEL),
            y_s.reshape(nb, ds, D_MODEL),
            k_p.reshape(batch, seq, 1, B_HEADS, B_VDIM),
            v_p.reshape(batch, seq, 1, B_HEADS, B_VDIM),
            k_s.reshape(nb, ds, 1, B_HEADS, B_VDIM),
            v_s.reshape(nb, ds, 1, B_HEADS, B_VDIM),
            va_p.reshape(batch, CHUNK, 1, A_WIDTH),
            va_s.reshape(nb, ds, 1, A_WIDTH))
```

```python
import functools
import math

import numpy as np
import jax
import jax.numpy as jnp
from jax import lax
from jax.experimental import pallas as pl
from jax.experimental.pallas import tpu as pltpu

F32 = jnp.float32
BF16 = jnp.bfloat16

D_MODEL = 1024
A_WIDTH = 512
A_HEAD = 64
A_GROUPS = 8
CHUNK = 128
B_WIDTH = 512
B_HEAD = 64
B_VDIM = 128
B_HEADS = 4
IN_WIDTH = 2 * A_WIDTH + 3 * B_WIDTH
NUM_BUCKETS = 32
MAX_DISTANCE = 128
PAGE_SIZE = 128
N_EXPERTS = 32
TOP_K = 4
D_EXPERT = 1024
SWIGLU_LIMIT = 7.0
SWIGLU_ALPHA = 1.702
NORM_EPS = 1e-6
LAM_INIT = 0.8 - 0.6 * math.exp(-0.3 * 0)

ROW_TILE = 256
ATTN_TILE = 256
ATTN_KEYS = 128
PAGES_PER_STEP = 16
LOG2E = math.log2(math.e)
LANES = 128
EXPERT_BLOCK = 256
VMEM_LIMIT = 56 * 1024 * 1024


def _cparams(*sem):
    return pltpu.CompilerParams(dimension_semantics=sem, vmem_limit_bytes=VMEM_LIMIT)


def _rms(x, g):
    return x * lax.rsqrt(jnp.mean(x * x, axis=-1, keepdims=True) + NORM_EPS) * g


def _ada_kernel(c_ref, w_ref, b_ref, o_ref):
    s = jax.nn.silu(c_ref[...]).astype(BF16)
    o_ref[...] = jnp.dot(s, w_ref[...].astype(BF16), preferred_element_type=F32) + b_ref[...]


def _ada(c_all, w_ada, b_ada):
    n = c_all.shape[0]
    tn = 1024
    return pl.pallas_call(
        _ada_kernel,
        grid=(6 * D_MODEL // tn,),
        in_specs=[pl.BlockSpec((n, D_MODEL), lambda j: (0, 0)),
                  pl.BlockSpec((D_MODEL, tn), lambda j: (0, j)),
                  pl.BlockSpec((1, tn), lambda j: (0, j))],
        out_specs=pl.BlockSpec((n, tn), lambda j: (0, j)),
        out_shape=jax.ShapeDtypeStruct((n, 6 * D_MODEL), F32),
        compiler_params=_cparams("arbitrary"),
        name="ada",
    )(c_all, w_ada, b_ada.reshape(1, -1))


def _inproj_kernel(x_ref, sh_ref, sc_ref, ng_ref, w_ref, lng_ref, lnb_ref, avg_ref, ws_ref, bs_ref,
                   a_ref, va_ref, q_ref, k_ref, v_ref, kb_ref, vt_ref, *, chunk, q_scale):
    tm = x_ref.shape[0]
    va_rows = va_ref.shape[0]
    h = (_rms(x_ref[...], ng_ref[...]) * (1.0 + sc_ref[0]) + sh_ref[0]).astype(BF16)

    u = jax.nn.gelu(jnp.dot(h, w_ref[:, 0:A_WIDTH], preferred_element_type=F32))
    gv = jax.nn.gelu(jnp.dot(h, w_ref[:, A_WIDTH:2 * A_WIDTH], preferred_element_type=F32))

    avg = avg_ref[...]

    def group_mean(t):
        hi = t.astype(BF16)
        lo = (t - hi.astype(F32)).astype(BF16)
        return (jnp.dot(hi, avg, preferred_element_type=F32)
                + jnp.dot(lo, avg, preferred_element_type=F32))

    xc = gv - group_mean(gv)
    va = xc * lax.rsqrt(group_mean(xc * xc) + NORM_EPS) * lng_ref[...] + lnb_ref[...]
    va_ref[...] = va[tm - va_rows:, :]

    vab = va.astype(BF16)
    lane = lax.broadcasted_iota(jnp.int32, (chunk, 128), 1)
    for c in range(tm // chunk):
        r0 = c * chunk
        for pair in range(A_GROUPS // 2):
            c0 = pair * 128
            vp = vab[r0:r0 + chunk, c0:c0 + 128]
            lo_half = jnp.where(lane < A_HEAD, vp, jnp.zeros_like(vp))
            hi_half = jnp.where(lane >= A_HEAD, vp, jnp.zeros_like(vp))
            s = (jnp.dot(ws_ref[2 * pair], lo_half, preferred_element_type=F32)
                 + jnp.dot(ws_ref[2 * pair + 1], hi_half, preferred_element_type=F32))
            a = u[r0:r0 + chunk, c0:c0 + 128] * (s + bs_ref[:, c0:c0 + 128])
            a_ref[r0:r0 + chunk, c0:c0 + 128] = a.astype(a_ref.dtype)

    q = jnp.dot(h, w_ref[:, 2 * A_WIDTH:2 * A_WIDTH + B_WIDTH], preferred_element_type=F32)
    q_ref[...] = (q * q_scale).astype(q_ref.dtype)
    k = jnp.dot(h, w_ref[:, 2 * A_WIDTH + B_WIDTH:2 * A_WIDTH + 2 * B_WIDTH], preferred_element_type=F32)
    kb_ref[...] = k.astype(BF16)
    v = jnp.dot(h, w_ref[:, 2 * A_WIDTH + 2 * B_WIDTH:IN_WIDTH], preferred_element_type=F32)
    for hh in range(B_HEADS):
        c0 = hh * B_VDIM
        k_ref[pl.ds(hh, tm, stride=B_HEADS), :] = k[:, c0:c0 + B_VDIM]
        v_ref[pl.ds(hh, tm, stride=B_HEADS), :] = v[:, c0:c0 + B_VDIM]
        vt_ref[0, hh, 0] = v[:, c0:c0 + B_VDIM].T.astype(BF16)


def _inproj(x2, sh, sc, ng, w_in_b, lng, lnb, avg, ws, bs, *, tiles_per_mod, chunk, q_dtype, q_scale, va_rows):
    rows = x2.shape[0]
    tm = ROW_TILE
    n_tiles = rows // tm
    mod_rows = sh.shape[1]
    row = lambda i: (i, 0)
    const2 = lambda i: (0, 0)
    mod_map = lambda i: (i // tiles_per_mod, 0, 0)
    out_w = lambda w, dt: jax.ShapeDtypeStruct((rows, w), dt)
    head_rows = jax.ShapeDtypeStruct((rows * B_HEADS, B_VDIM), F32)
    head_rows_spec = pl.BlockSpec((tm * B_HEADS, B_VDIM), row)
    vt_shape = jax.ShapeDtypeStruct((n_tiles // tiles_per_mod, B_HEADS, tiles_per_mod, B_VDIM, tm), BF16)
    vt_spec = pl.BlockSpec((1, B_HEADS, 1, B_VDIM, tm), lambda i: (i // tiles_per_mod, 0, i % tiles_per_mod, 0, 0))
    return pl.pallas_call(
        functools.partial(_inproj_kernel, chunk=chunk, q_scale=q_scale),
        grid=(rows // tm,),
        in_specs=[pl.BlockSpec((tm, D_MODEL), row),
                  pl.BlockSpec((1, mod_rows, D_MODEL), mod_map),
                  pl.BlockSpec((1, mod_rows, D_MODEL), mod_map),
                  pl.BlockSpec((1, D_MODEL), const2),
                  pl.BlockSpec((D_MODEL, IN_WIDTH), const2),
                  pl.BlockSpec((1, A_WIDTH), const2),
                  pl.BlockSpec((1, A_WIDTH), const2),
                  pl.BlockSpec((A_WIDTH, A_WIDTH), const2),
                  pl.BlockSpec((A_GROUPS, chunk, chunk), lambda i: (0, 0, 0)),
                  pl.BlockSpec((chunk, A_WIDTH), const2)],
        out_specs=[pl.BlockSpec((tm, A_WIDTH), row),
                   pl.BlockSpec((va_rows, A_WIDTH), lambda i: (i // tiles_per_mod, 0)),
                   pl.BlockSpec((tm, B_WIDTH), row), head_rows_spec, head_rows_spec,
                   pl.BlockSpec((tm, B_WIDTH), row), vt_spec],
        out_shape=[out_w(A_WIDTH, BF16),
                   jax.ShapeDtypeStruct((n_tiles // tiles_per_mod * va_rows, A_WIDTH), F32),
                   out_w(B_WIDTH, q_dtype), head_rows, head_rows, out_w(B_WIDTH, BF16), vt_shape],
        compiler_params=_cparams("arbitrary"),
        name="inproj",
    )(x2, sh, sc, ng, w_in_b, lng, lnb, avg, ws, bs)


def _stack_halves(q):
    lane = lax.broadcasted_iota(jnp.int32, q.shape, 1)
    zero = jnp.zeros_like(q)
    return jnp.concatenate([jnp.where(lane < B_HEAD, q, zero), jnp.where(lane >= B_HEAD, q, zero)], axis=0)


def _lambda(lam_ref):
    lp = lam_ref[...]
    return (jnp.exp(jnp.sum(lp[0:1] * lp[1:2], axis=-1, keepdims=True))
            - jnp.exp(jnp.sum(lp[2:3] * lp[3:4], axis=-1, keepdims=True)) + LAM_INIT)


def _diff_finish(acc, l, n, lam, g):
    o = acc[:n] / l[:n] - lam * (acc[n:] / l[n:])
    return _rms(o, g) * (1.0 - LAM_INIT)


def _attn_p_kernel(q_ref, k_ref, vt_ref, bias_ref, lam_ref, g_ref, o_ref,
                   acc_ref, m_ref, l_ref, s_ref, tmax_ref, p_ref, alpha_ref, *, tq, tk):
    qi = pl.program_id(2)
    q = q_ref[...]
    lane = lax.broadcasted_iota(jnp.int32, q.shape, 1)
    zero = jnp.zeros_like(q)
    q_half = (jnp.where(lane < B_HEAD, q, zero), jnp.where(lane >= B_HEAD, q, zero))
    m_ref[...] = jnp.full(m_ref.shape, -jnp.inf, F32)
    l_ref[...] = jnp.zeros(l_ref.shape, F32)
    acc_ref[...] = jnp.zeros(acc_ref.shape, F32)
    p_ref[1] = jnp.zeros(p_ref.shape[1:], BF16)
    alpha_ref[1] = jnp.ones(alpha_ref.shape[1:], F32)

    def scores(j):
        k = k_ref[pl.ds(pl.multiple_of(j * tk, tk), tk), :]
        return [lax.dot_general(k, q_half[c], (((1,), (1,)), ((), ())), preferred_element_type=F32)
                for c in range(2)]

    def keep_scores(j, kind, ss):
        slot = j % 2
        for c in range(2):
            s = ss[c]
            if kind is not None:
                s = s + bias_ref[0, kind, :, c * tq:(c + 1) * tq]
            s_ref[slot, c] = s
            tmax_ref[slot, c] = jnp.max(s, axis=0, keepdims=True)

    def softmax(j):
        slot = j % 2
        for c in range(2):
            m_old = m_ref[c]
            m_new = jnp.maximum(m_old, tmax_ref[slot, c])
            alpha = jnp.exp2(m_old - m_new)
            p = jnp.exp2(s_ref[slot, c] - m_new)
            l_ref[c] = alpha * l_ref[c] + jnp.sum(p, axis=0, keepdims=True)
            m_ref[c] = m_new
            p_ref[slot, c] = p.astype(BF16)
            alpha_ref[slot, c] = alpha

    def pv(j):
        slot = j % 2
        vt = vt_ref[0, 0, jnp.maximum(j, 0)]
        for c in range(2):
            acc_ref[c] = alpha_ref[slot, c] * acc_ref[c] + jnp.dot(vt, p_ref[slot, c], preferred_element_type=F32)

    def far_body(j, carry):
        ss = scores(j + 1)
        pv(j - 1)
        softmax(j)
        keep_scores(j + 1, None, ss)
        return carry

    def near_body(j, carry):
        nxt = jnp.minimum(j + 1, qi)
        ss = scores(nxt)
        pv(j - 1)
        softmax(j)
        keep_scores(nxt, qi - nxt, ss)
        return carry

    keep_scores(0, jnp.minimum(qi, 2), scores(0))
    n_far_fetch = jnp.maximum(qi - 2, 0)
    lax.fori_loop(0, n_far_fetch, far_body, 0)
    lax.fori_loop(n_far_fetch, qi + 1, near_body, 0)
    pv(qi)

    o = acc_ref[0] / l_ref[0] - _lambda(lam_ref) * (acc_ref[1] / l_ref[1])
    y = o * lax.rsqrt(jnp.mean(o * o, axis=0, keepdims=True) + NORM_EPS) * (1.0 - LAM_INIT)
    o_ref[...] = (y.T * g_ref[...]).astype(o_ref.dtype)


def _attn_p(qb, kb, vt, bias, lam_params, subln_g, *, batch, seq):
    tq = tk = ATTN_TILE
    nq = seq // tq
    return pl.pallas_call(
        functools.partial(_attn_p_kernel, tq=tq, tk=tk),
        grid=(batch, B_HEADS, nq),
        in_specs=[pl.BlockSpec((tq, B_VDIM), lambda b, h, i: (b * nq + i, h)),
                  pl.BlockSpec((seq, B_VDIM), lambda b, h, i: (b, h)),
                  pl.BlockSpec((1, 1, seq // tk, B_VDIM, tk), lambda b, h, i: (b, h, 0, 0, 0)),
                  pl.BlockSpec((1, 3, tk, 2 * tq), lambda b, h, i: (h, 0, 0, 0)),
                  pl.BlockSpec((4, B_HEAD), lambda b, h, i: (0, 0)),
                  pl.BlockSpec((1, B_VDIM), lambda b, h, i: (0, 0))],
        out_specs=pl.BlockSpec((tq, B_VDIM), lambda b, h, i: (b * nq + i, h)),
        out_shape=jax.ShapeDtypeStruct((batch * seq, B_WIDTH), BF16),
        scratch_shapes=[pltpu.VMEM((2, B_VDIM, tq), F32),
                        pltpu.VMEM((2, 1, tq), F32),
                        pltpu.VMEM((2, 1, tq), F32),
                        pltpu.VMEM((2, 2, tk, tq), F32),
                        pltpu.VMEM((2, 2, 1, tq), F32),
                        pltpu.VMEM((2, 2, tk, tq), BF16),
                        pltpu.VMEM((2, 2, 1, tq), F32)],
        compiler_params=_cparams("arbitrary", "arbitrary", "arbitrary"),
        name="attn_prompt",
    )(qb, kb, vt, bias, lam_params, subln_g)


def _attn_s_kernel(pt_ref, q_ref, kn_ref, vn_ref, maskb_ref, biasl_ref, biasn_ref, lam_ref, g_ref, *rest,
                   npages, ds):
    del pt_ref
    k_refs = rest[:npages]
    v_refs = rest[npages:2 * npages]
    o_ref = rest[2 * npages]
    acc_ref, m_ref, l_ref = rest[2 * npages + 1:]
    g = pl.program_id(1)
    last = g == pl.num_programs(1) - 1

    @pl.when(g == 0)
    def _():
        m_ref[...] = jnp.full(m_ref.shape, -jnp.inf, F32)
        l_ref[...] = jnp.zeros(l_ref.shape, F32)
        acc_ref[...] = jnp.zeros(acc_ref.shape, F32)

    q = q_ref[...] * (1.0 / math.sqrt(B_HEAD))
    qall = jnp.concatenate([_stack_halves(q[:, h * B_VDIM:(h + 1) * B_VDIM]) for h in range(B_HEADS)],
                           axis=0).astype(BF16)

    def process(k_blocks, v_blocks, biases):
        s = jnp.concatenate(
            [lax.dot_general(qall, kb.astype(BF16), (((1,), (1,)), ((), ())), preferred_element_type=F32) + bb
             for kb, bb in zip(k_blocks, biases)], axis=1)
        m_old = m_ref[...]
        m_new = jnp.maximum(m_old, jnp.max(s, axis=-1, keepdims=True))
        alpha = jnp.exp(m_old - m_new)
        pr = jnp.exp(s - m_new)
        l_ref[...] = alpha * l_ref[...] + jnp.sum(pr, axis=-1, keepdims=True)
        prb = pr.astype(BF16)
        pv = None
        off = 0
        for vb in v_blocks:
            n = vb.shape[0]
            t = jnp.dot(prb[:, off:off + n], vb.astype(BF16), preferred_element_type=F32)
            pv = t if pv is None else pv + t
            off += n
        acc_ref[...] = alpha * acc_ref[...] + pv
        m_ref[...] = m_new

    maskb = maskb_ref[...]
    newest = jnp.where(last, biasl_ref[...], maskb)
    process([k_refs[p][...] for p in range(npages)], [v_refs[p][...] for p in range(npages)],
            [maskb] * (npages - 1) + [newest])

    @pl.when(last)
    def _():
        process([kn_ref[0]], [vn_ref[0]], [biasn_ref[...]])
        lam = _lambda(lam_ref)
        acc = acc_ref[...]
        l = l_ref[...]
        for h in range(B_HEADS):
            r0 = h * 2 * ds
            o_ref[:, h * B_VDIM:(h + 1) * B_VDIM] = _diff_finish(
                acc[r0:r0 + 2 * ds], l[r0:r0 + 2 * ds], ds, lam, g_ref[...])


def _attn_s(page_table, qs, kn, vn, maskb, bias_last, bias_new, lam_params, subln_g, cache_k2, cache_v2, *, ds):
    nb, n_pages = page_table.shape
    npg = PAGES_PER_STEP
    steps = n_pages // npg
    page_rows = PAGE_SIZE * B_HEADS
    nrow = B_HEADS * 2 * ds

    def page_spec(p):
        return pl.BlockSpec((page_rows, B_VDIM), lambda b, g, pt, p=p: (pt[b, g * npg + p], 0))

    const2 = lambda b, g, pt: (0, 0)
    grid_spec = pltpu.PrefetchScalarGridSpec(
        num_scalar_prefetch=1,
        grid=(nb, steps),
        in_specs=[pl.BlockSpec((ds, B_WIDTH), lambda b, g, pt: (b, 0)),
                  pl.BlockSpec((1, PAGE_SIZE, B_VDIM), lambda b, g, pt: (b, 0, 0)),
                  pl.BlockSpec((1, PAGE_SIZE, B_VDIM), lambda b, g, pt: (b, 0, 0)),
                  pl.BlockSpec((nrow, page_rows), const2),
                  pl.BlockSpec((nrow, page_rows), const2),
                  pl.BlockSpec((nrow, PAGE_SIZE), const2),
                  pl.BlockSpec((4, B_HEAD), const2),
                  pl.BlockSpec((1, B_VDIM), const2)]
                 + [page_spec(p) for p in range(npg)] + [page_spec(p) for p in range(npg)],
        out_specs=pl.BlockSpec((ds, B_WIDTH), lambda b, g, pt: (b, 0)),
        scratch_shapes=[pltpu.VMEM((nrow, B_VDIM), F32),
                        pltpu.VMEM((nrow, 1), F32),
                        pltpu.VMEM((nrow, 1), F32)])
    return pl.pallas_call(
        functools.partial(_attn_s_kernel, npages=npg, ds=ds),
        grid_spec=grid_spec,
        out_shape=jax.ShapeDtypeStruct((nb * ds, B_WIDTH), F32),
        compiler_params=_cparams("arbitrary", "arbitrary"),
        name="attn_sample",
    )(page_table, qs, kn, vn, maskb, bias_last, bias_new, lam_params, subln_g,
      *([cache_k2] * npg), *([cache_v2] * npg))


def _outproj_kernel(ap_ref, as_ref, bp_ref, bs_ref, xp_ref, xs_ref, mp_ref, ms_ref, ng_ref, wo_ref,
                    wr_ref, br_ref, x1_ref, h2_ref, ti_ref, tr_ref, tw_ref, cnt_ref, run_ref, *, n_prompt_tiles):
    i = pl.program_id(0)
    tm = xp_ref.shape[0]
    is_p = i < n_prompt_tiles

    @pl.when(i == 0)
    def _():
        run_ref[...] = jnp.zeros(run_ref.shape, F32)

    a = jnp.where(is_p, ap_ref[...], as_ref[...])
    b = jnp.where(is_p, bp_ref[...], bs_ref[...].astype(BF16))
    x = jnp.where(is_p, xp_ref[...], xs_ref[...])
    g1 = jnp.where(is_p, mp_ref[0, 2:3, :], ms_ref[2])
    sh2 = jnp.where(is_p, mp_ref[0, 3:4, :], ms_ref[3])
    sc2 = jnp.where(is_p, mp_ref[0, 4:5, :], ms_ref[4])

    mix = (jnp.dot(a, wo_ref[0:A_WIDTH, :], preferred_element_type=F32)
           + jnp.dot(b, wo_ref[A_WIDTH:, :], preferred_element_type=F32))
    x1 = x + g1 * _rms(mix, ng_ref[1:2, :])
    x1_ref[...] = x1
    h2 = _rms(x1, ng_ref[2:3, :]) * (1.0 + sc2) + sh2
    h2_ref[...] = h2

    logits = jnp.dot(h2.astype(BF16), wr_ref[...], preferred_element_type=F32) + br_ref[...]
    work = logits.T[0:N_EXPERTS, :]
    sub = lax.broadcasted_iota(jnp.int32, work.shape, 0)
    vals, idxs = [], []
    for _ in range(TOP_K):
        mx = jnp.max(work, axis=0, keepdims=True)
        ix = jnp.min(jnp.where(work == mx, sub, N_EXPERTS), axis=0, keepdims=True)
        vals.append(mx)
        idxs.append(ix)
        work = jnp.where(sub == ix, -jnp.inf, work)
    exps = [jnp.exp(v - vals[0]) for v in vals]
    den = exps[0] + exps[1] + exps[2] + exps[3]

    sel = jnp.where(work == -jnp.inf, 1.0, 0.0)
    r_i = lax.broadcasted_iota(jnp.int32, (tm, tm), 0)
    c_i = lax.broadcasted_iota(jnp.int32, (tm, tm), 1)
    earlier = jnp.where(r_i < c_i, 1.0, 0.0).astype(BF16)
    before = jnp.dot(sel.astype(BF16), earlier, preferred_element_type=F32) + run_ref[...]
    run_ref[...] = run_ref[...] + jnp.sum(sel, axis=1, keepdims=True)
    cnt_ref[...] = run_ref[...]

    ranks = [jnp.sum(jnp.where(sub == ix, before, 0.0), axis=0, keepdims=True) for ix in idxs]
    ti_ref[...] = jnp.concatenate(idxs, axis=0)
    tr_ref[...] = jnp.concatenate(ranks, axis=0).astype(jnp.int32)
    tw_ref[...] = jnp.concatenate([e / den for e in exps], axis=0)


def _outproj(a_p, a_s, b_p, b_s, x_p, x_s, mod_p, mod_s, norm_g, w_out_b, w_router, b_router):
    tm = ROW_TILE
    npt = x_p.shape[0] // tm
    nst = x_s.shape[0] // tm
    assert nst == 1
    n_tiles = npt + nst
    rows = n_tiles * tm
    tiles_per_batch = npt // mod_p.shape[0]
    last_p = npt - 1
    prow = lambda i: (jnp.minimum(i, last_p), 0)
    srow = lambda i: (0, 0)
    row = lambda i: (i, 0)
    col = lambda i: (0, i)
    const2 = lambda i: (0, 0)
    return pl.pallas_call(
        functools.partial(_outproj_kernel, n_prompt_tiles=npt),
        grid=(n_tiles,),
        in_specs=[pl.BlockSpec((tm, A_WIDTH), prow), pl.BlockSpec((tm, A_WIDTH), srow),
                  pl.BlockSpec((tm, B_WIDTH), prow), pl.BlockSpec((tm, B_WIDTH), srow),
                  pl.BlockSpec((tm, D_MODEL), prow), pl.BlockSpec((tm, D_MODEL), srow),
                  pl.BlockSpec((1, 6, D_MODEL), lambda i: (jnp.minimum(i, last_p) // tiles_per_batch, 0, 0)),
                  pl.BlockSpec((6, tm, D_MODEL), lambda i: (0, 0, 0)),
                  pl.BlockSpec((4, D_MODEL), const2),
                  pl.BlockSpec((D_MODEL, D_MODEL), const2),
                  pl.BlockSpec((D_MODEL, LANES), const2),
                  pl.BlockSpec((1, LANES), const2)],
        out_specs=[pl.BlockSpec((tm, D_MODEL), row), pl.BlockSpec((tm, D_MODEL), row),
                   pl.BlockSpec((TOP_K, tm), col), pl.BlockSpec((TOP_K, tm), col),
                   pl.BlockSpec((TOP_K, tm), col), pl.BlockSpec((N_EXPERTS, 1), const2)],
        out_shape=[jax.ShapeDtypeStruct((rows, D_MODEL), F32), jax.ShapeDtypeStruct((rows, D_MODEL), F32),
                   jax.ShapeDtypeStruct((TOP_K, rows), jnp.int32), jax.ShapeDtypeStruct((TOP_K, rows), jnp.int32),
                   jax.ShapeDtypeStruct((TOP_K, rows), F32), jax.ShapeDtypeStruct((N_EXPERTS, 1), F32)],
        scratch_shapes=[pltpu.VMEM((N_EXPERTS, 1), F32)],
        compiler_params=_cparams("arbitrary"),
        name="outproj_router",
    )(a_p, a_s, b_p, b_s, x_p, x_s, mod_p, mod_s, norm_g, w_out_b,
      jnp.pad(w_router, ((0, 0), (0, LANES - N_EXPERTS))).astype(BF16),
      jnp.pad(b_router.reshape(1, -1), ((0, 0), (0, LANES - N_EXPERTS))))


def _dispatch_kernel(dest_ref, h_ref, xg_in_ref, xg_ref, sem):
    del xg_in_ref
    tm = h_ref.shape[0]

    def body(t, carry):
        for j in range(TOP_K):
            d = dest_ref[j * tm + t]
            pltpu.make_async_copy(h_ref.at[pl.ds(t, 1), :], xg_ref.at[pl.ds(d, 1), :], sem).start()
        return carry

    lax.fori_loop(0, tm, body, 0)
    n = tm * TOP_K
    pltpu.make_async_copy(xg_ref.at[pl.ds(0, n), :], xg_ref.at[pl.ds(0, n), :], sem).wait()


def _dispatch(dest_flat, h2, xg_zero):
    tm = ROW_TILE
    rows = h2.shape[0]
    return pl.pallas_call(
        _dispatch_kernel,
        grid=(rows // tm,),
        in_specs=[pl.BlockSpec((tm * TOP_K,), lambda i: (i,), memory_space=pltpu.SMEM),
                  pl.BlockSpec((tm, D_MODEL), lambda i: (i, 0)),
                  pl.BlockSpec(memory_space=pl.ANY)],
        out_specs=pl.BlockSpec(memory_space=pl.ANY),
        out_shape=jax.ShapeDtypeStruct(xg_zero.shape, xg_zero.dtype),
        scratch_shapes=[pltpu.SemaphoreType.DMA(())],
        input_output_aliases={2: 0},
        compiler_params=_cparams("arbitrary"),
        name="moe_dispatch",
    )(dest_flat, h2, xg_zero)


def _expert_kernel(be_ref, nu_ref, x_ref, wgu_ref, bgu_ref, wd_ref, bd_ref, y_ref, wgu_b, wd_b):
    i = pl.program_id(0)
    used = i < nu_ref[0]
    fresh = jnp.logical_or(i == 0, be_ref[i] != be_ref[jnp.maximum(i - 1, 0)])

    @pl.when(jnp.logical_and(used, fresh))
    def _():
        wgu_b[...] = wgu_ref[0].astype(BF16)
        wd_b[...] = wd_ref[0].astype(BF16)

    @pl.when(jnp.logical_not(used))
    def _():
        y_ref[...] = jnp.zeros(y_ref.shape, F32)

    @pl.when(used)
    def _():
        x = x_ref[...].astype(BF16)
        gu = jnp.dot(x, wgu_b[...], preferred_element_type=F32) + bgu_ref[0]
        glu = jnp.minimum(gu[:, :D_EXPERT], SWIGLU_LIMIT)
        lin = jnp.clip(gu[:, D_EXPERT:], -SWIGLU_LIMIT, SWIGLU_LIMIT)
        hid = glu * jax.nn.sigmoid(SWIGLU_ALPHA * glu) * (lin + 1.0)
        y_ref[...] = jnp.dot(hid.astype(BF16), wd_b[...], preferred_element_type=F32) + bd_ref[0]


def _experts(block_e, n_used, xg, w_gu_b, b_gu, w_down_b, b_down):
    bm = EXPERT_BLOCK
    n_blocks = xg.shape[0] // bm
    grid_spec = pltpu.PrefetchScalarGridSpec(
        num_scalar_prefetch=2,
        grid=(n_blocks,),
        in_specs=[pl.BlockSpec((bm, D_MODEL), lambda i, be, nu: (i, 0)),
                  pl.BlockSpec((1, D_MODEL, 2 * D_EXPERT), lambda i, be, nu: (be[i], 0, 0)),
                  pl.BlockSpec((1, 1, 2 * D_EXPERT), lambda i, be, nu: (be[i], 0, 0)),
                  pl.BlockSpec((1, D_EXPERT, D_MODEL), lambda i, be, nu: (be[i], 0, 0)),
                  pl.BlockSpec((1, 1, D_MODEL), lambda i, be, nu: (be[i], 0, 0))],
        out_specs=pl.BlockSpec((bm, D_MODEL), lambda i, be, nu: (i, 0)),
        scratch_shapes=[pltpu.VMEM((D_MODEL, 2 * D_EXPERT), BF16), pltpu.VMEM((D_EXPERT, D_MODEL), BF16)])
    return pl.pallas_call(
        _expert_kernel,
        grid_spec=grid_spec,
        out_shape=jax.ShapeDtypeStruct(xg.shape, F32),
        compiler_params=_cparams("arbitrary"),
        name="moe_experts",
    )(block_e, n_used, xg, w_gu_b, b_gu.reshape(N_EXPERTS, 1, -1), w_down_b, b_down.reshape(N_EXPERTS, 1, -1))


def _combine_kernel(dest_ref, tw_ref, x1_ref, mp_ref, ms_ref, ng_ref, yg_ref, yp_ref, ys_ref, rows_ref, sem,
                    *, n_prompt_tiles):
    i = pl.program_id(0)
    tm = x1_ref.shape[0]
    is_p = i < n_prompt_tiles

    def body(t, carry):
        for j in range(TOP_K):
            d = dest_ref[j * tm + t]
            pltpu.make_async_copy(yg_ref.at[pl.ds(d, 1), :], rows_ref.at[pl.ds(j * tm + t, 1), :], sem).start()
        return carry

    lax.fori_loop(0, tm, body, 0)
    n = tm * TOP_K
    pltpu.make_async_copy(yg_ref.at[pl.ds(0, n), :], rows_ref, sem).wait()

    tw = tw_ref[...]
    f = tw[:, 0:1] * rows_ref[0:tm, :]
    for j in range(1, TOP_K):
        f = f + tw[:, j:j + 1] * rows_ref[j * tm:(j + 1) * tm, :]
    g2 = jnp.where(is_p, mp_ref[0, 5:6, :], ms_ref[5])
    y = x1_ref[...] + g2 * _rms(f, ng_ref[3:4, :])

    @pl.when(is_p)
    def _():
        yp_ref[...] = y

    @pl.when(jnp.logical_not(is_p))
    def _():
        ys_ref[...] = y


def _combine(dest_flat, tw, x1, mod_p, mod_s, norm_g, yg, *, n_prompt_rows, n_sample_rows):
    tm = ROW_TILE
    npt = n_prompt_rows // tm
    assert n_sample_rows == tm
    n_tiles = npt + 1
    tiles_per_batch = npt // mod_p.shape[0]
    last_p = npt - 1
    return pl.pallas_call(
        functools.partial(_combine_kernel, n_prompt_tiles=npt),
        grid=(n_tiles,),
        in_specs=[pl.BlockSpec((tm * TOP_K,), lambda i: (i,), memory_space=pltpu.SMEM),
                  pl.BlockSpec((tm, TOP_K), lambda i: (i, 0)),
                  pl.BlockSpec((tm, D_MODEL), lambda i: (i, 0)),
                  pl.BlockSpec((1, 6, D_MODEL), lambda i: (jnp.minimum(i, last_p) // tiles_per_batch, 0, 0)),
                  pl.BlockSpec((6, tm, D_MODEL), lambda i: (0, 0, 0)),
                  pl.BlockSpec((4, D_MODEL), lambda i: (0, 0)),
                  pl.BlockSpec(memory_space=pl.ANY)],
        out_specs=[pl.BlockSpec((tm, D_MODEL), lambda i: (jnp.minimum(i, last_p), 0)),
                   pl.BlockSpec((tm, D_MODEL), lambda i: (0, 0))],
        out_shape=[jax.ShapeDtypeStruct((n_prompt_rows, D_MODEL), F32),
                   jax.ShapeDtypeStruct((n_sample_rows, D_MODEL), F32)],
        scratch_shapes=[pltpu.VMEM((tm * TOP_K, D_MODEL), F32), pltpu.SemaphoreType.DMA(())],
        compiler_params=_cparams("arbitrary"),
        name="moe_combine",
    )(dest_flat, tw, x1, mod_p, mod_s, norm_g, yg)


def _t5_bucket_np(dist):
    n = np.maximum(dist, 0)
    max_exact = NUM_BUCKETS // 2
    nf = np.maximum(n, 1).astype(np.float64)
    large = max_exact + (np.log(nf / max_exact) / math.log(MAX_DISTANCE / max_exact)
                         * (NUM_BUCKETS - max_exact)).astype(np.int32)
    large = np.minimum(large, NUM_BUCKETS - 1)
    return np.where(n < max_exact, n, large).astype(np.int32)


def _bias_from_dist(rel_bias, dist):
    onehot = jax.nn.one_hot(_t5_bucket_np(dist).reshape(-1), NUM_BUCKETS, dtype=F32)
    shifted = (rel_bias - rel_bias[NUM_BUCKETS - 1]).reshape(NUM_BUCKETS, -1)
    out = jnp.dot(onehot, shifted, precision=lax.Precision.HIGHEST).reshape(dist.shape + rel_bias.shape[1:])
    return jnp.where(jnp.asarray(dist >= 0)[..., None, None], out, -jnp.inf).astype(F32)


def _head_masked(t):
    h = t.shape[0]
    same = np.eye(h, dtype=bool)[:, None, None, None, :]
    full = jnp.where(same, t[..., None], -jnp.inf)
    return full.reshape(h * t.shape[1] * t.shape[2], t.shape[3] * h)


def kernel(x_prompt, x_sample, cache_k, cache_v, page_table, c_prompt, c_sample, w_ada, b_ada, norm_g, w_in,
           w_out, ln_v_g, ln_v_b, w_spatial, b_spatial, lam_params, subln_g, rel_bias, w_router, b_router,
           w_gu, b_gu, w_down, b_down):
    batch, seq, d = x_prompt.shape
    nb, ds = x_sample.shape[:2]
    n_pages = page_table.shape[1]
    past = n_pages * PAGE_SIZE
    rows_p = batch * seq
    rows_s = nb * ds
    assert rows_s == ROW_TILE and d == D_MODEL

    mod = _ada(jnp.concatenate([c_prompt, c_sample], axis=0), w_ada[0], b_ada[0])
    mod_p = mod[:batch].reshape(batch, 6, D_MODEL)
    mod_s = jnp.transpose(jnp.repeat(mod[batch:], ds, axis=0).reshape(rows_s, 6, D_MODEL), (1, 0, 2))

    ng = norm_g[0]
    w_in_b = w_in[0].astype(BF16)
    w_out_b = w_out[0].astype(BF16)
    lng = ln_v_g[0].reshape(1, A_WIDTH)
    lnb = ln_v_b[0].reshape(1, A_WIDTH)
    grp = np.arange(A_WIDTH) // A_HEAD
    avg = jnp.asarray((grp[:, None] == grp[None, :]).astype(np.float32) / A_HEAD, BF16)
    ws_p = jnp.tril(w_spatial[0]).astype(BF16)
    bs_p = jnp.repeat(b_spatial[0].T, A_HEAD, axis=1)
    w_small = jnp.tril(w_spatial[0][:, :ds, :ds])
    ws_s = jnp.einsum('ab,gts->gatbs', jnp.eye(nb, dtype=F32), w_small).reshape(A_GROUPS, rows_s, rows_s).astype(BF16)
    bs_s = jnp.tile(jnp.repeat(b_spatial[0][:, :ds].T, A_HEAD, axis=1), (nb, 1))

    xp2 = x_prompt.reshape(rows_p, D_MODEL)
    xs2 = x_sample.reshape(rows_s, D_MODEL)
    assert ATTN_TILE == ROW_TILE
    a_p, va_p, q_p, k_p, v_p, kb_p, vt_p = _inproj(
        xp2, mod_p[:, 0:1], mod_p[:, 1:2], ng[0:1], w_in_b, lng, lnb, avg, ws_p, bs_p,
        tiles_per_mod=seq // ROW_TILE, chunk=CHUNK, q_dtype=BF16, q_scale=LOG2E / math.sqrt(B_HEAD), va_rows=CHUNK)
    a_s, va_s, q_s, k_s, v_s, _, _ = _inproj(
        xs2, mod_s[0:1], mod_s[1:2], ng[0:1], w_in_b, lng, lnb, avg, ws_s, bs_s,
        tiles_per_mod=1, chunk=rows_s, q_dtype=F32, q_scale=1.0, va_rows=rows_s)

    ti = np.arange(ATTN_TILE)
    dist_p = np.stack([ti[:, None] - ti[None, :], ti[:, None] - ti[None, :] + ATTN_TILE])
    bias_p = jnp.transpose(_bias_from_dist(rel_bias, dist_p), (3, 0, 2, 4, 1))
    bias_p = bias_p.reshape(B_HEADS, 2, ATTN_TILE, 2 * ATTN_TILE) * LOG2E
    bias_p = jnp.concatenate([bias_p, jnp.zeros_like(bias_p[:, :1])], axis=1)
    b_p = _attn_p(q_p, kb_p, vt_p, bias_p, lam_params[0], subln_g, batch=batch, seq=seq)

    qi = np.arange(ds)
    ki = np.arange(PAGE_SIZE)
    dist_l = PAGE_SIZE + qi[:, None] - ki[None, :]
    kn_i = np.arange(PAGE_SIZE // B_HEADS)
    dist_n = np.where(kn_i[None, :] < ds, qi[:, None] - kn_i[None, :], -1)
    to_rows = lambda t: jnp.transpose(t, (2, 3, 0, 1))
    bias_l = _head_masked(to_rows(_bias_from_dist(rel_bias, dist_l)))
    bias_n = _head_masked(to_rows(_bias_from_dist(rel_bias, dist_n)))
    maskb = _head_masked(jnp.zeros((B_HEADS, 2, ds, PAGE_SIZE), F32))
    pad_keys = lambda t: jnp.pad(t.reshape(nb, ds * B_HEADS, B_VDIM), ((0, 0), (0, PAGE_SIZE - ds * B_HEADS), (0, 0)))
    n_phys = cache_k.shape[0]
    b_s = _attn_s(page_table, q_s, pad_keys(k_s), pad_keys(v_s), maskb, bias_l, bias_n, lam_params[0], subln_g,
                  cache_k.reshape(n_phys * PAGE_SIZE * B_HEADS, B_VDIM),
                  cache_v.reshape(n_phys * PAGE_SIZE * B_HEADS, B_VDIM), ds=ds)

    x1, h2, top_i, top_r, top_w, counts = _outproj(
        a_p, a_s, b_p, b_s, xp2, xs2, mod_p, mod_s, ng, w_out_b, w_router[0], b_router[0])

    bm = EXPERT_BLOCK
    rows = rows_p + rows_s
    n_blocks = rows * TOP_K // bm + N_EXPERTS
    cnt = counts[:, 0].astype(jnp.int32)
    padded = (cnt + bm - 1) // bm * bm
    pad_end = jnp.cumsum(padded)
    pad_start = pad_end - padded
    experts = jnp.arange(N_EXPERTS, dtype=jnp.int32)
    dest = jnp.sum(jnp.where(top_i[..., None] == experts, pad_start, 0), axis=-1) + top_r
    dest = jnp.transpose(dest.reshape(TOP_K, rows // ROW_TILE, ROW_TILE), (1, 0, 2)).reshape(-1)
    blk_start = jnp.arange(n_blocks, dtype=jnp.int32) * bm
    block_e = jnp.minimum(jnp.sum((pad_end[None, :] <= blk_start[:, None]).astype(jnp.int32), axis=1),
                          N_EXPERTS - 1)
    n_used = (pad_end[-1:] // bm).astype(jnp.int32)

    xg = _dispatch(dest, h2, jnp.zeros((n_blocks * bm, D_MODEL), F32))
    yg = _experts(block_e, n_used, xg, w_gu[0], b_gu[0], w_down[0], b_down[0])
    y_p, y_s = _combine(dest, top_w.T, x1, mod_p, mod_s, ng, yg, n_prompt_rows=rows_p, n_sample_rows=rows_s)

    return (y_p.reshape(batch, seq, D_MODEL),
            y_s.reshape(nb, ds, D_MODEL),
            k_p.reshape(batch, seq, 1, B_HEADS, B_VDIM),
            v_p.reshape(batch, seq, 1, B_HEADS, B_VDIM),
            k_s.reshape(nb, ds, 1, B_HEADS, B_VDIM),
            v_s.reshape(nb, ds, 1, B_HEADS, B_VDIM),
            va_p.reshape(batch, CHUNK, 1, A_WIDTH),
            va_s.reshape(nb, ds, 1, A_WIDTH))
```

```python
import functools
import math

import numpy as np
import jax
import jax.numpy as jnp
from jax import lax
from jax.experimental import pallas as pl
from jax.experimental.pallas import tpu as pltpu

F32 = jnp.float32
BF16 = jnp.bfloat16

D_MODEL = 1024
A_WIDTH = 512
A_HEAD = 64
A_GROUPS = 8
CHUNK = 128
B_WIDTH = 512
B_HEAD = 64
B_VDIM = 128
B_HEADS = 4
IN_WIDTH = 2 * A_WIDTH + 3 * B_WIDTH
NUM_BUCKETS = 32
MAX_DISTANCE = 128
PAGE_SIZE = 128
N_EXPERTS = 32
TOP_K = 4
D_EXPERT = 1024
SWIGLU_LIMIT = 7.0
SWIGLU_ALPHA = 1.702
NORM_EPS = 1e-6
LAM_INIT = 0.8 - 0.6 * math.exp(-0.3 * 0)

ROW_TILE = 256
ATTN_TILE = 256
ATTN_KEYS = 128
PAGES_PER_STEP = 16
LOG2E = math.log2(math.e)
LANES = 128
MXU_WIDTH = 256
EXPERT_BLOCK = 256
COMBINE_WINDOW = 72
VMEM_LIMIT = 56 * 1024 * 1024


def _cparams(*sem):
    return pltpu.CompilerParams(dimension_semantics=sem, vmem_limit_bytes=VMEM_LIMIT)


def _rms(x, g):
    return x * lax.rsqrt(jnp.mean(x * x, axis=-1, keepdims=True) + NORM_EPS) * g


def _ada_kernel(c_ref, w_ref, b_ref, o_ref):
    s = jax.nn.silu(c_ref[...]).astype(BF16)
    o_ref[...] = jnp.dot(s, w_ref[...].astype(BF16), preferred_element_type=F32) + b_ref[...]


def _ada(c_all, w_ada, b_ada):
    n = c_all.shape[0]
    tn = 1024
    return pl.pallas_call(
        _ada_kernel,
        grid=(6 * D_MODEL // tn,),
        in_specs=[pl.BlockSpec((n, D_MODEL), lambda j: (0, 0)),
                  pl.BlockSpec((D_MODEL, tn), lambda j: (0, j)),
                  pl.BlockSpec((1, tn), lambda j: (0, j))],
        out_specs=pl.BlockSpec((n, tn), lambda j: (0, j)),
        out_shape=jax.ShapeDtypeStruct((n, 6 * D_MODEL), F32),
        compiler_params=_cparams("arbitrary"),
        name="ada",
    )(c_all, w_ada, b_ada.reshape(1, -1))


def _inproj_kernel(x_ref, sh_ref, sc_ref, ng_ref, w_ref, lng_ref, lnb_ref, avg_ref, ws_ref, bs_ref,
                   a_ref, va_ref, q_ref, k_ref, v_ref, kb_ref, vt_ref, *, chunk, q_scale):
    tm = x_ref.shape[0]
    va_rows = va_ref.shape[0]
    h = (_rms(x_ref[...], ng_ref[...]) * (1.0 + sc_ref[0]) + sh_ref[0]).astype(BF16)

    u = jax.nn.gelu(jnp.dot(h, w_ref[:, 0:A_WIDTH], preferred_element_type=F32))
    gv = jax.nn.gelu(jnp.dot(h, w_ref[:, A_WIDTH:2 * A_WIDTH], preferred_element_type=F32))

    avg = avg_ref[...]

    def group_mean(t):
        hi = t.astype(BF16)
        lo = (t - hi.astype(F32)).astype(BF16)
        return (jnp.dot(hi, avg, preferred_element_type=F32)
                + jnp.dot(lo, avg, preferred_element_type=F32))

    xc = gv - group_mean(gv)
    va = xc * lax.rsqrt(group_mean(xc * xc) + NORM_EPS) * lng_ref[...] + lnb_ref[...]
    va_ref[...] = va[tm - va_rows:, :]

    vab = va.astype(BF16)
    lane = lax.broadcasted_iota(jnp.int32, (chunk, 128), 1)
    for c in range(tm // chunk):
        r0 = c * chunk
        for pair in range(A_GROUPS // 2):
            c0 = pair * 128
            vp = vab[r0:r0 + chunk, c0:c0 + 128]
            lo_half = jnp.where(lane < A_HEAD, vp, jnp.zeros_like(vp))
            hi_half = jnp.where(lane >= A_HEAD, vp, jnp.zeros_like(vp))
            s = (jnp.dot(ws_ref[2 * pair], lo_half, preferred_element_type=F32)
                 + jnp.dot(ws_ref[2 * pair + 1], hi_half, preferred_element_type=F32))
            a = u[r0:r0 + chunk, c0:c0 + 128] * (s + bs_ref[:, c0:c0 + 128])
            a_ref[r0:r0 + chunk, c0:c0 + 128] = a.astype(a_ref.dtype)

    q = jnp.dot(h, w_ref[:, 2 * A_WIDTH:2 * A_WIDTH + B_WIDTH], preferred_element_type=F32)
    q_ref[...] = (q * q_scale).astype(q_ref.dtype)
    k = jnp.dot(h, w_ref[:, 2 * A_WIDTH + B_WIDTH:2 * A_WIDTH + 2 * B_WIDTH], preferred_element_type=F32)
    kb_ref[...] = k.astype(BF16)
    v = jnp.dot(h, w_ref[:, 2 * A_WIDTH + 2 * B_WIDTH:IN_WIDTH], preferred_element_type=F32)
    for hh in range(B_HEADS):
        c0 = hh * B_VDIM
        k_ref[pl.ds(hh, tm, stride=B_HEADS), :] = k[:, c0:c0 + B_VDIM]
        v_ref[pl.ds(hh, tm, stride=B_HEADS), :] = v[:, c0:c0 + B_VDIM]
        vt_ref[0, hh, 0] = v[:, c0:c0 + B_VDIM].T.astype(BF16)


def _inproj(x2, sh, sc, ng, w_in_b, lng, lnb, avg, ws, bs, *, tiles_per_mod, chunk, q_dtype, q_scale, va_rows):
    rows = x2.shape[0]
    tm = ROW_TILE
    n_tiles = rows // tm
    mod_rows = sh.shape[1]
    row = lambda i: (i, 0)
    const2 = lambda i: (0, 0)
    mod_map = lambda i: (i // tiles_per_mod, 0, 0)
    out_w = lambda w, dt: jax.ShapeDtypeStruct((rows, w), dt)
    head_rows = jax.ShapeDtypeStruct((rows * B_HEADS, B_VDIM), F32)
    head_rows_spec = pl.BlockSpec((tm * B_HEADS, B_VDIM), row)
    vt_shape = jax.ShapeDtypeStruct((n_tiles // tiles_per_mod, B_HEADS, tiles_per_mod, B_VDIM, tm), BF16)
    vt_spec = pl.BlockSpec((1, B_HEADS, 1, B_VDIM, tm), lambda i: (i // tiles_per_mod, 0, i % tiles_per_mod, 0, 0))
    return pl.pallas_call(
        functools.partial(_inproj_kernel, chunk=chunk, q_scale=q_scale),
        grid=(rows // tm,),
        in_specs=[pl.BlockSpec((tm, D_MODEL), row),
                  pl.BlockSpec((1, mod_rows, D_MODEL), mod_map),
                  pl.BlockSpec((1, mod_rows, D_MODEL), mod_map),
                  pl.BlockSpec((1, D_MODEL), const2),
                  pl.BlockSpec((D_MODEL, IN_WIDTH), const2),
                  pl.BlockSpec((1, A_WIDTH), const2),
                  pl.BlockSpec((1, A_WIDTH), const2),
                  pl.BlockSpec((A_WIDTH, A_WIDTH), const2),
                  pl.BlockSpec((A_GROUPS, chunk, chunk), lambda i: (0, 0, 0)),
                  pl.BlockSpec((chunk, A_WIDTH), const2)],
        out_specs=[pl.BlockSpec((tm, A_WIDTH), row),
                   pl.BlockSpec((va_rows, A_WIDTH), lambda i: (i // tiles_per_mod, 0)),
                   pl.BlockSpec((tm, B_WIDTH), row), head_rows_spec, head_rows_spec,
                   pl.BlockSpec((tm, B_WIDTH), row), vt_spec],
        out_shape=[out_w(A_WIDTH, BF16),
                   jax.ShapeDtypeStruct((n_tiles // tiles_per_mod * va_rows, A_WIDTH), F32),
                   out_w(B_WIDTH, q_dtype), head_rows, head_rows, out_w(B_WIDTH, BF16), vt_shape],
        compiler_params=_cparams("arbitrary"),
        name="inproj",
    )(x2, sh, sc, ng, w_in_b, lng, lnb, avg, ws, bs)


def _stack_halves(q):
    lane = lax.broadcasted_iota(jnp.int32, q.shape, 1)
    zero = jnp.zeros_like(q)
    return jnp.concatenate([jnp.where(lane < B_HEAD, q, zero), jnp.where(lane >= B_HEAD, q, zero)], axis=0)


def _lambda(lam_ref):
    lp = lam_ref[...]
    return (jnp.exp(jnp.sum(lp[0:1] * lp[1:2], axis=-1, keepdims=True))
            - jnp.exp(jnp.sum(lp[2:3] * lp[3:4], axis=-1, keepdims=True)) + LAM_INIT)


def _diff_finish(acc, l, n, lam, g):
    o = acc[:n] / l[:n] - lam * (acc[n:] / l[n:])
    return _rms(o, g) * (1.0 - LAM_INIT)


def _attn_p_kernel(q_ref, k_ref, vt_ref, bias_ref, lam_ref, g_ref, o_ref,
                   acc_ref, m_ref, l_ref, s_ref, tmax_ref, p_ref, alpha_ref, *, tq, tk):
    qi = pl.program_id(2)
    q = q_ref[...]
    lane = lax.broadcasted_iota(jnp.int32, q.shape, 1)
    zero = jnp.zeros_like(q)
    q_half = (jnp.where(lane < B_HEAD, q, zero), jnp.where(lane >= B_HEAD, q, zero))
    m_ref[...] = jnp.full(m_ref.shape, -jnp.inf, F32)
    l_ref[...] = jnp.zeros(l_ref.shape, F32)
    acc_ref[...] = jnp.zeros(acc_ref.shape, F32)
    p_ref[1] = jnp.zeros(p_ref.shape[1:], BF16)
    alpha_ref[1] = jnp.ones(alpha_ref.shape[1:], F32)

    def scores(j):
        k = k_ref[pl.ds(pl.multiple_of(j * tk, tk), tk), :]
        return [lax.dot_general(k, q_half[c], (((1,), (1,)), ((), ())), preferred_element_type=F32)
                for c in range(2)]

    def keep_scores(j, kind, ss):
        slot = j % 2
        for c in range(2):
            s = ss[c]
            if kind is not None:
                s = s + bias_ref[0, kind, :, c * tq:(c + 1) * tq]
            s_ref[slot, c] = s
            tmax_ref[slot, c] = jnp.max(s, axis=0, keepdims=True)

    def softmax(j):
        slot = j % 2
        for c in range(2):
            m_old = m_ref[c]
            m_new = jnp.maximum(m_old, tmax_ref[slot, c])
            alpha = jnp.exp2(m_old - m_new)
            p = jnp.exp2(s_ref[slot, c] - m_new)
            l_ref[c] = alpha * l_ref[c] + jnp.sum(p, axis=0, keepdims=True)
            m_ref[c] = m_new
            p_ref[slot, c] = p.astype(BF16)
            alpha_ref[slot, c] = alpha

    def pv(j):
        slot = j % 2
        vt = vt_ref[0, 0, jnp.maximum(j, 0)]
        for c in range(2):
            acc_ref[c] = alpha_ref[slot, c] * acc_ref[c] + jnp.dot(vt, p_ref[slot, c], preferred_element_type=F32)

    def far_body(j, carry):
        ss = scores(j + 1)
        pv(j - 1)
        softmax(j)
        keep_scores(j + 1, None, ss)
        return carry

    def near_body(j, carry):
        nxt = jnp.minimum(j + 1, qi)
        ss = scores(nxt)
        pv(j - 1)
        softmax(j)
        keep_scores(nxt, qi - nxt, ss)
        return carry

    keep_scores(0, jnp.minimum(qi, 2), scores(0))
    n_far_fetch = jnp.maximum(qi - 2, 0)
    lax.fori_loop(0, n_far_fetch, far_body, 0)
    lax.fori_loop(n_far_fetch, qi + 1, near_body, 0)
    pv(qi)

    o = acc_ref[0] / l_ref[0] - _lambda(lam_ref) * (acc_ref[1] / l_ref[1])
    y = o * lax.rsqrt(jnp.mean(o * o, axis=0, keepdims=True) + NORM_EPS) * (1.0 - LAM_INIT)
    o_ref[...] = (y.T * g_ref[...]).astype(o_ref.dtype)


def _attn_p(qb, kb, vt, bias, lam_params, subln_g, *, batch, seq):
    tq = tk = ATTN_TILE
    nq = seq // tq
    return pl.pallas_call(
        functools.partial(_attn_p_kernel, tq=tq, tk=tk),
        grid=(batch, B_HEADS, nq),
        in_specs=[pl.BlockSpec((tq, B_VDIM), lambda b, h, i: (b * nq + i, h)),
                  pl.BlockSpec((seq, B_VDIM), lambda b, h, i: (b, h)),
                  pl.BlockSpec((1, 1, seq // tk, B_VDIM, tk), lambda b, h, i: (b, h, 0, 0, 0)),
                  pl.BlockSpec((1, 3, tk, 2 * tq), lambda b, h, i: (h, 0, 0, 0)),
                  pl.BlockSpec((4, B_HEAD), lambda b, h, i: (0, 0)),
                  pl.BlockSpec((1, B_VDIM), lambda b, h, i: (0, 0))],
        out_specs=pl.BlockSpec((tq, B_VDIM), lambda b, h, i: (b * nq + i, h)),
        out_shape=jax.ShapeDtypeStruct((batch * seq, B_WIDTH), BF16),
        scratch_shapes=[pltpu.VMEM((2, B_VDIM, tq), F32),
                        pltpu.VMEM((2, 1, tq), F32),
                        pltpu.VMEM((2, 1, tq), F32),
                        pltpu.VMEM((2, 2, tk, tq), F32),
                        pltpu.VMEM((2, 2, 1, tq), F32),
                        pltpu.VMEM((2, 2, tk, tq), BF16),
                        pltpu.VMEM((2, 2, 1, tq), F32)],
        compiler_params=_cparams("arbitrary", "arbitrary", "arbitrary"),
        name="attn_prompt",
    )(qb, kb, vt, bias, lam_params, subln_g)


def _attn_s_kernel(pt_ref, q_ref, kn_ref, vn_ref, maskb_ref, biasl_ref, biasn_ref, lam_ref, g_ref, *rest,
                   npages, ds):
    del pt_ref
    k_refs = rest[:npages]
    v_refs = rest[npages:2 * npages]
    o_ref = rest[2 * npages]
    acc_ref, m_ref, l_ref = rest[2 * npages + 1:]
    g = pl.program_id(1)
    last = g == pl.num_programs(1) - 1

    @pl.when(g == 0)
    def _():
        m_ref[...] = jnp.full(m_ref.shape, -jnp.inf, F32)
        l_ref[...] = jnp.zeros(l_ref.shape, F32)
        acc_ref[...] = jnp.zeros(acc_ref.shape, F32)

    q = q_ref[...] * (1.0 / math.sqrt(B_HEAD))
    qall = jnp.concatenate([_stack_halves(q[:, h * B_VDIM:(h + 1) * B_VDIM]) for h in range(B_HEADS)],
                           axis=0).astype(BF16)

    def process(k_blocks, v_blocks, biases):
        s = jnp.concatenate(
            [lax.dot_general(qall, kb.astype(BF16), (((1,), (1,)), ((), ())), preferred_element_type=F32) + bb
             for kb, bb in zip(k_blocks, biases)], axis=1)
        m_old = m_ref[...]
        m_new = jnp.maximum(m_old, jnp.max(s, axis=-1, keepdims=True))
        alpha = jnp.exp(m_old - m_new)
        pr = jnp.exp(s - m_new)
        l_ref[...] = alpha * l_ref[...] + jnp.sum(pr, axis=-1, keepdims=True)
        prb = pr.astype(BF16)
        pv = None
        off = 0
        for vb in v_blocks:
            n = vb.shape[0]
            t = jnp.dot(prb[:, off:off + n], vb.astype(BF16), preferred_element_type=F32)
            pv = t if pv is None else pv + t
            off += n
        acc_ref[...] = alpha * acc_ref[...] + pv
        m_ref[...] = m_new

    maskb = maskb_ref[...]
    newest = jnp.where(last, biasl_ref[...], maskb)
    process([k_refs[p][...] for p in range(npages)], [v_refs[p][...] for p in range(npages)],
            [maskb] * (npages - 1) + [newest])

    @pl.when(last)
    def _():
        process([kn_ref[0]], [vn_ref[0]], [biasn_ref[...]])
        lam = _lambda(lam_ref)
        acc = acc_ref[...]
        l = l_ref[...]
        for h in range(B_HEADS):
            r0 = h * 2 * ds
            o_ref[:, h * B_VDIM:(h + 1) * B_VDIM] = _diff_finish(
                acc[r0:r0 + 2 * ds], l[r0:r0 + 2 * ds], ds, lam, g_ref[...])


def _attn_s(page_table, qs, kn, vn, maskb, bias_last, bias_new, lam_params, subln_g, cache_k2, cache_v2, *, ds):
    nb, n_pages = page_table.shape
    npg = PAGES_PER_STEP
    steps = n_pages // npg
    page_rows = PAGE_SIZE * B_HEADS
    nrow = B_HEADS * 2 * ds

    def page_spec(p):
        return pl.BlockSpec((page_rows, B_VDIM), lambda b, g, pt, p=p: (pt[b, g * npg + p], 0))

    const2 = lambda b, g, pt: (0, 0)
    grid_spec = pltpu.PrefetchScalarGridSpec(
        num_scalar_prefetch=1,
        grid=(nb, steps),
        in_specs=[pl.BlockSpec((ds, B_WIDTH), lambda b, g, pt: (b, 0)),
                  pl.BlockSpec((1, PAGE_SIZE, B_VDIM), lambda b, g, pt: (b, 0, 0)),
                  pl.BlockSpec((1, PAGE_SIZE, B_VDIM), lambda b, g, pt: (b, 0, 0)),
                  pl.BlockSpec((nrow, page_rows), const2),
                  pl.BlockSpec((nrow, page_rows), const2),
                  pl.BlockSpec((nrow, PAGE_SIZE), const2),
                  pl.BlockSpec((4, B_HEAD), const2),
                  pl.BlockSpec((1, B_VDIM), const2)]
                 + [page_spec(p) for p in range(npg)] + [page_spec(p) for p in range(npg)],
        out_specs=pl.BlockSpec((ds, B_WIDTH), lambda b, g, pt: (b, 0)),
        scratch_shapes=[pltpu.VMEM((nrow, B_VDIM), F32),
                        pltpu.VMEM((nrow, 1), F32),
                        pltpu.VMEM((nrow, 1), F32)])
    return pl.pallas_call(
        functools.partial(_attn_s_kernel, npages=npg, ds=ds),
        grid_spec=grid_spec,
        out_shape=jax.ShapeDtypeStruct((nb * ds, B_WIDTH), F32),
        compiler_params=_cparams("arbitrary", "arbitrary"),
        name="attn_sample",
    )(page_table, qs, kn, vn, maskb, bias_last, bias_new, lam_params, subln_g,
      *([cache_k2] * npg), *([cache_v2] * npg))


def _outproj_kernel(ap_ref, as_ref, bp_ref, bs_ref, xp_ref, xs_ref, mp_ref, ms_ref, ng_ref, wo_ref,
                    wr_ref, br_ref, x1_ref, h2_ref, ti_ref, tr_ref, tw_ref, cnt_ref, base_ref, run_ref,
                    *, n_prompt_tiles):
    i = pl.program_id(0)
    tm = xp_ref.shape[0]
    is_p = i < n_prompt_tiles

    @pl.when(i == 0)
    def _():
        run_ref[...] = jnp.zeros(run_ref.shape, F32)

    a = jnp.where(is_p, ap_ref[...], as_ref[...])
    b = jnp.where(is_p, bp_ref[...], bs_ref[...].astype(BF16))
    x = jnp.where(is_p, xp_ref[...], xs_ref[...])
    g1 = jnp.where(is_p, mp_ref[0, 2:3, :], ms_ref[2])
    sh2 = jnp.where(is_p, mp_ref[0, 3:4, :], ms_ref[3])
    sc2 = jnp.where(is_p, mp_ref[0, 4:5, :], ms_ref[4])

    mix = (jnp.dot(a, wo_ref[0:A_WIDTH, :], preferred_element_type=F32)
           + jnp.dot(b, wo_ref[A_WIDTH:, :], preferred_element_type=F32))
    x1 = x + g1 * _rms(mix, ng_ref[1:2, :])
    x1_ref[...] = x1
    h2 = _rms(x1, ng_ref[2:3, :]) * (1.0 + sc2) + sh2
    h2_ref[...] = h2

    logits = jnp.dot(h2.astype(BF16), wr_ref[...], preferred_element_type=F32) + br_ref[...]
    work = logits.T[0:N_EXPERTS, :]
    sub = lax.broadcasted_iota(jnp.int32, work.shape, 0)
    vals, idxs = [], []
    for _ in range(TOP_K):
        mx = jnp.max(work, axis=0, keepdims=True)
        ix = jnp.min(jnp.where(work == mx, sub, N_EXPERTS), axis=0, keepdims=True)
        vals.append(mx)
        idxs.append(ix)
        work = jnp.where(sub == ix, -jnp.inf, work)
    exps = [jnp.exp(v - vals[0]) for v in vals]
    den = exps[0] + exps[1] + exps[2] + exps[3]

    sel = jnp.where(work == -jnp.inf, 1.0, 0.0)
    r_i = lax.broadcasted_iota(jnp.int32, (tm, tm), 0)
    c_i = lax.broadcasted_iota(jnp.int32, (tm, tm), 1)
    earlier = jnp.where(r_i < c_i, 1.0, 0.0).astype(BF16)
    base_ref[0] = run_ref[...]
    before = jnp.dot(sel.astype(BF16), earlier, preferred_element_type=F32) + run_ref[...]
    run_ref[...] = run_ref[...] + jnp.sum(sel, axis=1, keepdims=True)
    cnt_ref[...] = run_ref[...]

    ranks = [jnp.sum(jnp.where(sub == ix, before, 0.0), axis=0, keepdims=True) for ix in idxs]
    ti_ref[...] = jnp.concatenate(idxs, axis=0)
    tr_ref[...] = jnp.concatenate(ranks, axis=0).astype(jnp.int32)
    tw_ref[...] = jnp.concatenate([e / den for e in exps], axis=0)


def _outproj(a_p, a_s, b_p, b_s, x_p, x_s, mod_p, mod_s, norm_g, w_out_b, w_router, b_router):
    tm = ROW_TILE
    npt = x_p.shape[0] // tm
    nst = x_s.shape[0] // tm
    assert nst == 1
    n_tiles = npt + nst
    rows = n_tiles * tm
    tiles_per_batch = npt // mod_p.shape[0]
    last_p = npt - 1
    prow = lambda i: (jnp.minimum(i, last_p), 0)
    srow = lambda i: (0, 0)
    row = lambda i: (i, 0)
    col = lambda i: (0, i)
    const2 = lambda i: (0, 0)
    return pl.pallas_call(
        functools.partial(_outproj_kernel, n_prompt_tiles=npt),
        grid=(n_tiles,),
        in_specs=[pl.BlockSpec((tm, A_WIDTH), prow), pl.BlockSpec((tm, A_WIDTH), srow),
                  pl.BlockSpec((tm, B_WIDTH), prow), pl.BlockSpec((tm, B_WIDTH), srow),
                  pl.BlockSpec((tm, D_MODEL), prow), pl.BlockSpec((tm, D_MODEL), srow),
                  pl.BlockSpec((1, 6, D_MODEL), lambda i: (jnp.minimum(i, last_p) // tiles_per_batch, 0, 0)),
                  pl.BlockSpec((6, tm, D_MODEL), lambda i: (0, 0, 0)),
                  pl.BlockSpec((4, D_MODEL), const2),
                  pl.BlockSpec((D_MODEL, D_MODEL), const2),
                  pl.BlockSpec((D_MODEL, LANES), const2),
                  pl.BlockSpec((1, LANES), const2)],
        out_specs=[pl.BlockSpec((tm, D_MODEL), row), pl.BlockSpec((tm, D_MODEL), row),
                   pl.BlockSpec((TOP_K, tm), col), pl.BlockSpec((TOP_K, tm), col),
                   pl.BlockSpec((TOP_K, tm), col), pl.BlockSpec((N_EXPERTS, 1), const2),
                   pl.BlockSpec((1, N_EXPERTS, 1), lambda i: (i, 0, 0))],
        out_shape=[jax.ShapeDtypeStruct((rows, D_MODEL), F32), jax.ShapeDtypeStruct((rows, D_MODEL), F32),
                   jax.ShapeDtypeStruct((TOP_K, rows), jnp.int32), jax.ShapeDtypeStruct((TOP_K, rows), jnp.int32),
                   jax.ShapeDtypeStruct((TOP_K, rows), F32), jax.ShapeDtypeStruct((N_EXPERTS, 1), F32),
                   jax.ShapeDtypeStruct((n_tiles, N_EXPERTS, 1), F32)],
        scratch_shapes=[pltpu.VMEM((N_EXPERTS, 1), F32)],
        compiler_params=_cparams("arbitrary"),
        name="outproj_router",
    )(a_p, a_s, b_p, b_s, x_p, x_s, mod_p, mod_s, norm_g, w_out_b,
      jnp.pad(w_router, ((0, 0), (0, LANES - N_EXPERTS))).astype(BF16),
      jnp.pad(b_router.reshape(1, -1), ((0, 0), (0, LANES - N_EXPERTS))))


def _dispatch_kernel(pe_ref, cnt_ref, dest_ref, h_ref, xg_ref, zero_ref, sem, zsem, *, bm):
    tm = h_ref.shape[0]
    n_blocks = xg_ref.shape[0] // bm

    def zero_block(row0):
        return pltpu.make_async_copy(zero_ref, xg_ref.at[pl.ds(pl.multiple_of(row0, bm), bm), :], zsem)

    @pl.when(pl.program_id(0) == 0)
    def _():
        zero_ref[...] = jnp.zeros(zero_ref.shape, F32)
        first_unused = pe_ref[N_EXPERTS - 1] // bm
        for e in range(N_EXPERTS):
            @pl.when(cnt_ref[e] > 0)
            def _():
                zero_block(pe_ref[e] - bm).start()

        def start_unused(b, carry):
            zero_block(b * bm).start()
            return carry

        lax.fori_loop(first_unused, n_blocks, start_unused, 0)
        for e in range(N_EXPERTS):
            @pl.when(cnt_ref[e] > 0)
            def _():
                zero_block(pe_ref[e] - bm).wait()

        def wait_unused(b, carry):
            zero_block(b * bm).wait()
            return carry

        lax.fori_loop(first_unused, n_blocks, wait_unused, 0)

    def body(t, carry):
        for j in range(TOP_K):
            d = dest_ref[j * tm + t]
            pltpu.make_async_copy(h_ref.at[pl.ds(t, 1), :], xg_ref.at[pl.ds(d, 1), :], sem).start(priority=j % 2)
        return carry

    lax.fori_loop(0, tm, body, 0)
    n = tm * TOP_K
    pltpu.make_async_copy(xg_ref.at[pl.ds(0, n), :], xg_ref.at[pl.ds(0, n), :], sem).wait()


def _dispatch(pad_end, cnt, dest_flat, h2, n_blocks):
    tm = ROW_TILE
    bm = EXPERT_BLOCK
    rows = h2.shape[0]
    grid_spec = pltpu.PrefetchScalarGridSpec(
        num_scalar_prefetch=2,
        grid=(rows // tm,),
        in_specs=[pl.BlockSpec((tm * TOP_K,), lambda i, pe, cn: (i,), memory_space=pltpu.SMEM),
                  pl.BlockSpec((tm, D_MODEL), lambda i, pe, cn: (i, 0))],
        out_specs=pl.BlockSpec(memory_space=pl.ANY),
        scratch_shapes=[pltpu.VMEM((bm, D_MODEL), F32), pltpu.SemaphoreType.DMA(()), pltpu.SemaphoreType.DMA(())])
    return pl.pallas_call(
        functools.partial(_dispatch_kernel, bm=bm),
        grid_spec=grid_spec,
        out_shape=jax.ShapeDtypeStruct((n_blocks * bm, D_MODEL), F32),
        compiler_params=_cparams("arbitrary"),
        name="moe_dispatch",
    )(pad_end, cnt, dest_flat, h2)


def _expert_kernel(be_ref, nu_ref, x_ref, wgu_ref, bgu_ref, wd_ref, bd_ref, y_ref, wgu_b, wd_b):
    i = pl.program_id(0)
    used = i < nu_ref[0]
    fresh = jnp.logical_or(i == 0, be_ref[i] != be_ref[jnp.maximum(i - 1, 0)])

    @pl.when(jnp.logical_and(used, fresh))
    def _():
        wgu_b[...] = wgu_ref[0].astype(BF16)
        wd_b[...] = wd_ref[0].astype(BF16)

    @pl.when(jnp.logical_not(used))
    def _():
        y_ref[...] = jnp.zeros(y_ref.shape, F32)

    @pl.when(used)
    def _():
        x = x_ref[...].astype(BF16)
        gu = jnp.dot(x, wgu_b[...], preferred_element_type=F32) + bgu_ref[0]
        glu = jnp.minimum(gu[:, :D_EXPERT], SWIGLU_LIMIT)
        lin = jnp.clip(gu[:, D_EXPERT:], -SWIGLU_LIMIT, SWIGLU_LIMIT)
        hid = glu * jax.nn.sigmoid(SWIGLU_ALPHA * glu) * (lin + 1.0)
        y_ref[...] = jnp.dot(hid.astype(BF16), wd_b[...], preferred_element_type=F32) + bd_ref[0]


def _experts(block_e, n_used, xg, w_gu_b, b_gu, w_down_b, b_down):
    bm = EXPERT_BLOCK
    n_blocks = xg.shape[0] // bm
    grid_spec = pltpu.PrefetchScalarGridSpec(
        num_scalar_prefetch=2,
        grid=(n_blocks,),
        in_specs=[pl.BlockSpec((bm, D_MODEL), lambda i, be, nu: (jnp.minimum(i, nu[0] - 1), 0)),
                  pl.BlockSpec((1, D_MODEL, 2 * D_EXPERT), lambda i, be, nu: (be[i], 0, 0)),
                  pl.BlockSpec((1, 1, 2 * D_EXPERT), lambda i, be, nu: (be[i], 0, 0)),
                  pl.BlockSpec((1, D_EXPERT, D_MODEL), lambda i, be, nu: (be[i], 0, 0)),
                  pl.BlockSpec((1, 1, D_MODEL), lambda i, be, nu: (be[i], 0, 0))],
        out_specs=pl.BlockSpec((bm, D_MODEL), lambda i, be, nu: (i, 0)),
        scratch_shapes=[pltpu.VMEM((D_MODEL, 2 * D_EXPERT), BF16), pltpu.VMEM((D_EXPERT, D_MODEL), BF16)])
    return pl.pallas_call(
        _expert_kernel,
        grid_spec=grid_spec,
        out_shape=jax.ShapeDtypeStruct(xg.shape, F32),
        compiler_params=_cparams("arbitrary"),
        name="moe_experts",
    )(block_e, n_used, xg, w_gu_b, b_gu.reshape(N_EXPERTS, 1, -1), w_down_b, b_down.reshape(N_EXPERTS, 1, -1))


def _combine_kernel(win0_ref, fast_ref, dest_ref, col_ref, tw_ref, x1_ref, mp_ref, ms_ref, ng_ref, yg_ref,
                    yp_ref, ys_ref, rows_ref, g_ref, f_ref, sem, *, n_prompt_tiles):
    i = pl.program_id(0)
    n_steps = pl.num_programs(0)
    tm = x1_ref.shape[0]
    is_p = i < n_prompt_tiles
    win = COMBINE_WINDOW
    n_win_rows = N_EXPERTS * win
    slot = i % 2
    fast = fast_ref[i] != 0
    tw = tw_ref[...]

    def window_copy(tile, e, buf):
        row0 = pl.multiple_of(win0_ref[tile * N_EXPERTS + e], 8)
        return pltpu.make_async_copy(yg_ref.at[pl.ds(row0, win), :], rows_ref.at[buf, pl.ds(e * win, win), :],
                                     sem.at[buf])

    def fetch_windows(tile, buf):
        for e in range(N_EXPERTS):
            window_copy(tile, e, buf).start(priority=e % 2)

    @pl.when(jnp.logical_and(i == 0, fast))
    def _():
        fetch_windows(0, 0)

    nxt = jnp.minimum(i + 1, n_steps - 1)

    @pl.when(jnp.logical_and(i + 1 < n_steps, fast_ref[nxt] != 0))
    def _():
        fetch_windows(nxt, 1 - slot)

    @pl.when(fast)
    def _():
        pltpu.make_async_copy(yg_ref.at[pl.ds(0, n_win_rows), :], rows_ref.at[slot], sem.at[slot]).wait()
        col = col_ref[...]
        for c in range(n_win_rows // LANES):
            lane = lax.broadcasted_iota(jnp.int32, (tm, LANES), 1) + c * LANES
            g = jnp.zeros((tm, LANES), F32)
            for j in range(TOP_K):
                g = jnp.where(lane == col[:, j:j + 1], tw[:, j:j + 1], g)
            hi = g.astype(BF16)
            g_ref[0:tm, c * LANES:(c + 1) * LANES] = hi
            g_ref[tm:2 * tm, c * LANES:(c + 1) * LANES] = (g - hi.astype(F32)).astype(BF16)
        for n in range(D_MODEL // MXU_WIDTH):
            c0 = n * MXU_WIDTH
            both = jnp.dot(g_ref[...], rows_ref[slot, :, c0:c0 + MXU_WIDTH].astype(BF16),
                           preferred_element_type=F32)
            f_ref[:, c0:c0 + MXU_WIDTH] = both[:tm] + both[tm:]

    @pl.when(jnp.logical_not(fast))
    def _():
        def body(t, carry):
            for j in range(TOP_K):
                d = dest_ref[j * tm + t]
                pltpu.make_async_copy(yg_ref.at[pl.ds(d, 1), :], rows_ref.at[slot, pl.ds(j * tm + t, 1), :],
                                      sem.at[slot]).start(priority=j % 2)
            return carry

        lax.fori_loop(0, tm, body, 0)
        n = tm * TOP_K
        pltpu.make_async_copy(yg_ref.at[pl.ds(0, n), :], rows_ref.at[slot, pl.ds(0, n), :], sem.at[slot]).wait()
        f = tw[:, 0:1] * rows_ref[slot, 0:tm, :]
        for j in range(1, TOP_K):
            f = f + tw[:, j:j + 1] * rows_ref[slot, j * tm:(j + 1) * tm, :]
        f_ref[...] = f

    g2 = jnp.where(is_p, mp_ref[0, 5:6, :], ms_ref[5])
    y = x1_ref[...] + g2 * _rms(f_ref[...], ng_ref[3:4, :])

    @pl.when(is_p)
    def _():
        yp_ref[...] = y

    @pl.when(jnp.logical_not(is_p))
    def _():
        ys_ref[...] = y


def _combine(win0, fast, dest_flat, col, tw, x1, mod_p, mod_s, norm_g, yg, *, n_prompt_rows, n_sample_rows):
    tm = ROW_TILE
    npt = n_prompt_rows // tm
    assert n_sample_rows == tm and N_EXPERTS * COMBINE_WINDOW >= tm * TOP_K
    n_tiles = npt + 1
    tiles_per_batch = npt // mod_p.shape[0]
    last_p = npt - 1
    grid_spec = pltpu.PrefetchScalarGridSpec(
        num_scalar_prefetch=2,
        grid=(n_tiles,),
        in_specs=[pl.BlockSpec((tm * TOP_K,), lambda i, w0, fa: (i,), memory_space=pltpu.SMEM),
                  pl.BlockSpec((tm, TOP_K), lambda i, w0, fa: (i, 0)),
                  pl.BlockSpec((tm, TOP_K), lambda i, w0, fa: (i, 0)),
                  pl.BlockSpec((tm, D_MODEL), lambda i, w0, fa: (i, 0)),
                  pl.BlockSpec((1, 6, D_MODEL),
                               lambda i, w0, fa: (jnp.minimum(i, last_p) // tiles_per_batch, 0, 0)),
                  pl.BlockSpec((6, tm, D_MODEL), lambda i, w0, fa: (0, 0, 0)),
                  pl.BlockSpec((4, D_MODEL), lambda i, w0, fa: (0, 0)),
                  pl.BlockSpec(memory_space=pl.ANY)],
        out_specs=[pl.BlockSpec((tm, D_MODEL), lambda i, w0, fa: (jnp.minimum(i, last_p), 0)),
                   pl.BlockSpec((tm, D_MODEL), lambda i, w0, fa: (0, 0))],
        scratch_shapes=[pltpu.VMEM((2, N_EXPERTS * COMBINE_WINDOW, D_MODEL), F32),
                        pltpu.VMEM((2 * tm, N_EXPERTS * COMBINE_WINDOW), BF16),
                        pltpu.VMEM((tm, D_MODEL), F32),
                        pltpu.SemaphoreType.DMA((2,))])
    return pl.pallas_call(
        functools.partial(_combine_kernel, n_prompt_tiles=npt),
        grid_spec=grid_spec,
        out_shape=[jax.ShapeDtypeStruct((n_prompt_rows, D_MODEL), F32),
                   jax.ShapeDtypeStruct((n_sample_rows, D_MODEL), F32)],
        compiler_params=_cparams("arbitrary"),
        name="moe_combine",
    )(win0, fast, dest_flat, col, tw, x1, mod_p, mod_s, norm_g, yg)


def _t5_bucket_np(dist):
    n = np.maximum(dist, 0)
    max_exact = NUM_BUCKETS // 2
    nf = np.maximum(n, 1).astype(np.float64)
    large = max_exact + (np.log(nf / max_exact) / math.log(MAX_DISTANCE / max_exact)
                         * (NUM_BUCKETS - max_exact)).astype(np.int32)
    large = np.minimum(large, NUM_BUCKETS - 1)
    return np.where(n < max_exact, n, large).astype(np.int32)


def _bias_from_dist(rel_bias, dist):
    onehot = jax.nn.one_hot(_t5_bucket_np(dist).reshape(-1), NUM_BUCKETS, dtype=F32)
    shifted = (rel_bias - rel_bias[NUM_BUCKETS - 1]).reshape(NUM_BUCKETS, -1)
    out = jnp.dot(onehot, shifted, precision=lax.Precision.HIGHEST).reshape(dist.shape + rel_bias.shape[1:])
    return jnp.where(jnp.asarray(dist >= 0)[..., None, None], out, -jnp.inf).astype(F32)


def _head_masked(t):
    h = t.shape[0]
    same = np.eye(h, dtype=bool)[:, None, None, None, :]
    full = jnp.where(same, t[..., None], -jnp.inf)
    return full.reshape(h * t.shape[1] * t.shape[2], t.shape[3] * h)


def kernel(x_prompt, x_sample, cache_k, cache_v, page_table, c_prompt, c_sample, w_ada, b_ada, norm_g, w_in,
           w_out, ln_v_g, ln_v_b, w_spatial, b_spatial, lam_params, subln_g, rel_bias, w_router, b_router,
           w_gu, b_gu, w_down, b_down):
    batch, seq, d = x_prompt.shape
    nb, ds = x_sample.shape[:2]
    n_pages = page_table.shape[1]
    past = n_pages * PAGE_SIZE
    rows_p = batch * seq
    rows_s = nb * ds
    assert rows_s == ROW_TILE and d == D_MODEL

    mod = _ada(jnp.concatenate([c_prompt, c_sample], axis=0), w_ada[0], b_ada[0])
    mod_p = mod[:batch].reshape(batch, 6, D_MODEL)
    mod_s = jnp.transpose(jnp.repeat(mod[batch:], ds, axis=0).reshape(rows_s, 6, D_MODEL), (1, 0, 2))

    ng = norm_g[0]
    w_in_b = w_in[0].astype(BF16)
    w_out_b = w_out[0].astype(BF16)
    lng = ln_v_g[0].reshape(1, A_WIDTH)
    lnb = ln_v_b[0].reshape(1, A_WIDTH)
    grp = np.arange(A_WIDTH) // A_HEAD
    avg = jnp.asarray((grp[:, None] == grp[None, :]).astype(np.float32) / A_HEAD, BF16)
    ws_p = jnp.tril(w_spatial[0]).astype(BF16)
    bs_p = jnp.repeat(b_spatial[0].T, A_HEAD, axis=1)
    w_small = jnp.tril(w_spatial[0][:, :ds, :ds])
    ws_s = jnp.einsum('ab,gts->gatbs', jnp.eye(nb, dtype=F32), w_small).reshape(A_GROUPS, rows_s, rows_s).astype(BF16)
    bs_s = jnp.tile(jnp.repeat(b_spatial[0][:, :ds].T, A_HEAD, axis=1), (nb, 1))

    xp2 = x_prompt.reshape(rows_p, D_MODEL)
    xs2 = x_sample.reshape(rows_s, D_MODEL)
    assert ATTN_TILE == ROW_TILE
    a_p, va_p, q_p, k_p, v_p, kb_p, vt_p = _inproj(
        xp2, mod_p[:, 0:1], mod_p[:, 1:2], ng[0:1], w_in_b, lng, lnb, avg, ws_p, bs_p,
        tiles_per_mod=seq // ROW_TILE, chunk=CHUNK, q_dtype=BF16, q_scale=LOG2E / math.sqrt(B_HEAD), va_rows=CHUNK)
    a_s, va_s, q_s, k_s, v_s, _, _ = _inproj(
        xs2, mod_s[0:1], mod_s[1:2], ng[0:1], w_in_b, lng, lnb, avg, ws_s, bs_s,
        tiles_per_mod=1, chunk=rows_s, q_dtype=F32, q_scale=1.0, va_rows=rows_s)

    ti = np.arange(ATTN_TILE)
    dist_p = np.stack([ti[:, None] - ti[None, :], ti[:, None] - ti[None, :] + ATTN_TILE])
    bias_p = jnp.transpose(_bias_from_dist(rel_bias, dist_p), (3, 0, 2, 4, 1))
    bias_p = bias_p.reshape(B_HEADS, 2, ATTN_TILE, 2 * ATTN_TILE) * LOG2E
    bias_p = jnp.concatenate([bias_p, jnp.zeros_like(bias_p[:, :1])], axis=1)
    b_p = _attn_p(q_p, kb_p, vt_p, bias_p, lam_params[0], subln_g, batch=batch, seq=seq)

    qi = np.arange(ds)
    ki = np.arange(PAGE_SIZE)
    dist_l = PAGE_SIZE + qi[:, None] - ki[None, :]
    kn_i = np.arange(PAGE_SIZE // B_HEADS)
    dist_n = np.where(kn_i[None, :] < ds, qi[:, None] - kn_i[None, :], -1)
    to_rows = lambda t: jnp.transpose(t, (2, 3, 0, 1))
    bias_l = _head_masked(to_rows(_bias_from_dist(rel_bias, dist_l)))
    bias_n = _head_masked(to_rows(_bias_from_dist(rel_bias, dist_n)))
    maskb = _head_masked(jnp.zeros((B_HEADS, 2, ds, PAGE_SIZE), F32))
    pad_keys = lambda t: jnp.pad(t.reshape(nb, ds * B_HEADS, B_VDIM), ((0, 0), (0, PAGE_SIZE - ds * B_HEADS), (0, 0)))
    n_phys = cache_k.shape[0]
    b_s = _attn_s(page_table, q_s, pad_keys(k_s), pad_keys(v_s), maskb, bias_l, bias_n, lam_params[0], subln_g,
                  cache_k.reshape(n_phys * PAGE_SIZE * B_HEADS, B_VDIM),
                  cache_v.reshape(n_phys * PAGE_SIZE * B_HEADS, B_VDIM), ds=ds)

    x1, h2, top_i, top_r, top_w, counts, tile_base = _outproj(
        a_p, a_s, b_p, b_s, xp2, xs2, mod_p, mod_s, ng, w_out_b, w_router[0], b_router[0])

    bm = EXPERT_BLOCK
    rows = rows_p + rows_s
    n_blocks = rows * TOP_K // bm + N_EXPERTS
    cnt = counts[:, 0].astype(jnp.int32)
    padded = (cnt + bm - 1) // bm * bm
    pad_end = jnp.cumsum(padded)
    pad_start = pad_end - padded
    experts = jnp.arange(N_EXPERTS, dtype=jnp.int32)
    chosen = top_i[..., None] == experts
    dest = jnp.sum(jnp.where(chosen, pad_start, 0), axis=-1) + top_r
    n_tiles = rows // ROW_TILE
    base = tile_base[:, :, 0].astype(jnp.int32)
    first = pad_start[None, :] + base
    win0 = jnp.minimum(first // 8 * 8, n_blocks * bm - COMBINE_WINDOW)
    sent = jnp.concatenate([base[1:], cnt[None, :]], axis=0) - base
    fast = jnp.all(first - win0 + sent <= COMBINE_WINDOW, axis=1).astype(jnp.int32)
    win0_tok = jnp.repeat(win0, ROW_TILE, axis=0)
    col = top_i * COMBINE_WINDOW + dest - jnp.sum(jnp.where(chosen, win0_tok, 0), axis=-1)
    dest = jnp.transpose(dest.reshape(TOP_K, n_tiles, ROW_TILE), (1, 0, 2)).reshape(-1)
    blk_start = jnp.arange(n_blocks, dtype=jnp.int32) * bm
    block_e = jnp.minimum(jnp.sum((pad_end[None, :] <= blk_start[:, None]).astype(jnp.int32), axis=1),
                          N_EXPERTS - 1)
    n_used = (pad_end[-1:] // bm).astype(jnp.int32)

    xg = _dispatch(pad_end.astype(jnp.int32), cnt, dest, h2, n_blocks)
    yg = _experts(block_e, n_used, xg, w_gu[0], b_gu[0], w_down[0], b_down[0])
    y_p, y_s = _combine(win0.reshape(-1), fast, dest, col.T, top_w.T, x1, mod_p, mod_s, ng, yg,
                        n_prompt_rows=rows_p, n_sample_rows=rows_s)

    return (y_p.reshape(batch, seq, D_MODEL),
            y_s.reshape(nb, ds, D_MODEL),
            k_p.reshape(batch, seq, 1, B_HEADS, B_VDIM),
            v_p.reshape(batch, seq, 1, B_HEADS, B_VDIM),
            k_s.reshape(nb, ds, 1, B_HEADS, B_VDIM),
            v_s.reshape(nb, ds, 1, B_HEADS, B_VDIM),
            va_p.reshape(batch, CHUNK, 1, A_WIDTH),
            va_s.reshape(nb, ds, 1, A_WIDTH))
```

```python
import functools
import math

import numpy as np
import jax
import jax.numpy as jnp
from jax import lax
from jax.experimental import pallas as pl
from jax.experimental.pallas import tpu as pltpu

F32 = jnp.float32
BF16 = jnp.bfloat16

D_MODEL = 1024
A_WIDTH = 512
A_HEAD = 64
A_GROUPS = 8
CHUNK = 128
B_WIDTH = 512
B_HEAD = 64
B_VDIM = 128
B_HEADS = 4
IN_WIDTH = 2 * A_WIDTH + 3 * B_WIDTH
NUM_BUCKETS = 32
MAX_DISTANCE = 128
PAGE_SIZE = 128
N_EXPERTS = 32
TOP_K = 4
D_EXPERT = 1024
SWIGLU_LIMIT = 7.0
SWIGLU_ALPHA = 1.702
NORM_EPS = 1e-6
LAM_INIT = 0.8 - 0.6 * math.exp(-0.3 * 0)

ROW_TILE = 256
ATTN_TILE = 256
ATTN_KEYS = 128
PAGES_PER_STEP = 16
LOG2E = math.log2(math.e)
LANES = 128
MXU_WIDTH = 256
ROW_WIDTH = D_MODEL + LANES
EXPERT_BLOCK = 256
COMBINE_WINDOW = 72
VMEM_LIMIT = 56 * 1024 * 1024


def _cparams(*sem):
    return pltpu.CompilerParams(dimension_semantics=sem, vmem_limit_bytes=VMEM_LIMIT)


def _rms(x, g):
    return x * lax.rsqrt(jnp.mean(x * x, axis=-1, keepdims=True) + NORM_EPS) * g


def _ada_kernel(c_ref, w_ref, b_ref, o_ref):
    s = jax.nn.silu(c_ref[...]).astype(BF16)
    o_ref[...] = jnp.dot(s, w_ref[...].astype(BF16), preferred_element_type=F32) + b_ref[...]


def _ada(c_all, w_ada, b_ada):
    n = c_all.shape[0]
    tn = 1024
    return pl.pallas_call(
        _ada_kernel,
        grid=(6 * D_MODEL // tn,),
        in_specs=[pl.BlockSpec((n, D_MODEL), lambda j: (0, 0)),
                  pl.BlockSpec((D_MODEL, tn), lambda j: (0, j)),
                  pl.BlockSpec((1, tn), lambda j: (0, j))],
        out_specs=pl.BlockSpec((n, tn), lambda j: (0, j)),
        out_shape=jax.ShapeDtypeStruct((n, 6 * D_MODEL), F32),
        compiler_params=_cparams("arbitrary"),
        name="ada",
    )(c_all, w_ada, b_ada.reshape(1, -1))


def _inproj_kernel(x_ref, sh_ref, sc_ref, ng_ref, w_ref, lng_ref, lnb_ref, avg_ref, ws_ref, bs_ref,
                   a_ref, va_ref, q_ref, k_ref, v_ref, kb_ref, vt_ref, *, chunk, q_scale):
    tm = x_ref.shape[0]
    va_rows = va_ref.shape[0]
    h = (_rms(x_ref[...], ng_ref[...]) * (1.0 + sc_ref[0]) + sh_ref[0]).astype(BF16)

    u = jax.nn.gelu(jnp.dot(h, w_ref[:, 0:A_WIDTH], preferred_element_type=F32))
    gv = jax.nn.gelu(jnp.dot(h, w_ref[:, A_WIDTH:2 * A_WIDTH], preferred_element_type=F32))

    avg = avg_ref[...]

    def group_mean(t):
        hi = t.astype(BF16)
        lo = (t - hi.astype(F32)).astype(BF16)
        return (jnp.dot(hi, avg, preferred_element_type=F32)
                + jnp.dot(lo, avg, preferred_element_type=F32))

    xc = gv - group_mean(gv)
    va = xc * lax.rsqrt(group_mean(xc * xc) + NORM_EPS) * lng_ref[...] + lnb_ref[...]
    va_ref[...] = va[tm - va_rows:, :]

    vab = va.astype(BF16)
    lane = lax.broadcasted_iota(jnp.int32, (chunk, 128), 1)
    for c in range(tm // chunk):
        r0 = c * chunk
        for pair in range(A_GROUPS // 2):
            c0 = pair * 128
            vp = vab[r0:r0 + chunk, c0:c0 + 128]
            lo_half = jnp.where(lane < A_HEAD, vp, jnp.zeros_like(vp))
            hi_half = jnp.where(lane >= A_HEAD, vp, jnp.zeros_like(vp))
            s = (jnp.dot(ws_ref[2 * pair], lo_half, preferred_element_type=F32)
                 + jnp.dot(ws_ref[2 * pair + 1], hi_half, preferred_element_type=F32))
            a = u[r0:r0 + chunk, c0:c0 + 128] * (s + bs_ref[:, c0:c0 + 128])
            a_ref[r0:r0 + chunk, c0:c0 + 128] = a.astype(a_ref.dtype)

    q = jnp.dot(h, w_ref[:, 2 * A_WIDTH:2 * A_WIDTH + B_WIDTH], preferred_element_type=F32)
    q_ref[...] = (q * q_scale).astype(q_ref.dtype)
    k = jnp.dot(h, w_ref[:, 2 * A_WIDTH + B_WIDTH:2 * A_WIDTH + 2 * B_WIDTH], preferred_element_type=F32)
    kb_ref[...] = k.astype(BF16)
    v = jnp.dot(h, w_ref[:, 2 * A_WIDTH + 2 * B_WIDTH:IN_WIDTH], preferred_element_type=F32)
    for hh in range(B_HEADS):
        c0 = hh * B_VDIM
        k_ref[pl.ds(hh, tm, stride=B_HEADS), :] = k[:, c0:c0 + B_VDIM]
        v_ref[pl.ds(hh, tm, stride=B_HEADS), :] = v[:, c0:c0 + B_VDIM]
        vt_ref[0, hh, 0] = v[:, c0:c0 + B_VDIM].T.astype(BF16)


def _inproj(x2, sh, sc, ng, w_in_b, lng, lnb, avg, ws, bs, *, tiles_per_mod, chunk, q_dtype, q_scale, va_rows):
    rows = x2.shape[0]
    tm = ROW_TILE
    n_tiles = rows // tm
    mod_rows = sh.shape[1]
    row = lambda i: (i, 0)
    const2 = lambda i: (0, 0)
    mod_map = lambda i: (i // tiles_per_mod, 0, 0)
    out_w = lambda w, dt: jax.ShapeDtypeStruct((rows, w), dt)
    head_rows = jax.ShapeDtypeStruct((rows * B_HEADS, B_VDIM), F32)
    head_rows_spec = pl.BlockSpec((tm * B_HEADS, B_VDIM), row)
    vt_shape = jax.ShapeDtypeStruct((n_tiles // tiles_per_mod, B_HEADS, tiles_per_mod, B_VDIM, tm), BF16)
    vt_spec = pl.BlockSpec((1, B_HEADS, 1, B_VDIM, tm), lambda i: (i // tiles_per_mod, 0, i % tiles_per_mod, 0, 0))
    return pl.pallas_call(
        functools.partial(_inproj_kernel, chunk=chunk, q_scale=q_scale),
        grid=(rows // tm,),
        in_specs=[pl.BlockSpec((tm, D_MODEL), row),
                  pl.BlockSpec((1, mod_rows, D_MODEL), mod_map),
                  pl.BlockSpec((1, mod_rows, D_MODEL), mod_map),
                  pl.BlockSpec((1, D_MODEL), const2),
                  pl.BlockSpec((D_MODEL, IN_WIDTH), const2),
                  pl.BlockSpec((1, A_WIDTH), const2),
                  pl.BlockSpec((1, A_WIDTH), const2),
                  pl.BlockSpec((A_WIDTH, A_WIDTH), const2),
                  pl.BlockSpec((A_GROUPS, chunk, chunk), lambda i: (0, 0, 0)),
                  pl.BlockSpec((chunk, A_WIDTH), const2)],
        out_specs=[pl.BlockSpec((tm, A_WIDTH), row),
                   pl.BlockSpec((va_rows, A_WIDTH), lambda i: (i // tiles_per_mod, 0)),
                   pl.BlockSpec((tm, B_WIDTH), row), head_rows_spec, head_rows_spec,
                   pl.BlockSpec((tm, B_WIDTH), row), vt_spec],
        out_shape=[out_w(A_WIDTH, BF16),
                   jax.ShapeDtypeStruct((n_tiles // tiles_per_mod * va_rows, A_WIDTH), F32),
                   out_w(B_WIDTH, q_dtype), head_rows, head_rows, out_w(B_WIDTH, BF16), vt_shape],
        compiler_params=_cparams("arbitrary"),
        name="inproj",
    )(x2, sh, sc, ng, w_in_b, lng, lnb, avg, ws, bs)


def _stack_halves(q):
    lane = lax.broadcasted_iota(jnp.int32, q.shape, 1)
    zero = jnp.zeros_like(q)
    return jnp.concatenate([jnp.where(lane < B_HEAD, q, zero), jnp.where(lane >= B_HEAD, q, zero)], axis=0)


def _lambda(lam_ref):
    lp = lam_ref[...]
    return (jnp.exp(jnp.sum(lp[0:1] * lp[1:2], axis=-1, keepdims=True))
            - jnp.exp(jnp.sum(lp[2:3] * lp[3:4], axis=-1, keepdims=True)) + LAM_INIT)


def _diff_finish(acc, l, n, lam, g):
    o = acc[:n] / l[:n] - lam * (acc[n:] / l[n:])
    return _rms(o, g) * (1.0 - LAM_INIT)


def _attn_p_kernel(q_ref, k_ref, vt_ref, bias_ref, lam_ref, g_ref, o_ref,
                   acc_ref, m_ref, l_ref, s_ref, tmax_ref, p_ref, alpha_ref, *, tq, tk):
    qi = pl.program_id(2)
    q = q_ref[...]
    lane = lax.broadcasted_iota(jnp.int32, q.shape, 1)
    zero = jnp.zeros_like(q)
    q_half = (jnp.where(lane < B_HEAD, q, zero), jnp.where(lane >= B_HEAD, q, zero))
    m_ref[...] = jnp.full(m_ref.shape, -jnp.inf, F32)
    l_ref[...] = jnp.zeros(l_ref.shape, F32)
    acc_ref[...] = jnp.zeros(acc_ref.shape, F32)
    p_ref[1] = jnp.zeros(p_ref.shape[1:], BF16)
    alpha_ref[1] = jnp.ones(alpha_ref.shape[1:], F32)

    def scores(j):
        k = k_ref[pl.ds(pl.multiple_of(j * tk, tk), tk), :]
        return [lax.dot_general(k, q_half[c], (((1,), (1,)), ((), ())), preferred_element_type=F32)
                for c in range(2)]

    def keep_scores(j, kind, ss):
        slot = j % 2
        for c in range(2):
            s = ss[c]
            if kind is not None:
                s = s + bias_ref[0, kind, :, c * tq:(c + 1) * tq]
            s_ref[slot, c] = s
            tmax_ref[slot, c] = jnp.max(s, axis=0, keepdims=True)

    def softmax(j):
        slot = j % 2
        for c in range(2):
            m_old = m_ref[c]
            m_new = jnp.maximum(m_old, tmax_ref[slot, c])
            alpha = jnp.exp2(m_old - m_new)
            p = jnp.exp2(s_ref[slot, c] - m_new)
            l_ref[c] = alpha * l_ref[c] + jnp.sum(p, axis=0, keepdims=True)
            m_ref[c] = m_new
            p_ref[slot, c] = p.astype(BF16)
            alpha_ref[slot, c] = alpha

    def pv(j):
        slot = j % 2
        vt = vt_ref[0, 0, jnp.maximum(j, 0)]
        for c in range(2):
            acc_ref[c] = alpha_ref[slot, c] * acc_ref[c] + jnp.dot(vt, p_ref[slot, c], preferred_element_type=F32)

    def far_body(j, carry):
        ss = scores(j + 1)
        pv(j - 1)
        softmax(j)
        keep_scores(j + 1, None, ss)
        return carry

    def near_body(j, carry):
        nxt = jnp.minimum(j + 1, qi)
        ss = scores(nxt)
        pv(j - 1)
        softmax(j)
        keep_scores(nxt, qi - nxt, ss)
        return carry

    keep_scores(0, jnp.minimum(qi, 2), scores(0))
    n_far_fetch = jnp.maximum(qi - 2, 0)
    lax.fori_loop(0, n_far_fetch, far_body, 0)
    lax.fori_loop(n_far_fetch, qi + 1, near_body, 0)
    pv(qi)

    o = acc_ref[0] / l_ref[0] - _lambda(lam_ref) * (acc_ref[1] / l_ref[1])
    y = o * lax.rsqrt(jnp.mean(o * o, axis=0, keepdims=True) + NORM_EPS) * (1.0 - LAM_INIT)
    o_ref[...] = (y.T * g_ref[...]).astype(o_ref.dtype)


def _attn_p(qb, kb, vt, bias, lam_params, subln_g, *, batch, seq):
    tq = tk = ATTN_TILE
    nq = seq // tq
    return pl.pallas_call(
        functools.partial(_attn_p_kernel, tq=tq, tk=tk),
        grid=(batch, B_HEADS, nq),
        in_specs=[pl.BlockSpec((tq, B_VDIM), lambda b, h, i: (b * nq + i, h)),
                  pl.BlockSpec((seq, B_VDIM), lambda b, h, i: (b, h)),
                  pl.BlockSpec((1, 1, seq // tk, B_VDIM, tk), lambda b, h, i: (b, h, 0, 0, 0)),
                  pl.BlockSpec((1, 3, tk, 2 * tq), lambda b, h, i: (h, 0, 0, 0)),
                  pl.BlockSpec((4, B_HEAD), lambda b, h, i: (0, 0)),
                  pl.BlockSpec((1, B_VDIM), lambda b, h, i: (0, 0))],
        out_specs=pl.BlockSpec((tq, B_VDIM), lambda b, h, i: (b * nq + i, h)),
        out_shape=jax.ShapeDtypeStruct((batch * seq, B_WIDTH), BF16),
        scratch_shapes=[pltpu.VMEM((2, B_VDIM, tq), F32),
                        pltpu.VMEM((2, 1, tq), F32),
                        pltpu.VMEM((2, 1, tq), F32),
                        pltpu.VMEM((2, 2, tk, tq), F32),
                        pltpu.VMEM((2, 2, 1, tq), F32),
                        pltpu.VMEM((2, 2, tk, tq), BF16),
                        pltpu.VMEM((2, 2, 1, tq), F32)],
        compiler_params=_cparams("arbitrary", "arbitrary", "arbitrary"),
        name="attn_prompt",
    )(qb, kb, vt, bias, lam_params, subln_g)


def _attn_s_kernel(pt_ref, q_ref, kn_ref, vn_ref, maskb_ref, biasl_ref, biasn_ref, lam_ref, g_ref, *rest,
                   npages, ds):
    del pt_ref
    k_refs = rest[:npages]
    v_refs = rest[npages:2 * npages]
    o_ref = rest[2 * npages]
    acc_ref, m_ref, l_ref = rest[2 * npages + 1:]
    g = pl.program_id(1)
    last = g == pl.num_programs(1) - 1

    @pl.when(g == 0)
    def _():
        m_ref[...] = jnp.full(m_ref.shape, -jnp.inf, F32)
        l_ref[...] = jnp.zeros(l_ref.shape, F32)
        acc_ref[...] = jnp.zeros(acc_ref.shape, F32)

    q = q_ref[...] * (1.0 / math.sqrt(B_HEAD))
    qall = jnp.concatenate([_stack_halves(q[:, h * B_VDIM:(h + 1) * B_VDIM]) for h in range(B_HEADS)],
                           axis=0).astype(BF16)

    def process(k_blocks, v_blocks, biases):
        s = jnp.concatenate(
            [lax.dot_general(qall, kb.astype(BF16), (((1,), (1,)), ((), ())), preferred_element_type=F32) + bb
             for kb, bb in zip(k_blocks, biases)], axis=1)
        m_old = m_ref[...]
        m_new = jnp.maximum(m_old, jnp.max(s, axis=-1, keepdims=True))
        alpha = jnp.exp(m_old - m_new)
        pr = jnp.exp(s - m_new)
        l_ref[...] = alpha * l_ref[...] + jnp.sum(pr, axis=-1, keepdims=True)
        prb = pr.astype(BF16)
        pv = None
        off = 0
        for vb in v_blocks:
            n = vb.shape[0]
            t = jnp.dot(prb[:, off:off + n], vb.astype(BF16), preferred_element_type=F32)
            pv = t if pv is None else pv + t
            off += n
        acc_ref[...] = alpha * acc_ref[...] + pv
        m_ref[...] = m_new

    maskb = maskb_ref[...]
    newest = jnp.where(last, biasl_ref[...], maskb)
    process([k_refs[p][...] for p in range(npages)], [v_refs[p][...] for p in range(npages)],
            [maskb] * (npages - 1) + [newest])

    @pl.when(last)
    def _():
        process([kn_ref[0]], [vn_ref[0]], [biasn_ref[...]])
        lam = _lambda(lam_ref)
        acc = acc_ref[...]
        l = l_ref[...]
        for h in range(B_HEADS):
            r0 = h * 2 * ds
            o_ref[:, h * B_VDIM:(h + 1) * B_VDIM] = _diff_finish(
                acc[r0:r0 + 2 * ds], l[r0:r0 + 2 * ds], ds, lam, g_ref[...])


def _attn_s(page_table, qs, kn, vn, maskb, bias_last, bias_new, lam_params, subln_g, cache_k2, cache_v2, *, ds):
    nb, n_pages = page_table.shape
    npg = PAGES_PER_STEP
    steps = n_pages // npg
    page_rows = PAGE_SIZE * B_HEADS
    nrow = B_HEADS * 2 * ds

    def page_spec(p):
        return pl.BlockSpec((page_rows, B_VDIM), lambda b, g, pt, p=p: (pt[b, g * npg + p], 0))

    const2 = lambda b, g, pt: (0, 0)
    grid_spec = pltpu.PrefetchScalarGridSpec(
        num_scalar_prefetch=1,
        grid=(nb, steps),
        in_specs=[pl.BlockSpec((ds, B_WIDTH), lambda b, g, pt: (b, 0)),
                  pl.BlockSpec((1, PAGE_SIZE, B_VDIM), lambda b, g, pt: (b, 0, 0)),
                  pl.BlockSpec((1, PAGE_SIZE, B_VDIM), lambda b, g, pt: (b, 0, 0)),
                  pl.BlockSpec((nrow, page_rows), const2),
                  pl.BlockSpec((nrow, page_rows), const2),
                  pl.BlockSpec((nrow, PAGE_SIZE), const2),
                  pl.BlockSpec((4, B_HEAD), const2),
                  pl.BlockSpec((1, B_VDIM), const2)]
                 + [page_spec(p) for p in range(npg)] + [page_spec(p) for p in range(npg)],
        out_specs=pl.BlockSpec((ds, B_WIDTH), lambda b, g, pt: (b, 0)),
        scratch_shapes=[pltpu.VMEM((nrow, B_VDIM), F32),
                        pltpu.VMEM((nrow, 1), F32),
                        pltpu.VMEM((nrow, 1), F32)])
    return pl.pallas_call(
        functools.partial(_attn_s_kernel, npages=npg, ds=ds),
        grid_spec=grid_spec,
        out_shape=jax.ShapeDtypeStruct((nb * ds, B_WIDTH), F32),
        compiler_params=_cparams("arbitrary", "arbitrary"),
        name="attn_sample",
    )(page_table, qs, kn, vn, maskb, bias_last, bias_new, lam_params, subln_g,
      *([cache_k2] * npg), *([cache_v2] * npg))


def _outproj_kernel(ap_ref, as_ref, bp_ref, bs_ref, xp_ref, xs_ref, mp_ref, ms_ref, ng_ref, wo_ref,
                    wr_ref, br_ref, x1_ref, h2_ref, ti_ref, tr_ref, cnt_ref, base_ref, run_ref,
                    *, n_prompt_tiles):
    i = pl.program_id(0)
    tm = xp_ref.shape[0]
    is_p = i < n_prompt_tiles

    @pl.when(i == 0)
    def _():
        run_ref[...] = jnp.zeros(run_ref.shape, F32)

    a = jnp.where(is_p, ap_ref[...], as_ref[...])
    b = jnp.where(is_p, bp_ref[...], bs_ref[...].astype(BF16))
    x = jnp.where(is_p, xp_ref[...], xs_ref[...])
    g1 = jnp.where(is_p, mp_ref[0, 2:3, :], ms_ref[2])
    sh2 = jnp.where(is_p, mp_ref[0, 3:4, :], ms_ref[3])
    sc2 = jnp.where(is_p, mp_ref[0, 4:5, :], ms_ref[4])

    mix = (jnp.dot(a, wo_ref[0:A_WIDTH, :], preferred_element_type=F32)
           + jnp.dot(b, wo_ref[A_WIDTH:, :], preferred_element_type=F32))
    x1 = x + g1 * _rms(mix, ng_ref[1:2, :])
    x1_ref[...] = x1
    h2 = _rms(x1, ng_ref[2:3, :]) * (1.0 + sc2) + sh2
    h2_ref[:, 0:D_MODEL] = h2

    logits = jnp.dot(h2.astype(BF16), wr_ref[...], preferred_element_type=F32) + br_ref[...]
    work = logits.T[0:N_EXPERTS, :]
    sub = lax.broadcasted_iota(jnp.int32, work.shape, 0)
    vals, idxs = [], []
    for _ in range(TOP_K):
        mx = jnp.max(work, axis=0, keepdims=True)
        ix = jnp.min(jnp.where(work == mx, sub, N_EXPERTS), axis=0, keepdims=True)
        vals.append(mx)
        idxs.append(ix)
        work = jnp.where(sub == ix, -jnp.inf, work)
    exps = [jnp.exp(v - vals[0]) for v in vals]
    den = exps[0] + exps[1] + exps[2] + exps[3]

    sel = jnp.where(work == -jnp.inf, 1.0, 0.0)
    r_i = lax.broadcasted_iota(jnp.int32, (tm, tm), 0)
    c_i = lax.broadcasted_iota(jnp.int32, (tm, tm), 1)
    earlier = jnp.where(r_i < c_i, 1.0, 0.0).astype(BF16)
    base_ref[0] = run_ref[...]
    before = jnp.dot(sel.astype(BF16), earlier, preferred_element_type=F32) + run_ref[...]
    run_ref[...] = run_ref[...] + jnp.sum(sel, axis=1, keepdims=True)
    cnt_ref[...] = run_ref[...]

    ranks = [jnp.sum(jnp.where(sub == ix, before, 0.0), axis=0, keepdims=True) for ix in idxs]
    weights = [e / den for e in exps]
    ti_ref[...] = jnp.concatenate(idxs, axis=0)
    tr_ref[...] = jnp.concatenate(ranks, axis=0).astype(jnp.int32)
    meta = jnp.concatenate(weights + [ix.astype(F32) for ix in idxs]
                           + [jnp.zeros((LANES - 2 * TOP_K, tm), F32)], axis=0)
    h2_ref[:, D_MODEL:] = meta.T


def _outproj(a_p, a_s, b_p, b_s, x_p, x_s, mod_p, mod_s, norm_g, w_out_b, w_router, b_router):
    tm = ROW_TILE
    npt = x_p.shape[0] // tm
    nst = x_s.shape[0] // tm
    assert nst == 1
    n_tiles = npt + nst
    rows = n_tiles * tm
    tiles_per_batch = npt // mod_p.shape[0]
    last_p = npt - 1
    prow = lambda i: (jnp.minimum(i, last_p), 0)
    srow = lambda i: (0, 0)
    row = lambda i: (i, 0)
    col = lambda i: (0, i)
    const2 = lambda i: (0, 0)
    return pl.pallas_call(
        functools.partial(_outproj_kernel, n_prompt_tiles=npt),
        grid=(n_tiles,),
        in_specs=[pl.BlockSpec((tm, A_WIDTH), prow), pl.BlockSpec((tm, A_WIDTH), srow),
                  pl.BlockSpec((tm, B_WIDTH), prow), pl.BlockSpec((tm, B_WIDTH), srow),
                  pl.BlockSpec((tm, D_MODEL), prow), pl.BlockSpec((tm, D_MODEL), srow),
                  pl.BlockSpec((1, 6, D_MODEL), lambda i: (jnp.minimum(i, last_p) // tiles_per_batch, 0, 0)),
                  pl.BlockSpec((6, tm, D_MODEL), lambda i: (0, 0, 0)),
                  pl.BlockSpec((4, D_MODEL), const2),
                  pl.BlockSpec((D_MODEL, D_MODEL), const2),
                  pl.BlockSpec((D_MODEL, LANES), const2),
                  pl.BlockSpec((1, LANES), const2)],
        out_specs=[pl.BlockSpec((tm, D_MODEL), row), pl.BlockSpec((tm, ROW_WIDTH), row),
                   pl.BlockSpec((TOP_K, tm), col), pl.BlockSpec((TOP_K, tm), col),
                   pl.BlockSpec((N_EXPERTS, 1), const2),
                   pl.BlockSpec((1, N_EXPERTS, 1), lambda i: (i, 0, 0))],
        out_shape=[jax.ShapeDtypeStruct((rows, D_MODEL), F32), jax.ShapeDtypeStruct((rows, ROW_WIDTH), F32),
                   jax.ShapeDtypeStruct((TOP_K, rows), jnp.int32), jax.ShapeDtypeStruct((TOP_K, rows), jnp.int32),
                   jax.ShapeDtypeStruct((N_EXPERTS, 1), F32),
                   jax.ShapeDtypeStruct((n_tiles, N_EXPERTS, 1), F32)],
        scratch_shapes=[pltpu.VMEM((N_EXPERTS, 1), F32)],
        compiler_params=_cparams("arbitrary"),
        name="outproj_router",
    )(a_p, a_s, b_p, b_s, x_p, x_s, mod_p, mod_s, norm_g, w_out_b,
      jnp.pad(w_router, ((0, 0), (0, LANES - N_EXPERTS))).astype(BF16),
      jnp.pad(b_router.reshape(1, -1), ((0, 0), (0, LANES - N_EXPERTS))))


def _dispatch_kernel(pe_ref, cnt_ref, dest_ref, h_ref, xg_ref, zero_ref, sem, zsem, *, bm):
    tm = h_ref.shape[0]
    n_blocks = xg_ref.shape[0] // bm

    def zero_block(row0):
        return pltpu.make_async_copy(zero_ref, xg_ref.at[pl.ds(pl.multiple_of(row0, bm), bm), :], zsem)

    @pl.when(pl.program_id(0) == 0)
    def _():
        zero_ref[...] = jnp.zeros(zero_ref.shape, F32)
        first_unused = pe_ref[N_EXPERTS - 1] // bm
        for e in range(N_EXPERTS):
            @pl.when(cnt_ref[e] > 0)
            def _():
                zero_block(pe_ref[e] - bm).start()

        def start_unused(b, carry):
            zero_block(b * bm).start()
            return carry

        lax.fori_loop(first_unused, n_blocks, start_unused, 0)
        for e in range(N_EXPERTS):
            @pl.when(cnt_ref[e] > 0)
            def _():
                zero_block(pe_ref[e] - bm).wait()

        def wait_unused(b, carry):
            zero_block(b * bm).wait()
            return carry

        lax.fori_loop(first_unused, n_blocks, wait_unused, 0)

    def body(t, carry):
        for j in range(TOP_K):
            d = dest_ref[j * tm + t]
            pltpu.make_async_copy(h_ref.at[pl.ds(t, 1), :], xg_ref.at[pl.ds(d, 1), :], sem).start(priority=j % 2)
        return carry

    lax.fori_loop(0, tm, body, 0)
    n = tm * TOP_K
    pltpu.make_async_copy(xg_ref.at[pl.ds(0, n), :], xg_ref.at[pl.ds(0, n), :], sem).wait()


def _dispatch(pad_end, cnt, dest_flat, h2, n_blocks):
    tm = ROW_TILE
    bm = EXPERT_BLOCK
    rows = h2.shape[0]
    grid_spec = pltpu.PrefetchScalarGridSpec(
        num_scalar_prefetch=2,
        grid=(rows // tm,),
        in_specs=[pl.BlockSpec((tm * TOP_K,), lambda i, pe, cn: (i,), memory_space=pltpu.SMEM),
                  pl.BlockSpec((tm, ROW_WIDTH), lambda i, pe, cn: (i, 0))],
        out_specs=pl.BlockSpec(memory_space=pl.ANY),
        scratch_shapes=[pltpu.VMEM((bm, ROW_WIDTH), F32), pltpu.SemaphoreType.DMA(()), pltpu.SemaphoreType.DMA(())])
    return pl.pallas_call(
        functools.partial(_dispatch_kernel, bm=bm),
        grid_spec=grid_spec,
        out_shape=jax.ShapeDtypeStruct((n_blocks * bm, ROW_WIDTH), F32),
        compiler_params=_cparams("arbitrary"),
        name="moe_dispatch",
    )(pad_end, cnt, dest_flat, h2)


def _expert_kernel(be_ref, nu_ref, x_ref, wgu_ref, bgu_ref, wd_ref, bd_ref, y_ref, wgu_b, wd_b):
    i = pl.program_id(0)
    used = i < nu_ref[0]
    fresh = jnp.logical_or(i == 0, be_ref[i] != be_ref[jnp.maximum(i - 1, 0)])

    @pl.when(jnp.logical_and(used, fresh))
    def _():
        wgu_b[...] = wgu_ref[0].astype(BF16)
        wd_b[...] = wd_ref[0].astype(BF16)

    @pl.when(jnp.logical_not(used))
    def _():
        y_ref[...] = jnp.zeros(y_ref.shape, F32)

    @pl.when(used)
    def _():
        x = x_ref[:, 0:D_MODEL].astype(BF16)
        gu = jnp.dot(x, wgu_b[...], preferred_element_type=F32) + bgu_ref[0]
        glu = jnp.minimum(gu[:, :D_EXPERT], SWIGLU_LIMIT)
        lin = jnp.clip(gu[:, D_EXPERT:], -SWIGLU_LIMIT, SWIGLU_LIMIT)
        hid = glu * jax.nn.sigmoid(SWIGLU_ALPHA * glu) * (lin + 1.0)
        y = jnp.dot(hid.astype(BF16), wd_b[...], preferred_element_type=F32) + bd_ref[0]
        meta = x_ref[:, D_MODEL:]
        me = be_ref[i].astype(F32)
        w = jnp.zeros((x_ref.shape[0], 1), F32)
        for j in range(TOP_K):
            w = w + jnp.where(meta[:, TOP_K + j:TOP_K + j + 1] == me, meta[:, j:j + 1], 0.0)
        y_ref[...] = y * w


def _experts(block_e, n_used, xg, w_gu_b, b_gu, w_down_b, b_down):
    bm = EXPERT_BLOCK
    n_blocks = xg.shape[0] // bm
    grid_spec = pltpu.PrefetchScalarGridSpec(
        num_scalar_prefetch=2,
        grid=(n_blocks,),
        in_specs=[pl.BlockSpec((bm, ROW_WIDTH), lambda i, be, nu: (jnp.minimum(i, nu[0] - 1), 0)),
                  pl.BlockSpec((1, D_MODEL, 2 * D_EXPERT), lambda i, be, nu: (be[i], 0, 0)),
                  pl.BlockSpec((1, 1, 2 * D_EXPERT), lambda i, be, nu: (be[i], 0, 0)),
                  pl.BlockSpec((1, D_EXPERT, D_MODEL), lambda i, be, nu: (be[i], 0, 0)),
                  pl.BlockSpec((1, 1, D_MODEL), lambda i, be, nu: (be[i], 0, 0))],
        out_specs=pl.BlockSpec((bm, D_MODEL), lambda i, be, nu: (i, 0)),
        scratch_shapes=[pltpu.VMEM((D_MODEL, 2 * D_EXPERT), BF16), pltpu.VMEM((D_EXPERT, D_MODEL), BF16)])
    return pl.pallas_call(
        _expert_kernel,
        grid_spec=grid_spec,
        out_shape=jax.ShapeDtypeStruct((xg.shape[0], D_MODEL), F32),
        compiler_params=_cparams("arbitrary"),
        name="moe_experts",
    )(block_e, n_used, xg, w_gu_b, b_gu.reshape(N_EXPERTS, 1, -1), w_down_b, b_down.reshape(N_EXPERTS, 1, -1))


def _combine_kernel(win0_ref, fast_ref, dest_ref, col_ref, x1_ref, mp_ref, ms_ref, ng_ref, yg_ref,
                    yp_ref, ys_ref, rows_ref, g_ref, f_ref, sem, *, n_prompt_tiles):
    i = pl.program_id(0)
    n_steps = pl.num_programs(0)
    tm = x1_ref.shape[0]
    is_p = i < n_prompt_tiles
    win = COMBINE_WINDOW
    n_win_rows = N_EXPERTS * win
    slot = i % 2
    fast = fast_ref[i] != 0

    def window_copy(tile, e, buf):
        row0 = pl.multiple_of(win0_ref[tile * N_EXPERTS + e], 8)
        return pltpu.make_async_copy(yg_ref.at[pl.ds(row0, win), :], rows_ref.at[buf, pl.ds(e * win, win), :],
                                     sem.at[buf])

    def fetch_windows(tile, buf):
        for e in range(N_EXPERTS):
            window_copy(tile, e, buf).start(priority=e % 2)

    @pl.when(jnp.logical_and(i == 0, fast))
    def _():
        fetch_windows(0, 0)

    nxt = jnp.minimum(i + 1, n_steps - 1)

    @pl.when(jnp.logical_and(i + 1 < n_steps, fast_ref[nxt] != 0))
    def _():
        fetch_windows(nxt, 1 - slot)

    @pl.when(fast)
    def _():
        pltpu.make_async_copy(yg_ref.at[pl.ds(0, n_win_rows), :], rows_ref.at[slot], sem.at[slot]).wait()
        col = col_ref[...]
        for c in range(n_win_rows // LANES):
            lane = lax.broadcasted_iota(jnp.int32, (tm, LANES), 1) + c * LANES
            g = jnp.zeros((tm, LANES), F32)
            for j in range(TOP_K):
                g = jnp.where(lane == col[:, j:j + 1], 1.0, g)
            g_ref[:, c * LANES:(c + 1) * LANES] = g.astype(BF16)
        f_ref[...] = jnp.dot(g_ref[...], rows_ref[slot].astype(BF16), preferred_element_type=F32)

    @pl.when(jnp.logical_not(fast))
    def _():
        def body(t, carry):
            for j in range(TOP_K):
                d = dest_ref[j * tm + t]
                pltpu.make_async_copy(yg_ref.at[pl.ds(d, 1), :], rows_ref.at[slot, pl.ds(j * tm + t, 1), :],
                                      sem.at[slot]).start(priority=j % 2)
            return carry

        lax.fori_loop(0, tm, body, 0)
        n = tm * TOP_K
        pltpu.make_async_copy(yg_ref.at[pl.ds(0, n), :], rows_ref.at[slot, pl.ds(0, n), :], sem.at[slot]).wait()
        f = rows_ref[slot, 0:tm, :]
        for j in range(1, TOP_K):
            f = f + rows_ref[slot, j * tm:(j + 1) * tm, :]
        f_ref[...] = f

    g2 = jnp.where(is_p, mp_ref[0, 5:6, :], ms_ref[5])
    y = x1_ref[...] + g2 * _rms(f_ref[...], ng_ref[3:4, :])

    @pl.when(is_p)
    def _():
        yp_ref[...] = y

    @pl.when(jnp.logical_not(is_p))
    def _():
        ys_ref[...] = y


def _combine(win0, fast, dest_flat, col, x1, mod_p, mod_s, norm_g, yg, *, n_prompt_rows, n_sample_rows):
    tm = ROW_TILE
    npt = n_prompt_rows // tm
    assert n_sample_rows == tm and N_EXPERTS * COMBINE_WINDOW >= tm * TOP_K
    n_tiles = npt + 1
    tiles_per_batch = npt // mod_p.shape[0]
    last_p = npt - 1
    grid_spec = pltpu.PrefetchScalarGridSpec(
        num_scalar_prefetch=2,
        grid=(n_tiles,),
        in_specs=[pl.BlockSpec((tm * TOP_K,), lambda i, w0, fa: (i,), memory_space=pltpu.SMEM),
                  pl.BlockSpec((tm, TOP_K), lambda i, w0, fa: (i, 0)),
                  pl.BlockSpec((tm, D_MODEL), lambda i, w0, fa: (i, 0)),
                  pl.BlockSpec((1, 6, D_MODEL),
                               lambda i, w0, fa: (jnp.minimum(i, last_p) // tiles_per_batch, 0, 0)),
                  pl.BlockSpec((6, tm, D_MODEL), lambda i, w0, fa: (0, 0, 0)),
                  pl.BlockSpec((4, D_MODEL), lambda i, w0, fa: (0, 0)),
                  pl.BlockSpec(memory_space=pl.ANY)],
        out_specs=[pl.BlockSpec((tm, D_MODEL), lambda i, w0, fa: (jnp.minimum(i, last_p), 0)),
                   pl.BlockSpec((tm, D_MODEL), lambda i, w0, fa: (0, 0))],
        scratch_shapes=[pltpu.VMEM((2, N_EXPERTS * COMBINE_WINDOW, D_MODEL), F32),
                        pltpu.VMEM((tm, N_EXPERTS * COMBINE_WINDOW), BF16),
                        pltpu.VMEM((tm, D_MODEL), F32),
                        pltpu.SemaphoreType.DMA((2,))])
    return pl.pallas_call(
        functools.partial(_combine_kernel, n_prompt_tiles=npt),
        grid_spec=grid_spec,
        out_shape=[jax.ShapeDtypeStruct((n_prompt_rows, D_MODEL), F32),
                   jax.ShapeDtypeStruct((n_sample_rows, D_MODEL), F32)],
        compiler_params=_cparams("arbitrary"),
        name="moe_combine",
    )(win0, fast, dest_flat, col, x1, mod_p, mod_s, norm_g, yg)


def _t5_bucket_np(dist):
    n = np.maximum(dist, 0)
    max_exact = NUM_BUCKETS // 2
    nf = np.maximum(n, 1).astype(np.float64)
    large = max_exact + (np.log(nf / max_exact) / math.log(MAX_DISTANCE / max_exact)
                         * (NUM_BUCKETS - max_exact)).astype(np.int32)
    large = np.minimum(large, NUM_BUCKETS - 1)
    return np.where(n < max_exact, n, large).astype(np.int32)


def _bias_from_dist(rel_bias, dist):
    onehot = jax.nn.one_hot(_t5_bucket_np(dist).reshape(-1), NUM_BUCKETS, dtype=F32)
    shifted = (rel_bias - rel_bias[NUM_BUCKETS - 1]).reshape(NUM_BUCKETS, -1)
    out = jnp.dot(onehot, shifted, precision=lax.Precision.HIGHEST).reshape(dist.shape + rel_bias.shape[1:])
    return jnp.where(jnp.asarray(dist >= 0)[..., None, None], out, -jnp.inf).astype(F32)


def _head_masked(t):
    h = t.shape[0]
    same = np.eye(h, dtype=bool)[:, None, None, None, :]
    full = jnp.where(same, t[..., None], -jnp.inf)
    return full.reshape(h * t.shape[1] * t.shape[2], t.shape[3] * h)


def kernel(x_prompt, x_sample, cache_k, cache_v, page_table, c_prompt, c_sample, w_ada, b_ada, norm_g, w_in,
           w_out, ln_v_g, ln_v_b, w_spatial, b_spatial, lam_params, subln_g, rel_bias, w_router, b_router,
           w_gu, b_gu, w_down, b_down):
    batch, seq, d = x_prompt.shape
    nb, ds = x_sample.shape[:2]
    n_pages = page_table.shape[1]
    past = n_pages * PAGE_SIZE
    rows_p = batch * seq
    rows_s = nb * ds
    assert rows_s == ROW_TILE and d == D_MODEL

    mod = _ada(jnp.concatenate([c_prompt, c_sample], axis=0), w_ada[0], b_ada[0])
    mod_p = mod[:batch].reshape(batch, 6, D_MODEL)
    mod_s = jnp.transpose(jnp.repeat(mod[batch:], ds, axis=0).reshape(rows_s, 6, D_MODEL), (1, 0, 2))

    ng = norm_g[0]
    w_in_b = w_in[0].astype(BF16)
    w_out_b = w_out[0].astype(BF16)
    lng = ln_v_g[0].reshape(1, A_WIDTH)
    lnb = ln_v_b[0].reshape(1, A_WIDTH)
    grp = np.arange(A_WIDTH) // A_HEAD
    avg = jnp.asarray((grp[:, None] == grp[None, :]).astype(np.float32) / A_HEAD, BF16)
    ws_p = jnp.tril(w_spatial[0]).astype(BF16)
    bs_p = jnp.repeat(b_spatial[0].T, A_HEAD, axis=1)
    w_small = jnp.tril(w_spatial[0][:, :ds, :ds])
    ws_s = jnp.einsum('ab,gts->gatbs', jnp.eye(nb, dtype=F32), w_small).reshape(A_GROUPS, rows_s, rows_s).astype(BF16)
    bs_s = jnp.tile(jnp.repeat(b_spatial[0][:, :ds].T, A_HEAD, axis=1), (nb, 1))

    xp2 = x_prompt.reshape(rows_p, D_MODEL)
    xs2 = x_sample.reshape(rows_s, D_MODEL)
    assert ATTN_TILE == ROW_TILE
    a_p, va_p, q_p, k_p, v_p, kb_p, vt_p = _inproj(
        xp2, mod_p[:, 0:1], mod_p[:, 1:2], ng[0:1], w_in_b, lng, lnb, avg, ws_p, bs_p,
        tiles_per_mod=seq // ROW_TILE, chunk=CHUNK, q_dtype=BF16, q_scale=LOG2E / math.sqrt(B_HEAD), va_rows=CHUNK)
    a_s, va_s, q_s, k_s, v_s, _, _ = _inproj(
        xs2, mod_s[0:1], mod_s[1:2], ng[0:1], w_in_b, lng, lnb, avg, ws_s, bs_s,
        tiles_per_mod=1, chunk=rows_s, q_dtype=F32, q_scale=1.0, va_rows=rows_s)

    ti = np.arange(ATTN_TILE)
    dist_p = np.stack([ti[:, None] - ti[None, :], ti[:, None] - ti[None, :] + ATTN_TILE])
    bias_p = jnp.transpose(_bias_from_dist(rel_bias, dist_p), (3, 0, 2, 4, 1))
    bias_p = bias_p.reshape(B_HEADS, 2, ATTN_TILE, 2 * ATTN_TILE) * LOG2E
    bias_p = jnp.concatenate([bias_p, jnp.zeros_like(bias_p[:, :1])], axis=1)
    b_p = _attn_p(q_p, kb_p, vt_p, bias_p, lam_params[0], subln_g, batch=batch, seq=seq)

    qi = np.arange(ds)
    ki = np.arange(PAGE_SIZE)
    dist_l = PAGE_SIZE + qi[:, None] - ki[None, :]
    kn_i = np.arange(PAGE_SIZE // B_HEADS)
    dist_n = np.where(kn_i[None, :] < ds, qi[:, None] - kn_i[None, :], -1)
    to_rows = lambda t: jnp.transpose(t, (2, 3, 0, 1))
    bias_l = _head_masked(to_rows(_bias_from_dist(rel_bias, dist_l)))
    bias_n = _head_masked(to_rows(_bias_from_dist(rel_bias, dist_n)))
    maskb = _head_masked(jnp.zeros((B_HEADS, 2, ds, PAGE_SIZE), F32))
    pad_keys = lambda t: jnp.pad(t.reshape(nb, ds * B_HEADS, B_VDIM), ((0, 0), (0, PAGE_SIZE - ds * B_HEADS), (0, 0)))
    n_phys = cache_k.shape[0]
    b_s = _attn_s(page_table, q_s, pad_keys(k_s), pad_keys(v_s), maskb, bias_l, bias_n, lam_params[0], subln_g,
                  cache_k.reshape(n_phys * PAGE_SIZE * B_HEADS, B_VDIM),
                  cache_v.reshape(n_phys * PAGE_SIZE * B_HEADS, B_VDIM), ds=ds)

    x1, h2, top_i, top_r, counts, tile_base = _outproj(
        a_p, a_s, b_p, b_s, xp2, xs2, mod_p, mod_s, ng, w_out_b, w_router[0], b_router[0])

    bm = EXPERT_BLOCK
    rows = rows_p + rows_s
    n_blocks = rows * TOP_K // bm + N_EXPERTS
    cnt = counts[:, 0].astype(jnp.int32)
    padded = (cnt + bm - 1) // bm * bm
    pad_end = jnp.cumsum(padded)
    pad_start = pad_end - padded
    experts = jnp.arange(N_EXPERTS, dtype=jnp.int32)
    chosen = top_i[..., None] == experts
    dest = jnp.sum(jnp.where(chosen, pad_start, 0), axis=-1) + top_r
    n_tiles = rows // ROW_TILE
    base = tile_base[:, :, 0].astype(jnp.int32)
    first = pad_start[None, :] + base
    win0 = jnp.minimum(first // 8 * 8, n_blocks * bm - COMBINE_WINDOW)
    sent = jnp.concatenate([base[1:], cnt[None, :]], axis=0) - base
    fast = jnp.all(first - win0 + sent <= COMBINE_WINDOW, axis=1).astype(jnp.int32)
    win0_tok = jnp.repeat(win0, ROW_TILE, axis=0)
    col = top_i * COMBINE_WINDOW + dest - jnp.sum(jnp.where(chosen, win0_tok, 0), axis=-1)
    dest = jnp.transpose(dest.reshape(TOP_K, n_tiles, ROW_TILE), (1, 0, 2)).reshape(-1)
    blk_start = jnp.arange(n_blocks, dtype=jnp.int32) * bm
    block_e = jnp.minimum(jnp.sum((pad_end[None, :] <= blk_start[:, None]).astype(jnp.int32), axis=1),
                          N_EXPERTS - 1)
    n_used = (pad_end[-1:] // bm).astype(jnp.int32)

    xg = _dispatch(pad_end.astype(jnp.int32), cnt, dest, h2, n_blocks)
    yg = _experts(block_e, n_used, xg, w_gu[0], b_gu[0], w_down[0], b_down[0])
    y_p, y_s = _combine(win0.reshape(-1), fast, dest, col.T, x1, mod_p, mod_s, ng, yg,
                        n_prompt_rows=rows_p, n_sample_rows=rows_s)

    return (y_p.reshape(batch, seq, D_MODEL),
            y_s.reshape(nb, ds, D_MODEL),
            k_p.reshape(batch, seq, 1, B_HEADS, B_VDIM),
            v_p.reshape(batch, seq, 1, B_HEADS, B_VDIM),
            k_s.reshape(nb, ds, 1, B_HEADS, B_VDIM),
            v_s.reshape(nb, ds, 1, B_HEADS, B_VDIM),
            va_p.reshape(batch, CHUNK, 1, A_WIDTH),
            va_s.reshape(nb, ds, 1, A_WIDTH))
```

```python
import functools
import math

import numpy as np
import jax
import jax.numpy as jnp
from jax import lax
from jax.experimental import pallas as pl
from jax.experimental.pallas import tpu as pltpu

F32 = jnp.float32
BF16 = jnp.bfloat16

D_MODEL = 1024
A_WIDTH = 512
A_HEAD = 64
A_GROUPS = 8
CHUNK = 128
B_WIDTH = 512
B_HEAD = 64
B_VDIM = 128
B_HEADS = 4
IN_WIDTH = 2 * A_WIDTH + 3 * B_WIDTH
NUM_BUCKETS = 32
MAX_DISTANCE = 128
PAGE_SIZE = 128
N_EXPERTS = 32
TOP_K = 4
D_EXPERT = 1024
SWIGLU_LIMIT = 7.0
SWIGLU_ALPHA = 1.702
NORM_EPS = 1e-6
LAM_INIT = 0.8 - 0.6 * math.exp(-0.3 * 0)

ROW_TILE = 256
ATTN_TILE = 256
ATTN_KEYS = 128
PAGES_PER_STEP = 16
LOG2E = math.log2(math.e)
LANES = 128
MXU_WIDTH = 256
ROW_WIDTH = D_MODEL + LANES
EXPERT_BLOCK = 256
COMBINE_WINDOW = 40
COMBINE_SLOTS = 64
VMEM_LIMIT = 56 * 1024 * 1024


def _cparams(*sem):
    return pltpu.CompilerParams(dimension_semantics=sem, vmem_limit_bytes=VMEM_LIMIT)


def _rms(x, g):
    return x * lax.rsqrt(jnp.mean(x * x, axis=-1, keepdims=True) + NORM_EPS) * g


def _ada_kernel(c_ref, w_ref, b_ref, o_ref):
    s = jax.nn.silu(c_ref[...]).astype(BF16)
    o_ref[...] = jnp.dot(s, w_ref[...].astype(BF16), preferred_element_type=F32) + b_ref[...]


def _ada(c_all, w_ada, b_ada):
    n = c_all.shape[0]
    tn = 1024
    return pl.pallas_call(
        _ada_kernel,
        grid=(6 * D_MODEL // tn,),
        in_specs=[pl.BlockSpec((n, D_MODEL), lambda j: (0, 0)),
                  pl.BlockSpec((D_MODEL, tn), lambda j: (0, j)),
                  pl.BlockSpec((1, tn), lambda j: (0, j))],
        out_specs=pl.BlockSpec((n, tn), lambda j: (0, j)),
        out_shape=jax.ShapeDtypeStruct((n, 6 * D_MODEL), F32),
        compiler_params=_cparams("arbitrary"),
        name="ada",
    )(c_all, w_ada, b_ada.reshape(1, -1))


def _inproj_kernel(x_ref, sh_ref, sc_ref, ng_ref, w_ref, lng_ref, lnb_ref, avg_ref, ws_ref, bs_ref,
                   a_ref, va_ref, q_ref, k_ref, v_ref, kb_ref, vt_ref, *, chunk, q_scale):
    tm = x_ref.shape[0]
    va_rows = va_ref.shape[0]
    h = (_rms(x_ref[...], ng_ref[...]) * (1.0 + sc_ref[0]) + sh_ref[0]).astype(BF16)

    u = jax.nn.gelu(jnp.dot(h, w_ref[:, 0:A_WIDTH], preferred_element_type=F32))
    gv = jax.nn.gelu(jnp.dot(h, w_ref[:, A_WIDTH:2 * A_WIDTH], preferred_element_type=F32))

    avg = avg_ref[...]

    def group_mean(t):
        hi = t.astype(BF16)
        lo = (t - hi.astype(F32)).astype(BF16)
        return (jnp.dot(hi, avg, preferred_element_type=F32)
                + jnp.dot(lo, avg, preferred_element_type=F32))

    xc = gv - group_mean(gv)
    va = xc * lax.rsqrt(group_mean(xc * xc) + NORM_EPS) * lng_ref[...] + lnb_ref[...]
    va_ref[...] = va[tm - va_rows:, :]

    vab = va.astype(BF16)
    lane = lax.broadcasted_iota(jnp.int32, (chunk, 128), 1)
    for c in range(tm // chunk):
        r0 = c * chunk
        for pair in range(A_GROUPS // 2):
            c0 = pair * 128
            vp = vab[r0:r0 + chunk, c0:c0 + 128]
            lo_half = jnp.where(lane < A_HEAD, vp, jnp.zeros_like(vp))
            hi_half = jnp.where(lane >= A_HEAD, vp, jnp.zeros_like(vp))
            s = (jnp.dot(ws_ref[2 * pair], lo_half, preferred_element_type=F32)
                 + jnp.dot(ws_ref[2 * pair + 1], hi_half, preferred_element_type=F32))
            a = u[r0:r0 + chunk, c0:c0 + 128] * (s + bs_ref[:, c0:c0 + 128])
            a_ref[r0:r0 + chunk, c0:c0 + 128] = a.astype(a_ref.dtype)

    q = jnp.dot(h, w_ref[:, 2 * A_WIDTH:2 * A_WIDTH + B_WIDTH], preferred_element_type=F32)
    q_ref[...] = (q * q_scale).astype(q_ref.dtype)
    k = jnp.dot(h, w_ref[:, 2 * A_WIDTH + B_WIDTH:2 * A_WIDTH + 2 * B_WIDTH], preferred_element_type=F32)
    kb_ref[...] = k.astype(BF16)
    v = jnp.dot(h, w_ref[:, 2 * A_WIDTH + 2 * B_WIDTH:IN_WIDTH], preferred_element_type=F32)
    for hh in range(B_HEADS):
        c0 = hh * B_VDIM
        k_ref[pl.ds(hh, tm, stride=B_HEADS), :] = k[:, c0:c0 + B_VDIM]
        v_ref[pl.ds(hh, tm, stride=B_HEADS), :] = v[:, c0:c0 + B_VDIM]
        vt_ref[0, hh, 0] = v[:, c0:c0 + B_VDIM].T.astype(BF16)


def _inproj(x2, sh, sc, ng, w_in_b, lng, lnb, avg, ws, bs, *, tiles_per_mod, chunk, q_dtype, q_scale, va_rows):
    rows = x2.shape[0]
    tm = ROW_TILE
    n_tiles = rows // tm
    mod_rows = sh.shape[1]
    row = lambda i: (i, 0)
    const2 = lambda i: (0, 0)
    mod_map = lambda i: (i // tiles_per_mod, 0, 0)
    out_w = lambda w, dt: jax.ShapeDtypeStruct((rows, w), dt)
    head_rows = jax.ShapeDtypeStruct((rows * B_HEADS, B_VDIM), F32)
    head_rows_spec = pl.BlockSpec((tm * B_HEADS, B_VDIM), row)
    vt_shape = jax.ShapeDtypeStruct((n_tiles // tiles_per_mod, B_HEADS, tiles_per_mod, B_VDIM, tm), BF16)
    vt_spec = pl.BlockSpec((1, B_HEADS, 1, B_VDIM, tm), lambda i: (i // tiles_per_mod, 0, i % tiles_per_mod, 0, 0))
    return pl.pallas_call(
        functools.partial(_inproj_kernel, chunk=chunk, q_scale=q_scale),
        grid=(rows // tm,),
        in_specs=[pl.BlockSpec((tm, D_MODEL), row),
                  pl.BlockSpec((1, mod_rows, D_MODEL), mod_map),
                  pl.BlockSpec((1, mod_rows, D_MODEL), mod_map),
                  pl.BlockSpec((1, D_MODEL), const2),
                  pl.BlockSpec((D_MODEL, IN_WIDTH), const2),
                  pl.BlockSpec((1, A_WIDTH), const2),
                  pl.BlockSpec((1, A_WIDTH), const2),
                  pl.BlockSpec((A_WIDTH, A_WIDTH), const2),
                  pl.BlockSpec((A_GROUPS, chunk, chunk), lambda i: (0, 0, 0)),
                  pl.BlockSpec((chunk, A_WIDTH), const2)],
        out_specs=[pl.BlockSpec((tm, A_WIDTH), row),
                   pl.BlockSpec((va_rows, A_WIDTH), lambda i: (i // tiles_per_mod, 0)),
                   pl.BlockSpec((tm, B_WIDTH), row), head_rows_spec, head_rows_spec,
                   pl.BlockSpec((tm, B_WIDTH), row), vt_spec],
        out_shape=[out_w(A_WIDTH, BF16),
                   jax.ShapeDtypeStruct((n_tiles // tiles_per_mod * va_rows, A_WIDTH), F32),
                   out_w(B_WIDTH, q_dtype), head_rows, head_rows, out_w(B_WIDTH, BF16), vt_shape],
        compiler_params=_cparams("arbitrary"),
        name="inproj",
    )(x2, sh, sc, ng, w_in_b, lng, lnb, avg, ws, bs)


def _stack_halves(q):
    lane = lax.broadcasted_iota(jnp.int32, q.shape, 1)
    zero = jnp.zeros_like(q)
    return jnp.concatenate([jnp.where(lane < B_HEAD, q, zero), jnp.where(lane >= B_HEAD, q, zero)], axis=0)


def _lambda(lam_ref):
    lp = lam_ref[...]
    return (jnp.exp(jnp.sum(lp[0:1] * lp[1:2], axis=-1, keepdims=True))
            - jnp.exp(jnp.sum(lp[2:3] * lp[3:4], axis=-1, keepdims=True)) + LAM_INIT)


def _diff_finish(acc, l, n, lam, g):
    o = acc[:n] / l[:n] - lam * (acc[n:] / l[n:])
    return _rms(o, g) * (1.0 - LAM_INIT)


def _attn_p_kernel(q_ref, k_ref, vt_ref, bias_ref, lam_ref, g_ref, o_ref,
                   acc_ref, m_ref, l_ref, s_ref, tmax_ref, p_ref, alpha_ref, *, tq, tk):
    qi = pl.program_id(2)
    q = q_ref[...]
    lane = lax.broadcasted_iota(jnp.int32, q.shape, 1)
    zero = jnp.zeros_like(q)
    q_half = (jnp.where(lane < B_HEAD, q, zero), jnp.where(lane >= B_HEAD, q, zero))
    m_ref[...] = jnp.full(m_ref.shape, -jnp.inf, F32)
    l_ref[...] = jnp.zeros(l_ref.shape, F32)
    acc_ref[...] = jnp.zeros(acc_ref.shape, F32)
    p_ref[1] = jnp.zeros(p_ref.shape[1:], BF16)
    alpha_ref[1] = jnp.ones(alpha_ref.shape[1:], F32)

    def scores(j):
        k = k_ref[pl.ds(pl.multiple_of(j * tk, tk), tk), :]
        return [lax.dot_general(k, q_half[c], (((1,), (1,)), ((), ())), preferred_element_type=F32)
                for c in range(2)]

    def keep_scores(j, kind, ss):
        slot = j % 2
        for c in range(2):
            s = ss[c]
            if kind is not None:
                s = s + bias_ref[0, kind, :, c * tq:(c + 1) * tq]
            s_ref[slot, c] = s
            tmax_ref[slot, c] = jnp.max(s, axis=0, keepdims=True)

    def softmax(j):
        slot = j % 2
        for c in range(2):
            m_old = m_ref[c]
            m_new = jnp.maximum(m_old, tmax_ref[slot, c])
            alpha = jnp.exp2(m_old - m_new)
            p = jnp.exp2(s_ref[slot, c] - m_new)
            l_ref[c] = alpha * l_ref[c] + jnp.sum(p, axis=0, keepdims=True)
            m_ref[c] = m_new
            p_ref[slot, c] = p.astype(BF16)
            alpha_ref[slot, c] = alpha

    def pv(j):
        slot = j % 2
        vt = vt_ref[0, 0, jnp.maximum(j, 0)]
        for c in range(2):
            acc_ref[c] = alpha_ref[slot, c] * acc_ref[c] + jnp.dot(vt, p_ref[slot, c], preferred_element_type=F32)

    def far_body(j, carry):
        ss = scores(j + 1)
        pv(j - 1)
        softmax(j)
        keep_scores(j + 1, None, ss)
        return carry

    def near_body(j, carry):
        nxt = jnp.minimum(j + 1, qi)
        ss = scores(nxt)
        pv(j - 1)
        softmax(j)
        keep_scores(nxt, qi - nxt, ss)
        return carry

    keep_scores(0, jnp.minimum(qi, 2), scores(0))
    n_far_fetch = jnp.maximum(qi - 2, 0)
    lax.fori_loop(0, n_far_fetch, far_body, 0)
    lax.fori_loop(n_far_fetch, qi + 1, near_body, 0)
    pv(qi)

    o = acc_ref[0] / l_ref[0] - _lambda(lam_ref) * (acc_ref[1] / l_ref[1])
    y = o * lax.rsqrt(jnp.mean(o * o, axis=0, keepdims=True) + NORM_EPS) * (1.0 - LAM_INIT)
    o_ref[...] = (y.T * g_ref[...]).astype(o_ref.dtype)


def _attn_p(qb, kb, vt, bias, lam_params, subln_g, *, batch, seq):
    tq = tk = ATTN_TILE
    nq = seq // tq
    return pl.pallas_call(
        functools.partial(_attn_p_kernel, tq=tq, tk=tk),
        grid=(batch, B_HEADS, nq),
        in_specs=[pl.BlockSpec((tq, B_VDIM), lambda b, h, i: (b * nq + i, h)),
                  pl.BlockSpec((seq, B_VDIM), lambda b, h, i: (b, h)),
                  pl.BlockSpec((1, 1, seq // tk, B_VDIM, tk), lambda b, h, i: (b, h, 0, 0, 0)),
                  pl.BlockSpec((1, 3, tk, 2 * tq), lambda b, h, i: (h, 0, 0, 0)),
                  pl.BlockSpec((4, B_HEAD), lambda b, h, i: (0, 0)),
                  pl.BlockSpec((1, B_VDIM), lambda b, h, i: (0, 0))],
        out_specs=pl.BlockSpec((tq, B_VDIM), lambda b, h, i: (b * nq + i, h)),
        out_shape=jax.ShapeDtypeStruct((batch * seq, B_WIDTH), BF16),
        scratch_shapes=[pltpu.VMEM((2, B_VDIM, tq), F32),
                        pltpu.VMEM((2, 1, tq), F32),
                        pltpu.VMEM((2, 1, tq), F32),
                        pltpu.VMEM((2, 2, tk, tq), F32),
                        pltpu.VMEM((2, 2, 1, tq), F32),
                        pltpu.VMEM((2, 2, tk, tq), BF16),
                        pltpu.VMEM((2, 2, 1, tq), F32)],
        compiler_params=_cparams("arbitrary", "arbitrary", "arbitrary"),
        name="attn_prompt",
    )(qb, kb, vt, bias, lam_params, subln_g)


def _attn_s_kernel(pt_ref, q_ref, kn_ref, vn_ref, maskb_ref, biasl_ref, biasn_ref, lam_ref, g_ref, *rest,
                   npages, ds):
    del pt_ref
    k_refs = rest[:npages]
    v_refs = rest[npages:2 * npages]
    o_ref = rest[2 * npages]
    acc_ref, m_ref, l_ref = rest[2 * npages + 1:]
    g = pl.program_id(1)
    last = g == pl.num_programs(1) - 1

    @pl.when(g == 0)
    def _():
        m_ref[...] = jnp.full(m_ref.shape, -jnp.inf, F32)
        l_ref[...] = jnp.zeros(l_ref.shape, F32)
        acc_ref[...] = jnp.zeros(acc_ref.shape, F32)

    q = q_ref[...] * (1.0 / math.sqrt(B_HEAD))
    qall = jnp.concatenate([_stack_halves(q[:, h * B_VDIM:(h + 1) * B_VDIM]) for h in range(B_HEADS)],
                           axis=0).astype(BF16)

    def process(k_blocks, v_blocks, biases):
        s = jnp.concatenate(
            [lax.dot_general(qall, kb.astype(BF16), (((1,), (1,)), ((), ())), preferred_element_type=F32) + bb
             for kb, bb in zip(k_blocks, biases)], axis=1)
        m_old = m_ref[...]
        m_new = jnp.maximum(m_old, jnp.max(s, axis=-1, keepdims=True))
        alpha = jnp.exp(m_old - m_new)
        pr = jnp.exp(s - m_new)
        l_ref[...] = alpha * l_ref[...] + jnp.sum(pr, axis=-1, keepdims=True)
        prb = pr.astype(BF16)
        pv = None
        off = 0
        for vb in v_blocks:
            n = vb.shape[0]
            t = jnp.dot(prb[:, off:off + n], vb.astype(BF16), preferred_element_type=F32)
            pv = t if pv is None else pv + t
            off += n
        acc_ref[...] = alpha * acc_ref[...] + pv
        m_ref[...] = m_new

    maskb = maskb_ref[...]
    newest = jnp.where(last, biasl_ref[...], maskb)
    process([k_refs[p][...] for p in range(npages)], [v_refs[p][...] for p in range(npages)],
            [maskb] * (npages - 1) + [newest])

    @pl.when(last)
    def _():
        process([kn_ref[0]], [vn_ref[0]], [biasn_ref[...]])
        lam = _lambda(lam_ref)
        acc = acc_ref[...]
        l = l_ref[...]
        for h in range(B_HEADS):
            r0 = h * 2 * ds
            o_ref[:, h * B_VDIM:(h + 1) * B_VDIM] = _diff_finish(
                acc[r0:r0 + 2 * ds], l[r0:r0 + 2 * ds], ds, lam, g_ref[...])


def _attn_s(page_table, qs, kn, vn, maskb, bias_last, bias_new, lam_params, subln_g, cache_k2, cache_v2, *, ds):
    nb, n_pages = page_table.shape
    npg = PAGES_PER_STEP
    steps = n_pages // npg
    page_rows = PAGE_SIZE * B_HEADS
    nrow = B_HEADS * 2 * ds

    def page_spec(p):
        return pl.BlockSpec((page_rows, B_VDIM), lambda b, g, pt, p=p: (pt[b, g * npg + p], 0))

    const2 = lambda b, g, pt: (0, 0)
    grid_spec = pltpu.PrefetchScalarGridSpec(
        num_scalar_prefetch=1,
        grid=(nb, steps),
        in_specs=[pl.BlockSpec((ds, B_WIDTH), lambda b, g, pt: (b, 0)),
                  pl.BlockSpec((1, PAGE_SIZE, B_VDIM), lambda b, g, pt: (b, 0, 0)),
                  pl.BlockSpec((1, PAGE_SIZE, B_VDIM), lambda b, g, pt: (b, 0, 0)),
                  pl.BlockSpec((nrow, page_rows), const2),
                  pl.BlockSpec((nrow, page_rows), const2),
                  pl.BlockSpec((nrow, PAGE_SIZE), const2),
                  pl.BlockSpec((4, B_HEAD), const2),
                  pl.BlockSpec((1, B_VDIM), const2)]
                 + [page_spec(p) for p in range(npg)] + [page_spec(p) for p in range(npg)],
        out_specs=pl.BlockSpec((ds, B_WIDTH), lambda b, g, pt: (b, 0)),
        scratch_shapes=[pltpu.VMEM((nrow, B_VDIM), F32),
                        pltpu.VMEM((nrow, 1), F32),
                        pltpu.VMEM((nrow, 1), F32)])
    return pl.pallas_call(
        functools.partial(_attn_s_kernel, npages=npg, ds=ds),
        grid_spec=grid_spec,
        out_shape=jax.ShapeDtypeStruct((nb * ds, B_WIDTH), F32),
        compiler_params=_cparams("arbitrary", "arbitrary"),
        name="attn_sample",
    )(page_table, qs, kn, vn, maskb, bias_last, bias_new, lam_params, subln_g,
      *([cache_k2] * npg), *([cache_v2] * npg))


def _outproj_kernel(ap_ref, as_ref, bp_ref, bs_ref, xp_ref, xs_ref, mp_ref, ms_ref, ng_ref, wo_ref,
                    wr_ref, br_ref, x1_ref, h2_ref, ti_ref, tr_ref, cnt_ref, base_ref, run_ref,
                    *, n_prompt_tiles):
    i = pl.program_id(0)
    tm = xp_ref.shape[0]
    is_p = i < n_prompt_tiles

    @pl.when(i == 0)
    def _():
        run_ref[...] = jnp.zeros(run_ref.shape, F32)

    a = jnp.where(is_p, ap_ref[...], as_ref[...])
    b = jnp.where(is_p, bp_ref[...], bs_ref[...].astype(BF16))
    x = jnp.where(is_p, xp_ref[...], xs_ref[...])
    g1 = jnp.where(is_p, mp_ref[0, 2:3, :], ms_ref[2])
    sh2 = jnp.where(is_p, mp_ref[0, 3:4, :], ms_ref[3])
    sc2 = jnp.where(is_p, mp_ref[0, 4:5, :], ms_ref[4])

    mix = (jnp.dot(a, wo_ref[0:A_WIDTH, :], preferred_element_type=F32)
           + jnp.dot(b, wo_ref[A_WIDTH:, :], preferred_element_type=F32))
    x1 = x + g1 * _rms(mix, ng_ref[1:2, :])
    x1_ref[...] = x1
    h2 = _rms(x1, ng_ref[2:3, :]) * (1.0 + sc2) + sh2
    h2_ref[:, 0:D_MODEL] = h2

    logits = jnp.dot(h2.astype(BF16), wr_ref[...], preferred_element_type=F32) + br_ref[...]
    work = logits.T[0:N_EXPERTS, :]
    sub = lax.broadcasted_iota(jnp.int32, work.shape, 0)
    vals, idxs = [], []
    for _ in range(TOP_K):
        mx = jnp.max(work, axis=0, keepdims=True)
        ix = jnp.min(jnp.where(work == mx, sub, N_EXPERTS), axis=0, keepdims=True)
        vals.append(mx)
        idxs.append(ix)
        work = jnp.where(sub == ix, -jnp.inf, work)
    exps = [jnp.exp(v - vals[0]) for v in vals]
    den = exps[0] + exps[1] + exps[2] + exps[3]

    sel = jnp.where(work == -jnp.inf, 1.0, 0.0)
    r_i = lax.broadcasted_iota(jnp.int32, (tm, tm), 0)
    c_i = lax.broadcasted_iota(jnp.int32, (tm, tm), 1)
    earlier = jnp.where(r_i < c_i, 1.0, 0.0).astype(BF16)
    base_ref[0] = run_ref[...]
    before = jnp.dot(sel.astype(BF16), earlier, preferred_element_type=F32) + run_ref[...]
    run_ref[...] = run_ref[...] + jnp.sum(sel, axis=1, keepdims=True)
    cnt_ref[...] = run_ref[...]

    ranks = [jnp.sum(jnp.where(sub == ix, before, 0.0), axis=0, keepdims=True) for ix in idxs]
    weights = [e / den for e in exps]
    ti_ref[...] = jnp.concatenate(idxs, axis=0)
    tr_ref[...] = jnp.concatenate(ranks, axis=0).astype(jnp.int32)
    meta = jnp.concatenate(weights + [ix.astype(F32) for ix in idxs]
                           + [jnp.zeros((LANES - 2 * TOP_K, tm), F32)], axis=0)
    h2_ref[:, D_MODEL:] = meta.T


def _outproj(a_p, a_s, b_p, b_s, x_p, x_s, mod_p, mod_s, norm_g, w_out_b, w_router, b_router):
    tm = ROW_TILE
    npt = x_p.shape[0] // tm
    nst = x_s.shape[0] // tm
    assert nst == 1
    n_tiles = npt + nst
    rows = n_tiles * tm
    tiles_per_batch = npt // mod_p.shape[0]
    last_p = npt - 1
    prow = lambda i: (jnp.minimum(i, last_p), 0)
    srow = lambda i: (0, 0)
    row = lambda i: (i, 0)
    col = lambda i: (0, i)
    const2 = lambda i: (0, 0)
    return pl.pallas_call(
        functools.partial(_outproj_kernel, n_prompt_tiles=npt),
        grid=(n_tiles,),
        in_specs=[pl.BlockSpec((tm, A_WIDTH), prow), pl.BlockSpec((tm, A_WIDTH), srow),
                  pl.BlockSpec((tm, B_WIDTH), prow), pl.BlockSpec((tm, B_WIDTH), srow),
                  pl.BlockSpec((tm, D_MODEL), prow), pl.BlockSpec((tm, D_MODEL), srow),
                  pl.BlockSpec((1, 6, D_MODEL), lambda i: (jnp.minimum(i, last_p) // tiles_per_batch, 0, 0)),
                  pl.BlockSpec((6, tm, D_MODEL), lambda i: (0, 0, 0)),
                  pl.BlockSpec((4, D_MODEL), const2),
                  pl.BlockSpec((D_MODEL, D_MODEL), const2),
                  pl.BlockSpec((D_MODEL, LANES), const2),
                  pl.BlockSpec((1, LANES), const2)],
        out_specs=[pl.BlockSpec((tm, D_MODEL), row), pl.BlockSpec((tm, ROW_WIDTH), row),
                   pl.BlockSpec((TOP_K, tm), col), pl.BlockSpec((TOP_K, tm), col),
                   pl.BlockSpec((N_EXPERTS, 1), const2),
                   pl.BlockSpec((1, N_EXPERTS, 1), lambda i: (i, 0, 0))],
        out_shape=[jax.ShapeDtypeStruct((rows, D_MODEL), F32), jax.ShapeDtypeStruct((rows, ROW_WIDTH), F32),
                   jax.ShapeDtypeStruct((TOP_K, rows), jnp.int32), jax.ShapeDtypeStruct((TOP_K, rows), jnp.int32),
                   jax.ShapeDtypeStruct((N_EXPERTS, 1), F32),
                   jax.ShapeDtypeStruct((n_tiles, N_EXPERTS, 1), F32)],
        scratch_shapes=[pltpu.VMEM((N_EXPERTS, 1), F32)],
        compiler_params=_cparams("arbitrary"),
        name="outproj_router",
    )(a_p, a_s, b_p, b_s, x_p, x_s, mod_p, mod_s, norm_g, w_out_b,
      jnp.pad(w_router, ((0, 0), (0, LANES - N_EXPERTS))).astype(BF16),
      jnp.pad(b_router.reshape(1, -1), ((0, 0), (0, LANES - N_EXPERTS))))


def _dispatch_kernel(pe_ref, cnt_ref, dest_ref, h_ref, xg_ref, zero_ref, sem, zsem, *, bm):
    tm = h_ref.shape[0]
    n_blocks = xg_ref.shape[0] // bm

    def zero_block(row0):
        return pltpu.make_async_copy(zero_ref, xg_ref.at[pl.ds(pl.multiple_of(row0, bm), bm), :], zsem)

    @pl.when(pl.program_id(0) == 0)
    def _():
        zero_ref[...] = jnp.zeros(zero_ref.shape, F32)
        first_unused = pe_ref[N_EXPERTS - 1] // bm
        for e in range(N_EXPERTS):
            @pl.when(cnt_ref[e] > 0)
            def _():
                zero_block(pe_ref[e] - bm).start()

        def start_unused(b, carry):
            zero_block(b * bm).start()
            return carry

        lax.fori_loop(first_unused, n_blocks, start_unused, 0)
        for e in range(N_EXPERTS):
            @pl.when(cnt_ref[e] > 0)
            def _():
                zero_block(pe_ref[e] - bm).wait()

        def wait_unused(b, carry):
            zero_block(b * bm).wait()
            return carry

        lax.fori_loop(first_unused, n_blocks, wait_unused, 0)

    def body(t, carry):
        for j in range(TOP_K):
            d = dest_ref[j * tm + t]
            pltpu.make_async_copy(h_ref.at[pl.ds(t, 1), :], xg_ref.at[pl.ds(d, 1), :], sem).start()
        return carry

    lax.fori_loop(0, tm, body, 0)
    n = tm * TOP_K
    pltpu.make_async_copy(xg_ref.at[pl.ds(0, n), :], xg_ref.at[pl.ds(0, n), :], sem).wait()


def _dispatch(pad_end, cnt, dest_flat, h2, n_blocks):
    tm = ROW_TILE
    bm = EXPERT_BLOCK
    rows = h2.shape[0]
    grid_spec = pltpu.PrefetchScalarGridSpec(
        num_scalar_prefetch=2,
        grid=(rows // tm,),
        in_specs=[pl.BlockSpec((tm * TOP_K,), lambda i, pe, cn: (i,), memory_space=pltpu.SMEM),
                  pl.BlockSpec((tm, ROW_WIDTH), lambda i, pe, cn: (i, 0))],
        out_specs=pl.BlockSpec(memory_space=pl.ANY),
        scratch_shapes=[pltpu.VMEM((bm, ROW_WIDTH), F32), pltpu.SemaphoreType.DMA(()), pltpu.SemaphoreType.DMA(())])
    return pl.pallas_call(
        functools.partial(_dispatch_kernel, bm=bm),
        grid_spec=grid_spec,
        out_shape=jax.ShapeDtypeStruct((n_blocks * bm, ROW_WIDTH), F32),
        compiler_params=_cparams("arbitrary"),
        name="moe_dispatch",
    )(pad_end, cnt, dest_flat, h2)


def _expert_kernel(be_ref, nu_ref, x_ref, wgu_ref, bgu_ref, wd_ref, bd_ref, y_ref, wgu_b, wd_b):
    i = pl.program_id(0)
    used = i < nu_ref[0]
    fresh = jnp.logical_or(i == 0, be_ref[i] != be_ref[jnp.maximum(i - 1, 0)])

    @pl.when(jnp.logical_and(used, fresh))
    def _():
        wgu_b[...] = wgu_ref[0].astype(BF16)
        wd_b[...] = wd_ref[0].astype(BF16)

    @pl.when(jnp.logical_not(used))
    def _():
        y_ref[...] = jnp.zeros(y_ref.shape, F32)

    @pl.when(used)
    def _():
        x = x_ref[:, 0:D_MODEL].astype(BF16)
        gu = jnp.dot(x, wgu_b[...], preferred_element_type=F32) + bgu_ref[0]
        glu = jnp.minimum(gu[:, :D_EXPERT], SWIGLU_LIMIT)
        lin = jnp.clip(gu[:, D_EXPERT:], -SWIGLU_LIMIT, SWIGLU_LIMIT)
        hid = glu * jax.nn.sigmoid(SWIGLU_ALPHA * glu) * (lin + 1.0)
        y = jnp.dot(hid.astype(BF16), wd_b[...], preferred_element_type=F32) + bd_ref[0]
        meta = x_ref[:, D_MODEL:]
        me = be_ref[i].astype(F32)
        w = jnp.zeros((x_ref.shape[0], 1), F32)
        for j in range(TOP_K):
            w = w + jnp.where(meta[:, TOP_K + j:TOP_K + j + 1] == me, meta[:, j:j + 1], 0.0)
        y_ref[...] = y * w


def _experts(block_e, n_used, xg, w_gu_b, b_gu, w_down_b, b_down):
    bm = EXPERT_BLOCK
    n_blocks = xg.shape[0] // bm
    grid_spec = pltpu.PrefetchScalarGridSpec(
        num_scalar_prefetch=2,
        grid=(n_blocks,),
        in_specs=[pl.BlockSpec((bm, ROW_WIDTH), lambda i, be, nu: (jnp.minimum(i, nu[0] - 1), 0)),
                  pl.BlockSpec((1, D_MODEL, 2 * D_EXPERT), lambda i, be, nu: (be[i], 0, 0)),
                  pl.BlockSpec((1, 1, 2 * D_EXPERT), lambda i, be, nu: (be[i], 0, 0)),
                  pl.BlockSpec((1, D_EXPERT, D_MODEL), lambda i, be, nu: (be[i], 0, 0)),
                  pl.BlockSpec((1, 1, D_MODEL), lambda i, be, nu: (be[i], 0, 0))],
        out_specs=pl.BlockSpec((bm, D_MODEL), lambda i, be, nu: (i, 0)),
        scratch_shapes=[pltpu.VMEM((D_MODEL, 2 * D_EXPERT), BF16), pltpu.VMEM((D_EXPERT, D_MODEL), BF16)])
    return pl.pallas_call(
        _expert_kernel,
        grid_spec=grid_spec,
        out_shape=jax.ShapeDtypeStruct((xg.shape[0], D_MODEL), F32),
        compiler_params=_cparams("arbitrary"),
        name="moe_experts",
    )(block_e, n_used, xg, w_gu_b, b_gu.reshape(N_EXPERTS, 1, -1), w_down_b, b_down.reshape(N_EXPERTS, 1, -1))


def _combine_kernel(row0_ref, col_ref, x1_ref, mp_ref, ms_ref, ng_ref, yg_ref,
                    yp_ref, ys_ref, rows_ref, g_ref, sem, *, n_prompt_tiles):
    i = pl.program_id(0)
    n_steps = pl.num_programs(0)
    tm = x1_ref.shape[0]
    is_p = i < n_prompt_tiles
    win = COMBINE_WINDOW
    n_win_rows = COMBINE_SLOTS * win
    slot = i % 2

    def fetch_windows(tile, buf):
        for s in range(COMBINE_SLOTS):
            row0 = pl.multiple_of(row0_ref[tile * COMBINE_SLOTS + s], 8)
            pltpu.make_async_copy(yg_ref.at[pl.ds(row0, win), :], rows_ref.at[buf, pl.ds(s * win, win), :],
                                  sem.at[buf]).start()

    @pl.when(i == 0)
    def _():
        fetch_windows(0, 0)

    @pl.when(i + 1 < n_steps)
    def _():
        fetch_windows(i + 1, 1 - slot)

    pltpu.make_async_copy(yg_ref.at[pl.ds(0, n_win_rows), :], rows_ref.at[slot], sem.at[slot]).wait()
    col = col_ref[...]
    for c in range(n_win_rows // LANES):
        lane = lax.broadcasted_iota(jnp.int32, (tm, LANES), 1) + c * LANES
        g = jnp.zeros((tm, LANES), F32)
        for j in range(TOP_K):
            g = jnp.where(lane == col[:, j:j + 1], 1.0, g)
        g_ref[:, c * LANES:(c + 1) * LANES] = g.astype(BF16)
    f = jnp.dot(g_ref[...], rows_ref[slot].astype(BF16), preferred_element_type=F32)

    g2 = jnp.where(is_p, mp_ref[0, 5:6, :], ms_ref[5])
    y = x1_ref[...] + g2 * _rms(f, ng_ref[3:4, :])

    @pl.when(is_p)
    def _():
        yp_ref[...] = y

    @pl.when(jnp.logical_not(is_p))
    def _():
        ys_ref[...] = y


def _combine(row0, col, x1, mod_p, mod_s, norm_g, yg, *, n_prompt_rows, n_sample_rows):
    tm = ROW_TILE
    npt = n_prompt_rows // tm
    assert n_sample_rows == tm and (COMBINE_SLOTS * COMBINE_WINDOW) % LANES == 0
    n_tiles = npt + 1
    tiles_per_batch = npt // mod_p.shape[0]
    last_p = npt - 1
    grid_spec = pltpu.PrefetchScalarGridSpec(
        num_scalar_prefetch=1,
        grid=(n_tiles,),
        in_specs=[pl.BlockSpec((tm, TOP_K), lambda i, r0: (i, 0)),
                  pl.BlockSpec((tm, D_MODEL), lambda i, r0: (i, 0)),
                  pl.BlockSpec((1, 6, D_MODEL), lambda i, r0: (jnp.minimum(i, last_p) // tiles_per_batch, 0, 0)),
                  pl.BlockSpec((6, tm, D_MODEL), lambda i, r0: (0, 0, 0)),
                  pl.BlockSpec((4, D_MODEL), lambda i, r0: (0, 0)),
                  pl.BlockSpec(memory_space=pl.ANY)],
        out_specs=[pl.BlockSpec((tm, D_MODEL), lambda i, r0: (jnp.minimum(i, last_p), 0)),
                   pl.BlockSpec((tm, D_MODEL), lambda i, r0: (0, 0))],
        scratch_shapes=[pltpu.VMEM((2, COMBINE_SLOTS * COMBINE_WINDOW, D_MODEL), F32),
                        pltpu.VMEM((tm, COMBINE_SLOTS * COMBINE_WINDOW), BF16),
                        pltpu.SemaphoreType.DMA((2,))])
    return pl.pallas_call(
        functools.partial(_combine_kernel, n_prompt_tiles=npt),
        grid_spec=grid_spec,
        out_shape=[jax.ShapeDtypeStruct((n_prompt_rows, D_MODEL), F32),
                   jax.ShapeDtypeStruct((n_sample_rows, D_MODEL), F32)],
        compiler_params=_cparams("arbitrary"),
        name="moe_combine",
    )(row0, col, x1, mod_p, mod_s, norm_g, yg)


def _t5_bucket_np(dist):
    n = np.maximum(dist, 0)
    max_exact = NUM_BUCKETS // 2
    nf = np.maximum(n, 1).astype(np.float64)
    large = max_exact + (np.log(nf / max_exact) / math.log(MAX_DISTANCE / max_exact)
                         * (NUM_BUCKETS - max_exact)).astype(np.int32)
    large = np.minimum(large, NUM_BUCKETS - 1)
    return np.where(n < max_exact, n, large).astype(np.int32)


def _bias_from_dist(rel_bias, dist):
    onehot = jax.nn.one_hot(_t5_bucket_np(dist).reshape(-1), NUM_BUCKETS, dtype=F32)
    shifted = (rel_bias - rel_bias[NUM_BUCKETS - 1]).reshape(NUM_BUCKETS, -1)
    out = jnp.dot(onehot, shifted, precision=lax.Precision.HIGHEST).reshape(dist.shape + rel_bias.shape[1:])
    return jnp.where(jnp.asarray(dist >= 0)[..., None, None], out, -jnp.inf).astype(F32)


def _head_masked(t):
    h = t.shape[0]
    same = np.eye(h, dtype=bool)[:, None, None, None, :]
    full = jnp.where(same, t[..., None], -jnp.inf)
    return full.reshape(h * t.shape[1] * t.shape[2], t.shape[3] * h)


def kernel(x_prompt, x_sample, cache_k, cache_v, page_table, c_prompt, c_sample, w_ada, b_ada, norm_g, w_in,
           w_out, ln_v_g, ln_v_b, w_spatial, b_spatial, lam_params, subln_g, rel_bias, w_router, b_router,
           w_gu, b_gu, w_down, b_down):
    batch, seq, d = x_prompt.shape
    nb, ds = x_sample.shape[:2]
    n_pages = page_table.shape[1]
    past = n_pages * PAGE_SIZE
    rows_p = batch * seq
    rows_s = nb * ds
    assert rows_s == ROW_TILE and d == D_MODEL

    mod = _ada(jnp.concatenate([c_prompt, c_sample], axis=0), w_ada[0], b_ada[0])
    mod_p = mod[:batch].reshape(batch, 6, D_MODEL)
    mod_s = jnp.transpose(jnp.repeat(mod[batch:], ds, axis=0).reshape(rows_s, 6, D_MODEL), (1, 0, 2))

    ng = norm_g[0]
    w_in_b = w_in[0].astype(BF16)
    w_out_b = w_out[0].astype(BF16)
    lng = ln_v_g[0].reshape(1, A_WIDTH)
    lnb = ln_v_b[0].reshape(1, A_WIDTH)
    grp = np.arange(A_WIDTH) // A_HEAD
    avg = jnp.asarray((grp[:, None] == grp[None, :]).astype(np.float32) / A_HEAD, BF16)
    ws_p = jnp.tril(w_spatial[0]).astype(BF16)
    bs_p = jnp.repeat(b_spatial[0].T, A_HEAD, axis=1)
    w_small = jnp.tril(w_spatial[0][:, :ds, :ds])
    ws_s = jnp.einsum('ab,gts->gatbs', jnp.eye(nb, dtype=F32), w_small).reshape(A_GROUPS, rows_s, rows_s).astype(BF16)
    bs_s = jnp.tile(jnp.repeat(b_spatial[0][:, :ds].T, A_HEAD, axis=1), (nb, 1))

    xp2 = x_prompt.reshape(rows_p, D_MODEL)
    xs2 = x_sample.reshape(rows_s, D_MODEL)
    assert ATTN_TILE == ROW_TILE
    a_p, va_p, q_p, k_p, v_p, kb_p, vt_p = _inproj(
        xp2, mod_p[:, 0:1], mod_p[:, 1:2], ng[0:1], w_in_b, lng, lnb, avg, ws_p, bs_p,
        tiles_per_mod=seq // ROW_TILE, chunk=CHUNK, q_dtype=BF16, q_scale=LOG2E / math.sqrt(B_HEAD), va_rows=CHUNK)
    a_s, va_s, q_s, k_s, v_s, _, _ = _inproj(
        xs2, mod_s[0:1], mod_s[1:2], ng[0:1], w_in_b, lng, lnb, avg, ws_s, bs_s,
        tiles_per_mod=1, chunk=rows_s, q_dtype=F32, q_scale=1.0, va_rows=rows_s)

    ti = np.arange(ATTN_TILE)
    dist_p = np.stack([ti[:, None] - ti[None, :], ti[:, None] - ti[None, :] + ATTN_TILE])
    bias_p = jnp.transpose(_bias_from_dist(rel_bias, dist_p), (3, 0, 2, 4, 1))
    bias_p = bias_p.reshape(B_HEADS, 2, ATTN_TILE, 2 * ATTN_TILE) * LOG2E
    bias_p = jnp.concatenate([bias_p, jnp.zeros_like(bias_p[:, :1])], axis=1)
    b_p = _attn_p(q_p, kb_p, vt_p, bias_p, lam_params[0], subln_g, batch=batch, seq=seq)

    qi = np.arange(ds)
    ki = np.arange(PAGE_SIZE)
    dist_l = PAGE_SIZE + qi[:, None] - ki[None, :]
    kn_i = np.arange(PAGE_SIZE // B_HEADS)
    dist_n = np.where(kn_i[None, :] < ds, qi[:, None] - kn_i[None, :], -1)
    to_rows = lambda t: jnp.transpose(t, (2, 3, 0, 1))
    bias_l = _head_masked(to_rows(_bias_from_dist(rel_bias, dist_l)))
    bias_n = _head_masked(to_rows(_bias_from_dist(rel_bias, dist_n)))
    maskb = _head_masked(jnp.zeros((B_HEADS, 2, ds, PAGE_SIZE), F32))
    pad_keys = lambda t: jnp.pad(t.reshape(nb, ds * B_HEADS, B_VDIM), ((0, 0), (0, PAGE_SIZE - ds * B_HEADS), (0, 0)))
    n_phys = cache_k.shape[0]
    b_s = _attn_s(page_table, q_s, pad_keys(k_s), pad_keys(v_s), maskb, bias_l, bias_n, lam_params[0], subln_g,
                  cache_k.reshape(n_phys * PAGE_SIZE * B_HEADS, B_VDIM),
                  cache_v.reshape(n_phys * PAGE_SIZE * B_HEADS, B_VDIM), ds=ds)

    x1, h2, top_i, top_r, counts, tile_base = _outproj(
        a_p, a_s, b_p, b_s, xp2, xs2, mod_p, mod_s, ng, w_out_b, w_router[0], b_router[0])

    bm = EXPERT_BLOCK
    rows = rows_p + rows_s
    n_blocks = rows * TOP_K // bm + N_EXPERTS + 1
    cnt = counts[:, 0].astype(jnp.int32)
    padded = (cnt + bm - 1) // bm * bm
    pad_end = jnp.cumsum(padded)
    pad_start = pad_end - padded
    experts = jnp.arange(N_EXPERTS, dtype=jnp.int32)
    chosen = top_i[..., None] == experts
    dest = jnp.sum(jnp.where(chosen, pad_start, 0), axis=-1) + top_r
    n_tiles = rows // ROW_TILE
    win = COMBINE_WINDOW
    assert (ROW_TILE * TOP_K + N_EXPERTS * 7) // win + N_EXPERTS <= COMBINE_SLOTS
    base = tile_base[:, :, 0].astype(jnp.int32)
    first = pad_start[None, :] + base
    win0 = first // 8 * 8
    sent = jnp.concatenate([base[1:], cnt[None, :]], axis=0) - base
    n_win = jnp.where(sent > 0, (first - win0 + sent + win - 1) // win, 0)
    slot_end = jnp.cumsum(n_win, axis=1)
    slot_start = slot_end - n_win
    slots = jnp.arange(COMBINE_SLOTS, dtype=jnp.int32)
    slot_e = jnp.minimum(jnp.sum((slot_end[:, None, :] <= slots[None, :, None]).astype(jnp.int32), axis=2),
                         N_EXPERTS - 1)
    pick = slot_e[..., None] == experts
    slot_row0 = (jnp.sum(jnp.where(pick, win0[:, None, :], 0), axis=2)
                 + win * (slots[None, :] - jnp.sum(jnp.where(pick, slot_start[:, None, :], 0), axis=2)))
    slot_row0 = jnp.where(slots[None, :] < slot_end[:, -1:], slot_row0, 0)
    local = dest - jnp.sum(jnp.where(chosen, jnp.repeat(win0, ROW_TILE, axis=0), 0), axis=-1)
    col = jnp.sum(jnp.where(chosen, jnp.repeat(slot_start, ROW_TILE, axis=0), 0), axis=-1) * win + local
    dest = jnp.transpose(dest.reshape(TOP_K, n_tiles, ROW_TILE), (1, 0, 2)).reshape(-1)
    blk_start = jnp.arange(n_blocks, dtype=jnp.int32) * bm
    block_e = jnp.minimum(jnp.sum((pad_end[None, :] <= blk_start[:, None]).astype(jnp.int32), axis=1),
                          N_EXPERTS - 1)
    n_used = (pad_end[-1:] // bm).astype(jnp.int32)

    xg = _dispatch(pad_end.astype(jnp.int32), cnt, dest, h2, n_blocks)
    yg = _experts(block_e, n_used, xg, w_gu[0], b_gu[0], w_down[0], b_down[0])
    y_p, y_s = _combine(slot_row0.reshape(-1), col.T, x1, mod_p, mod_s, ng, yg,
                        n_prompt_rows=rows_p, n_sample_rows=rows_s)

    return (y_p.reshape(batch, seq, D_MODEL),
            y_s.reshape(nb, ds, D_MODEL),
            k_p.reshape(batch, seq, 1, B_HEADS, B_VDIM),
            v_p.reshape(batch, seq, 1, B_HEADS, B_VDIM),
            k_s.reshape(nb, ds, 1, B_HEADS, B_VDIM),
            v_s.reshape(nb, ds, 1, B_HEADS, B_VDIM),
            va_p.reshape(batch, CHUNK, 1, A_WIDTH),
            va_s.reshape(nb, ds, 1, A_WIDTH))
```

```python
import functools
import math

import numpy as np
import jax
import jax.numpy as jnp
from jax import lax
from jax.experimental import pallas as pl
from jax.experimental.pallas import tpu as pltpu

F32 = jnp.float32
BF16 = jnp.bfloat16

D_MODEL = 1024
A_WIDTH = 512
A_HEAD = 64
A_GROUPS = 8
CHUNK = 128
B_WIDTH = 512
B_HEAD = 64
B_VDIM = 128
B_HEADS = 4
IN_WIDTH = 2 * A_WIDTH + 3 * B_WIDTH
NUM_BUCKETS = 32
MAX_DISTANCE = 128
PAGE_SIZE = 128
N_EXPERTS = 32
TOP_K = 4
D_EXPERT = 1024
SWIGLU_LIMIT = 7.0
SWIGLU_ALPHA = 1.702
NORM_EPS = 1e-6
LAM_INIT = 0.8 - 0.6 * math.exp(-0.3 * 0)

ROW_TILE = 256
ATTN_TILE = 512
PAGES_PER_STEP = 16
LOG2E = math.log2(math.e)
LANES = 128
MXU_WIDTH = 256
ROW_WIDTH = D_MODEL + LANES
EXPERT_BLOCK = 256
COMBINE_ALIGN = 16
COMBINE_WINDOW = 48
COMBINE_SLOTS = 64
VMEM_LIMIT = 56 * 1024 * 1024


def _cparams(*sem):
    return pltpu.CompilerParams(dimension_semantics=sem, vmem_limit_bytes=VMEM_LIMIT)


def _rms(x, g):
    return x * lax.rsqrt(jnp.mean(x * x, axis=-1, keepdims=True) + NORM_EPS) * g


def _ada_kernel(c_ref, w_ref, b_ref, o_ref):
    s = jax.nn.silu(c_ref[...]).astype(BF16)
    o_ref[...] = jnp.dot(s, w_ref[...].astype(BF16), preferred_element_type=F32) + b_ref[...]


def _ada(c_all, w_ada, b_ada):
    n = c_all.shape[0]
    tn = 1024
    return pl.pallas_call(
        _ada_kernel,
        grid=(6 * D_MODEL // tn,),
        in_specs=[pl.BlockSpec((n, D_MODEL), lambda j: (0, 0)),
                  pl.BlockSpec((D_MODEL, tn), lambda j: (0, j)),
                  pl.BlockSpec((1, tn), lambda j: (0, j))],
        out_specs=pl.BlockSpec((n, tn), lambda j: (0, j)),
        out_shape=jax.ShapeDtypeStruct((n, 6 * D_MODEL), F32),
        compiler_params=_cparams("arbitrary"),
        name="ada",
    )(c_all, w_ada, b_ada.reshape(1, -1))


def _inproj_kernel(x_ref, sh_ref, sc_ref, ng_ref, w_ref, lng_ref, lnb_ref, avg_ref, ws_ref, bs_ref,
                   a_ref, va_ref, q_ref, k_ref, v_ref, kb_ref, vt_ref, *, chunk, q_scale):
    tm = x_ref.shape[0]
    va_rows = va_ref.shape[0]
    h = (_rms(x_ref[...], ng_ref[...]) * (1.0 + sc_ref[0]) + sh_ref[0]).astype(BF16)

    u = jax.nn.gelu(jnp.dot(h, w_ref[:, 0:A_WIDTH], preferred_element_type=F32))
    gv = jax.nn.gelu(jnp.dot(h, w_ref[:, A_WIDTH:2 * A_WIDTH], preferred_element_type=F32))

    avg = avg_ref[...]

    def group_mean(t):
        hi = t.astype(BF16)
        lo = (t - hi.astype(F32)).astype(BF16)
        return (jnp.dot(hi, avg, preferred_element_type=F32)
                + jnp.dot(lo, avg, preferred_element_type=F32))

    xc = gv - group_mean(gv)
    va = xc * lax.rsqrt(group_mean(xc * xc) + NORM_EPS) * lng_ref[...] + lnb_ref[...]
    va_ref[...] = va[tm - va_rows:, :]

    vab = va.astype(BF16)
    lane = lax.broadcasted_iota(jnp.int32, (chunk, 128), 1)
    for c in range(tm // chunk):
        r0 = c * chunk
        for pair in range(A_GROUPS // 2):
            c0 = pair * 128
            vp = vab[r0:r0 + chunk, c0:c0 + 128]
            lo_half = jnp.where(lane < A_HEAD, vp, jnp.zeros_like(vp))
            hi_half = jnp.where(lane >= A_HEAD, vp, jnp.zeros_like(vp))
            s = (jnp.dot(ws_ref[2 * pair], lo_half, preferred_element_type=F32)
                 + jnp.dot(ws_ref[2 * pair + 1], hi_half, preferred_element_type=F32))
            a = u[r0:r0 + chunk, c0:c0 + 128] * (s + bs_ref[:, c0:c0 + 128])
            a_ref[r0:r0 + chunk, c0:c0 + 128] = a.astype(a_ref.dtype)

    q = jnp.dot(h, w_ref[:, 2 * A_WIDTH:2 * A_WIDTH + B_WIDTH], preferred_element_type=F32)
    q_ref[...] = (q * q_scale).astype(q_ref.dtype)
    k = jnp.dot(h, w_ref[:, 2 * A_WIDTH + B_WIDTH:2 * A_WIDTH + 2 * B_WIDTH], preferred_element_type=F32)
    kb_ref[...] = k.astype(BF16)
    v = jnp.dot(h, w_ref[:, 2 * A_WIDTH + 2 * B_WIDTH:IN_WIDTH], preferred_element_type=F32)
    for hh in range(B_HEADS):
        c0 = hh * B_VDIM
        k_ref[pl.ds(hh, tm, stride=B_HEADS), :] = k[:, c0:c0 + B_VDIM]
        v_ref[pl.ds(hh, tm, stride=B_HEADS), :] = v[:, c0:c0 + B_VDIM]
        vt_ref[0, hh, 0] = v[:, c0:c0 + B_VDIM].T.astype(BF16)


def _inproj(x2, sh, sc, ng, w_in_b, lng, lnb, avg, ws, bs, *, tiles_per_mod, chunk, q_dtype, q_scale, va_rows):
    rows = x2.shape[0]
    tm = ROW_TILE
    n_tiles = rows // tm
    mod_rows = sh.shape[1]
    row = lambda i: (i, 0)
    const2 = lambda i: (0, 0)
    mod_map = lambda i: (i // tiles_per_mod, 0, 0)
    out_w = lambda w, dt: jax.ShapeDtypeStruct((rows, w), dt)
    head_rows = jax.ShapeDtypeStruct((rows * B_HEADS, B_VDIM), F32)
    head_rows_spec = pl.BlockSpec((tm * B_HEADS, B_VDIM), row)
    vt_shape = jax.ShapeDtypeStruct((n_tiles // tiles_per_mod, B_HEADS, tiles_per_mod, B_VDIM, tm), BF16)
    vt_spec = pl.BlockSpec((1, B_HEADS, 1, B_VDIM, tm), lambda i: (i // tiles_per_mod, 0, i % tiles_per_mod, 0, 0))
    return pl.pallas_call(
        functools.partial(_inproj_kernel, chunk=chunk, q_scale=q_scale),
        grid=(rows // tm,),
        in_specs=[pl.BlockSpec((tm, D_MODEL), row),
                  pl.BlockSpec((1, mod_rows, D_MODEL), mod_map),
                  pl.BlockSpec((1, mod_rows, D_MODEL), mod_map),
                  pl.BlockSpec((1, D_MODEL), const2),
                  pl.BlockSpec((D_MODEL, IN_WIDTH), const2),
                  pl.BlockSpec((1, A_WIDTH), const2),
                  pl.BlockSpec((1, A_WIDTH), const2),
                  pl.BlockSpec((A_WIDTH, A_WIDTH), const2),
                  pl.BlockSpec((A_GROUPS, chunk, chunk), lambda i: (0, 0, 0)),
                  pl.BlockSpec((chunk, A_WIDTH), const2)],
        out_specs=[pl.BlockSpec((tm, A_WIDTH), row),
                   pl.BlockSpec((va_rows, A_WIDTH), lambda i: (i // tiles_per_mod, 0)),
                   pl.BlockSpec((tm, B_WIDTH), row), head_rows_spec, head_rows_spec,
                   pl.BlockSpec((tm, B_WIDTH), row), vt_spec],
        out_shape=[out_w(A_WIDTH, BF16),
                   jax.ShapeDtypeStruct((n_tiles // tiles_per_mod * va_rows, A_WIDTH), F32),
                   out_w(B_WIDTH, q_dtype), head_rows, head_rows, out_w(B_WIDTH, BF16), vt_shape],
        compiler_params=_cparams("arbitrary"),
        name="inproj",
    )(x2, sh, sc, ng, w_in_b, lng, lnb, avg, ws, bs)


def _stack_halves(q):
    lane = lax.broadcasted_iota(jnp.int32, q.shape, 1)
    zero = jnp.zeros_like(q)
    return jnp.concatenate([jnp.where(lane < B_HEAD, q, zero), jnp.where(lane >= B_HEAD, q, zero)], axis=0)


def _lambda(lam_ref):
    lp = lam_ref[...]
    return (jnp.exp(jnp.sum(lp[0:1] * lp[1:2], axis=-1, keepdims=True))
            - jnp.exp(jnp.sum(lp[2:3] * lp[3:4], axis=-1, keepdims=True)) + LAM_INIT)


def _diff_finish(acc, l, n, lam, g):
    o = acc[:n] / l[:n] - lam * (acc[n:] / l[n:])
    return _rms(o, g) * (1.0 - LAM_INIT)


def _attn_p_kernel(q_ref, k_ref, vt_ref, bias_ref, lam_ref, g_ref, o_ref,
                   acc_ref, m_ref, l_ref, s_ref, tmax_ref, p_ref, alpha_ref, *, tq, tk):
    qi = pl.program_id(2)
    q = q_ref[...]
    lane = lax.broadcasted_iota(jnp.int32, q.shape, 1)
    zero = jnp.zeros_like(q)
    q_half = (jnp.where(lane < B_HEAD, q, zero), jnp.where(lane >= B_HEAD, q, zero))
    m_ref[...] = jnp.full(m_ref.shape, -jnp.inf, F32)
    l_ref[...] = jnp.zeros(l_ref.shape, F32)
    acc_ref[...] = jnp.zeros(acc_ref.shape, F32)
    p_ref[1] = jnp.zeros(p_ref.shape[1:], BF16)
    alpha_ref[1] = jnp.ones(alpha_ref.shape[1:], F32)

    def scores(j):
        k = k_ref[pl.ds(pl.multiple_of(j * tk, tk), tk), :]
        return [lax.dot_general(k, q_half[c], (((1,), (1,)), ((), ())), preferred_element_type=F32)
                for c in range(2)]

    def keep_scores(j, kind, ss):
        slot = j % 2
        for c in range(2):
            s = ss[c]
            if kind is not None:
                s = s + bias_ref[0, kind, :, c * tq:(c + 1) * tq]
            s_ref[slot, c] = s
            tmax_ref[slot, c] = jnp.max(s, axis=0, keepdims=True)

    def softmax(j):
        slot = j % 2
        for c in range(2):
            m_old = m_ref[c]
            m_new = jnp.maximum(m_old, tmax_ref[slot, c])
            alpha = jnp.exp2(m_old - m_new)
            p = jnp.exp2(s_ref[slot, c] - m_new)
            l_ref[c] = alpha * l_ref[c] + jnp.sum(p, axis=0, keepdims=True)
            m_ref[c] = m_new
            p_ref[slot, c] = p.astype(BF16)
            alpha_ref[slot, c] = alpha

    def pv(j):
        slot = j % 2
        pieces = tk // vt_ref.shape[-1]
        first = jnp.maximum(j, 0) * pieces
        for c in range(2):
            add = None
            for r in range(pieces):
                w = vt_ref.shape[-1]
                t = jnp.dot(vt_ref[0, 0, first + r], p_ref[slot, c, r * w:(r + 1) * w, :],
                            preferred_element_type=F32)
                add = t if add is None else add + t
            acc_ref[c] = alpha_ref[slot, c] * acc_ref[c] + add

    def far_body(j, carry):
        ss = scores(j + 1)
        pv(j - 1)
        softmax(j)
        keep_scores(j + 1, None, ss)
        return carry

    def near_body(j, carry):
        nxt = jnp.minimum(j + 1, qi)
        ss = scores(nxt)
        pv(j - 1)
        softmax(j)
        keep_scores(nxt, qi - nxt, ss)
        return carry

    keep_scores(0, jnp.minimum(qi, 2), scores(0))
    n_far_fetch = jnp.maximum(qi - 2, 0)
    lax.fori_loop(0, n_far_fetch, far_body, 0)
    lax.fori_loop(n_far_fetch, qi + 1, near_body, 0)
    pv(qi)

    o = acc_ref[0] / l_ref[0] - _lambda(lam_ref) * (acc_ref[1] / l_ref[1])
    y = o * lax.rsqrt(jnp.mean(o * o, axis=0, keepdims=True) + NORM_EPS) * (1.0 - LAM_INIT)
    o_ref[...] = (y.T * g_ref[...]).astype(o_ref.dtype)


def _attn_p(qb, kb, vt, bias, lam_params, subln_g, *, batch, seq):
    tq = tk = ATTN_TILE
    nq = seq // tq
    return pl.pallas_call(
        functools.partial(_attn_p_kernel, tq=tq, tk=tk),
        grid=(batch, B_HEADS, nq),
        in_specs=[pl.BlockSpec((tq, B_VDIM), lambda b, h, i: (b * nq + i, h)),
                  pl.BlockSpec((seq, B_VDIM), lambda b, h, i: (b, h)),
                  pl.BlockSpec((1, 1, seq // ROW_TILE, B_VDIM, ROW_TILE), lambda b, h, i: (b, h, 0, 0, 0)),
                  pl.BlockSpec((1, 3, tk, 2 * tq), lambda b, h, i: (h, 0, 0, 0)),
                  pl.BlockSpec((4, B_HEAD), lambda b, h, i: (0, 0)),
                  pl.BlockSpec((1, B_VDIM), lambda b, h, i: (0, 0))],
        out_specs=pl.BlockSpec((tq, B_VDIM), lambda b, h, i: (b * nq + i, h)),
        out_shape=jax.ShapeDtypeStruct((batch * seq, B_WIDTH), BF16),
        scratch_shapes=[pltpu.VMEM((2, B_VDIM, tq), F32),
                        pltpu.VMEM((2, 1, tq), F32),
                        pltpu.VMEM((2, 1, tq), F32),
                        pltpu.VMEM((2, 2, tk, tq), F32),
                        pltpu.VMEM((2, 2, 1, tq), F32),
                        pltpu.VMEM((2, 2, tk, tq), BF16),
                        pltpu.VMEM((2, 2, 1, tq), F32)],
        compiler_params=_cparams("arbitrary", "arbitrary", "arbitrary"),
        name="attn_prompt",
    )(qb, kb, vt, bias, lam_params, subln_g)


def _attn_s_kernel(pt_ref, q_ref, kn_ref, vn_ref, maskb_ref, biasl_ref, biasn_ref, lam_ref, g_ref, *rest,
                   npages, ds):
    del pt_ref
    k_refs = rest[:npages]
    v_refs = rest[npages:2 * npages]
    o_ref = rest[2 * npages]
    acc_ref, m_ref, l_ref = rest[2 * npages + 1:]
    g = pl.program_id(1)
    last = g == pl.num_programs(1) - 1

    @pl.when(g == 0)
    def _():
        m_ref[...] = jnp.full(m_ref.shape, -jnp.inf, F32)
        l_ref[...] = jnp.zeros(l_ref.shape, F32)
        acc_ref[...] = jnp.zeros(acc_ref.shape, F32)

    q = q_ref[...] * (1.0 / math.sqrt(B_HEAD))
    qall = jnp.concatenate([_stack_halves(q[:, h * B_VDIM:(h + 1) * B_VDIM]) for h in range(B_HEADS)],
                           axis=0).astype(BF16)

    def process(k_blocks, v_blocks, biases):
        s = jnp.concatenate(
            [lax.dot_general(qall, kb.astype(BF16), (((1,), (1,)), ((), ())), preferred_element_type=F32) + bb
             for kb, bb in zip(k_blocks, biases)], axis=1)
        m_old = m_ref[...]
        m_new = jnp.maximum(m_old, jnp.max(s, axis=-1, keepdims=True))
        alpha = jnp.exp(m_old - m_new)
        pr = jnp.exp(s - m_new)
        l_ref[...] = alpha * l_ref[...] + jnp.sum(pr, axis=-1, keepdims=True)
        prb = pr.astype(BF16)
        pv = None
        off = 0
        for vb in v_blocks:
            n = vb.shape[0]
            t = jnp.dot(prb[:, off:off + n], vb.astype(BF16), preferred_element_type=F32)
            pv = t if pv is None else pv + t
            off += n
        acc_ref[...] = alpha * acc_ref[...] + pv
        m_ref[...] = m_new

    maskb = maskb_ref[...]
    newest = jnp.where(last, biasl_ref[...], maskb)
    process([k_refs[p][...] for p in range(npages)], [v_refs[p][...] for p in range(npages)],
            [maskb] * (npages - 1) + [newest])

    @pl.when(last)
    def _():
        process([kn_ref[0]], [vn_ref[0]], [biasn_ref[...]])
        lam = _lambda(lam_ref)
        acc = acc_ref[...]
        l = l_ref[...]
        for h in range(B_HEADS):
            r0 = h * 2 * ds
            o_ref[:, h * B_VDIM:(h + 1) * B_VDIM] = _diff_finish(
                acc[r0:r0 + 2 * ds], l[r0:r0 + 2 * ds], ds, lam, g_ref[...])


def _attn_s(page_table, qs, kn, vn, maskb, bias_last, bias_new, lam_params, subln_g, cache_k2, cache_v2, *, ds):
    nb, n_pages = page_table.shape
    npg = PAGES_PER_STEP
    steps = n_pages // npg
    page_rows = PAGE_SIZE * B_HEADS
    nrow = B_HEADS * 2 * ds

    def page_spec(p):
        return pl.BlockSpec((page_rows, B_VDIM), lambda b, g, pt, p=p: (pt[b, g * npg + p], 0))

    const2 = lambda b, g, pt: (0, 0)
    grid_spec = pltpu.PrefetchScalarGridSpec(
        num_scalar_prefetch=1,
        grid=(nb, steps),
        in_specs=[pl.BlockSpec((ds, B_WIDTH), lambda b, g, pt: (b, 0)),
                  pl.BlockSpec((1, PAGE_SIZE, B_VDIM), lambda b, g, pt: (b, 0, 0)),
                  pl.BlockSpec((1, PAGE_SIZE, B_VDIM), lambda b, g, pt: (b, 0, 0)),
                  pl.BlockSpec((nrow, page_rows), const2),
                  pl.BlockSpec((nrow, page_rows), const2),
                  pl.BlockSpec((nrow, PAGE_SIZE), const2),
                  pl.BlockSpec((4, B_HEAD), const2),
                  pl.BlockSpec((1, B_VDIM), const2)]
                 + [page_spec(p) for p in range(npg)] + [page_spec(p) for p in range(npg)],
        out_specs=pl.BlockSpec((ds, B_WIDTH), lambda b, g, pt: (b, 0)),
        scratch_shapes=[pltpu.VMEM((nrow, B_VDIM), F32),
                        pltpu.VMEM((nrow, 1), F32),
                        pltpu.VMEM((nrow, 1), F32)])
    return pl.pallas_call(
        functools.partial(_attn_s_kernel, npages=npg, ds=ds),
        grid_spec=grid_spec,
        out_shape=jax.ShapeDtypeStruct((nb * ds, B_WIDTH), F32),
        compiler_params=_cparams("arbitrary", "arbitrary"),
        name="attn_sample",
    )(page_table, qs, kn, vn, maskb, bias_last, bias_new, lam_params, subln_g,
      *([cache_k2] * npg), *([cache_v2] * npg))


def _outproj_kernel(ap_ref, as_ref, bp_ref, bs_ref, xp_ref, xs_ref, mp_ref, ms_ref, ng_ref, wo_ref,
                    wr_ref, br_ref, x1_ref, h2_ref, ti_ref, tr_ref, cnt_ref, base_ref, run_ref,
                    *, n_prompt_tiles):
    i = pl.program_id(0)
    tm = xp_ref.shape[0]
    is_p = i < n_prompt_tiles

    @pl.when(i == 0)
    def _():
        run_ref[...] = jnp.zeros(run_ref.shape, F32)

    a = jnp.where(is_p, ap_ref[...], as_ref[...])
    b = jnp.where(is_p, bp_ref[...], bs_ref[...].astype(BF16))
    x = jnp.where(is_p, xp_ref[...], xs_ref[...])
    g1 = jnp.where(is_p, mp_ref[0, 2:3, :], ms_ref[2])
    sh2 = jnp.where(is_p, mp_ref[0, 3:4, :], ms_ref[3])
    sc2 = jnp.where(is_p, mp_ref[0, 4:5, :], ms_ref[4])

    mix = (jnp.dot(a, wo_ref[0:A_WIDTH, :], preferred_element_type=F32)
           + jnp.dot(b, wo_ref[A_WIDTH:, :], preferred_element_type=F32))
    x1 = x + g1 * _rms(mix, ng_ref[1:2, :])
    x1_ref[...] = x1
    h2 = _rms(x1, ng_ref[2:3, :]) * (1.0 + sc2) + sh2
    h2_ref[:, 0:D_MODEL] = h2

    logits = jnp.dot(h2.astype(BF16), wr_ref[...], preferred_element_type=F32) + br_ref[...]
    work = logits.T[0:N_EXPERTS, :]
    sub = lax.broadcasted_iota(jnp.int32, work.shape, 0)
    vals, idxs = [], []
    for _ in range(TOP_K):
        mx = jnp.max(work, axis=0, keepdims=True)
        ix = jnp.min(jnp.where(work == mx, sub, N_EXPERTS), axis=0, keepdims=True)
        vals.append(mx)
        idxs.append(ix)
        work = jnp.where(sub == ix, -jnp.inf, work)
    exps = [jnp.exp(v - vals[0]) for v in vals]
    den = exps[0] + exps[1] + exps[2] + exps[3]

    sel = jnp.where(work == -jnp.inf, 1.0, 0.0)
    r_i = lax.broadcasted_iota(jnp.int32, (tm, tm), 0)
    c_i = lax.broadcasted_iota(jnp.int32, (tm, tm), 1)
    earlier = jnp.where(r_i < c_i, 1.0, 0.0).astype(BF16)
    base_ref[0] = run_ref[...]
    before = jnp.dot(sel.astype(BF16), earlier, preferred_element_type=F32) + run_ref[...]
    run_ref[...] = run_ref[...] + jnp.sum(sel, axis=1, keepdims=True)
    cnt_ref[...] = run_ref[...]

    ranks = [jnp.sum(jnp.where(sub == ix, before, 0.0), axis=0, keepdims=True) for ix in idxs]
    weights = [e / den for e in exps]
    ti_ref[...] = jnp.concatenate(idxs, axis=0)
    tr_ref[...] = jnp.concatenate(ranks, axis=0).astype(jnp.int32)
    meta = jnp.concatenate(weights + [ix.astype(F32) for ix in idxs]
                           + [jnp.zeros((LANES - 2 * TOP_K, tm), F32)], axis=0)
    h2_ref[:, D_MODEL:] = meta.T


def _outproj(a_p, a_s, b_p, b_s, x_p, x_s, mod_p, mod_s, norm_g, w_out_b, w_router, b_router):
    tm = ROW_TILE
    npt = x_p.shape[0] // tm
    nst = x_s.shape[0] // tm
    assert nst == 1
    n_tiles = npt + nst
    rows = n_tiles * tm
    tiles_per_batch = npt // mod_p.shape[0]
    last_p = npt - 1
    prow = lambda i: (jnp.minimum(i, last_p), 0)
    srow = lambda i: (0, 0)
    row = lambda i: (i, 0)
    col = lambda i: (0, i)
    const2 = lambda i: (0, 0)
    return pl.pallas_call(
        functools.partial(_outproj_kernel, n_prompt_tiles=npt),
        grid=(n_tiles,),
        in_specs=[pl.BlockSpec((tm, A_WIDTH), prow), pl.BlockSpec((tm, A_WIDTH), srow),
                  pl.BlockSpec((tm, B_WIDTH), prow), pl.BlockSpec((tm, B_WIDTH), srow),
                  pl.BlockSpec((tm, D_MODEL), prow), pl.BlockSpec((tm, D_MODEL), srow),
                  pl.BlockSpec((1, 6, D_MODEL), lambda i: (jnp.minimum(i, last_p) // tiles_per_batch, 0, 0)),
                  pl.BlockSpec((6, tm, D_MODEL), lambda i: (0, 0, 0)),
                  pl.BlockSpec((4, D_MODEL), const2),
                  pl.BlockSpec((D_MODEL, D_MODEL), const2),
                  pl.BlockSpec((D_MODEL, LANES), const2),
                  pl.BlockSpec((1, LANES), const2)],
        out_specs=[pl.BlockSpec((tm, D_MODEL), row), pl.BlockSpec((tm, ROW_WIDTH), row),
                   pl.BlockSpec((TOP_K, tm), col), pl.BlockSpec((TOP_K, tm), col),
                   pl.BlockSpec((N_EXPERTS, 1), const2),
                   pl.BlockSpec((1, N_EXPERTS, 1), lambda i: (i, 0, 0))],
        out_shape=[jax.ShapeDtypeStruct((rows, D_MODEL), F32), jax.ShapeDtypeStruct((rows, ROW_WIDTH), F32),
                   jax.ShapeDtypeStruct((TOP_K, rows), jnp.int32), jax.ShapeDtypeStruct((TOP_K, rows), jnp.int32),
                   jax.ShapeDtypeStruct((N_EXPERTS, 1), F32),
                   jax.ShapeDtypeStruct((n_tiles, N_EXPERTS, 1), F32)],
        scratch_shapes=[pltpu.VMEM((N_EXPERTS, 1), F32)],
        compiler_params=_cparams("arbitrary"),
        name="outproj_router",
    )(a_p, a_s, b_p, b_s, x_p, x_s, mod_p, mod_s, norm_g, w_out_b,
      jnp.pad(w_router, ((0, 0), (0, LANES - N_EXPERTS))).astype(BF16),
      jnp.pad(b_router.reshape(1, -1), ((0, 0), (0, LANES - N_EXPERTS))))


def _dispatch_kernel(pe_ref, cnt_ref, dest_ref, h_ref, xg_ref, zero_ref, sem, zsem, *, bm):
    tm = h_ref.shape[0]
    n_blocks = xg_ref.shape[0] // bm

    def zero_block(row0):
        return pltpu.make_async_copy(zero_ref, xg_ref.at[pl.ds(pl.multiple_of(row0, bm), bm), :], zsem)

    @pl.when(pl.program_id(0) == 0)
    def _():
        zero_ref[...] = jnp.zeros(zero_ref.shape, F32)
        first_unused = pe_ref[N_EXPERTS - 1] // bm
        for e in range(N_EXPERTS):
            @pl.when(cnt_ref[e] > 0)
            def _():
                zero_block(pe_ref[e] - bm).start()

        def start_unused(b, carry):
            zero_block(b * bm).start()
            return carry

        lax.fori_loop(first_unused, n_blocks, start_unused, 0)
        for e in range(N_EXPERTS):
            @pl.when(cnt_ref[e] > 0)
            def _():
                zero_block(pe_ref[e] - bm).wait()

        def wait_unused(b, carry):
            zero_block(b * bm).wait()
            return carry

        lax.fori_loop(first_unused, n_blocks, wait_unused, 0)

    def body(t, carry):
        for j in range(TOP_K):
            d = dest_ref[j * tm + t]
            pltpu.make_async_copy(h_ref.at[pl.ds(t, 1), :], xg_ref.at[pl.ds(d, 1), :], sem).start()
        return carry

    lax.fori_loop(0, tm, body, 0)
    n = tm * TOP_K
    pltpu.make_async_copy(xg_ref.at[pl.ds(0, n), :], xg_ref.at[pl.ds(0, n), :], sem).wait()


def _dispatch(pad_end, cnt, dest_flat, h2, n_blocks):
    tm = ROW_TILE
    bm = EXPERT_BLOCK
    rows = h2.shape[0]
    grid_spec = pltpu.PrefetchScalarGridSpec(
        num_scalar_prefetch=2,
        grid=(rows // tm,),
        in_specs=[pl.BlockSpec((tm * TOP_K,), lambda i, pe, cn: (i,), memory_space=pltpu.SMEM),
                  pl.BlockSpec((tm, ROW_WIDTH), lambda i, pe, cn: (i, 0))],
        out_specs=pl.BlockSpec(memory_space=pl.ANY),
        scratch_shapes=[pltpu.VMEM((bm, ROW_WIDTH), F32), pltpu.SemaphoreType.DMA(()), pltpu.SemaphoreType.DMA(())])
    return pl.pallas_call(
        functools.partial(_dispatch_kernel, bm=bm),
        grid_spec=grid_spec,
        out_shape=jax.ShapeDtypeStruct((n_blocks * bm, ROW_WIDTH), F32),
        compiler_params=_cparams("arbitrary"),
        name="moe_dispatch",
    )(pad_end, cnt, dest_flat, h2)


def _expert_kernel(be_ref, nu_ref, x_ref, wgu_ref, bgu_ref, wd_ref, bd_ref, y_ref, wgu_b, wd_b):
    i = pl.program_id(0)
    used = i < nu_ref[0]
    fresh = jnp.logical_or(i == 0, be_ref[i] != be_ref[jnp.maximum(i - 1, 0)])

    @pl.when(jnp.logical_and(used, fresh))
    def _():
        wgu_b[...] = wgu_ref[0].astype(BF16)
        wd_b[...] = wd_ref[0].astype(BF16)

    @pl.when(jnp.logical_not(used))
    def _():
        y_ref[...] = jnp.zeros(y_ref.shape, y_ref.dtype)

    @pl.when(used)
    def _():
        x = x_ref[:, 0:D_MODEL].astype(BF16)
        gu = jnp.dot(x, wgu_b[...], preferred_element_type=F32) + bgu_ref[0]
        glu = jnp.minimum(gu[:, :D_EXPERT], SWIGLU_LIMIT)
        lin = jnp.clip(gu[:, D_EXPERT:], -SWIGLU_LIMIT, SWIGLU_LIMIT)
        hid = glu * jax.nn.sigmoid(SWIGLU_ALPHA * glu) * (lin + 1.0)
        y = jnp.dot(hid.astype(BF16), wd_b[...], preferred_element_type=F32) + bd_ref[0]
        meta = x_ref[:, D_MODEL:]
        me = be_ref[i].astype(F32)
        w = jnp.zeros((x_ref.shape[0], 1), F32)
        for j in range(TOP_K):
            w = w + jnp.where(meta[:, TOP_K + j:TOP_K + j + 1] == me, meta[:, j:j + 1], 0.0)
        y_ref[...] = (y * w).astype(y_ref.dtype)


def _experts(block_e, n_used, xg, w_gu_b, b_gu, w_down_b, b_down):
    bm = EXPERT_BLOCK
    n_blocks = xg.shape[0] // bm
    grid_spec = pltpu.PrefetchScalarGridSpec(
        num_scalar_prefetch=2,
        grid=(n_blocks,),
        in_specs=[pl.BlockSpec((bm, ROW_WIDTH), lambda i, be, nu: (jnp.minimum(i, nu[0] - 1), 0)),
                  pl.BlockSpec((1, D_MODEL, 2 * D_EXPERT), lambda i, be, nu: (be[i], 0, 0)),
                  pl.BlockSpec((1, 1, 2 * D_EXPERT), lambda i, be, nu: (be[i], 0, 0)),
                  pl.BlockSpec((1, D_EXPERT, D_MODEL), lambda i, be, nu: (be[i], 0, 0)),
                  pl.BlockSpec((1, 1, D_MODEL), lambda i, be, nu: (be[i], 0, 0))],
        out_specs=pl.BlockSpec((bm, D_MODEL), lambda i, be, nu: (i, 0)),
        scratch_shapes=[pltpu.VMEM((D_MODEL, 2 * D_EXPERT), BF16), pltpu.VMEM((D_EXPERT, D_MODEL), BF16)])
    return pl.pallas_call(
        _expert_kernel,
        grid_spec=grid_spec,
        out_shape=jax.ShapeDtypeStruct((xg.shape[0], D_MODEL), BF16),
        compiler_params=_cparams("arbitrary"),
        name="moe_experts",
    )(block_e, n_used, xg, w_gu_b, b_gu.reshape(N_EXPERTS, 1, -1), w_down_b, b_down.reshape(N_EXPERTS, 1, -1))


def _combine_kernel(row0_ref, col_ref, x1_ref, mp_ref, ms_ref, ng_ref, yg_ref,
                    yp_ref, ys_ref, rows_ref, g_ref, sem, *, n_prompt_tiles):
    i = pl.program_id(0)
    n_steps = pl.num_programs(0)
    tm = x1_ref.shape[0]
    is_p = i < n_prompt_tiles
    win = COMBINE_WINDOW
    n_win_rows = COMBINE_SLOTS * win
    slot = i % 2

    def fetch_windows(tile, buf):
        for s in range(COMBINE_SLOTS):
            row0 = pl.multiple_of(row0_ref[tile * COMBINE_SLOTS + s], COMBINE_ALIGN)
            pltpu.make_async_copy(yg_ref.at[pl.ds(row0, win), :], rows_ref.at[buf, pl.ds(s * win, win), :],
                                  sem.at[buf]).start()

    @pl.when(i == 0)
    def _():
        fetch_windows(0, 0)

    @pl.when(i + 1 < n_steps)
    def _():
        fetch_windows(i + 1, 1 - slot)

    pltpu.make_async_copy(yg_ref.at[pl.ds(0, n_win_rows), :], rows_ref.at[slot], sem.at[slot]).wait()
    col = col_ref[...]
    for c in range(n_win_rows // LANES):
        lane = lax.broadcasted_iota(jnp.int32, (tm, LANES), 1) + c * LANES
        g = jnp.zeros((tm, LANES), F32)
        for j in range(TOP_K):
            g = jnp.where(lane == col[:, j:j + 1], 1.0, g)
        g_ref[:, c * LANES:(c + 1) * LANES] = g.astype(BF16)
    f = jnp.dot(g_ref[...], rows_ref[slot], preferred_element_type=F32)

    g2 = jnp.where(is_p, mp_ref[0, 5:6, :], ms_ref[5])
    y = x1_ref[...] + g2 * _rms(f, ng_ref[3:4, :])

    @pl.when(is_p)
    def _():
        yp_ref[...] = y

    @pl.when(jnp.logical_not(is_p))
    def _():
        ys_ref[...] = y


def _combine(row0, col, x1, mod_p, mod_s, norm_g, yg, *, n_prompt_rows, n_sample_rows):
    tm = ROW_TILE
    npt = n_prompt_rows // tm
    assert n_sample_rows == tm and (COMBINE_SLOTS * COMBINE_WINDOW) % LANES == 0
    n_tiles = npt + 1
    tiles_per_batch = npt // mod_p.shape[0]
    last_p = npt - 1
    grid_spec = pltpu.PrefetchScalarGridSpec(
        num_scalar_prefetch=1,
        grid=(n_tiles,),
        in_specs=[pl.BlockSpec((tm, TOP_K), lambda i, r0: (i, 0)),
                  pl.BlockSpec((tm, D_MODEL), lambda i, r0: (i, 0)),
                  pl.BlockSpec((1, 6, D_MODEL), lambda i, r0: (jnp.minimum(i, last_p) // tiles_per_batch, 0, 0)),
                  pl.BlockSpec((6, tm, D_MODEL), lambda i, r0: (0, 0, 0)),
                  pl.BlockSpec((4, D_MODEL), lambda i, r0: (0, 0)),
                  pl.BlockSpec(memory_space=pl.ANY)],
        out_specs=[pl.BlockSpec((tm, D_MODEL), lambda i, r0: (jnp.minimum(i, last_p), 0)),
                   pl.BlockSpec((tm, D_MODEL), lambda i, r0: (0, 0))],
        scratch_shapes=[pltpu.VMEM((2, COMBINE_SLOTS * COMBINE_WINDOW, D_MODEL), BF16),
                        pltpu.VMEM((tm, COMBINE_SLOTS * COMBINE_WINDOW), BF16),
                        pltpu.SemaphoreType.DMA((2,))])
    return pl.pallas_call(
        functools.partial(_combine_kernel, n_prompt_tiles=npt),
        grid_spec=grid_spec,
        out_shape=[jax.ShapeDtypeStruct((n_prompt_rows, D_MODEL), F32),
                   jax.ShapeDtypeStruct((n_sample_rows, D_MODEL), F32)],
        compiler_params=_cparams("arbitrary"),
        name="moe_combine",
    )(row0, col, x1, mod_p, mod_s, norm_g, yg)


def _t5_bucket_np(dist):
    n = np.maximum(dist, 0)
    max_exact = NUM_BUCKETS // 2
    nf = np.maximum(n, 1).astype(np.float64)
    large = max_exact + (np.log(nf / max_exact) / math.log(MAX_DISTANCE / max_exact)
                         * (NUM_BUCKETS - max_exact)).astype(np.int32)
    large = np.minimum(large, NUM_BUCKETS - 1)
    return np.where(n < max_exact, n, large).astype(np.int32)


def _bias_from_dist(rel_bias, dist):
    onehot = jax.nn.one_hot(_t5_bucket_np(dist).reshape(-1), NUM_BUCKETS, dtype=F32)
    shifted = (rel_bias - rel_bias[NUM_BUCKETS - 1]).reshape(NUM_BUCKETS, -1)
    out = jnp.dot(onehot, shifted, precision=lax.Precision.HIGHEST).reshape(dist.shape + rel_bias.shape[1:])
    return jnp.where(jnp.asarray(dist >= 0)[..., None, None], out, -jnp.inf).astype(F32)


def _head_masked(t):
    h = t.shape[0]
    same = np.eye(h, dtype=bool)[:, None, None, None, :]
    full = jnp.where(same, t[..., None], -jnp.inf)
    return full.reshape(h * t.shape[1] * t.shape[2], t.shape[3] * h)


def kernel(x_prompt, x_sample, cache_k, cache_v, page_table, c_prompt, c_sample, w_ada, b_ada, norm_g, w_in,
           w_out, ln_v_g, ln_v_b, w_spatial, b_spatial, lam_params, subln_g, rel_bias, w_router, b_router,
           w_gu, b_gu, w_down, b_down):
    batch, seq, d = x_prompt.shape
    nb, ds = x_sample.shape[:2]
    n_pages = page_table.shape[1]
    past = n_pages * PAGE_SIZE
    rows_p = batch * seq
    rows_s = nb * ds
    assert rows_s == ROW_TILE and d == D_MODEL

    mod = _ada(jnp.concatenate([c_prompt, c_sample], axis=0), w_ada[0], b_ada[0])
    mod_p = mod[:batch].reshape(batch, 6, D_MODEL)
    mod_s = jnp.transpose(jnp.repeat(mod[batch:], ds, axis=0).reshape(rows_s, 6, D_MODEL), (1, 0, 2))

    ng = norm_g[0]
    w_in_b = w_in[0].astype(BF16)
    w_out_b = w_out[0].astype(BF16)
    lng = ln_v_g[0].reshape(1, A_WIDTH)
    lnb = ln_v_b[0].reshape(1, A_WIDTH)
    grp = np.arange(A_WIDTH) // A_HEAD
    avg = jnp.asarray((grp[:, None] == grp[None, :]).astype(np.float32) / A_HEAD, BF16)
    ws_p = jnp.tril(w_spatial[0]).astype(BF16)
    bs_p = jnp.repeat(b_spatial[0].T, A_HEAD, axis=1)
    w_small = jnp.tril(w_spatial[0][:, :ds, :ds])
    ws_s = jnp.einsum('ab,gts->gatbs', jnp.eye(nb, dtype=F32), w_small).reshape(A_GROUPS, rows_s, rows_s).astype(BF16)
    bs_s = jnp.tile(jnp.repeat(b_spatial[0][:, :ds].T, A_HEAD, axis=1), (nb, 1))

    xp2 = x_prompt.reshape(rows_p, D_MODEL)
    xs2 = x_sample.reshape(rows_s, D_MODEL)
    assert ATTN_TILE % ROW_TILE == 0 and seq % ATTN_TILE == 0
    a_p, va_p, q_p, k_p, v_p, kb_p, vt_p = _inproj(
        xp2, mod_p[:, 0:1], mod_p[:, 1:2], ng[0:1], w_in_b, lng, lnb, avg, ws_p, bs_p,
        tiles_per_mod=seq // ROW_TILE, chunk=CHUNK, q_dtype=BF16, q_scale=LOG2E / math.sqrt(B_HEAD), va_rows=CHUNK)
    a_s, va_s, q_s, k_s, v_s, _, _ = _inproj(
        xs2, mod_s[0:1], mod_s[1:2], ng[0:1], w_in_b, lng, lnb, avg, ws_s, bs_s,
        tiles_per_mod=1, chunk=rows_s, q_dtype=F32, q_scale=1.0, va_rows=rows_s)

    ti = np.arange(ATTN_TILE)
    dist_p = np.stack([ti[:, None] - ti[None, :], ti[:, None] - ti[None, :] + ATTN_TILE])
    bias_p = jnp.transpose(_bias_from_dist(rel_bias, dist_p), (3, 0, 2, 4, 1))
    bias_p = bias_p.reshape(B_HEADS, 2, ATTN_TILE, 2 * ATTN_TILE) * LOG2E
    bias_p = jnp.concatenate([bias_p, jnp.zeros_like(bias_p[:, :1])], axis=1)
    b_p = _attn_p(q_p, kb_p, vt_p, bias_p, lam_params[0], subln_g, batch=batch, seq=seq)

    qi = np.arange(ds)
    ki = np.arange(PAGE_SIZE)
    dist_l = PAGE_SIZE + qi[:, None] - ki[None, :]
    kn_i = np.arange(PAGE_SIZE // B_HEADS)
    dist_n = np.where(kn_i[None, :] < ds, qi[:, None] - kn_i[None, :], -1)
    to_rows = lambda t: jnp.transpose(t, (2, 3, 0, 1))
    bias_l = _head_masked(to_rows(_bias_from_dist(rel_bias, dist_l)))
    bias_n = _head_masked(to_rows(_bias_from_dist(rel_bias, dist_n)))
    maskb = _head_masked(jnp.zeros((B_HEADS, 2, ds, PAGE_SIZE), F32))
    pad_keys = lambda t: jnp.pad(t.reshape(nb, ds * B_HEADS, B_VDIM), ((0, 0), (0, PAGE_SIZE - ds * B_HEADS), (0, 0)))
    n_phys = cache_k.shape[0]
    b_s = _attn_s(page_table, q_s, pad_keys(k_s), pad_keys(v_s), maskb, bias_l, bias_n, lam_params[0], subln_g,
                  cache_k.reshape(n_phys * PAGE_SIZE * B_HEADS, B_VDIM),
                  cache_v.reshape(n_phys * PAGE_SIZE * B_HEADS, B_VDIM), ds=ds)

    x1, h2, top_i, top_r, counts, tile_base = _outproj(
        a_p, a_s, b_p, b_s, xp2, xs2, mod_p, mod_s, ng, w_out_b, w_router[0], b_router[0])

    bm = EXPERT_BLOCK
    rows = rows_p + rows_s
    n_blocks = rows * TOP_K // bm + N_EXPERTS + 1
    cnt = counts[:, 0].astype(jnp.int32)
    padded = (cnt + bm - 1) // bm * bm
    pad_end = jnp.cumsum(padded)
    pad_start = pad_end - padded
    experts = jnp.arange(N_EXPERTS, dtype=jnp.int32)
    chosen = top_i[..., None] == experts
    dest = jnp.sum(jnp.where(chosen, pad_start, 0), axis=-1) + top_r
    n_tiles = rows // ROW_TILE
    win = COMBINE_WINDOW
    assert win % COMBINE_ALIGN == 0
    assert (ROW_TILE * TOP_K + N_EXPERTS * (COMBINE_ALIGN - 1)) // win + N_EXPERTS <= COMBINE_SLOTS
    base = tile_base[:, :, 0].astype(jnp.int32)
    first = pad_start[None, :] + base
    win0 = first // COMBINE_ALIGN * COMBINE_ALIGN
    sent = jnp.concatenate([base[1:], cnt[None, :]], axis=0) - base
    n_win = jnp.where(sent > 0, (first - win0 + sent + win - 1) // win, 0)
    slot_end = jnp.cumsum(n_win, axis=1)
    slot_start = slot_end - n_win
    slots = jnp.arange(COMBINE_SLOTS, dtype=jnp.int32)
    slot_e = jnp.minimum(jnp.sum((slot_end[:, None, :] <= slots[None, :, None]).astype(jnp.int32), axis=2),
                         N_EXPERTS - 1)
    pick = slot_e[..., None] == experts
    slot_row0 = (jnp.sum(jnp.where(pick, win0[:, None, :], 0), axis=2)
                 + win * (slots[None, :] - jnp.sum(jnp.where(pick, slot_start[:, None, :], 0), axis=2)))
    slot_row0 = jnp.where(slots[None, :] < slot_end[:, -1:], slot_row0, 0)
    local = dest - jnp.sum(jnp.where(chosen, jnp.repeat(win0, ROW_TILE, axis=0), 0), axis=-1)
    col = jnp.sum(jnp.where(chosen, jnp.repeat(slot_start, ROW_TILE, axis=0), 0), axis=-1) * win + local
    dest = jnp.transpose(dest.reshape(TOP_K, n_tiles, ROW_TILE), (1, 0, 2)).reshape(-1)
    blk_start = jnp.arange(n_blocks, dtype=jnp.int32) * bm
    block_e = jnp.minimum(jnp.sum((pad_end[None, :] <= blk_start[:, None]).astype(jnp.int32), axis=1),
                          N_EXPERTS - 1)
    n_used = (pad_end[-1:] // bm).astype(jnp.int32)

    xg = _dispatch(pad_end.astype(jnp.int32), cnt, dest, h2, n_blocks)
    yg = _experts(block_e, n_used, xg, w_gu[0], b_gu[0], w_down[0], b_down[0])
    y_p, y_s = _combine(slot_row0.reshape(-1), col.T, x1, mod_p, mod_s, ng, yg,
                        n_prompt_rows=rows_p, n_sample_rows=rows_s)

    return (y_p.reshape(batch, seq, D_MODEL),
            y_s.reshape(nb, ds, D_MODEL),
            k_p.reshape(batch, seq, 1, B_HEADS, B_VDIM),
            v_p.reshape(batch, seq, 1, B_HEADS, B_VDIM),
            k_s.reshape(nb, ds, 1, B_HEADS, B_VDIM),
            v_s.reshape(nb, ds, 1, B_HEADS, B_VDIM),
            va_p.reshape(batch, CHUNK, 1, A_WIDTH),
            va_s.reshape(nb, ds, 1, A_WIDTH))
```

```python
import functools
import math

import numpy as np
import jax
import jax.numpy as jnp
from jax import lax
from jax.experimental import pallas as pl
from jax.experimental.pallas import tpu as pltpu

F32 = jnp.float32
BF16 = jnp.bfloat16

D_MODEL = 1024
A_WIDTH = 512
A_HEAD = 64
A_GROUPS = 8
CHUNK = 128
B_WIDTH = 512
B_HEAD = 64
B_VDIM = 128
B_HEADS = 4
IN_WIDTH = 2 * A_WIDTH + 3 * B_WIDTH
NUM_BUCKETS = 32
MAX_DISTANCE = 128
PAGE_SIZE = 128
N_EXPERTS = 32
TOP_K = 4
D_EXPERT = 1024
SWIGLU_LIMIT = 7.0
SWIGLU_ALPHA = 1.702
NORM_EPS = 1e-6
LAM_INIT = 0.8 - 0.6 * math.exp(-0.3 * 0)

ROW_TILE = 256
ATTN_TILE = 512
PAGES_PER_STEP = 16
LOG2E = math.log2(math.e)
LANES = 128
MXU_WIDTH = 256
ROW_WIDTH = D_MODEL + LANES
EXPERT_BLOCK = 512
DISPATCH_TILES = 5
COMBINE_ALIGN = 16
COMBINE_WINDOW = 48
COMBINE_SLOTS = 64
VMEM_LIMIT = 56 * 1024 * 1024


def _cparams(*sem):
    return pltpu.CompilerParams(dimension_semantics=sem, vmem_limit_bytes=VMEM_LIMIT)


def _rms(x, g):
    return x * lax.rsqrt(jnp.mean(x * x, axis=-1, keepdims=True) + NORM_EPS) * g


def _ada_kernel(c_ref, w_ref, b_ref, o_ref):
    s = jax.nn.silu(c_ref[...]).astype(BF16)
    o_ref[...] = jnp.dot(s, w_ref[...].astype(BF16), preferred_element_type=F32) + b_ref[...]


def _ada(c_all, w_ada, b_ada):
    n = c_all.shape[0]
    tn = 1024
    return pl.pallas_call(
        _ada_kernel,
        grid=(6 * D_MODEL // tn,),
        in_specs=[pl.BlockSpec((n, D_MODEL), lambda j: (0, 0)),
                  pl.BlockSpec((D_MODEL, tn), lambda j: (0, j)),
                  pl.BlockSpec((1, tn), lambda j: (0, j))],
        out_specs=pl.BlockSpec((n, tn), lambda j: (0, j)),
        out_shape=jax.ShapeDtypeStruct((n, 6 * D_MODEL), F32),
        compiler_params=_cparams("arbitrary"),
        name="ada",
    )(c_all, w_ada, b_ada.reshape(1, -1))


def _inproj_kernel(x_ref, sh_ref, sc_ref, ng_ref, w_ref, lng_ref, lnb_ref, avg_ref, ws_ref, bs_ref,
                   a_ref, va_ref, q_ref, k_ref, v_ref, kb_ref, vt_ref, *, chunk, q_scale):
    tm = x_ref.shape[0]
    va_rows = va_ref.shape[0]
    h = (_rms(x_ref[...], ng_ref[...]) * (1.0 + sc_ref[0]) + sh_ref[0]).astype(BF16)

    u = jax.nn.gelu(jnp.dot(h, w_ref[:, 0:A_WIDTH], preferred_element_type=F32))
    gv = jax.nn.gelu(jnp.dot(h, w_ref[:, A_WIDTH:2 * A_WIDTH], preferred_element_type=F32))

    avg = avg_ref[...]

    def group_mean(t):
        hi = t.astype(BF16)
        lo = (t - hi.astype(F32)).astype(BF16)
        return (jnp.dot(hi, avg, preferred_element_type=F32)
                + jnp.dot(lo, avg, preferred_element_type=F32))

    xc = gv - group_mean(gv)
    va = xc * lax.rsqrt(group_mean(xc * xc) + NORM_EPS) * lng_ref[...] + lnb_ref[...]
    va_ref[...] = va[tm - va_rows:, :]

    vab = va.astype(BF16)
    lane = lax.broadcasted_iota(jnp.int32, (chunk, 128), 1)
    for c in range(tm // chunk):
        r0 = c * chunk
        for pair in range(A_GROUPS // 2):
            c0 = pair * 128
            vp = vab[r0:r0 + chunk, c0:c0 + 128]
            lo_half = jnp.where(lane < A_HEAD, vp, jnp.zeros_like(vp))
            hi_half = jnp.where(lane >= A_HEAD, vp, jnp.zeros_like(vp))
            s = (jnp.dot(ws_ref[2 * pair], lo_half, preferred_element_type=F32)
                 + jnp.dot(ws_ref[2 * pair + 1], hi_half, preferred_element_type=F32))
            a = u[r0:r0 + chunk, c0:c0 + 128] * (s + bs_ref[:, c0:c0 + 128])
            a_ref[r0:r0 + chunk, c0:c0 + 128] = a.astype(a_ref.dtype)

    q = jnp.dot(h, w_ref[:, 2 * A_WIDTH:2 * A_WIDTH + B_WIDTH], preferred_element_type=F32)
    q_ref[...] = (q * q_scale).astype(q_ref.dtype)
    k = jnp.dot(h, w_ref[:, 2 * A_WIDTH + B_WIDTH:2 * A_WIDTH + 2 * B_WIDTH], preferred_element_type=F32)
    kb_ref[...] = k.astype(BF16)
    v = jnp.dot(h, w_ref[:, 2 * A_WIDTH + 2 * B_WIDTH:IN_WIDTH], preferred_element_type=F32)
    for hh in range(B_HEADS):
        c0 = hh * B_VDIM
        k_ref[pl.ds(hh, tm, stride=B_HEADS), :] = k[:, c0:c0 + B_VDIM]
        v_ref[pl.ds(hh, tm, stride=B_HEADS), :] = v[:, c0:c0 + B_VDIM]
        vt_ref[0, hh, 0] = v[:, c0:c0 + B_VDIM].T.astype(BF16)


def _inproj(x2, sh, sc, ng, w_in_b, lng, lnb, avg, ws, bs, *, tiles_per_mod, chunk, q_dtype, q_scale, va_rows):
    rows = x2.shape[0]
    tm = ROW_TILE
    n_tiles = rows // tm
    mod_rows = sh.shape[1]
    row = lambda i: (i, 0)
    const2 = lambda i: (0, 0)
    mod_map = lambda i: (i // tiles_per_mod, 0, 0)
    out_w = lambda w, dt: jax.ShapeDtypeStruct((rows, w), dt)
    head_rows = jax.ShapeDtypeStruct((rows * B_HEADS, B_VDIM), F32)
    head_rows_spec = pl.BlockSpec((tm * B_HEADS, B_VDIM), row)
    vt_shape = jax.ShapeDtypeStruct((n_tiles // tiles_per_mod, B_HEADS, tiles_per_mod, B_VDIM, tm), BF16)
    vt_spec = pl.BlockSpec((1, B_HEADS, 1, B_VDIM, tm), lambda i: (i // tiles_per_mod, 0, i % tiles_per_mod, 0, 0))
    return pl.pallas_call(
        functools.partial(_inproj_kernel, chunk=chunk, q_scale=q_scale),
        grid=(rows // tm,),
        in_specs=[pl.BlockSpec((tm, D_MODEL), row),
                  pl.BlockSpec((1, mod_rows, D_MODEL), mod_map),
                  pl.BlockSpec((1, mod_rows, D_MODEL), mod_map),
                  pl.BlockSpec((1, D_MODEL), const2),
                  pl.BlockSpec((D_MODEL, IN_WIDTH), const2),
                  pl.BlockSpec((1, A_WIDTH), const2),
                  pl.BlockSpec((1, A_WIDTH), const2),
                  pl.BlockSpec((A_WIDTH, A_WIDTH), const2),
                  pl.BlockSpec((A_GROUPS, chunk, chunk), lambda i: (0, 0, 0)),
                  pl.BlockSpec((chunk, A_WIDTH), const2)],
        out_specs=[pl.BlockSpec((tm, A_WIDTH), row),
                   pl.BlockSpec((va_rows, A_WIDTH), lambda i: (i // tiles_per_mod, 0)),
                   pl.BlockSpec((tm, B_WIDTH), row), head_rows_spec, head_rows_spec,
                   pl.BlockSpec((tm, B_WIDTH), row), vt_spec],
        out_shape=[out_w(A_WIDTH, BF16),
                   jax.ShapeDtypeStruct((n_tiles // tiles_per_mod * va_rows, A_WIDTH), F32),
                   out_w(B_WIDTH, q_dtype), head_rows, head_rows, out_w(B_WIDTH, BF16), vt_shape],
        compiler_params=_cparams("arbitrary"),
        name="inproj",
    )(x2, sh, sc, ng, w_in_b, lng, lnb, avg, ws, bs)


def _stack_halves(q):
    lane = lax.broadcasted_iota(jnp.int32, q.shape, 1)
    zero = jnp.zeros_like(q)
    return jnp.concatenate([jnp.where(lane < B_HEAD, q, zero), jnp.where(lane >= B_HEAD, q, zero)], axis=0)


def _lambda(lam_ref):
    lp = lam_ref[...]
    return (jnp.exp(jnp.sum(lp[0:1] * lp[1:2], axis=-1, keepdims=True))
            - jnp.exp(jnp.sum(lp[2:3] * lp[3:4], axis=-1, keepdims=True)) + LAM_INIT)


def _diff_finish(acc, l, n, lam, g):
    o = acc[:n] / l[:n] - lam * (acc[n:] / l[n:])
    return _rms(o, g) * (1.0 - LAM_INIT)


def _attn_p_kernel(q_ref, k_ref, vt_ref, bias_ref, lam_ref, g_ref, o_ref,
                   acc_ref, m_ref, l_ref, s_ref, tmax_ref, p_ref, alpha_ref, *, tq, tk):
    qi = pl.program_id(2)
    q = q_ref[...]
    lane = lax.broadcasted_iota(jnp.int32, q.shape, 1)
    zero = jnp.zeros_like(q)
    q_half = (jnp.where(lane < B_HEAD, q, zero), jnp.where(lane >= B_HEAD, q, zero))
    m_ref[...] = jnp.full(m_ref.shape, -jnp.inf, F32)
    l_ref[...] = jnp.zeros(l_ref.shape, F32)
    acc_ref[...] = jnp.zeros(acc_ref.shape, F32)
    p_ref[1] = jnp.zeros(p_ref.shape[1:], BF16)
    alpha_ref[1] = jnp.ones(alpha_ref.shape[1:], F32)

    def scores(j):
        k = k_ref[pl.ds(pl.multiple_of(j * tk, tk), tk), :]
        return [lax.dot_general(k, q_half[c], (((1,), (1,)), ((), ())), preferred_element_type=F32)
                for c in range(2)]

    def keep_scores(j, kind, ss):
        slot = j % 2
        for c in range(2):
            s = ss[c]
            if kind is not None:
                s = s + bias_ref[0, kind, :, c * tq:(c + 1) * tq]
            s_ref[slot, c] = s
            tmax_ref[slot, c] = jnp.max(s, axis=0, keepdims=True)

    def softmax(j):
        slot = j % 2
        for c in range(2):
            m_old = m_ref[c]
            m_new = jnp.maximum(m_old, tmax_ref[slot, c])
            alpha = jnp.exp2(m_old - m_new)
            p = jnp.exp2(s_ref[slot, c] - m_new)
            l_ref[c] = alpha * l_ref[c] + jnp.sum(p, axis=0, keepdims=True)
            m_ref[c] = m_new
            p_ref[slot, c] = p.astype(BF16)
            alpha_ref[slot, c] = alpha

    def pv(j):
        slot = j % 2
        pieces = tk // vt_ref.shape[-1]
        first = jnp.maximum(j, 0) * pieces
        for c in range(2):
            add = None
            for r in range(pieces):
                w = vt_ref.shape[-1]
                t = jnp.dot(vt_ref[0, 0, first + r], p_ref[slot, c, r * w:(r + 1) * w, :],
                            preferred_element_type=F32)
                add = t if add is None else add + t
            acc_ref[c] = alpha_ref[slot, c] * acc_ref[c] + add

    def far_body(j, carry):
        ss = scores(j + 1)
        pv(j - 1)
        softmax(j)
        keep_scores(j + 1, None, ss)
        return carry

    def near_body(j, carry):
        nxt = jnp.minimum(j + 1, qi)
        ss = scores(nxt)
        pv(j - 1)
        softmax(j)
        keep_scores(nxt, qi - nxt, ss)
        return carry

    keep_scores(0, jnp.minimum(qi, 2), scores(0))
    n_far_fetch = jnp.maximum(qi - 2, 0)
    lax.fori_loop(0, n_far_fetch, far_body, 0)
    lax.fori_loop(n_far_fetch, qi + 1, near_body, 0)
    pv(qi)

    o = acc_ref[0] / l_ref[0] - _lambda(lam_ref) * (acc_ref[1] / l_ref[1])
    y = o * lax.rsqrt(jnp.mean(o * o, axis=0, keepdims=True) + NORM_EPS) * (1.0 - LAM_INIT)
    o_ref[...] = (y.T * g_ref[...]).astype(o_ref.dtype)


def _attn_p(qb, kb, vt, bias, lam_params, subln_g, *, batch, seq):
    tq = tk = ATTN_TILE
    nq = seq // tq
    return pl.pallas_call(
        functools.partial(_attn_p_kernel, tq=tq, tk=tk),
        grid=(batch, B_HEADS, nq),
        in_specs=[pl.BlockSpec((tq, B_VDIM), lambda b, h, i: (b * nq + i, h)),
                  pl.BlockSpec((seq, B_VDIM), lambda b, h, i: (b, h)),
                  pl.BlockSpec((1, 1, seq // ROW_TILE, B_VDIM, ROW_TILE), lambda b, h, i: (b, h, 0, 0, 0)),
                  pl.BlockSpec((1, 3, tk, 2 * tq), lambda b, h, i: (h, 0, 0, 0)),
                  pl.BlockSpec((4, B_HEAD), lambda b, h, i: (0, 0)),
                  pl.BlockSpec((1, B_VDIM), lambda b, h, i: (0, 0))],
        out_specs=pl.BlockSpec((tq, B_VDIM), lambda b, h, i: (b * nq + i, h)),
        out_shape=jax.ShapeDtypeStruct((batch * seq, B_WIDTH), BF16),
        scratch_shapes=[pltpu.VMEM((2, B_VDIM, tq), F32),
                        pltpu.VMEM((2, 1, tq), F32),
                        pltpu.VMEM((2, 1, tq), F32),
                        pltpu.VMEM((2, 2, tk, tq), F32),
                        pltpu.VMEM((2, 2, 1, tq), F32),
                        pltpu.VMEM((2, 2, tk, tq), BF16),
                        pltpu.VMEM((2, 2, 1, tq), F32)],
        compiler_params=_cparams("arbitrary", "arbitrary", "arbitrary"),
        name="attn_prompt",
    )(qb, kb, vt, bias, lam_params, subln_g)


def _attn_s_kernel(pt_ref, q_ref, kn_ref, vn_ref, maskb_ref, biasl_ref, biasn_ref, lam_ref, g_ref, *rest,
                   npages, ds):
    del pt_ref
    k_refs = rest[:npages]
    v_refs = rest[npages:2 * npages]
    o_ref = rest[2 * npages]
    acc_ref, m_ref, l_ref = rest[2 * npages + 1:]
    g = pl.program_id(1)
    last = g == pl.num_programs(1) - 1

    @pl.when(g == 0)
    def _():
        m_ref[...] = jnp.full(m_ref.shape, -jnp.inf, F32)
        l_ref[...] = jnp.zeros(l_ref.shape, F32)
        acc_ref[...] = jnp.zeros(acc_ref.shape, F32)

    q = q_ref[...] * (1.0 / math.sqrt(B_HEAD))
    qall = jnp.concatenate([_stack_halves(q[:, h * B_VDIM:(h + 1) * B_VDIM]) for h in range(B_HEADS)],
                           axis=0).astype(BF16)

    def process(k_blocks, v_blocks, biases):
        s = jnp.concatenate(
            [lax.dot_general(qall, kb.astype(BF16), (((1,), (1,)), ((), ())), preferred_element_type=F32) + bb
             for kb, bb in zip(k_blocks, biases)], axis=1)
        m_old = m_ref[...]
        m_new = jnp.maximum(m_old, jnp.max(s, axis=-1, keepdims=True))
        alpha = jnp.exp(m_old - m_new)
        pr = jnp.exp(s - m_new)
        l_ref[...] = alpha * l_ref[...] + jnp.sum(pr, axis=-1, keepdims=True)
        prb = pr.astype(BF16)
        pv = None
        off = 0
        for vb in v_blocks:
            n = vb.shape[0]
            t = jnp.dot(prb[:, off:off + n], vb.astype(BF16), preferred_element_type=F32)
            pv = t if pv is None else pv + t
            off += n
        acc_ref[...] = alpha * acc_ref[...] + pv
        m_ref[...] = m_new

    maskb = maskb_ref[...]
    newest = jnp.where(last, biasl_ref[...], maskb)
    process([k_refs[p][...] for p in range(npages)], [v_refs[p][...] for p in range(npages)],
            [maskb] * (npages - 1) + [newest])

    @pl.when(last)
    def _():
        process([kn_ref[0]], [vn_ref[0]], [biasn_ref[...]])
        lam = _lambda(lam_ref)
        acc = acc_ref[...]
        l = l_ref[...]
        for h in range(B_HEADS):
            r0 = h * 2 * ds
            o_ref[:, h * B_VDIM:(h + 1) * B_VDIM] = _diff_finish(
                acc[r0:r0 + 2 * ds], l[r0:r0 + 2 * ds], ds, lam, g_ref[...])


def _attn_s(page_table, qs, kn, vn, maskb, bias_last, bias_new, lam_params, subln_g, cache_k2, cache_v2, *, ds):
    nb, n_pages = page_table.shape
    npg = PAGES_PER_STEP
    steps = n_pages // npg
    page_rows = PAGE_SIZE * B_HEADS
    nrow = B_HEADS * 2 * ds

    def page_spec(p):
        return pl.BlockSpec((page_rows, B_VDIM), lambda b, g, pt, p=p: (pt[b, g * npg + p], 0))

    const2 = lambda b, g, pt: (0, 0)
    grid_spec = pltpu.PrefetchScalarGridSpec(
        num_scalar_prefetch=1,
        grid=(nb, steps),
        in_specs=[pl.BlockSpec((ds, B_WIDTH), lambda b, g, pt: (b, 0)),
                  pl.BlockSpec((1, PAGE_SIZE, B_VDIM), lambda b, g, pt: (b, 0, 0)),
                  pl.BlockSpec((1, PAGE_SIZE, B_VDIM), lambda b, g, pt: (b, 0, 0)),
                  pl.BlockSpec((nrow, page_rows), const2),
                  pl.BlockSpec((nrow, page_rows), const2),
                  pl.BlockSpec((nrow, PAGE_SIZE), const2),
                  pl.BlockSpec((4, B_HEAD), const2),
                  pl.BlockSpec((1, B_VDIM), const2)]
                 + [page_spec(p) for p in range(npg)] + [page_spec(p) for p in range(npg)],
        out_specs=pl.BlockSpec((ds, B_WIDTH), lambda b, g, pt: (b, 0)),
        scratch_shapes=[pltpu.VMEM((nrow, B_VDIM), F32),
                        pltpu.VMEM((nrow, 1), F32),
                        pltpu.VMEM((nrow, 1), F32)])
    return pl.pallas_call(
        functools.partial(_attn_s_kernel, npages=npg, ds=ds),
        grid_spec=grid_spec,
        out_shape=jax.ShapeDtypeStruct((nb * ds, B_WIDTH), F32),
        compiler_params=_cparams("arbitrary", "arbitrary"),
        name="attn_sample",
    )(page_table, qs, kn, vn, maskb, bias_last, bias_new, lam_params, subln_g,
      *([cache_k2] * npg), *([cache_v2] * npg))


def _outproj_kernel(ap_ref, as_ref, bp_ref, bs_ref, xp_ref, xs_ref, mp_ref, ms_ref, ng_ref, wo_ref,
                    wr_ref, br_ref, x1_ref, h2_ref, ti_ref, tr_ref, cnt_ref, base_ref, run_ref,
                    *, n_prompt_tiles):
    i = pl.program_id(0)
    tm = xp_ref.shape[0]
    is_p = i < n_prompt_tiles

    @pl.when(i == 0)
    def _():
        run_ref[...] = jnp.zeros(run_ref.shape, F32)

    a = jnp.where(is_p, ap_ref[...], as_ref[...])
    b = jnp.where(is_p, bp_ref[...], bs_ref[...].astype(BF16))
    x = jnp.where(is_p, xp_ref[...], xs_ref[...])
    g1 = jnp.where(is_p, mp_ref[0, 2:3, :], ms_ref[2])
    sh2 = jnp.where(is_p, mp_ref[0, 3:4, :], ms_ref[3])
    sc2 = jnp.where(is_p, mp_ref[0, 4:5, :], ms_ref[4])

    mix = (jnp.dot(a, wo_ref[0:A_WIDTH, :], preferred_element_type=F32)
           + jnp.dot(b, wo_ref[A_WIDTH:, :], preferred_element_type=F32))
    x1 = x + g1 * _rms(mix, ng_ref[1:2, :])
    x1_ref[...] = x1
    h2 = _rms(x1, ng_ref[2:3, :]) * (1.0 + sc2) + sh2
    h2_ref[:, 0:D_MODEL] = h2

    logits = jnp.dot(h2.astype(BF16), wr_ref[...], preferred_element_type=F32) + br_ref[...]
    work = logits.T[0:N_EXPERTS, :]
    sub = lax.broadcasted_iota(jnp.int32, work.shape, 0)
    vals, idxs = [], []
    for _ in range(TOP_K):
        mx = jnp.max(work, axis=0, keepdims=True)
        ix = jnp.min(jnp.where(work == mx, sub, N_EXPERTS), axis=0, keepdims=True)
        vals.append(mx)
        idxs.append(ix)
        work = jnp.where(sub == ix, -jnp.inf, work)
    exps = [jnp.exp(v - vals[0]) for v in vals]
    den = exps[0] + exps[1] + exps[2] + exps[3]

    sel = jnp.where(work == -jnp.inf, 1.0, 0.0)
    r_i = lax.broadcasted_iota(jnp.int32, (tm, tm), 0)
    c_i = lax.broadcasted_iota(jnp.int32, (tm, tm), 1)
    earlier = jnp.where(r_i < c_i, 1.0, 0.0).astype(BF16)
    base_ref[0] = run_ref[...]
    before = jnp.dot(sel.astype(BF16), earlier, preferred_element_type=F32) + run_ref[...]
    run_ref[...] = run_ref[...] + jnp.sum(sel, axis=1, keepdims=True)
    cnt_ref[...] = run_ref[...]

    ranks = [jnp.sum(jnp.where(sub == ix, before, 0.0), axis=0, keepdims=True) for ix in idxs]
    weights = [e / den for e in exps]
    ti_ref[...] = jnp.concatenate(idxs, axis=0)
    tr_ref[...] = jnp.concatenate(ranks, axis=0).astype(jnp.int32)
    meta = jnp.concatenate(weights + [ix.astype(F32) for ix in idxs]
                           + [jnp.zeros((LANES - 2 * TOP_K, tm), F32)], axis=0)
    h2_ref[:, D_MODEL:] = meta.T


def _outproj(a_p, a_s, b_p, b_s, x_p, x_s, mod_p, mod_s, norm_g, w_out_b, w_router, b_router):
    tm = ROW_TILE
    npt = x_p.shape[0] // tm
    nst = x_s.shape[0] // tm
    assert nst == 1
    n_tiles = npt + nst
    rows = n_tiles * tm
    tiles_per_batch = npt // mod_p.shape[0]
    last_p = npt - 1
    prow = lambda i: (jnp.minimum(i, last_p), 0)
    srow = lambda i: (0, 0)
    row = lambda i: (i, 0)
    col = lambda i: (0, i)
    const2 = lambda i: (0, 0)
    return pl.pallas_call(
        functools.partial(_outproj_kernel, n_prompt_tiles=npt),
        grid=(n_tiles,),
        in_specs=[pl.BlockSpec((tm, A_WIDTH), prow), pl.BlockSpec((tm, A_WIDTH), srow),
                  pl.BlockSpec((tm, B_WIDTH), prow), pl.BlockSpec((tm, B_WIDTH), srow),
                  pl.BlockSpec((tm, D_MODEL), prow), pl.BlockSpec((tm, D_MODEL), srow),
                  pl.BlockSpec((1, 6, D_MODEL), lambda i: (jnp.minimum(i, last_p) // tiles_per_batch, 0, 0)),
                  pl.BlockSpec((6, tm, D_MODEL), lambda i: (0, 0, 0)),
                  pl.BlockSpec((4, D_MODEL), const2),
                  pl.BlockSpec((D_MODEL, D_MODEL), const2),
                  pl.BlockSpec((D_MODEL, LANES), const2),
                  pl.BlockSpec((1, LANES), const2)],
        out_specs=[pl.BlockSpec((tm, D_MODEL), row), pl.BlockSpec((tm, ROW_WIDTH), row),
                   pl.BlockSpec((TOP_K, tm), col), pl.BlockSpec((TOP_K, tm), col),
                   pl.BlockSpec((N_EXPERTS, 1), const2),
                   pl.BlockSpec((1, N_EXPERTS, 1), lambda i: (i, 0, 0))],
        out_shape=[jax.ShapeDtypeStruct((rows, D_MODEL), F32), jax.ShapeDtypeStruct((rows, ROW_WIDTH), F32),
                   jax.ShapeDtypeStruct((TOP_K, rows), jnp.int32), jax.ShapeDtypeStruct((TOP_K, rows), jnp.int32),
                   jax.ShapeDtypeStruct((N_EXPERTS, 1), F32),
                   jax.ShapeDtypeStruct((n_tiles, N_EXPERTS, 1), F32)],
        scratch_shapes=[pltpu.VMEM((N_EXPERTS, 1), F32)],
        compiler_params=_cparams("arbitrary"),
        name="outproj_router",
    )(a_p, a_s, b_p, b_s, x_p, x_s, mod_p, mod_s, norm_g, w_out_b,
      jnp.pad(w_router, ((0, 0), (0, LANES - N_EXPERTS))).astype(BF16),
      jnp.pad(b_router.reshape(1, -1), ((0, 0), (0, LANES - N_EXPERTS))))


def _dispatch_kernel(pe_ref, cnt_ref, dest_ref, h_ref, xg_ref, zero_ref, sem, zsem, *, bm):
    tm = h_ref.shape[0]
    n_blocks = xg_ref.shape[0] // bm

    def zero_block(row0):
        return pltpu.make_async_copy(zero_ref, xg_ref.at[pl.ds(pl.multiple_of(row0, bm), bm), :], zsem)

    @pl.when(pl.program_id(0) == 0)
    def _():
        zero_ref[...] = jnp.zeros(zero_ref.shape, F32)
        first_unused = pe_ref[N_EXPERTS - 1] // bm
        for e in range(N_EXPERTS):
            @pl.when(cnt_ref[e] > 0)
            def _():
                zero_block(pe_ref[e] - bm).start()

        def start_unused(b, carry):
            zero_block(b * bm).start()
            return carry

        lax.fori_loop(first_unused, n_blocks, start_unused, 0)
        for e in range(N_EXPERTS):
            @pl.when(cnt_ref[e] > 0)
            def _():
                zero_block(pe_ref[e] - bm).wait()

        def wait_unused(b, carry):
            zero_block(b * bm).wait()
            return carry

        lax.fori_loop(first_unused, n_blocks, wait_unused, 0)

    for sub in range(tm // ROW_TILE):
        def body(t, carry, sub=sub):
            for j in range(TOP_K):
                d = dest_ref[(sub * TOP_K + j) * ROW_TILE + t]
                pltpu.make_async_copy(h_ref.at[pl.ds(sub * ROW_TILE + t, 1), :], xg_ref.at[pl.ds(d, 1), :],
                                      sem).start()
            return carry

        lax.fori_loop(0, ROW_TILE, body, 0)
    n = tm * TOP_K
    pltpu.make_async_copy(xg_ref.at[pl.ds(0, n), :], xg_ref.at[pl.ds(0, n), :], sem).wait()


def _dispatch(pad_end, cnt, dest_flat, h2, n_blocks):
    tm = ROW_TILE * DISPATCH_TILES
    bm = EXPERT_BLOCK
    rows = h2.shape[0]
    assert rows % tm == 0
    grid_spec = pltpu.PrefetchScalarGridSpec(
        num_scalar_prefetch=2,
        grid=(rows // tm,),
        in_specs=[pl.BlockSpec((tm * TOP_K,), lambda i, pe, cn: (i,), memory_space=pltpu.SMEM),
                  pl.BlockSpec((tm, ROW_WIDTH), lambda i, pe, cn: (i, 0))],
        out_specs=pl.BlockSpec(memory_space=pl.ANY),
        scratch_shapes=[pltpu.VMEM((bm, ROW_WIDTH), F32), pltpu.SemaphoreType.DMA(()), pltpu.SemaphoreType.DMA(())])
    return pl.pallas_call(
        functools.partial(_dispatch_kernel, bm=bm),
        grid_spec=grid_spec,
        out_shape=jax.ShapeDtypeStruct((n_blocks * bm, ROW_WIDTH), F32),
        compiler_params=_cparams("arbitrary"),
        name="moe_dispatch",
    )(pad_end, cnt, dest_flat, h2)


def _expert_kernel(be_ref, nu_ref, x_ref, wgu_ref, bgu_ref, wd_ref, bd_ref, y_ref, wgu_b, wd_b):
    i = pl.program_id(0)
    used = i < nu_ref[0]
    fresh = jnp.logical_or(i == 0, be_ref[i] != be_ref[jnp.maximum(i - 1, 0)])

    @pl.when(jnp.logical_and(used, fresh))
    def _():
        wgu_b[...] = wgu_ref[0].astype(BF16)
        wd_b[...] = wd_ref[0].astype(BF16)

    @pl.when(jnp.logical_not(used))
    def _():
        y_ref[...] = jnp.zeros(y_ref.shape, y_ref.dtype)

    @pl.when(used)
    def _():
        x = x_ref[:, 0:D_MODEL].astype(BF16)
        gu = jnp.dot(x, wgu_b[...], preferred_element_type=F32) + bgu_ref[0]
        glu = jnp.minimum(gu[:, :D_EXPERT], SWIGLU_LIMIT)
        lin = jnp.clip(gu[:, D_EXPERT:], -SWIGLU_LIMIT, SWIGLU_LIMIT)
        hid = glu * jax.nn.sigmoid(SWIGLU_ALPHA * glu) * (lin + 1.0)
        y = jnp.dot(hid.astype(BF16), wd_b[...], preferred_element_type=F32) + bd_ref[0]
        meta = x_ref[:, D_MODEL:]
        me = be_ref[i].astype(F32)
        w = jnp.zeros((x_ref.shape[0], 1), F32)
        for j in range(TOP_K):
            w = w + jnp.where(meta[:, TOP_K + j:TOP_K + j + 1] == me, meta[:, j:j + 1], 0.0)
        y_ref[...] = (y * w).astype(y_ref.dtype)


def _experts(block_e, n_used, xg, w_gu_b, b_gu, w_down_b, b_down):
    bm = EXPERT_BLOCK
    n_blocks = xg.shape[0] // bm
    grid_spec = pltpu.PrefetchScalarGridSpec(
        num_scalar_prefetch=2,
        grid=(n_blocks,),
        in_specs=[pl.BlockSpec((bm, ROW_WIDTH), lambda i, be, nu: (jnp.minimum(i, nu[0] - 1), 0)),
                  pl.BlockSpec((1, D_MODEL, 2 * D_EXPERT), lambda i, be, nu: (be[i], 0, 0)),
                  pl.BlockSpec((1, 1, 2 * D_EXPERT), lambda i, be, nu: (be[i], 0, 0)),
                  pl.BlockSpec((1, D_EXPERT, D_MODEL), lambda i, be, nu: (be[i], 0, 0)),
                  pl.BlockSpec((1, 1, D_MODEL), lambda i, be, nu: (be[i], 0, 0))],
        out_specs=pl.BlockSpec((bm, D_MODEL), lambda i, be, nu: (i, 0)),
        scratch_shapes=[pltpu.VMEM((D_MODEL, 2 * D_EXPERT), BF16), pltpu.VMEM((D_EXPERT, D_MODEL), BF16)])
    return pl.pallas_call(
        _expert_kernel,
        grid_spec=grid_spec,
        out_shape=jax.ShapeDtypeStruct((xg.shape[0], D_MODEL), BF16),
        compiler_params=_cparams("arbitrary"),
        name="moe_experts",
    )(block_e, n_used, xg, w_gu_b, b_gu.reshape(N_EXPERTS, 1, -1), w_down_b, b_down.reshape(N_EXPERTS, 1, -1))


def _combine_kernel(row0_ref, col_ref, x1_ref, mp_ref, ms_ref, ng_ref, yg_ref,
                    yp_ref, ys_ref, rows_ref, g_ref, sem, *, n_prompt_tiles):
    i = pl.program_id(0)
    n_steps = pl.num_programs(0)
    tm = x1_ref.shape[0]
    is_p = i < n_prompt_tiles
    win = COMBINE_WINDOW
    n_win_rows = COMBINE_SLOTS * win
    slot = i % 2

    def fetch_windows(tile, buf):
        for s in range(COMBINE_SLOTS):
            row0 = pl.multiple_of(row0_ref[tile * COMBINE_SLOTS + s], COMBINE_ALIGN)
            pltpu.make_async_copy(yg_ref.at[pl.ds(row0, win), :], rows_ref.at[buf, pl.ds(s * win, win), :],
                                  sem.at[buf]).start()

    @pl.when(i == 0)
    def _():
        fetch_windows(0, 0)

    @pl.when(i + 1 < n_steps)
    def _():
        fetch_windows(i + 1, 1 - slot)

    pltpu.make_async_copy(yg_ref.at[pl.ds(0, n_win_rows), :], rows_ref.at[slot], sem.at[slot]).wait()
    col = col_ref[...]
    for c in range(n_win_rows // LANES):
        lane = lax.broadcasted_iota(jnp.int32, (tm, LANES), 1) + c * LANES
        g = jnp.zeros((tm, LANES), F32)
        for j in range(TOP_K):
            g = jnp.where(lane == col[:, j:j + 1], 1.0, g)
        g_ref[:, c * LANES:(c + 1) * LANES] = g.astype(BF16)
    f = jnp.dot(g_ref[...], rows_ref[slot], preferred_element_type=F32)

    g2 = jnp.where(is_p, mp_ref[0, 5:6, :], ms_ref[5])
    y = x1_ref[...] + g2 * _rms(f, ng_ref[3:4, :])

    @pl.when(is_p)
    def _():
        yp_ref[...] = y

    @pl.when(jnp.logical_not(is_p))
    def _():
        ys_ref[...] = y


def _combine(row0, col, x1, mod_p, mod_s, norm_g, yg, *, n_prompt_rows, n_sample_rows):
    tm = ROW_TILE
    npt = n_prompt_rows // tm
    assert n_sample_rows == tm and (COMBINE_SLOTS * COMBINE_WINDOW) % LANES == 0
    n_tiles = npt + 1
    tiles_per_batch = npt // mod_p.shape[0]
    last_p = npt - 1
    grid_spec = pltpu.PrefetchScalarGridSpec(
        num_scalar_prefetch=1,
        grid=(n_tiles,),
        in_specs=[pl.BlockSpec((tm, TOP_K), lambda i, r0: (i, 0)),
                  pl.BlockSpec((tm, D_MODEL), lambda i, r0: (i, 0)),
                  pl.BlockSpec((1, 6, D_MODEL), lambda i, r0: (jnp.minimum(i, last_p) // tiles_per_batch, 0, 0)),
                  pl.BlockSpec((6, tm, D_MODEL), lambda i, r0: (0, 0, 0)),
                  pl.BlockSpec((4, D_MODEL), lambda i, r0: (0, 0)),
                  pl.BlockSpec(memory_space=pl.ANY)],
        out_specs=[pl.BlockSpec((tm, D_MODEL), lambda i, r0: (jnp.minimum(i, last_p), 0)),
                   pl.BlockSpec((tm, D_MODEL), lambda i, r0: (0, 0))],
        scratch_shapes=[pltpu.VMEM((2, COMBINE_SLOTS * COMBINE_WINDOW, D_MODEL), BF16),
                        pltpu.VMEM((tm, COMBINE_SLOTS * COMBINE_WINDOW), BF16),
                        pltpu.SemaphoreType.DMA((2,))])
    return pl.pallas_call(
        functools.partial(_combine_kernel, n_prompt_tiles=npt),
        grid_spec=grid_spec,
        out_shape=[jax.ShapeDtypeStruct((n_prompt_rows, D_MODEL), F32),
                   jax.ShapeDtypeStruct((n_sample_rows, D_MODEL), F32)],
        compiler_params=_cparams("arbitrary"),
        name="moe_combine",
    )(row0, col, x1, mod_p, mod_s, norm_g, yg)


def _t5_bucket_np(dist):
    n = np.maximum(dist, 0)
    max_exact = NUM_BUCKETS // 2
    nf = np.maximum(n, 1).astype(np.float64)
    large = max_exact + (np.log(nf / max_exact) / math.log(MAX_DISTANCE / max_exact)
                         * (NUM_BUCKETS - max_exact)).astype(np.int32)
    large = np.minimum(large, NUM_BUCKETS - 1)
    return np.where(n < max_exact, n, large).astype(np.int32)


def _bias_from_dist(rel_bias, dist):
    onehot = jax.nn.one_hot(_t5_bucket_np(dist).reshape(-1), NUM_BUCKETS, dtype=F32)
    shifted = (rel_bias - rel_bias[NUM_BUCKETS - 1]).reshape(NUM_BUCKETS, -1)
    out = jnp.dot(onehot, shifted, precision=lax.Precision.HIGHEST).reshape(dist.shape + rel_bias.shape[1:])
    return jnp.where(jnp.asarray(dist >= 0)[..., None, None], out, -jnp.inf).astype(F32)


def _head_masked(t):
    h = t.shape[0]
    same = np.eye(h, dtype=bool)[:, None, None, None, :]
    full = jnp.where(same, t[..., None], -jnp.inf)
    return full.reshape(h * t.shape[1] * t.shape[2], t.shape[3] * h)


def kernel(x_prompt, x_sample, cache_k, cache_v, page_table, c_prompt, c_sample, w_ada, b_ada, norm_g, w_in,
           w_out, ln_v_g, ln_v_b, w_spatial, b_spatial, lam_params, subln_g, rel_bias, w_router, b_router,
           w_gu, b_gu, w_down, b_down):
    batch, seq, d = x_prompt.shape
    nb, ds = x_sample.shape[:2]
    n_pages = page_table.shape[1]
    past = n_pages * PAGE_SIZE
    rows_p = batch * seq
    rows_s = nb * ds
    assert rows_s == ROW_TILE and d == D_MODEL

    mod = _ada(jnp.concatenate([c_prompt, c_sample], axis=0), w_ada[0], b_ada[0])
    mod_p = mod[:batch].reshape(batch, 6, D_MODEL)
    mod_s = jnp.transpose(jnp.repeat(mod[batch:], ds, axis=0).reshape(rows_s, 6, D_MODEL), (1, 0, 2))

    ng = norm_g[0]
    w_in_b = w_in[0].astype(BF16)
    w_out_b = w_out[0].astype(BF16)
    lng = ln_v_g[0].reshape(1, A_WIDTH)
    lnb = ln_v_b[0].reshape(1, A_WIDTH)
    grp = np.arange(A_WIDTH) // A_HEAD
    avg = jnp.asarray((grp[:, None] == grp[None, :]).astype(np.float32) / A_HEAD, BF16)
    ws_p = jnp.tril(w_spatial[0]).astype(BF16)
    bs_p = jnp.repeat(b_spatial[0].T, A_HEAD, axis=1)
    w_small = jnp.tril(w_spatial[0][:, :ds, :ds])
    same_batch = np.kron(np.eye(nb, dtype=np.float32), np.ones((ds, ds), np.float32))
    ws_s = (jnp.tile(w_small, (1, nb, nb)) * same_batch).astype(BF16)
    bs_s = jnp.tile(jnp.repeat(b_spatial[0][:, :ds].T, A_HEAD, axis=1), (nb, 1))

    xp2 = x_prompt.reshape(rows_p, D_MODEL)
    xs2 = x_sample.reshape(rows_s, D_MODEL)
    assert ATTN_TILE % ROW_TILE == 0 and seq % ATTN_TILE == 0
    a_p, va_p, q_p, k_p, v_p, kb_p, vt_p = _inproj(
        xp2, mod_p[:, 0:1], mod_p[:, 1:2], ng[0:1], w_in_b, lng, lnb, avg, ws_p, bs_p,
        tiles_per_mod=seq // ROW_TILE, chunk=CHUNK, q_dtype=BF16, q_scale=LOG2E / math.sqrt(B_HEAD), va_rows=CHUNK)
    a_s, va_s, q_s, k_s, v_s, _, _ = _inproj(
        xs2, mod_s[0:1], mod_s[1:2], ng[0:1], w_in_b, lng, lnb, avg, ws_s, bs_s,
        tiles_per_mod=1, chunk=rows_s, q_dtype=F32, q_scale=1.0, va_rows=rows_s)

    ti = np.arange(ATTN_TILE)
    near = ti[:, None] - ti[None, :]
    far = np.full_like(near, 4 * MAX_DISTANCE)
    dist_p = np.stack([near, near + ATTN_TILE, far])
    bias_p = jnp.transpose(_bias_from_dist(rel_bias, dist_p), (3, 0, 2, 4, 1))
    bias_p = bias_p.reshape(B_HEADS, 3, ATTN_TILE, 2 * ATTN_TILE) * LOG2E
    b_p = _attn_p(q_p, kb_p, vt_p, bias_p, lam_params[0], subln_g, batch=batch, seq=seq)

    qi = np.arange(ds)
    ki = np.arange(PAGE_SIZE)
    dist_l = PAGE_SIZE + qi[:, None] - ki[None, :]
    kn_i = np.arange(PAGE_SIZE // B_HEADS)
    dist_n = np.where(kn_i[None, :] < ds, qi[:, None] - kn_i[None, :], -1)
    to_rows = lambda t: jnp.transpose(t, (2, 3, 0, 1))
    bias_l = _head_masked(to_rows(_bias_from_dist(rel_bias, dist_l)))
    bias_n = _head_masked(to_rows(_bias_from_dist(rel_bias, dist_n)))
    maskb = _head_masked(jnp.zeros((B_HEADS, 2, ds, PAGE_SIZE), F32))
    pad_keys = lambda t: jnp.pad(t.reshape(nb, ds * B_HEADS, B_VDIM), ((0, 0), (0, PAGE_SIZE - ds * B_HEADS), (0, 0)))
    n_phys = cache_k.shape[0]
    b_s = _attn_s(page_table, q_s, pad_keys(k_s), pad_keys(v_s), maskb, bias_l, bias_n, lam_params[0], subln_g,
                  cache_k.reshape(n_phys * PAGE_SIZE * B_HEADS, B_VDIM),
                  cache_v.reshape(n_phys * PAGE_SIZE * B_HEADS, B_VDIM), ds=ds)

    x1, h2, top_i, top_r, counts, tile_base = _outproj(
        a_p, a_s, b_p, b_s, xp2, xs2, mod_p, mod_s, ng, w_out_b, w_router[0], b_router[0])

    bm = EXPERT_BLOCK
    rows = rows_p + rows_s
    n_blocks = rows * TOP_K // bm + N_EXPERTS + 1
    cnt = counts[:, 0].astype(jnp.int32)
    padded = (cnt + bm - 1) // bm * bm
    pad_end = jnp.cumsum(padded)
    pad_start = pad_end - padded
    experts = jnp.arange(N_EXPERTS, dtype=jnp.int32)
    chosen = top_i[..., None] == experts
    dest = jnp.sum(jnp.where(chosen, pad_start, 0), axis=-1) + top_r
    n_tiles = rows // ROW_TILE
    win = COMBINE_WINDOW
    assert win % COMBINE_ALIGN == 0
    assert (ROW_TILE * TOP_K + N_EXPERTS * (COMBINE_ALIGN - 1)) // win + N_EXPERTS <= COMBINE_SLOTS
    base = tile_base[:, :, 0].astype(jnp.int32)
    first = pad_start[None, :] + base
    win0 = first // COMBINE_ALIGN * COMBINE_ALIGN
    sent = jnp.concatenate([base[1:], cnt[None, :]], axis=0) - base
    n_win = jnp.where(sent > 0, (first - win0 + sent + win - 1) // win, 0)
    slot_end = jnp.cumsum(n_win, axis=1)
    slot_start = slot_end - n_win
    slots = jnp.arange(COMBINE_SLOTS, dtype=jnp.int32)
    slot_e = jnp.minimum(jnp.sum((slot_end[:, None, :] <= slots[None, :, None]).astype(jnp.int32), axis=2),
                         N_EXPERTS - 1)
    pick = slot_e[..., None] == experts
    slot_row0 = (jnp.sum(jnp.where(pick, win0[:, None, :], 0), axis=2)
                 + win * (slots[None, :] - jnp.sum(jnp.where(pick, slot_start[:, None, :], 0), axis=2)))
    slot_row0 = jnp.where(slots[None, :] < slot_end[:, -1:], slot_row0, 0)
    local = dest - jnp.sum(jnp.where(chosen, jnp.repeat(win0, ROW_TILE, axis=0), 0), axis=-1)
    col = jnp.sum(jnp.where(chosen, jnp.repeat(slot_start, ROW_TILE, axis=0), 0), axis=-1) * win + local
    dest = jnp.transpose(dest.reshape(TOP_K, n_tiles, ROW_TILE), (1, 0, 2)).reshape(-1)
    blk_start = jnp.arange(n_blocks, dtype=jnp.int32) * bm
    block_e = jnp.minimum(jnp.sum((pad_end[None, :] <= blk_start[:, None]).astype(jnp.int32), axis=1),
                          N_EXPERTS - 1)
    n_used = (pad_end[-1:] // bm).astype(jnp.int32)

    xg = _dispatch(pad_end.astype(jnp.int32), cnt, dest, h2, n_blocks)
    yg = _experts(block_e, n_used, xg, w_gu[0], b_gu[0], w_down[0], b_down[0])
    y_p, y_s = _combine(slot_row0.reshape(-1), col.T, x1, mod_p, mod_s, ng, yg,
                        n_prompt_rows=rows_p, n_sample_rows=rows_s)

    return (y_p.reshape(batch, seq, D_MODEL),
            y_s.reshape(nb, ds, D_MODEL),
            k_p.reshape(batch, seq, 1, B_HEADS, B_VDIM),
            v_p.reshape(batch, seq, 1, B_HEADS, B_VDIM),
            k_s.reshape(nb, ds, 1, B_HEADS, B_VDIM),
            v_s.reshape(nb, ds, 1, B_HEADS, B_VDIM),
            va_p.reshape(batch, CHUNK, 1, A_WIDTH),
            va_s.reshape(nb, ds, 1, A_WIDTH))
```

```python
import functools
import math

import numpy as np
import jax
import jax.numpy as jnp
from jax import lax
from jax.experimental import pallas as pl
from jax.experimental.pallas import tpu as pltpu

F32 = jnp.float32
BF16 = jnp.bfloat16

D_MODEL = 1024
A_WIDTH = 512
A_HEAD = 64
A_GROUPS = 8
CHUNK = 128
B_WIDTH = 512
B_HEAD = 64
B_VDIM = 128
B_HEADS = 4
IN_WIDTH = 2 * A_WIDTH + 3 * B_WIDTH
NUM_BUCKETS = 32
MAX_DISTANCE = 128
PAGE_SIZE = 128
N_EXPERTS = 32
TOP_K = 4
D_EXPERT = 1024
SWIGLU_LIMIT = 7.0
SWIGLU_ALPHA = 1.702
NORM_EPS = 1e-6
LAM_INIT = 0.8 - 0.6 * math.exp(-0.3 * 0)

ROW_TILE = 256
ATTN_TILE = 512
ATTN_BATCHES = 1
PAGES_PER_STEP = 32
LOG2E = math.log2(math.e)
LANES = 128
MXU_WIDTH = 256
ROW_WIDTH = D_MODEL + LANES
EXPERT_BLOCK = 512
DISPATCH_TILES = 5
COMBINE_ALIGN = 16
COMBINE_WINDOW = 16
COMBINE_SLOTS = 128
VMEM_LIMIT = 56 * 1024 * 1024


def _cparams(*sem):
    return pltpu.CompilerParams(dimension_semantics=sem, vmem_limit_bytes=VMEM_LIMIT)


def _rms(x, g):
    return x * lax.rsqrt(jnp.mean(x * x, axis=-1, keepdims=True) + NORM_EPS) * g


def _ada_kernel(c_ref, w_ref, b_ref, o_ref):
    s = jax.nn.silu(c_ref[...]).astype(BF16)
    o_ref[...] = jnp.dot(s, w_ref[...].astype(BF16), preferred_element_type=F32) + b_ref[...]


def _ada(c_all, w_ada, b_ada):
    n = c_all.shape[0]
    tn = 1024
    return pl.pallas_call(
        _ada_kernel,
        grid=(6 * D_MODEL // tn,),
        in_specs=[pl.BlockSpec((n, D_MODEL), lambda j: (0, 0)),
                  pl.BlockSpec((D_MODEL, tn), lambda j: (0, j)),
                  pl.BlockSpec((1, tn), lambda j: (0, j))],
        out_specs=pl.BlockSpec((n, tn), lambda j: (0, j)),
        out_shape=jax.ShapeDtypeStruct((n, 6 * D_MODEL), F32),
        compiler_params=_cparams("arbitrary"),
        name="ada",
    )(c_all, w_ada, b_ada.reshape(1, -1))


def _inproj_kernel(x_ref, sh_ref, sc_ref, ng_ref, w_ref, lng_ref, lnb_ref, avg_ref, ws_ref, bs_ref,
                   a_ref, va_ref, q_ref, k_ref, v_ref, kb_ref, vt_ref, *, chunk, q_scale):
    tm = x_ref.shape[0]
    va_rows = va_ref.shape[0]
    h = (_rms(x_ref[...], ng_ref[...]) * (1.0 + sc_ref[0]) + sh_ref[0]).astype(BF16)

    u = jax.nn.gelu(jnp.dot(h, w_ref[:, 0:A_WIDTH], preferred_element_type=F32))
    gv = jax.nn.gelu(jnp.dot(h, w_ref[:, A_WIDTH:2 * A_WIDTH], preferred_element_type=F32))

    avg = avg_ref[...]

    def group_mean(t):
        hi = t.astype(BF16)
        lo = (t - hi.astype(F32)).astype(BF16)
        return (jnp.dot(hi, avg, preferred_element_type=F32)
                + jnp.dot(lo, avg, preferred_element_type=F32))

    xc = gv - group_mean(gv)
    va = xc * lax.rsqrt(group_mean(xc * xc) + NORM_EPS) * lng_ref[...] + lnb_ref[...]
    va_ref[...] = va[tm - va_rows:, :]

    vab = va.astype(BF16)
    lane = lax.broadcasted_iota(jnp.int32, (chunk, 128), 1)
    for c in range(tm // chunk):
        r0 = c * chunk
        for pair in range(A_GROUPS // 2):
            c0 = pair * 128
            vp = vab[r0:r0 + chunk, c0:c0 + 128]
            lo_half = jnp.where(lane < A_HEAD, vp, jnp.zeros_like(vp))
            hi_half = jnp.where(lane >= A_HEAD, vp, jnp.zeros_like(vp))
            s = (jnp.dot(ws_ref[2 * pair], lo_half, preferred_element_type=F32)
                 + jnp.dot(ws_ref[2 * pair + 1], hi_half, preferred_element_type=F32))
            a = u[r0:r0 + chunk, c0:c0 + 128] * (s + bs_ref[:, c0:c0 + 128])
            a_ref[r0:r0 + chunk, c0:c0 + 128] = a.astype(a_ref.dtype)

    q = jnp.dot(h, w_ref[:, 2 * A_WIDTH:2 * A_WIDTH + B_WIDTH], preferred_element_type=F32)
    q_ref[...] = (q * q_scale).astype(q_ref.dtype)
    k = jnp.dot(h, w_ref[:, 2 * A_WIDTH + B_WIDTH:2 * A_WIDTH + 2 * B_WIDTH], preferred_element_type=F32)
    kb_ref[...] = k.astype(BF16)
    v = jnp.dot(h, w_ref[:, 2 * A_WIDTH + 2 * B_WIDTH:IN_WIDTH], preferred_element_type=F32)
    for hh in range(B_HEADS):
        c0 = hh * B_VDIM
        k_ref[pl.ds(hh, tm, stride=B_HEADS), :] = k[:, c0:c0 + B_VDIM]
        v_ref[pl.ds(hh, tm, stride=B_HEADS), :] = v[:, c0:c0 + B_VDIM]
        vt_ref[0, hh, 0] = v[:, c0:c0 + B_VDIM].T.astype(BF16)


def _inproj(x2, sh, sc, ng, w_in_b, lng, lnb, avg, ws, bs, *, tiles_per_mod, chunk, q_dtype, q_scale, va_rows):
    rows = x2.shape[0]
    tm = ROW_TILE
    n_tiles = rows // tm
    mod_rows = sh.shape[1]
    row = lambda i: (i, 0)
    const2 = lambda i: (0, 0)
    mod_map = lambda i: (i // tiles_per_mod, 0, 0)
    out_w = lambda w, dt: jax.ShapeDtypeStruct((rows, w), dt)
    head_rows = jax.ShapeDtypeStruct((rows * B_HEADS, B_VDIM), F32)
    head_rows_spec = pl.BlockSpec((tm * B_HEADS, B_VDIM), row)
    vt_shape = jax.ShapeDtypeStruct((n_tiles // tiles_per_mod, B_HEADS, tiles_per_mod, B_VDIM, tm), BF16)
    vt_spec = pl.BlockSpec((1, B_HEADS, 1, B_VDIM, tm), lambda i: (i // tiles_per_mod, 0, i % tiles_per_mod, 0, 0))
    return pl.pallas_call(
        functools.partial(_inproj_kernel, chunk=chunk, q_scale=q_scale),
        grid=(rows // tm,),
        in_specs=[pl.BlockSpec((tm, D_MODEL), row),
                  pl.BlockSpec((1, mod_rows, D_MODEL), mod_map),
                  pl.BlockSpec((1, mod_rows, D_MODEL), mod_map),
                  pl.BlockSpec((1, D_MODEL), const2),
                  pl.BlockSpec((D_MODEL, IN_WIDTH), const2),
                  pl.BlockSpec((1, A_WIDTH), const2),
                  pl.BlockSpec((1, A_WIDTH), const2),
                  pl.BlockSpec((A_WIDTH, A_WIDTH), const2),
                  pl.BlockSpec((A_GROUPS, chunk, chunk), lambda i: (0, 0, 0)),
                  pl.BlockSpec((chunk, A_WIDTH), const2)],
        out_specs=[pl.BlockSpec((tm, A_WIDTH), row),
                   pl.BlockSpec((va_rows, A_WIDTH), lambda i: (i // tiles_per_mod, 0)),
                   pl.BlockSpec((tm, B_WIDTH), row), head_rows_spec, head_rows_spec,
                   pl.BlockSpec((tm, B_WIDTH), row), vt_spec],
        out_shape=[out_w(A_WIDTH, BF16),
                   jax.ShapeDtypeStruct((n_tiles // tiles_per_mod * va_rows, A_WIDTH), F32),
                   out_w(B_WIDTH, q_dtype), head_rows, head_rows, out_w(B_WIDTH, BF16), vt_shape],
        compiler_params=_cparams("arbitrary"),
        name="inproj",
    )(x2, sh, sc, ng, w_in_b, lng, lnb, avg, ws, bs)


def _stack_halves(q):
    lane = lax.broadcasted_iota(jnp.int32, q.shape, 1)
    zero = jnp.zeros_like(q)
    return jnp.concatenate([jnp.where(lane < B_HEAD, q, zero), jnp.where(lane >= B_HEAD, q, zero)], axis=0)


def _lambda(lam_ref):
    lp = lam_ref[...]
    return (jnp.exp(jnp.sum(lp[0:1] * lp[1:2], axis=-1, keepdims=True))
            - jnp.exp(jnp.sum(lp[2:3] * lp[3:4], axis=-1, keepdims=True)) + LAM_INIT)


def _diff_finish(acc, l, n, lam, g):
    o = acc[:n] / l[:n] - lam * (acc[n:] / l[n:])
    return _rms(o, g) * (1.0 - LAM_INIT)


def _attn_p_kernel(q_ref, k_ref, vt_ref, bias_ref, lam_ref, g_ref, o_ref,
                   acc_ref, m_ref, l_ref, s_ref, tmax_ref, p_ref, alpha_ref, *, tq, tk):
    qi = pl.program_id(2)
    nbb = q_ref.shape[0]
    streams = [(bb, c) for bb in range(nbb) for c in range(2)]
    q_half = []
    for bb in range(nbb):
        q = q_ref[bb]
        lane = lax.broadcasted_iota(jnp.int32, q.shape, 1)
        zero = jnp.zeros_like(q)
        q_half += [jnp.where(lane < B_HEAD, q, zero), jnp.where(lane >= B_HEAD, q, zero)]
    m_ref[...] = jnp.full(m_ref.shape, -jnp.inf, F32)
    l_ref[...] = jnp.zeros(l_ref.shape, F32)
    acc_ref[...] = jnp.zeros(acc_ref.shape, F32)
    p_ref[1] = jnp.zeros(p_ref.shape[1:], BF16)
    alpha_ref[1] = jnp.ones(alpha_ref.shape[1:], F32)

    def scores(j):
        out = []
        for bb in range(nbb):
            k = k_ref[bb, pl.ds(pl.multiple_of(j * tk, tk), tk), :]
            out += [lax.dot_general(k, q_half[2 * bb + c], (((1,), (1,)), ((), ())), preferred_element_type=F32)
                    for c in range(2)]
        return out

    def keep_scores(j, kind, ss):
        slot = j % 2
        for u, (bb, c) in enumerate(streams):
            s = ss[u]
            if kind is not None:
                s = s + bias_ref[0, kind, :, c * tq:(c + 1) * tq]
            s_ref[slot, u] = s
            tmax_ref[slot, u] = jnp.max(s, axis=0, keepdims=True)

    def softmax(j):
        slot = j % 2
        for u in range(len(streams)):
            m_old = m_ref[u]
            m_new = jnp.maximum(m_old, tmax_ref[slot, u])
            alpha = jnp.exp2(m_old - m_new)
            p = jnp.exp2(s_ref[slot, u] - m_new)
            l_ref[u] = alpha * l_ref[u] + jnp.sum(p, axis=0, keepdims=True)
            m_ref[u] = m_new
            p_ref[slot, u] = p.astype(BF16)
            alpha_ref[slot, u] = alpha

    def pv(j):
        slot = j % 2
        pieces = tk // vt_ref.shape[-1]
        first = jnp.maximum(j, 0) * pieces
        for u, (bb, c) in enumerate(streams):
            add = None
            for r in range(pieces):
                w = vt_ref.shape[-1]
                t = jnp.dot(vt_ref[bb, 0, first + r], p_ref[slot, u, r * w:(r + 1) * w, :],
                            preferred_element_type=F32)
                add = t if add is None else add + t
            acc_ref[u] = alpha_ref[slot, u] * acc_ref[u] + add

    def far_body(j, carry):
        ss = scores(j + 1)
        pv(j - 1)
        softmax(j)
        keep_scores(j + 1, None, ss)
        return carry

    def near_body(j, carry):
        nxt = jnp.minimum(j + 1, qi)
        ss = scores(nxt)
        pv(j - 1)
        softmax(j)
        keep_scores(nxt, qi - nxt, ss)
        return carry

    keep_scores(0, jnp.minimum(qi, 2), scores(0))
    n_far_fetch = jnp.maximum(qi - 2, 0)
    lax.fori_loop(0, n_far_fetch, far_body, 0)
    lax.fori_loop(n_far_fetch, qi + 1, near_body, 0)
    pv(qi)

    lam = _lambda(lam_ref)
    for bb in range(nbb):
        o = acc_ref[2 * bb] / l_ref[2 * bb] - lam * (acc_ref[2 * bb + 1] / l_ref[2 * bb + 1])
        y = o * lax.rsqrt(jnp.mean(o * o, axis=0, keepdims=True) + NORM_EPS) * (1.0 - LAM_INIT)
        o_ref[bb] = (y.T * g_ref[...]).astype(o_ref.dtype)


def _attn_p(qb, kb, vt, bias, lam_params, subln_g, *, batch, seq):
    tq = tk = ATTN_TILE
    nq = seq // tq
    nbb = ATTN_BATCHES
    ns = 2 * nbb
    assert batch % nbb == 0
    out = pl.pallas_call(
        functools.partial(_attn_p_kernel, tq=tq, tk=tk),
        grid=(batch // nbb, B_HEADS, nq),
        in_specs=[pl.BlockSpec((nbb, tq, B_VDIM), lambda b, h, i: (b, i, h)),
                  pl.BlockSpec((nbb, seq, B_VDIM), lambda b, h, i: (b, 0, h)),
                  pl.BlockSpec((nbb, 1, seq // ROW_TILE, B_VDIM, ROW_TILE), lambda b, h, i: (b, h, 0, 0, 0)),
                  pl.BlockSpec((1, 3, tk, 2 * tq), lambda b, h, i: (h, 0, 0, 0)),
                  pl.BlockSpec((4, B_HEAD), lambda b, h, i: (0, 0)),
                  pl.BlockSpec((1, B_VDIM), lambda b, h, i: (0, 0))],
        out_specs=pl.BlockSpec((nbb, tq, B_VDIM), lambda b, h, i: (b, i, h)),
        out_shape=jax.ShapeDtypeStruct((batch, seq, B_WIDTH), BF16),
        scratch_shapes=[pltpu.VMEM((ns, B_VDIM, tq), F32),
                        pltpu.VMEM((ns, 1, tq), F32),
                        pltpu.VMEM((ns, 1, tq), F32),
                        pltpu.VMEM((2, ns, tk, tq), F32),
                        pltpu.VMEM((2, ns, 1, tq), F32),
                        pltpu.VMEM((2, ns, tk, tq), BF16),
                        pltpu.VMEM((2, ns, 1, tq), F32)],
        compiler_params=_cparams("arbitrary", "arbitrary", "arbitrary"),
        name="attn_prompt",
    )(qb.reshape(batch, seq, B_WIDTH), kb.reshape(batch, seq, B_WIDTH), vt, bias, lam_params, subln_g)
    return out.reshape(batch * seq, B_WIDTH)


def _attn_s_kernel(pt_ref, q_ref, kn_ref, vn_ref, maskb_ref, biasl_ref, biasn_ref, lam_ref, g_ref, *rest,
                   npages, ds):
    del pt_ref
    k_refs = rest[:npages]
    v_refs = rest[npages:2 * npages]
    o_ref = rest[2 * npages]
    acc_ref, m_ref, l_ref = rest[2 * npages + 1:]
    g = pl.program_id(1)
    last = g == pl.num_programs(1) - 1

    @pl.when(g == 0)
    def _():
        m_ref[...] = jnp.full(m_ref.shape, -jnp.inf, F32)
        l_ref[...] = jnp.zeros(l_ref.shape, F32)
        acc_ref[...] = jnp.zeros(acc_ref.shape, F32)

    q = q_ref[...] * (1.0 / math.sqrt(B_HEAD))
    qall = jnp.concatenate([_stack_halves(q[:, h * B_VDIM:(h + 1) * B_VDIM]) for h in range(B_HEADS)],
                           axis=0).astype(BF16)

    def process(k_blocks, v_blocks, biases):
        s = jnp.concatenate(
            [lax.dot_general(qall, kb.astype(BF16), (((1,), (1,)), ((), ())), preferred_element_type=F32) + bb
             for kb, bb in zip(k_blocks, biases)], axis=1)
        m_old = m_ref[...]
        m_new = jnp.maximum(m_old, jnp.max(s, axis=-1, keepdims=True))
        alpha = jnp.exp(m_old - m_new)
        pr = jnp.exp(s - m_new)
        l_ref[...] = alpha * l_ref[...] + jnp.sum(pr, axis=-1, keepdims=True)
        prb = pr.astype(BF16)
        pv = None
        off = 0
        for vb in v_blocks:
            n = vb.shape[0]
            t = jnp.dot(prb[:, off:off + n], vb.astype(BF16), preferred_element_type=F32)
            pv = t if pv is None else pv + t
            off += n
        acc_ref[...] = alpha * acc_ref[...] + pv
        m_ref[...] = m_new

    maskb = maskb_ref[...]
    newest = jnp.where(last, biasl_ref[...], maskb)
    process([k_refs[p][...] for p in range(npages)], [v_refs[p][...] for p in range(npages)],
            [maskb] * (npages - 1) + [newest])

    @pl.when(last)
    def _():
        process([kn_ref[0]], [vn_ref[0]], [biasn_ref[...]])
        lam = _lambda(lam_ref)
        acc = acc_ref[...]
        l = l_ref[...]
        for h in range(B_HEADS):
            r0 = h * 2 * ds
            o_ref[:, h * B_VDIM:(h + 1) * B_VDIM] = _diff_finish(
                acc[r0:r0 + 2 * ds], l[r0:r0 + 2 * ds], ds, lam, g_ref[...])


def _attn_s(page_table, qs, kn, vn, maskb, bias_last, bias_new, lam_params, subln_g, cache_k2, cache_v2, *, ds):
    nb, n_pages = page_table.shape
    npg = PAGES_PER_STEP
    steps = n_pages // npg
    page_rows = PAGE_SIZE * B_HEADS
    nrow = B_HEADS * 2 * ds

    def page_spec(p):
        return pl.BlockSpec((page_rows, B_VDIM), lambda b, g, pt, p=p: (pt[b, g * npg + p], 0))

    const2 = lambda b, g, pt: (0, 0)
    grid_spec = pltpu.PrefetchScalarGridSpec(
        num_scalar_prefetch=1,
        grid=(nb, steps),
        in_specs=[pl.BlockSpec((ds, B_WIDTH), lambda b, g, pt: (b, 0)),
                  pl.BlockSpec((1, PAGE_SIZE, B_VDIM), lambda b, g, pt: (b, 0, 0)),
                  pl.BlockSpec((1, PAGE_SIZE, B_VDIM), lambda b, g, pt: (b, 0, 0)),
                  pl.BlockSpec((nrow, page_rows), const2),
                  pl.BlockSpec((nrow, page_rows), const2),
                  pl.BlockSpec((nrow, PAGE_SIZE), const2),
                  pl.BlockSpec((4, B_HEAD), const2),
                  pl.BlockSpec((1, B_VDIM), const2)]
                 + [page_spec(p) for p in range(npg)] + [page_spec(p) for p in range(npg)],
        out_specs=pl.BlockSpec((ds, B_WIDTH), lambda b, g, pt: (b, 0)),
        scratch_shapes=[pltpu.VMEM((nrow, B_VDIM), F32),
                        pltpu.VMEM((nrow, 1), F32),
                        pltpu.VMEM((nrow, 1), F32)])
    return pl.pallas_call(
        functools.partial(_attn_s_kernel, npages=npg, ds=ds),
        grid_spec=grid_spec,
        out_shape=jax.ShapeDtypeStruct((nb * ds, B_WIDTH), F32),
        compiler_params=_cparams("arbitrary", "arbitrary"),
        name="attn_sample",
    )(page_table, qs, kn, vn, maskb, bias_last, bias_new, lam_params, subln_g,
      *([cache_k2] * npg), *([cache_v2] * npg))


def _outproj_kernel(ap_ref, as_ref, bp_ref, bs_ref, xp_ref, xs_ref, mp_ref, ms_ref, ng_ref, wo_ref,
                    wr_ref, br_ref, x1_ref, h2_ref, ti_ref, tr_ref, cnt_ref, base_ref, run_ref,
                    *, n_prompt_tiles):
    i = pl.program_id(0)
    tm = xp_ref.shape[0]
    is_p = i < n_prompt_tiles

    @pl.when(i == 0)
    def _():
        run_ref[...] = jnp.zeros(run_ref.shape, F32)

    a = jnp.where(is_p, ap_ref[...], as_ref[...])
    b = jnp.where(is_p, bp_ref[...], bs_ref[...].astype(BF16))
    x = jnp.where(is_p, xp_ref[...], xs_ref[...])
    g1 = jnp.where(is_p, mp_ref[0, 2:3, :], ms_ref[2])
    sh2 = jnp.where(is_p, mp_ref[0, 3:4, :], ms_ref[3])
    sc2 = jnp.where(is_p, mp_ref[0, 4:5, :], ms_ref[4])

    mix = (jnp.dot(a, wo_ref[0:A_WIDTH, :], preferred_element_type=F32)
           + jnp.dot(b, wo_ref[A_WIDTH:, :], preferred_element_type=F32))
    x1 = x + g1 * _rms(mix, ng_ref[1:2, :])
    x1_ref[...] = x1
    h2 = _rms(x1, ng_ref[2:3, :]) * (1.0 + sc2) + sh2
    h2_ref[:, 0:D_MODEL] = h2

    logits = jnp.dot(h2.astype(BF16), wr_ref[...], preferred_element_type=F32) + br_ref[...]
    work = logits.T[0:N_EXPERTS, :]
    sub = lax.broadcasted_iota(jnp.int32, work.shape, 0)
    vals, idxs = [], []
    for _ in range(TOP_K):
        mx = jnp.max(work, axis=0, keepdims=True)
        ix = jnp.min(jnp.where(work == mx, sub, N_EXPERTS), axis=0, keepdims=True)
        vals.append(mx)
        idxs.append(ix)
        work = jnp.where(sub == ix, -jnp.inf, work)
    exps = [jnp.exp(v - vals[0]) for v in vals]
    den = exps[0] + exps[1] + exps[2] + exps[3]

    sel = jnp.where(work == -jnp.inf, 1.0, 0.0)
    r_i = lax.broadcasted_iota(jnp.int32, (tm, tm), 0)
    c_i = lax.broadcasted_iota(jnp.int32, (tm, tm), 1)
    earlier = jnp.where(r_i < c_i, 1.0, 0.0).astype(BF16)
    base_ref[0] = run_ref[...]
    before = jnp.dot(sel.astype(BF16), earlier, preferred_element_type=F32) + run_ref[...]
    run_ref[...] = run_ref[...] + jnp.sum(sel, axis=1, keepdims=True)
    cnt_ref[...] = run_ref[...]

    ranks = [jnp.sum(jnp.where(sub == ix, before, 0.0), axis=0, keepdims=True) for ix in idxs]
    weights = [e / den for e in exps]
    ti_ref[...] = jnp.concatenate(idxs, axis=0)
    tr_ref[...] = jnp.concatenate(ranks, axis=0).astype(jnp.int32)
    meta = jnp.concatenate(weights + [ix.astype(F32) for ix in idxs]
                           + [jnp.zeros((LANES - 2 * TOP_K, tm), F32)], axis=0)
    h2_ref[:, D_MODEL:] = meta.T


def _outproj(a_p, a_s, b_p, b_s, x_p, x_s, mod_p, mod_s, norm_g, w_out_b, w_router, b_router):
    tm = ROW_TILE
    npt = x_p.shape[0] // tm
    nst = x_s.shape[0] // tm
    assert nst == 1
    n_tiles = npt + nst
    rows = n_tiles * tm
    tiles_per_batch = npt // mod_p.shape[0]
    last_p = npt - 1
    prow = lambda i: (jnp.minimum(i, last_p), 0)
    srow = lambda i: (0, 0)
    row = lambda i: (i, 0)
    col = lambda i: (0, i)
    const2 = lambda i: (0, 0)
    return pl.pallas_call(
        functools.partial(_outproj_kernel, n_prompt_tiles=npt),
        grid=(n_tiles,),
        in_specs=[pl.BlockSpec((tm, A_WIDTH), prow), pl.BlockSpec((tm, A_WIDTH), srow),
                  pl.BlockSpec((tm, B_WIDTH), prow), pl.BlockSpec((tm, B_WIDTH), srow),
                  pl.BlockSpec((tm, D_MODEL), prow), pl.BlockSpec((tm, D_MODEL), srow),
                  pl.BlockSpec((1, 6, D_MODEL), lambda i: (jnp.minimum(i, last_p) // tiles_per_batch, 0, 0)),
                  pl.BlockSpec((6, tm, D_MODEL), lambda i: (0, 0, 0)),
                  pl.BlockSpec((4, D_MODEL), const2),
                  pl.BlockSpec((D_MODEL, D_MODEL), const2),
                  pl.BlockSpec((D_MODEL, LANES), const2),
                  pl.BlockSpec((1, LANES), const2)],
        out_specs=[pl.BlockSpec((tm, D_MODEL), row), pl.BlockSpec((tm, ROW_WIDTH), row),
                   pl.BlockSpec((TOP_K, tm), col), pl.BlockSpec((TOP_K, tm), col),
                   pl.BlockSpec((N_EXPERTS, 1), const2),
                   pl.BlockSpec((1, N_EXPERTS, 1), lambda i: (i, 0, 0))],
        out_shape=[jax.ShapeDtypeStruct((rows, D_MODEL), F32), jax.ShapeDtypeStruct((rows, ROW_WIDTH), F32),
                   jax.ShapeDtypeStruct((TOP_K, rows), jnp.int32), jax.ShapeDtypeStruct((TOP_K, rows), jnp.int32),
                   jax.ShapeDtypeStruct((N_EXPERTS, 1), F32),
                   jax.ShapeDtypeStruct((n_tiles, N_EXPERTS, 1), F32)],
        scratch_shapes=[pltpu.VMEM((N_EXPERTS, 1), F32)],
        compiler_params=_cparams("arbitrary"),
        name="outproj_router",
    )(a_p, a_s, b_p, b_s, x_p, x_s, mod_p, mod_s, norm_g, w_out_b,
      jnp.pad(w_router, ((0, 0), (0, LANES - N_EXPERTS))).astype(BF16),
      jnp.pad(b_router.reshape(1, -1), ((0, 0), (0, LANES - N_EXPERTS))))


def _dispatch_kernel(pe_ref, cnt_ref, dest_ref, h_ref, xg_ref, zero_ref, sem, zsem, *, bm):
    tm = h_ref.shape[0]
    n_blocks = xg_ref.shape[0] // bm

    def zero_block(row0):
        return pltpu.make_async_copy(zero_ref, xg_ref.at[pl.ds(pl.multiple_of(row0, bm), bm), :], zsem)

    @pl.when(pl.program_id(0) == 0)
    def _():
        zero_ref[...] = jnp.zeros(zero_ref.shape, F32)
        first_unused = pe_ref[N_EXPERTS - 1] // bm
        for e in range(N_EXPERTS):
            @pl.when(cnt_ref[e] > 0)
            def _():
                zero_block(pe_ref[e] - bm).start()

        def start_unused(b, carry):
            zero_block(b * bm).start()
            return carry

        lax.fori_loop(first_unused, n_blocks, start_unused, 0)
        for e in range(N_EXPERTS):
            @pl.when(cnt_ref[e] > 0)
            def _():
                zero_block(pe_ref[e] - bm).wait()

        def wait_unused(b, carry):
            zero_block(b * bm).wait()
            return carry

        lax.fori_loop(first_unused, n_blocks, wait_unused, 0)

    for sub in range(tm // ROW_TILE):
        def body(t, carry, sub=sub):
            for j in range(TOP_K):
                d = dest_ref[(sub * TOP_K + j) * ROW_TILE + t]
                pltpu.make_async_copy(h_ref.at[pl.ds(sub * ROW_TILE + t, 1), :], xg_ref.at[pl.ds(d, 1), :],
                                      sem).start()
            return carry

        lax.fori_loop(0, ROW_TILE, body, 0)
    n = tm * TOP_K
    pltpu.make_async_copy(xg_ref.at[pl.ds(0, n), :], xg_ref.at[pl.ds(0, n), :], sem).wait()


def _dispatch(pad_end, cnt, dest_flat, h2, n_blocks):
    tm = ROW_TILE * DISPATCH_TILES
    bm = EXPERT_BLOCK
    rows = h2.shape[0]
    assert rows % tm == 0
    grid_spec = pltpu.PrefetchScalarGridSpec(
        num_scalar_prefetch=2,
        grid=(rows // tm,),
        in_specs=[pl.BlockSpec((tm * TOP_K,), lambda i, pe, cn: (i,), memory_space=pltpu.SMEM),
                  pl.BlockSpec((tm, ROW_WIDTH), lambda i, pe, cn: (i, 0))],
        out_specs=pl.BlockSpec(memory_space=pl.ANY),
        scratch_shapes=[pltpu.VMEM((bm, ROW_WIDTH), F32), pltpu.SemaphoreType.DMA(()), pltpu.SemaphoreType.DMA(())])
    return pl.pallas_call(
        functools.partial(_dispatch_kernel, bm=bm),
        grid_spec=grid_spec,
        out_shape=jax.ShapeDtypeStruct((n_blocks * bm, ROW_WIDTH), F32),
        compiler_params=_cparams("arbitrary"),
        name="moe_dispatch",
    )(pad_end, cnt, dest_flat, h2)


def _expert_kernel(be_ref, nu_ref, valid_ref, x_ref, wgu_ref, bgu_ref, wd_ref, bd_ref, y_ref, wgu_b, wd_b):
    i = pl.program_id(0)
    used = i < nu_ref[0]
    half = x_ref.shape[0] // 2
    wide = valid_ref[i] > half
    fresh = jnp.logical_or(i == 0, be_ref[i] != be_ref[jnp.maximum(i - 1, 0)])

    @pl.when(jnp.logical_and(used, fresh))
    def _():
        wgu_b[...] = wgu_ref[0].astype(BF16)
        wd_b[...] = wd_ref[0].astype(BF16)

    @pl.when(jnp.logical_not(used))
    def _():
        y_ref[...] = jnp.zeros(y_ref.shape, y_ref.dtype)

    def ffn(r0, nrows):
        x = x_ref[r0:r0 + nrows, 0:D_MODEL].astype(BF16)
        gu = jnp.dot(x, wgu_b[...], preferred_element_type=F32) + bgu_ref[0]
        glu = jnp.minimum(gu[:, :D_EXPERT], SWIGLU_LIMIT)
        lin = jnp.clip(gu[:, D_EXPERT:], -SWIGLU_LIMIT, SWIGLU_LIMIT)
        hid = glu * jax.nn.sigmoid(SWIGLU_ALPHA * glu) * (lin + 1.0)
        y = jnp.dot(hid.astype(BF16), wd_b[...], preferred_element_type=F32) + bd_ref[0]
        meta = x_ref[r0:r0 + nrows, D_MODEL:]
        me = be_ref[i].astype(F32)
        w = jnp.zeros((nrows, 1), F32)
        for j in range(TOP_K):
            w = w + jnp.where(meta[:, TOP_K + j:TOP_K + j + 1] == me, meta[:, j:j + 1], 0.0)
        y_ref[r0:r0 + nrows, :] = (y * w).astype(y_ref.dtype)

    @pl.when(jnp.logical_and(used, wide))
    def _():
        ffn(0, 2 * half)

    @pl.when(jnp.logical_and(used, jnp.logical_not(wide)))
    def _():
        ffn(0, half)
        y_ref[half:, :] = jnp.zeros((half, y_ref.shape[1]), y_ref.dtype)


def _experts(block_e, n_used, valid, xg, w_gu_b, b_gu, w_down_b, b_down):
    bm = EXPERT_BLOCK
    n_blocks = xg.shape[0] // bm
    grid_spec = pltpu.PrefetchScalarGridSpec(
        num_scalar_prefetch=3,
        grid=(n_blocks,),
        in_specs=[pl.BlockSpec((bm, ROW_WIDTH), lambda i, be, nu, va: (jnp.minimum(i, nu[0] - 1), 0)),
                  pl.BlockSpec((1, D_MODEL, 2 * D_EXPERT), lambda i, be, nu, va: (be[i], 0, 0)),
                  pl.BlockSpec((1, 1, 2 * D_EXPERT), lambda i, be, nu, va: (be[i], 0, 0)),
                  pl.BlockSpec((1, D_EXPERT, D_MODEL), lambda i, be, nu, va: (be[i], 0, 0)),
                  pl.BlockSpec((1, 1, D_MODEL), lambda i, be, nu, va: (be[i], 0, 0))],
        out_specs=pl.BlockSpec((bm, D_MODEL), lambda i, be, nu, va: (i, 0)),
        scratch_shapes=[pltpu.VMEM((D_MODEL, 2 * D_EXPERT), BF16), pltpu.VMEM((D_EXPERT, D_MODEL), BF16)])
    return pl.pallas_call(
        _expert_kernel,
        grid_spec=grid_spec,
        out_shape=jax.ShapeDtypeStruct((xg.shape[0], D_MODEL), BF16),
        compiler_params=_cparams("arbitrary"),
        name="moe_experts",
    )(block_e, n_used, valid, xg, w_gu_b, b_gu.reshape(N_EXPERTS, 1, -1), w_down_b,
      b_down.reshape(N_EXPERTS, 1, -1))


def _combine_kernel(row0_ref, col_ref, x1_ref, mp_ref, ms_ref, ng_ref, yg_ref,
                    yp_ref, ys_ref, rows_ref, g_ref, sem, *, n_prompt_tiles):
    i = pl.program_id(0)
    n_steps = pl.num_programs(0)
    tm = x1_ref.shape[0]
    is_p = i < n_prompt_tiles
    win = COMBINE_WINDOW
    n_win_rows = COMBINE_SLOTS * win
    slot = i % 2

    def fetch_windows(tile, buf):
        for s in range(COMBINE_SLOTS):
            row0 = pl.multiple_of(row0_ref[tile * COMBINE_SLOTS + s], COMBINE_ALIGN)
            pltpu.make_async_copy(yg_ref.at[pl.ds(row0, win), :], rows_ref.at[buf, pl.ds(s * win, win), :],
                                  sem.at[buf]).start()

    @pl.when(i == 0)
    def _():
        fetch_windows(0, 0)

    @pl.when(i + 1 < n_steps)
    def _():
        fetch_windows(i + 1, 1 - slot)

    pltpu.make_async_copy(yg_ref.at[pl.ds(0, n_win_rows), :], rows_ref.at[slot], sem.at[slot]).wait()
    col = col_ref[...]
    for c in range(n_win_rows // LANES):
        lane = lax.broadcasted_iota(jnp.int32, (tm, LANES), 1) + c * LANES
        g = jnp.zeros((tm, LANES), F32)
        for j in range(TOP_K):
            g = jnp.where(lane == col[:, j:j + 1], 1.0, g)
        g_ref[:, c * LANES:(c + 1) * LANES] = g.astype(BF16)
    f = jnp.dot(g_ref[...], rows_ref[slot], preferred_element_type=F32)

    g2 = jnp.where(is_p, mp_ref[0, 5:6, :], ms_ref[5])
    y = x1_ref[...] + g2 * _rms(f, ng_ref[3:4, :])

    @pl.when(is_p)
    def _():
        yp_ref[...] = y

    @pl.when(jnp.logical_not(is_p))
    def _():
        ys_ref[...] = y


def _combine(row0, col, x1, mod_p, mod_s, norm_g, yg, *, n_prompt_rows, n_sample_rows):
    tm = ROW_TILE
    npt = n_prompt_rows // tm
    assert n_sample_rows == tm and (COMBINE_SLOTS * COMBINE_WINDOW) % LANES == 0
    n_tiles = npt + 1
    tiles_per_batch = npt // mod_p.shape[0]
    last_p = npt - 1
    grid_spec = pltpu.PrefetchScalarGridSpec(
        num_scalar_prefetch=1,
        grid=(n_tiles,),
        in_specs=[pl.BlockSpec((tm, TOP_K), lambda i, r0: (i, 0)),
                  pl.BlockSpec((tm, D_MODEL), lambda i, r0: (i, 0)),
                  pl.BlockSpec((1, 6, D_MODEL), lambda i, r0: (jnp.minimum(i, last_p) // tiles_per_batch, 0, 0)),
                  pl.BlockSpec((6, tm, D_MODEL), lambda i, r0: (0, 0, 0)),
                  pl.BlockSpec((4, D_MODEL), lambda i, r0: (0, 0)),
                  pl.BlockSpec(memory_space=pl.ANY)],
        out_specs=[pl.BlockSpec((tm, D_MODEL), lambda i, r0: (jnp.minimum(i, last_p), 0)),
                   pl.BlockSpec((tm, D_MODEL), lambda i, r0: (0, 0))],
        scratch_shapes=[pltpu.VMEM((2, COMBINE_SLOTS * COMBINE_WINDOW, D_MODEL), BF16),
                        pltpu.VMEM((tm, COMBINE_SLOTS * COMBINE_WINDOW), BF16),
                        pltpu.SemaphoreType.DMA((2,))])
    return pl.pallas_call(
        functools.partial(_combine_kernel, n_prompt_tiles=npt),
        grid_spec=grid_spec,
        out_shape=[jax.ShapeDtypeStruct((n_prompt_rows, D_MODEL), F32),
                   jax.ShapeDtypeStruct((n_sample_rows, D_MODEL), F32)],
        compiler_params=_cparams("arbitrary"),
        name="moe_combine",
    )(row0, col, x1, mod_p, mod_s, norm_g, yg)


def _t5_bucket_np(dist):
    n = np.maximum(dist, 0)
    max_exact = NUM_BUCKETS // 2
    nf = np.maximum(n, 1).astype(np.float64)
    large = max_exact + (np.log(nf / max_exact) / math.log(MAX_DISTANCE / max_exact)
                         * (NUM_BUCKETS - max_exact)).astype(np.int32)
    large = np.minimum(large, NUM_BUCKETS - 1)
    return np.where(n < max_exact, n, large).astype(np.int32)


def _bias_from_dist(rel_bias, dist):
    onehot = jax.nn.one_hot(_t5_bucket_np(dist).reshape(-1), NUM_BUCKETS, dtype=F32)
    shifted = (rel_bias - rel_bias[NUM_BUCKETS - 1]).reshape(NUM_BUCKETS, -1)
    out = jnp.dot(onehot, shifted, precision=lax.Precision.HIGHEST).reshape(dist.shape + rel_bias.shape[1:])
    return jnp.where(jnp.asarray(dist >= 0)[..., None, None], out, -jnp.inf).astype(F32)


def _head_masked(t):
    h = t.shape[0]
    same = np.eye(h, dtype=bool)[:, None, None, None, :]
    full = jnp.where(same, t[..., None], -jnp.inf)
    return full.reshape(h * t.shape[1] * t.shape[2], t.shape[3] * h)


def kernel(x_prompt, x_sample, cache_k, cache_v, page_table, c_prompt, c_sample, w_ada, b_ada, norm_g, w_in,
           w_out, ln_v_g, ln_v_b, w_spatial, b_spatial, lam_params, subln_g, rel_bias, w_router, b_router,
           w_gu, b_gu, w_down, b_down):
    batch, seq, d = x_prompt.shape
    nb, ds = x_sample.shape[:2]
    n_pages = page_table.shape[1]
    past = n_pages * PAGE_SIZE
    rows_p = batch * seq
    rows_s = nb * ds
    assert rows_s == ROW_TILE and d == D_MODEL

    mod = _ada(jnp.concatenate([c_prompt, c_sample], axis=0), w_ada[0], b_ada[0])
    mod_p = mod[:batch].reshape(batch, 6, D_MODEL)
    mod_s = jnp.transpose(jnp.repeat(mod[batch:], ds, axis=0).reshape(rows_s, 6, D_MODEL), (1, 0, 2))

    ng = norm_g[0]
    w_in_b = w_in[0].astype(BF16)
    w_out_b = w_out[0].astype(BF16)
    lng = ln_v_g[0].reshape(1, A_WIDTH)
    lnb = ln_v_b[0].reshape(1, A_WIDTH)
    grp = np.arange(A_WIDTH) // A_HEAD
    avg = jnp.asarray((grp[:, None] == grp[None, :]).astype(np.float32) / A_HEAD, BF16)
    ws_p = jnp.tril(w_spatial[0]).astype(BF16)
    bs_p = jnp.repeat(b_spatial[0].T, A_HEAD, axis=1)
    w_small = jnp.tril(w_spatial[0][:, :ds, :ds])
    same_batch = np.kron(np.eye(nb, dtype=np.float32), np.ones((ds, ds), np.float32))
    ws_s = (jnp.tile(w_small, (1, nb, nb)) * same_batch).astype(BF16)
    bs_s = jnp.tile(jnp.repeat(b_spatial[0][:, :ds].T, A_HEAD, axis=1), (nb, 1))

    xp2 = x_prompt.reshape(rows_p, D_MODEL)
    xs2 = x_sample.reshape(rows_s, D_MODEL)
    assert ATTN_TILE % ROW_TILE == 0 and seq % ATTN_TILE == 0
    a_p, va_p, q_p, k_p, v_p, kb_p, vt_p = _inproj(
        xp2, mod_p[:, 0:1], mod_p[:, 1:2], ng[0:1], w_in_b, lng, lnb, avg, ws_p, bs_p,
        tiles_per_mod=seq // ROW_TILE, chunk=CHUNK, q_dtype=BF16, q_scale=LOG2E / math.sqrt(B_HEAD), va_rows=CHUNK)
    a_s, va_s, q_s, k_s, v_s, _, _ = _inproj(
        xs2, mod_s[0:1], mod_s[1:2], ng[0:1], w_in_b, lng, lnb, avg, ws_s, bs_s,
        tiles_per_mod=1, chunk=rows_s, q_dtype=F32, q_scale=1.0, va_rows=rows_s)

    ti = np.arange(ATTN_TILE)
    near = ti[:, None] - ti[None, :]
    far = np.full_like(near, 4 * MAX_DISTANCE)
    dist_p = np.stack([near, near + ATTN_TILE, far])
    bias_p = jnp.transpose(_bias_from_dist(rel_bias, dist_p), (3, 0, 2, 4, 1))
    bias_p = bias_p.reshape(B_HEADS, 3, ATTN_TILE, 2 * ATTN_TILE) * LOG2E
    b_p = _attn_p(q_p, kb_p, vt_p, bias_p, lam_params[0], subln_g, batch=batch, seq=seq)

    qi = np.arange(ds)
    ki = np.arange(PAGE_SIZE)
    dist_l = PAGE_SIZE + qi[:, None] - ki[None, :]
    kn_i = np.arange(PAGE_SIZE // B_HEADS)
    dist_n = np.where(kn_i[None, :] < ds, qi[:, None] - kn_i[None, :], -1)
    to_rows = lambda t: jnp.transpose(t, (2, 3, 0, 1))
    bias_l = _head_masked(to_rows(_bias_from_dist(rel_bias, dist_l)))
    bias_n = _head_masked(to_rows(_bias_from_dist(rel_bias, dist_n)))
    maskb = _head_masked(jnp.zeros((B_HEADS, 2, ds, PAGE_SIZE), F32))
    pad_keys = lambda t: jnp.pad(t.reshape(nb, ds * B_HEADS, B_VDIM), ((0, 0), (0, PAGE_SIZE - ds * B_HEADS), (0, 0)))
    n_phys = cache_k.shape[0]
    b_s = _attn_s(page_table, q_s, pad_keys(k_s), pad_keys(v_s), maskb, bias_l, bias_n, lam_params[0], subln_g,
                  cache_k.reshape(n_phys * PAGE_SIZE * B_HEADS, B_VDIM),
                  cache_v.reshape(n_phys * PAGE_SIZE * B_HEADS, B_VDIM), ds=ds)

    x1, h2, top_i, top_r, counts, tile_base = _outproj(
        a_p, a_s, b_p, b_s, xp2, xs2, mod_p, mod_s, ng, w_out_b, w_router[0], b_router[0])

    bm = EXPERT_BLOCK
    rows = rows_p + rows_s
    n_blocks = rows * TOP_K // bm + N_EXPERTS + 1
    cnt = counts[:, 0].astype(jnp.int32)
    padded = (cnt + bm - 1) // bm * bm
    pad_end = jnp.cumsum(padded)
    pad_start = pad_end - padded
    experts = jnp.arange(N_EXPERTS, dtype=jnp.int32)
    chosen = top_i[..., None] == experts
    dest = jnp.sum(jnp.where(chosen, pad_start, 0), axis=-1) + top_r
    n_tiles = rows // ROW_TILE
    win = COMBINE_WINDOW
    assert win % COMBINE_ALIGN == 0
    assert (ROW_TILE * TOP_K + N_EXPERTS * (COMBINE_ALIGN - 1)) // win + N_EXPERTS <= COMBINE_SLOTS
    base = tile_base[:, :, 0].astype(jnp.int32)
    first = pad_start[None, :] + base
    win0 = first // COMBINE_ALIGN * COMBINE_ALIGN
    sent = jnp.concatenate([base[1:], cnt[None, :]], axis=0) - base
    n_win = jnp.where(sent > 0, (first - win0 + sent + win - 1) // win, 0)
    slot_end = jnp.cumsum(n_win, axis=1)
    slot_start = slot_end - n_win
    slots = jnp.arange(COMBINE_SLOTS, dtype=jnp.int32)
    slot_e = jnp.minimum(jnp.sum((slot_end[:, None, :] <= slots[None, :, None]).astype(jnp.int32), axis=2),
                         N_EXPERTS - 1)
    pick = slot_e[..., None] == experts
    slot_row0 = (jnp.sum(jnp.where(pick, win0[:, None, :], 0), axis=2)
                 + win * (slots[None, :] - jnp.sum(jnp.where(pick, slot_start[:, None, :], 0), axis=2)))
    slot_row0 = jnp.where(slots[None, :] < slot_end[:, -1:], slot_row0, 0)
    local = dest - jnp.sum(jnp.where(chosen, jnp.repeat(win0, ROW_TILE, axis=0), 0), axis=-1)
    col = jnp.sum(jnp.where(chosen, jnp.repeat(slot_start, ROW_TILE, axis=0), 0), axis=-1) * win + local
    dest = jnp.transpose(dest.reshape(TOP_K, n_tiles, ROW_TILE), (1, 0, 2)).reshape(-1)
    blk_start = jnp.arange(n_blocks, dtype=jnp.int32) * bm
    block_e = jnp.minimum(jnp.sum((pad_end[None, :] <= blk_start[:, None]).astype(jnp.int32), axis=1),
                          N_EXPERTS - 1)
    n_used = (pad_end[-1:] // bm).astype(jnp.int32)

    xg = _dispatch(pad_end.astype(jnp.int32), cnt, dest, h2, n_blocks)
    sel_blk = block_e[:, None] == experts
    blk_valid = jnp.clip(jnp.sum(jnp.where(sel_blk, (pad_start + cnt)[None, :], 0), axis=1) - blk_start, 0, bm)
    yg = _experts(block_e, n_used, blk_valid.astype(jnp.int32), xg, w_gu[0], b_gu[0], w_down[0], b_down[0])
    y_p, y_s = _combine(slot_row0.reshape(-1), col.T, x1, mod_p, mod_s, ng, yg,
                        n_prompt_rows=rows_p, n_sample_rows=rows_s)

    return (y_p.reshape(batch, seq, D_MODEL),
            y_s.reshape(nb, ds, D_MODEL),
            k_p.reshape(batch, seq, 1, B_HEADS, B_VDIM),
            v_p.reshape(batch, seq, 1, B_HEADS, B_VDIM),
            k_s.reshape(nb, ds, 1, B_HEADS, B_VDIM),
            v_s.reshape(nb, ds, 1, B_HEADS, B_VDIM),
            va_p.reshape(batch, CHUNK, 1, A_WIDTH),
            va_s.reshape(nb, ds, 1, A_WIDTH))
```

```python
import functools
import math

import numpy as np
import jax
import jax.numpy as jnp
from jax import lax
from jax.experimental import pallas as pl
from jax.experimental.pallas import tpu as pltpu

F32 = jnp.float32
BF16 = jnp.bfloat16

D_MODEL = 1024
A_WIDTH = 512
A_HEAD = 64
A_GROUPS = 8
CHUNK = 128
B_WIDTH = 512
B_HEAD = 64
B_VDIM = 128
B_HEADS = 4
IN_WIDTH = 2 * A_WIDTH + 3 * B_WIDTH
NUM_BUCKETS = 32
MAX_DISTANCE = 128
PAGE_SIZE = 128
N_EXPERTS = 32
TOP_K = 4
D_EXPERT = 1024
SWIGLU_LIMIT = 7.0
SWIGLU_ALPHA = 1.702
NORM_EPS = 1e-6
LAM_INIT = 0.8 - 0.6 * math.exp(-0.3 * 0)

ROW_TILE = 256
ATTN_TILE = 512
ATTN_BATCHES = 1
PAGES_PER_STEP = 32
LOG2E = math.log2(math.e)
LANES = 128
MXU_WIDTH = 256
PACKED = D_MODEL // 2
ROW_WIDTH = PACKED + LANES
EXPERT_BLOCK = 512
DISPATCH_TILES = 5
COMBINE_ALIGN = 16
COMBINE_WINDOW = 16
COMBINE_SLOTS = 128
VMEM_LIMIT = 56 * 1024 * 1024


def _cparams(*sem):
    return pltpu.CompilerParams(dimension_semantics=sem, vmem_limit_bytes=VMEM_LIMIT)


def _rms(x, g):
    return x * lax.rsqrt(jnp.mean(x * x, axis=-1, keepdims=True) + NORM_EPS) * g


def _ada_kernel(c_ref, w_ref, b_ref, o_ref):
    s = jax.nn.silu(c_ref[...]).astype(BF16)
    o_ref[...] = jnp.dot(s, w_ref[...].astype(BF16), preferred_element_type=F32) + b_ref[...]


def _ada(c_all, w_ada, b_ada):
    n = c_all.shape[0]
    tn = 1024
    return pl.pallas_call(
        _ada_kernel,
        grid=(6 * D_MODEL // tn,),
        in_specs=[pl.BlockSpec((n, D_MODEL), lambda j: (0, 0)),
                  pl.BlockSpec((D_MODEL, tn), lambda j: (0, j)),
                  pl.BlockSpec((1, tn), lambda j: (0, j))],
        out_specs=pl.BlockSpec((n, tn), lambda j: (0, j)),
        out_shape=jax.ShapeDtypeStruct((n, 6 * D_MODEL), F32),
        compiler_params=_cparams("arbitrary"),
        name="ada",
    )(c_all, w_ada, b_ada.reshape(1, -1))


def _inproj_kernel(x_ref, sh_ref, sc_ref, ng_ref, w_ref, lng_ref, lnb_ref, avg_ref, ws_ref, bs_ref,
                   a_ref, va_ref, q_ref, k_ref, v_ref, kb_ref, vt_ref, *, chunk, q_scale):
    tm = x_ref.shape[0]
    va_rows = va_ref.shape[0]
    h = (_rms(x_ref[...], ng_ref[...]) * (1.0 + sc_ref[0]) + sh_ref[0]).astype(BF16)

    u = jax.nn.gelu(jnp.dot(h, w_ref[:, 0:A_WIDTH], preferred_element_type=F32))
    gv = jax.nn.gelu(jnp.dot(h, w_ref[:, A_WIDTH:2 * A_WIDTH], preferred_element_type=F32))

    avg = avg_ref[...]

    def group_mean(t):
        return jnp.dot(t.astype(BF16), avg, preferred_element_type=F32)

    xc = gv - group_mean(gv)
    va = xc * lax.rsqrt(group_mean(xc * xc) + NORM_EPS) * lng_ref[...] + lnb_ref[...]
    va_ref[...] = va[tm - va_rows:, :]

    vab = va.astype(BF16)
    lane = lax.broadcasted_iota(jnp.int32, (chunk, 128), 1)
    for c in range(tm // chunk):
        r0 = c * chunk
        for pair in range(A_GROUPS // 2):
            c0 = pair * 128
            vp = vab[r0:r0 + chunk, c0:c0 + 128]
            lo_half = jnp.where(lane < A_HEAD, vp, jnp.zeros_like(vp))
            hi_half = jnp.where(lane >= A_HEAD, vp, jnp.zeros_like(vp))
            s = (jnp.dot(ws_ref[2 * pair], lo_half, preferred_element_type=F32)
                 + jnp.dot(ws_ref[2 * pair + 1], hi_half, preferred_element_type=F32))
            a = u[r0:r0 + chunk, c0:c0 + 128] * (s + bs_ref[:, c0:c0 + 128])
            a_ref[r0:r0 + chunk, c0:c0 + 128] = a.astype(a_ref.dtype)

    q = jnp.dot(h, w_ref[:, 2 * A_WIDTH:2 * A_WIDTH + B_WIDTH], preferred_element_type=F32)
    q_ref[...] = (q * q_scale).astype(q_ref.dtype)
    k = jnp.dot(h, w_ref[:, 2 * A_WIDTH + B_WIDTH:2 * A_WIDTH + 2 * B_WIDTH], preferred_element_type=F32)
    kb_ref[...] = k.astype(BF16)
    v = jnp.dot(h, w_ref[:, 2 * A_WIDTH + 2 * B_WIDTH:IN_WIDTH], preferred_element_type=F32)
    for hh in range(B_HEADS):
        c0 = hh * B_VDIM
        k_ref[pl.ds(hh, tm, stride=B_HEADS), :] = k[:, c0:c0 + B_VDIM]
        v_ref[pl.ds(hh, tm, stride=B_HEADS), :] = v[:, c0:c0 + B_VDIM]
        vt_ref[0, hh, 0] = v[:, c0:c0 + B_VDIM].T.astype(BF16)


def _inproj(x2, sh, sc, ng, w_in_b, lng, lnb, avg, ws, bs, *, tiles_per_mod, chunk, q_dtype, q_scale, va_rows):
    rows = x2.shape[0]
    tm = ROW_TILE
    n_tiles = rows // tm
    mod_rows = sh.shape[1]
    row = lambda i: (i, 0)
    const2 = lambda i: (0, 0)
    mod_map = lambda i: (i // tiles_per_mod, 0, 0)
    out_w = lambda w, dt: jax.ShapeDtypeStruct((rows, w), dt)
    head_rows = jax.ShapeDtypeStruct((rows * B_HEADS, B_VDIM), F32)
    head_rows_spec = pl.BlockSpec((tm * B_HEADS, B_VDIM), row)
    vt_shape = jax.ShapeDtypeStruct((n_tiles // tiles_per_mod, B_HEADS, tiles_per_mod, B_VDIM, tm), BF16)
    vt_spec = pl.BlockSpec((1, B_HEADS, 1, B_VDIM, tm), lambda i: (i // tiles_per_mod, 0, i % tiles_per_mod, 0, 0))
    return pl.pallas_call(
        functools.partial(_inproj_kernel, chunk=chunk, q_scale=q_scale),
        grid=(rows // tm,),
        in_specs=[pl.BlockSpec((tm, D_MODEL), row),
                  pl.BlockSpec((1, mod_rows, D_MODEL), mod_map),
                  pl.BlockSpec((1, mod_rows, D_MODEL), mod_map),
                  pl.BlockSpec((1, D_MODEL), const2),
                  pl.BlockSpec((D_MODEL, IN_WIDTH), const2),
                  pl.BlockSpec((1, A_WIDTH), const2),
                  pl.BlockSpec((1, A_WIDTH), const2),
                  pl.BlockSpec((A_WIDTH, A_WIDTH), const2),
                  pl.BlockSpec((A_GROUPS, chunk, chunk), lambda i: (0, 0, 0)),
                  pl.BlockSpec((chunk, A_WIDTH), const2)],
        out_specs=[pl.BlockSpec((tm, A_WIDTH), row),
                   pl.BlockSpec((va_rows, A_WIDTH), lambda i: (i // tiles_per_mod, 0)),
                   pl.BlockSpec((tm, B_WIDTH), row), head_rows_spec, head_rows_spec,
                   pl.BlockSpec((tm, B_WIDTH), row), vt_spec],
        out_shape=[out_w(A_WIDTH, BF16),
                   jax.ShapeDtypeStruct((n_tiles // tiles_per_mod * va_rows, A_WIDTH), F32),
                   out_w(B_WIDTH, q_dtype), head_rows, head_rows, out_w(B_WIDTH, BF16), vt_shape],
        compiler_params=_cparams("arbitrary"),
        name="inproj",
    )(x2, sh, sc, ng, w_in_b, lng, lnb, avg, ws, bs)


def _stack_halves(q):
    lane = lax.broadcasted_iota(jnp.int32, q.shape, 1)
    zero = jnp.zeros_like(q)
    return jnp.concatenate([jnp.where(lane < B_HEAD, q, zero), jnp.where(lane >= B_HEAD, q, zero)], axis=0)


def _lambda(lam_ref):
    lp = lam_ref[...]
    return (jnp.exp(jnp.sum(lp[0:1] * lp[1:2], axis=-1, keepdims=True))
            - jnp.exp(jnp.sum(lp[2:3] * lp[3:4], axis=-1, keepdims=True)) + LAM_INIT)


def _diff_finish(acc, l, n, lam, g):
    o = acc[:n] / l[:n] - lam * (acc[n:] / l[n:])
    return _rms(o, g) * (1.0 - LAM_INIT)


def _attn_p_kernel(q_ref, k_ref, vt_ref, bias_ref, lam_ref, g_ref, o_ref,
                   acc_ref, m_ref, l_ref, s_ref, tmax_ref, p_ref, alpha_ref, *, tq, tk):
    qi = pl.program_id(2)
    nbb = q_ref.shape[0]
    streams = [(bb, c) for bb in range(nbb) for c in range(2)]
    q_half = []
    for bb in range(nbb):
        q = q_ref[bb]
        lane = lax.broadcasted_iota(jnp.int32, q.shape, 1)
        zero = jnp.zeros_like(q)
        q_half += [jnp.where(lane < B_HEAD, q, zero), jnp.where(lane >= B_HEAD, q, zero)]
    m_ref[...] = jnp.full(m_ref.shape, -jnp.inf, F32)
    l_ref[...] = jnp.zeros(l_ref.shape, F32)
    acc_ref[...] = jnp.zeros(acc_ref.shape, F32)
    p_ref[1] = jnp.zeros(p_ref.shape[1:], BF16)
    alpha_ref[1] = jnp.ones(alpha_ref.shape[1:], F32)

    def scores(j):
        out = []
        for bb in range(nbb):
            k = k_ref[bb, pl.ds(pl.multiple_of(j * tk, tk), tk), :]
            out += [lax.dot_general(k, q_half[2 * bb + c], (((1,), (1,)), ((), ())), preferred_element_type=F32)
                    for c in range(2)]
        return out

    def keep_scores(j, kind, ss):
        slot = j % 2
        for u, (bb, c) in enumerate(streams):
            s = ss[u]
            if kind is not None:
                s = s + bias_ref[0, kind, :, c * tq:(c + 1) * tq]
            s_ref[slot, u] = s
            tmax_ref[slot, u] = jnp.max(s, axis=0, keepdims=True)

    def softmax(j):
        slot = j % 2
        for u in range(len(streams)):
            m_old = m_ref[u]
            m_new = jnp.maximum(m_old, tmax_ref[slot, u])
            alpha = jnp.exp2(m_old - m_new)
            p = jnp.exp2(s_ref[slot, u] - m_new)
            l_ref[u] = alpha * l_ref[u] + jnp.sum(p, axis=0, keepdims=True)
            m_ref[u] = m_new
            p_ref[slot, u] = p.astype(BF16)
            alpha_ref[slot, u] = alpha

    def pv(j):
        slot = j % 2
        pieces = tk // vt_ref.shape[-1]
        first = jnp.maximum(j, 0) * pieces
        for u, (bb, c) in enumerate(streams):
            add = None
            for r in range(pieces):
                w = vt_ref.shape[-1]
                t = jnp.dot(vt_ref[bb, 0, first + r], p_ref[slot, u, r * w:(r + 1) * w, :],
                            preferred_element_type=F32)
                add = t if add is None else add + t
            acc_ref[u] = alpha_ref[slot, u] * acc_ref[u] + add

    def far_body(j, carry):
        ss = scores(j + 1)
        pv(j - 1)
        softmax(j)
        keep_scores(j + 1, None, ss)
        return carry

    def near_body(j, carry):
        nxt = jnp.minimum(j + 1, qi)
        ss = scores(nxt)
        pv(j - 1)
        softmax(j)
        keep_scores(nxt, qi - nxt, ss)
        return carry

    keep_scores(0, jnp.minimum(qi, 2), scores(0))
    n_far_fetch = jnp.maximum(qi - 2, 0)
    lax.fori_loop(0, n_far_fetch, far_body, 0)
    lax.fori_loop(n_far_fetch, qi + 1, near_body, 0)
    pv(qi)

    lam = _lambda(lam_ref)
    for bb in range(nbb):
        o = acc_ref[2 * bb] / l_ref[2 * bb] - lam * (acc_ref[2 * bb + 1] / l_ref[2 * bb + 1])
        y = o * lax.rsqrt(jnp.mean(o * o, axis=0, keepdims=True) + NORM_EPS) * (1.0 - LAM_INIT)
        o_ref[bb] = (y.T * g_ref[...]).astype(o_ref.dtype)


def _attn_p(qb, kb, vt, bias, lam_params, subln_g, *, batch, seq):
    tq = tk = ATTN_TILE
    nq = seq // tq
    nbb = ATTN_BATCHES
    ns = 2 * nbb
    assert batch % nbb == 0
    out = pl.pallas_call(
        functools.partial(_attn_p_kernel, tq=tq, tk=tk),
        grid=(batch // nbb, B_HEADS, nq),
        in_specs=[pl.BlockSpec((nbb, tq, B_VDIM), lambda b, h, i: (b, i, h)),
                  pl.BlockSpec((nbb, seq, B_VDIM), lambda b, h, i: (b, 0, h)),
                  pl.BlockSpec((nbb, 1, seq // ROW_TILE, B_VDIM, ROW_TILE), lambda b, h, i: (b, h, 0, 0, 0)),
                  pl.BlockSpec((1, 3, tk, 2 * tq), lambda b, h, i: (h, 0, 0, 0)),
                  pl.BlockSpec((4, B_HEAD), lambda b, h, i: (0, 0)),
                  pl.BlockSpec((1, B_VDIM), lambda b, h, i: (0, 0))],
        out_specs=pl.BlockSpec((nbb, tq, B_VDIM), lambda b, h, i: (b, i, h)),
        out_shape=jax.ShapeDtypeStruct((batch, seq, B_WIDTH), BF16),
        scratch_shapes=[pltpu.VMEM((ns, B_VDIM, tq), F32),
                        pltpu.VMEM((ns, 1, tq), F32),
                        pltpu.VMEM((ns, 1, tq), F32),
                        pltpu.VMEM((2, ns, tk, tq), F32),
                        pltpu.VMEM((2, ns, 1, tq), F32),
                        pltpu.VMEM((2, ns, tk, tq), BF16),
                        pltpu.VMEM((2, ns, 1, tq), F32)],
        compiler_params=_cparams("arbitrary", "arbitrary", "arbitrary"),
        name="attn_prompt",
    )(qb.reshape(batch, seq, B_WIDTH), kb.reshape(batch, seq, B_WIDTH), vt, bias, lam_params, subln_g)
    return out.reshape(batch * seq, B_WIDTH)


def _attn_s_kernel(pt_ref, q_ref, kn_ref, vn_ref, maskb_ref, biasl_ref, biasn_ref, lam_ref, g_ref, *rest,
                   npages, ds):
    del pt_ref
    k_refs = rest[:npages]
    v_refs = rest[npages:2 * npages]
    o_ref = rest[2 * npages]
    acc_ref, m_ref, l_ref = rest[2 * npages + 1:]
    g = pl.program_id(1)
    last = g == pl.num_programs(1) - 1

    @pl.when(g == 0)
    def _():
        m_ref[...] = jnp.full(m_ref.shape, -jnp.inf, F32)
        l_ref[...] = jnp.zeros(l_ref.shape, F32)
        acc_ref[...] = jnp.zeros(acc_ref.shape, F32)

    q = q_ref[...] * (1.0 / math.sqrt(B_HEAD))
    qall = jnp.concatenate([_stack_halves(q[:, h * B_VDIM:(h + 1) * B_VDIM]) for h in range(B_HEADS)],
                           axis=0).astype(BF16)

    def process(k_blocks, v_blocks, biases):
        s = jnp.concatenate(
            [lax.dot_general(qall, kb.astype(BF16), (((1,), (1,)), ((), ())), preferred_element_type=F32) + bb
             for kb, bb in zip(k_blocks, biases)], axis=1)
        m_old = m_ref[...]
        m_new = jnp.maximum(m_old, jnp.max(s, axis=-1, keepdims=True))
        alpha = jnp.exp(m_old - m_new)
        pr = jnp.exp(s - m_new)
        l_ref[...] = alpha * l_ref[...] + jnp.sum(pr, axis=-1, keepdims=True)
        prb = pr.astype(BF16)
        pv = None
        off = 0
        for vb in v_blocks:
            n = vb.shape[0]
            t = jnp.dot(prb[:, off:off + n], vb.astype(BF16), preferred_element_type=F32)
            pv = t if pv is None else pv + t
            off += n
        acc_ref[...] = alpha * acc_ref[...] + pv
        m_ref[...] = m_new

    maskb = maskb_ref[...]
    newest = jnp.where(last, biasl_ref[...], maskb)
    process([k_refs[p][...] for p in range(npages)], [v_refs[p][...] for p in range(npages)],
            [maskb] * (npages - 1) + [newest])

    @pl.when(last)
    def _():
        process([kn_ref[0]], [vn_ref[0]], [biasn_ref[...]])
        lam = _lambda(lam_ref)
        acc = acc_ref[...]
        l = l_ref[...]
        for h in range(B_HEADS):
            r0 = h * 2 * ds
            o_ref[:, h * B_VDIM:(h + 1) * B_VDIM] = _diff_finish(
                acc[r0:r0 + 2 * ds], l[r0:r0 + 2 * ds], ds, lam, g_ref[...])


def _attn_s(page_table, qs, kn, vn, maskb, bias_last, bias_new, lam_params, subln_g, cache_k2, cache_v2, *, ds):
    nb, n_pages = page_table.shape
    npg = PAGES_PER_STEP
    steps = n_pages // npg
    page_rows = PAGE_SIZE * B_HEADS
    nrow = B_HEADS * 2 * ds

    def page_spec(p):
        return pl.BlockSpec((page_rows, B_VDIM), lambda b, g, pt, p=p: (pt[b, g * npg + p], 0))

    const2 = lambda b, g, pt: (0, 0)
    grid_spec = pltpu.PrefetchScalarGridSpec(
        num_scalar_prefetch=1,
        grid=(nb, steps),
        in_specs=[pl.BlockSpec((ds, B_WIDTH), lambda b, g, pt: (b, 0)),
                  pl.BlockSpec((1, PAGE_SIZE, B_VDIM), lambda b, g, pt: (b, 0, 0)),
                  pl.BlockSpec((1, PAGE_SIZE, B_VDIM), lambda b, g, pt: (b, 0, 0)),
                  pl.BlockSpec((nrow, page_rows), const2),
                  pl.BlockSpec((nrow, page_rows), const2),
                  pl.BlockSpec((nrow, PAGE_SIZE), const2),
                  pl.BlockSpec((4, B_HEAD), const2),
                  pl.BlockSpec((1, B_VDIM), const2)]
                 + [page_spec(p) for p in range(npg)] + [page_spec(p) for p in range(npg)],
        out_specs=pl.BlockSpec((ds, B_WIDTH), lambda b, g, pt: (b, 0)),
        scratch_shapes=[pltpu.VMEM((nrow, B_VDIM), F32),
                        pltpu.VMEM((nrow, 1), F32),
                        pltpu.VMEM((nrow, 1), F32)])
    return pl.pallas_call(
        functools.partial(_attn_s_kernel, npages=npg, ds=ds),
        grid_spec=grid_spec,
        out_shape=jax.ShapeDtypeStruct((nb * ds, B_WIDTH), F32),
        compiler_params=_cparams("arbitrary", "arbitrary"),
        name="attn_sample",
    )(page_table, qs, kn, vn, maskb, bias_last, bias_new, lam_params, subln_g,
      *([cache_k2] * npg), *([cache_v2] * npg))


def _outproj_kernel(ap_ref, as_ref, bp_ref, bs_ref, xp_ref, xs_ref, mp_ref, ms_ref, ng_ref, wo_ref,
                    wr_ref, br_ref, x1_ref, h2_ref, ti_ref, tr_ref, cnt_ref, base_ref, run_ref,
                    *, n_prompt_tiles):
    i = pl.program_id(0)
    tm = xp_ref.shape[0]
    is_p = i < n_prompt_tiles

    @pl.when(i == 0)
    def _():
        run_ref[...] = jnp.zeros(run_ref.shape, F32)

    a = jnp.where(is_p, ap_ref[...], as_ref[...])
    b = jnp.where(is_p, bp_ref[...], bs_ref[...].astype(BF16))
    x = jnp.where(is_p, xp_ref[...], xs_ref[...])
    g1 = jnp.where(is_p, mp_ref[0, 2:3, :], ms_ref[2])
    sh2 = jnp.where(is_p, mp_ref[0, 3:4, :], ms_ref[3])
    sc2 = jnp.where(is_p, mp_ref[0, 4:5, :], ms_ref[4])

    mix = (jnp.dot(a, wo_ref[0:A_WIDTH, :], preferred_element_type=F32)
           + jnp.dot(b, wo_ref[A_WIDTH:, :], preferred_element_type=F32))
    x1 = x + g1 * _rms(mix, ng_ref[1:2, :])
    x1_ref[...] = x1
    h2 = _rms(x1, ng_ref[2:3, :]) * (1.0 + sc2) + sh2
    low = lax.bitcast_convert_type(h2[:, 0:PACKED].astype(BF16).astype(F32), jnp.uint32)
    high = lax.bitcast_convert_type(h2[:, PACKED:].astype(BF16).astype(F32), jnp.uint32)
    words = jnp.bitwise_or(jnp.bitwise_and(high, jnp.uint32(0xFFFF0000)), lax.shift_right_logical(low, jnp.uint32(16)))
    h2_ref[:, 0:PACKED] = lax.bitcast_convert_type(words, F32)

    logits = jnp.dot(h2.astype(BF16), wr_ref[...], preferred_element_type=F32) + br_ref[...]
    work = logits.T[0:N_EXPERTS, :]
    sub = lax.broadcasted_iota(jnp.int32, work.shape, 0)
    vals, idxs = [], []
    for _ in range(TOP_K):
        mx = jnp.max(work, axis=0, keepdims=True)
        ix = jnp.min(jnp.where(work == mx, sub, N_EXPERTS), axis=0, keepdims=True)
        vals.append(mx)
        idxs.append(ix)
        work = jnp.where(sub == ix, -jnp.inf, work)
    exps = [jnp.exp(v - vals[0]) for v in vals]
    den = exps[0] + exps[1] + exps[2] + exps[3]

    sel = jnp.where(work == -jnp.inf, 1.0, 0.0)
    r_i = lax.broadcasted_iota(jnp.int32, (tm, tm), 0)
    c_i = lax.broadcasted_iota(jnp.int32, (tm, tm), 1)
    earlier = jnp.where(r_i < c_i, 1.0, 0.0).astype(BF16)
    base_ref[0] = run_ref[...]
    before = jnp.dot(sel.astype(BF16), earlier, preferred_element_type=F32) + run_ref[...]
    run_ref[...] = run_ref[...] + jnp.sum(sel, axis=1, keepdims=True)
    cnt_ref[...] = run_ref[...]

    ranks = [jnp.sum(jnp.where(sub == ix, before, 0.0), axis=0, keepdims=True) for ix in idxs]
    weights = [e / den for e in exps]
    ti_ref[...] = jnp.concatenate(idxs, axis=0)
    tr_ref[...] = jnp.concatenate(ranks, axis=0).astype(jnp.int32)
    meta = jnp.concatenate(weights + [ix.astype(F32) for ix in idxs]
                           + [jnp.zeros((LANES - 2 * TOP_K, tm), F32)], axis=0)
    h2_ref[:, PACKED:] = meta.T


def _outproj(a_p, a_s, b_p, b_s, x_p, x_s, mod_p, mod_s, norm_g, w_out_b, w_router, b_router):
    tm = ROW_TILE
    npt = x_p.shape[0] // tm
    nst = x_s.shape[0] // tm
    assert nst == 1
    n_tiles = npt + nst
    rows = n_tiles * tm
    tiles_per_batch = npt // mod_p.shape[0]
    last_p = npt - 1
    prow = lambda i: (jnp.minimum(i, last_p), 0)
    srow = lambda i: (0, 0)
    row = lambda i: (i, 0)
    col = lambda i: (0, i)
    const2 = lambda i: (0, 0)
    return pl.pallas_call(
        functools.partial(_outproj_kernel, n_prompt_tiles=npt),
        grid=(n_tiles,),
        in_specs=[pl.BlockSpec((tm, A_WIDTH), prow), pl.BlockSpec((tm, A_WIDTH), srow),
                  pl.BlockSpec((tm, B_WIDTH), prow), pl.BlockSpec((tm, B_WIDTH), srow),
                  pl.BlockSpec((tm, D_MODEL), prow), pl.BlockSpec((tm, D_MODEL), srow),
                  pl.BlockSpec((1, 6, D_MODEL), lambda i: (jnp.minimum(i, last_p) // tiles_per_batch, 0, 0)),
                  pl.BlockSpec((6, tm, D_MODEL), lambda i: (0, 0, 0)),
                  pl.BlockSpec((4, D_MODEL), const2),
                  pl.BlockSpec((D_MODEL, D_MODEL), const2),
                  pl.BlockSpec((D_MODEL, LANES), const2),
                  pl.BlockSpec((1, LANES), const2)],
        out_specs=[pl.BlockSpec((tm, D_MODEL), row), pl.BlockSpec((tm, ROW_WIDTH), row),
                   pl.BlockSpec((TOP_K, tm), col), pl.BlockSpec((TOP_K, tm), col),
                   pl.BlockSpec((N_EXPERTS, 1), const2),
                   pl.BlockSpec((1, N_EXPERTS, 1), lambda i: (i, 0, 0))],
        out_shape=[jax.ShapeDtypeStruct((rows, D_MODEL), F32), jax.ShapeDtypeStruct((rows, ROW_WIDTH), F32),
                   jax.ShapeDtypeStruct((TOP_K, rows), jnp.int32), jax.ShapeDtypeStruct((TOP_K, rows), jnp.int32),
                   jax.ShapeDtypeStruct((N_EXPERTS, 1), F32),
                   jax.ShapeDtypeStruct((n_tiles, N_EXPERTS, 1), F32)],
        scratch_shapes=[pltpu.VMEM((N_EXPERTS, 1), F32)],
        compiler_params=_cparams("arbitrary"),
        name="outproj_router",
    )(a_p, a_s, b_p, b_s, x_p, x_s, mod_p, mod_s, norm_g, w_out_b,
      jnp.pad(w_router, ((0, 0), (0, LANES - N_EXPERTS))).astype(BF16),
      jnp.pad(b_router.reshape(1, -1), ((0, 0), (0, LANES - N_EXPERTS))))


def _dispatch_kernel(pe_ref, cnt_ref, dest_ref, h_ref, xg_ref, zero_ref, sem, zsem, *, bm):
    tm = h_ref.shape[0]
    n_blocks = xg_ref.shape[0] // bm

    def zero_block(row0):
        return pltpu.make_async_copy(zero_ref, xg_ref.at[pl.ds(pl.multiple_of(row0, bm), bm), :], zsem)

    @pl.when(pl.program_id(0) == 0)
    def _():
        zero_ref[...] = jnp.zeros(zero_ref.shape, F32)
        first_unused = pe_ref[N_EXPERTS - 1] // bm
        for e in range(N_EXPERTS):
            @pl.when(cnt_ref[e] > 0)
            def _():
                zero_block(pe_ref[e] - bm).start()

        def start_unused(b, carry):
            zero_block(b * bm).start()
            return carry

        lax.fori_loop(first_unused, n_blocks, start_unused, 0)
        for e in range(N_EXPERTS):
            @pl.when(cnt_ref[e] > 0)
            def _():
                zero_block(pe_ref[e] - bm).wait()

        def wait_unused(b, carry):
            zero_block(b * bm).wait()
            return carry

        lax.fori_loop(first_unused, n_blocks, wait_unused, 0)

    for sub in range(tm // ROW_TILE):
        def body(t, carry, sub=sub):
            for j in range(TOP_K):
                d = dest_ref[(sub * TOP_K + j) * ROW_TILE + t]
                pltpu.make_async_copy(h_ref.at[pl.ds(sub * ROW_TILE + t, 1), :], xg_ref.at[pl.ds(d, 1), :],
                                      sem).start()
            return carry

        lax.fori_loop(0, ROW_TILE, body, 0)
    n = tm * TOP_K
    pltpu.make_async_copy(xg_ref.at[pl.ds(0, n), :], xg_ref.at[pl.ds(0, n), :], sem).wait()


def _dispatch(pad_end, cnt, dest_flat, h2, n_blocks):
    tm = ROW_TILE * DISPATCH_TILES
    bm = EXPERT_BLOCK
    rows = h2.shape[0]
    assert rows % tm == 0
    grid_spec = pltpu.PrefetchScalarGridSpec(
        num_scalar_prefetch=2,
        grid=(rows // tm,),
        in_specs=[pl.BlockSpec((tm * TOP_K,), lambda i, pe, cn: (i,), memory_space=pltpu.SMEM),
                  pl.BlockSpec((tm, ROW_WIDTH), lambda i, pe, cn: (i, 0))],
        out_specs=pl.BlockSpec(memory_space=pl.ANY),
        scratch_shapes=[pltpu.VMEM((bm, ROW_WIDTH), F32), pltpu.SemaphoreType.DMA(()), pltpu.SemaphoreType.DMA(())])
    return pl.pallas_call(
        functools.partial(_dispatch_kernel, bm=bm),
        grid_spec=grid_spec,
        out_shape=jax.ShapeDtypeStruct((n_blocks * bm, ROW_WIDTH), F32),
        compiler_params=_cparams("arbitrary"),
        name="moe_dispatch",
    )(pad_end, cnt, dest_flat, h2)


def _expert_kernel(be_ref, nu_ref, x_ref, wgu_ref, bgu_ref, wd_ref, bd_ref, y_ref, wgu_b, wd_b):
    i = pl.program_id(0)
    used = i < nu_ref[0]
    fresh = jnp.logical_or(i == 0, be_ref[i] != be_ref[jnp.maximum(i - 1, 0)])

    @pl.when(jnp.logical_and(used, fresh))
    def _():
        wgu_b[...] = wgu_ref[0].astype(BF16)
        wd_b[...] = wd_ref[0].astype(BF16)

    @pl.when(jnp.logical_not(used))
    def _():
        y_ref[...] = jnp.zeros(y_ref.shape, y_ref.dtype)

    @pl.when(used)
    def _():
        words = lax.bitcast_convert_type(x_ref[:, 0:PACKED], jnp.uint32)
        low = lax.bitcast_convert_type(lax.shift_left(words, jnp.uint32(16)), F32)
        high = lax.bitcast_convert_type(jnp.bitwise_and(words, jnp.uint32(0xFFFF0000)), F32)
        x = jnp.concatenate([low.astype(BF16), high.astype(BF16)], axis=1)
        gu = jnp.dot(x, wgu_b[...], preferred_element_type=F32) + bgu_ref[0]
        glu = jnp.minimum(gu[:, :D_EXPERT], SWIGLU_LIMIT)
        lin = jnp.clip(gu[:, D_EXPERT:], -SWIGLU_LIMIT, SWIGLU_LIMIT)
        hid = glu * jax.nn.sigmoid(SWIGLU_ALPHA * glu) * (lin + 1.0)
        y = jnp.dot(hid.astype(BF16), wd_b[...], preferred_element_type=F32) + bd_ref[0]
        meta = x_ref[:, PACKED:]
        me = be_ref[i].astype(F32)
        w = jnp.zeros((x_ref.shape[0], 1), F32)
        for j in range(TOP_K):
            w = w + jnp.where(meta[:, TOP_K + j:TOP_K + j + 1] == me, meta[:, j:j + 1], 0.0)
        y_ref[...] = (y * w).astype(y_ref.dtype)


def _experts(block_e, n_used, xg, w_gu_b, b_gu, w_down_b, b_down):
    bm = EXPERT_BLOCK
    n_blocks = xg.shape[0] // bm
    grid_spec = pltpu.PrefetchScalarGridSpec(
        num_scalar_prefetch=2,
        grid=(n_blocks,),
        in_specs=[pl.BlockSpec((bm, ROW_WIDTH), lambda i, be, nu: (jnp.minimum(i, nu[0] - 1), 0)),
                  pl.BlockSpec((1, D_MODEL, 2 * D_EXPERT), lambda i, be, nu: (be[i], 0, 0)),
                  pl.BlockSpec((1, 1, 2 * D_EXPERT), lambda i, be, nu: (be[i], 0, 0)),
                  pl.BlockSpec((1, D_EXPERT, D_MODEL), lambda i, be, nu: (be[i], 0, 0)),
                  pl.BlockSpec((1, 1, D_MODEL), lambda i, be, nu: (be[i], 0, 0))],
        out_specs=pl.BlockSpec((bm, D_MODEL), lambda i, be, nu: (i, 0)),
        scratch_shapes=[pltpu.VMEM((D_MODEL, 2 * D_EXPERT), BF16), pltpu.VMEM((D_EXPERT, D_MODEL), BF16)])
    return pl.pallas_call(
        _expert_kernel,
        grid_spec=grid_spec,
        out_shape=jax.ShapeDtypeStruct((xg.shape[0], D_MODEL), BF16),
        compiler_params=_cparams("arbitrary"),
        name="moe_experts",
    )(block_e, n_used, xg, w_gu_b, b_gu.reshape(N_EXPERTS, 1, -1), w_down_b, b_down.reshape(N_EXPERTS, 1, -1))


def _combine_kernel(row0_ref, col_ref, x1_ref, mp_ref, ms_ref, ng_ref, yg_ref,
                    yp_ref, ys_ref, rows_ref, g_ref, sem, *, n_prompt_tiles):
    i = pl.program_id(0)
    n_steps = pl.num_programs(0)
    tm = x1_ref.shape[0]
    is_p = i < n_prompt_tiles
    win = COMBINE_WINDOW
    n_win_rows = COMBINE_SLOTS * win
    slot = i % 2

    def fetch_windows(tile, buf):
        for s in range(COMBINE_SLOTS):
            row0 = pl.multiple_of(row0_ref[tile * COMBINE_SLOTS + s], COMBINE_ALIGN)
            pltpu.make_async_copy(yg_ref.at[pl.ds(row0, win), :], rows_ref.at[buf, pl.ds(s * win, win), :],
                                  sem.at[buf]).start()

    @pl.when(i == 0)
    def _():
        fetch_windows(0, 0)

    @pl.when(i + 1 < n_steps)
    def _():
        fetch_windows(i + 1, 1 - slot)

    pltpu.make_async_copy(yg_ref.at[pl.ds(0, n_win_rows), :], rows_ref.at[slot], sem.at[slot]).wait()
    col = col_ref[...]
    for c in range(n_win_rows // LANES):
        lane = lax.broadcasted_iota(jnp.int32, (tm, LANES), 1) + c * LANES
        g = jnp.zeros((tm, LANES), F32)
        for j in range(TOP_K):
            g = jnp.where(lane == col[:, j:j + 1], 1.0, g)
        g_ref[:, c * LANES:(c + 1) * LANES] = g.astype(BF16)
    f = jnp.dot(g_ref[...], rows_ref[slot], preferred_element_type=F32)

    g2 = jnp.where(is_p, mp_ref[0, 5:6, :], ms_ref[5])
    y = x1_ref[...] + g2 * _rms(f, ng_ref[3:4, :])

    @pl.when(is_p)
    def _():
        yp_ref[...] = y

    @pl.when(jnp.logical_not(is_p))
    def _():
        ys_ref[...] = y


def _combine(row0, col, x1, mod_p, mod_s, norm_g, yg, *, n_prompt_rows, n_sample_rows):
    tm = ROW_TILE
    npt = n_prompt_rows // tm
    assert n_sample_rows == tm and (COMBINE_SLOTS * COMBINE_WINDOW) % LANES == 0
    n_tiles = npt + 1
    tiles_per_batch = npt // mod_p.shape[0]
    last_p = npt - 1
    grid_spec = pltpu.PrefetchScalarGridSpec(
        num_scalar_prefetch=1,
        grid=(n_tiles,),
        in_specs=[pl.BlockSpec((tm, TOP_K), lambda i, r0: (i, 0)),
                  pl.BlockSpec((tm, D_MODEL), lambda i, r0: (i, 0)),
                  pl.BlockSpec((1, 6, D_MODEL), lambda i, r0: (jnp.minimum(i, last_p) // tiles_per_batch, 0, 0)),
                  pl.BlockSpec((6, tm, D_MODEL), lambda i, r0: (0, 0, 0)),
                  pl.BlockSpec((4, D_MODEL), lambda i, r0: (0, 0)),
                  pl.BlockSpec(memory_space=pl.ANY)],
        out_specs=[pl.BlockSpec((tm, D_MODEL), lambda i, r0: (jnp.minimum(i, last_p), 0)),
                   pl.BlockSpec((tm, D_MODEL), lambda i, r0: (0, 0))],
        scratch_shapes=[pltpu.VMEM((2, COMBINE_SLOTS * COMBINE_WINDOW, D_MODEL), BF16),
                        pltpu.VMEM((tm, COMBINE_SLOTS * COMBINE_WINDOW), BF16),
                        pltpu.SemaphoreType.DMA((2,))])
    return pl.pallas_call(
        functools.partial(_combine_kernel, n_prompt_tiles=npt),
        grid_spec=grid_spec,
        out_shape=[jax.ShapeDtypeStruct((n_prompt_rows, D_MODEL), F32),
                   jax.ShapeDtypeStruct((n_sample_rows, D_MODEL), F32)],
        compiler_params=_cparams("arbitrary"),
        name="moe_combine",
    )(row0, col, x1, mod_p, mod_s, norm_g, yg)


def _t5_bucket_np(dist):
    n = np.maximum(dist, 0)
    max_exact = NUM_BUCKETS // 2
    nf = np.maximum(n, 1).astype(np.float64)
    large = max_exact + (np.log(nf / max_exact) / math.log(MAX_DISTANCE / max_exact)
                         * (NUM_BUCKETS - max_exact)).astype(np.int32)
    large = np.minimum(large, NUM_BUCKETS - 1)
    return np.where(n < max_exact, n, large).astype(np.int32)


def _bias_from_dist(rel_bias, dist):
    onehot = jax.nn.one_hot(_t5_bucket_np(dist).reshape(-1), NUM_BUCKETS, dtype=F32)
    shifted = (rel_bias - rel_bias[NUM_BUCKETS - 1]).reshape(NUM_BUCKETS, -1)
    out = jnp.dot(onehot, shifted, precision=lax.Precision.HIGHEST).reshape(dist.shape + rel_bias.shape[1:])
    return jnp.where(jnp.asarray(dist >= 0)[..., None, None], out, -jnp.inf).astype(F32)


def _head_masked(t):
    h = t.shape[0]
    same = np.eye(h, dtype=bool)[:, None, None, None, :]
    full = jnp.where(same, t[..., None], -jnp.inf)
    return full.reshape(h * t.shape[1] * t.shape[2], t.shape[3] * h)


def kernel(x_prompt, x_sample, cache_k, cache_v, page_table, c_prompt, c_sample, w_ada, b_ada, norm_g, w_in,
           w_out, ln_v_g, ln_v_b, w_spatial, b_spatial, lam_params, subln_g, rel_bias, w_router, b_router,
           w_gu, b_gu, w_down, b_down):
    batch, seq, d = x_prompt.shape
    nb, ds = x_sample.shape[:2]
    n_pages = page_table.shape[1]
    past = n_pages * PAGE_SIZE
    rows_p = batch * seq
    rows_s = nb * ds
    assert rows_s == ROW_TILE and d == D_MODEL

    mod = _ada(jnp.concatenate([c_prompt, c_sample], axis=0), w_ada[0], b_ada[0])
    mod_p = mod[:batch].reshape(batch, 6, D_MODEL)
    mod_s = jnp.transpose(jnp.repeat(mod[batch:], ds, axis=0).reshape(rows_s, 6, D_MODEL), (1, 0, 2))

    ng = norm_g[0]
    w_in_b = w_in[0].astype(BF16)
    w_out_b = w_out[0].astype(BF16)
    lng = ln_v_g[0].reshape(1, A_WIDTH)
    lnb = ln_v_b[0].reshape(1, A_WIDTH)
    grp = np.arange(A_WIDTH) // A_HEAD
    avg = jnp.asarray((grp[:, None] == grp[None, :]).astype(np.float32) / A_HEAD, BF16)
    ws_p = jnp.tril(w_spatial[0]).astype(BF16)
    bs_p = jnp.repeat(b_spatial[0].T, A_HEAD, axis=1)
    w_small = jnp.tril(w_spatial[0][:, :ds, :ds])
    same_batch = np.kron(np.eye(nb, dtype=np.float32), np.ones((ds, ds), np.float32))
    ws_s = (jnp.tile(w_small, (1, nb, nb)) * same_batch).astype(BF16)
    bs_s = jnp.tile(jnp.repeat(b_spatial[0][:, :ds].T, A_HEAD, axis=1), (nb, 1))

    xp2 = x_prompt.reshape(rows_p, D_MODEL)
    xs2 = x_sample.reshape(rows_s, D_MODEL)
    assert ATTN_TILE % ROW_TILE == 0 and seq % ATTN_TILE == 0
    a_p, va_p, q_p, k_p, v_p, kb_p, vt_p = _inproj(
        xp2, mod_p[:, 0:1], mod_p[:, 1:2], ng[0:1], w_in_b, lng, lnb, avg, ws_p, bs_p,
        tiles_per_mod=seq // ROW_TILE, chunk=CHUNK, q_dtype=BF16, q_scale=LOG2E / math.sqrt(B_HEAD), va_rows=CHUNK)
    a_s, va_s, q_s, k_s, v_s, _, _ = _inproj(
        xs2, mod_s[0:1], mod_s[1:2], ng[0:1], w_in_b, lng, lnb, avg, ws_s, bs_s,
        tiles_per_mod=1, chunk=rows_s, q_dtype=F32, q_scale=1.0, va_rows=rows_s)

    ti = np.arange(ATTN_TILE)
    near = ti[:, None] - ti[None, :]
    far = np.full_like(near, 4 * MAX_DISTANCE)
    dist_p = np.stack([near, near + ATTN_TILE, far])
    bias_p = jnp.transpose(_bias_from_dist(rel_bias, dist_p), (3, 0, 2, 4, 1))
    bias_p = bias_p.reshape(B_HEADS, 3, ATTN_TILE, 2 * ATTN_TILE) * LOG2E
    b_p = _attn_p(q_p, kb_p, vt_p, bias_p, lam_params[0], subln_g, batch=batch, seq=seq)

    qi = np.arange(ds)
    ki = np.arange(PAGE_SIZE)
    dist_l = PAGE_SIZE + qi[:, None] - ki[None, :]
    kn_i = np.arange(PAGE_SIZE // B_HEADS)
    dist_n = np.where(kn_i[None, :] < ds, qi[:, None] - kn_i[None, :], -1)
    to_rows = lambda t: jnp.transpose(t, (2, 3, 0, 1))
    bias_l = _head_masked(to_rows(_bias_from_dist(rel_bias, dist_l)))
    bias_n = _head_masked(to_rows(_bias_from_dist(rel_bias, dist_n)))
    maskb = _head_masked(jnp.zeros((B_HEADS, 2, ds, PAGE_SIZE), F32))
    pad_keys = lambda t: jnp.pad(t.reshape(nb, ds * B_HEADS, B_VDIM), ((0, 0), (0, PAGE_SIZE - ds * B_HEADS), (0, 0)))
    n_phys = cache_k.shape[0]
    b_s = _attn_s(page_table, q_s, pad_keys(k_s), pad_keys(v_s), maskb, bias_l, bias_n, lam_params[0], subln_g,
                  cache_k.reshape(n_phys * PAGE_SIZE * B_HEADS, B_VDIM),
                  cache_v.reshape(n_phys * PAGE_SIZE * B_HEADS, B_VDIM), ds=ds)

    x1, h2, top_i, top_r, counts, tile_base = _outproj(
        a_p, a_s, b_p, b_s, xp2, xs2, mod_p, mod_s, ng, w_out_b, w_router[0], b_router[0])

    bm = EXPERT_BLOCK
    rows = rows_p + rows_s
    n_blocks = rows * TOP_K // bm + N_EXPERTS + 1
    cnt = counts[:, 0].astype(jnp.int32)
    padded = (cnt + bm - 1) // bm * bm
    pad_end = jnp.cumsum(padded)
    pad_start = pad_end - padded
    experts = jnp.arange(N_EXPERTS, dtype=jnp.int32)
    chosen = top_i[..., None] == experts
    dest = jnp.sum(jnp.where(chosen, pad_start, 0), axis=-1) + top_r
    n_tiles = rows // ROW_TILE
    win = COMBINE_WINDOW
    assert win % COMBINE_ALIGN == 0
    assert (ROW_TILE * TOP_K + N_EXPERTS * (COMBINE_ALIGN - 1)) // win + N_EXPERTS <= COMBINE_SLOTS
    base = tile_base[:, :, 0].astype(jnp.int32)
    first = pad_start[None, :] + base
    win0 = first // COMBINE_ALIGN * COMBINE_ALIGN
    sent = jnp.concatenate([base[1:], cnt[None, :]], axis=0) - base
    n_win = jnp.where(sent > 0, (first - win0 + sent + win - 1) // win, 0)
    slot_end = jnp.cumsum(n_win, axis=1)
    slot_start = slot_end - n_win
    slots = jnp.arange(COMBINE_SLOTS, dtype=jnp.int32)
    slot_e = jnp.minimum(jnp.sum((slot_end[:, None, :] <= slots[None, :, None]).astype(jnp.int32), axis=2),
                         N_EXPERTS - 1)
    pick = slot_e[..., None] == experts
    slot_row0 = (jnp.sum(jnp.where(pick, win0[:, None, :], 0), axis=2)
                 + win * (slots[None, :] - jnp.sum(jnp.where(pick, slot_start[:, None, :], 0), axis=2)))
    slot_row0 = jnp.where(slots[None, :] < slot_end[:, -1:], slot_row0, 0)
    local = dest - jnp.sum(jnp.where(chosen, jnp.repeat(win0, ROW_TILE, axis=0), 0), axis=-1)
    col = jnp.sum(jnp.where(chosen, jnp.repeat(slot_start, ROW_TILE, axis=0), 0), axis=-1) * win + local
    dest = jnp.transpose(dest.reshape(TOP_K, n_tiles, ROW_TILE), (1, 0, 2)).reshape(-1)
    blk_start = jnp.arange(n_blocks, dtype=jnp.int32) * bm
    block_e = jnp.minimum(jnp.sum((pad_end[None, :] <= blk_start[:, None]).astype(jnp.int32), axis=1),
                          N_EXPERTS - 1)
    n_used = (pad_end[-1:] // bm).astype(jnp.int32)

    xg = _dispatch(pad_end.astype(jnp.int32), cnt, dest, h2, n_blocks)
    yg = _experts(block_e, n_used, xg, w_gu[0], b_gu[0], w_down[0], b_down[0])
    y_p, y_s = _combine(slot_row0.reshape(-1), col.T, x1, mod_p, mod_s, ng, yg,
                        n_prompt_rows=rows_p, n_sample_rows=rows_s)

    return (y_p.reshape(batch, seq, D_MODEL),
            y_s.reshape(nb, ds, D_MODEL),
            k_p.reshape(batch, seq, 1, B_HEADS, B_VDIM),
            v_p.reshape(batch, seq, 1, B_HEADS, B_VDIM),
            k_s.reshape(nb, ds, 1, B_HEADS, B_VDIM),
            v_s.reshape(nb, ds, 1, B_HEADS, B_VDIM),
            va_p.reshape(batch, CHUNK, 1, A_WIDTH),
            va_s.reshape(nb, ds, 1, A_WIDTH))
```

```python
import functools
import math

import numpy as np
import jax
import jax.numpy as jnp
from jax import lax
from jax.experimental import pallas as pl
from jax.experimental.pallas import tpu as pltpu

F32 = jnp.float32
BF16 = jnp.bfloat16

D_MODEL = 1024
A_WIDTH = 512
A_HEAD = 64
A_GROUPS = 8
CHUNK = 128
B_WIDTH = 512
B_HEAD = 64
B_VDIM = 128
B_HEADS = 4
IN_WIDTH = 2 * A_WIDTH + 3 * B_WIDTH
NUM_BUCKETS = 32
MAX_DISTANCE = 128
PAGE_SIZE = 128
N_EXPERTS = 32
TOP_K = 4
D_EXPERT = 1024
SWIGLU_LIMIT = 7.0
SWIGLU_ALPHA = 1.702
NORM_EPS = 1e-6
LAM_INIT = 0.8 - 0.6 * math.exp(-0.3 * 0)

ROW_TILE = 256
ATTN_TILE = 512
ATTN_BATCHES = 1
PAGES_PER_STEP = 32
LOG2E = math.log2(math.e)
LANES = 128
MXU_WIDTH = 256
PACKED = D_MODEL // 2
ROW_WIDTH = PACKED + LANES
EXPERT_BLOCK = 512
DISPATCH_TILES = 5
COMBINE_ALIGN = 16
COMBINE_WINDOW = 16
COMBINE_SLOTS = 128
VMEM_LIMIT = 56 * 1024 * 1024


def _cparams(*sem):
    return pltpu.CompilerParams(dimension_semantics=sem, vmem_limit_bytes=VMEM_LIMIT)


def _rms(x, g):
    return x * lax.rsqrt(jnp.mean(x * x, axis=-1, keepdims=True) + NORM_EPS) * g


def _ada_kernel(c_ref, w_ref, b_ref, o_ref):
    s = jax.nn.silu(c_ref[...]).astype(BF16)
    o_ref[...] = jnp.dot(s, w_ref[...].astype(BF16), preferred_element_type=F32) + b_ref[...]


def _ada(c_all, w_ada, b_ada):
    n = c_all.shape[0]
    tn = 1024
    return pl.pallas_call(
        _ada_kernel,
        grid=(6 * D_MODEL // tn,),
        in_specs=[pl.BlockSpec((n, D_MODEL), lambda j: (0, 0)),
                  pl.BlockSpec((D_MODEL, tn), lambda j: (0, j)),
                  pl.BlockSpec((1, tn), lambda j: (0, j))],
        out_specs=pl.BlockSpec((n, tn), lambda j: (0, j)),
        out_shape=jax.ShapeDtypeStruct((n, 6 * D_MODEL), F32),
        compiler_params=_cparams("arbitrary"),
        name="ada",
    )(c_all, w_ada, b_ada.reshape(1, -1))


def _inproj_kernel(x_ref, sh_ref, sc_ref, ng_ref, w_ref, lng_ref, lnb_ref, avg_ref, ws_ref, bs_ref,
                   a_ref, va_ref, q_ref, k_ref, v_ref, kb_ref, vt_ref, *, chunk, q_scale):
    tm = x_ref.shape[0]
    va_rows = va_ref.shape[0]
    h = (_rms(x_ref[...], ng_ref[...]) * (1.0 + sc_ref[0]) + sh_ref[0]).astype(BF16)

    u = jax.nn.gelu(jnp.dot(h, w_ref[:, 0:A_WIDTH], preferred_element_type=F32))
    gv = jax.nn.gelu(jnp.dot(h, w_ref[:, A_WIDTH:2 * A_WIDTH], preferred_element_type=F32))

    avg = avg_ref[...]

    def group_mean(t):
        return jnp.dot(t.astype(BF16), avg, preferred_element_type=F32)

    xc = gv - group_mean(gv)
    va = xc * lax.rsqrt(group_mean(xc * xc) + NORM_EPS) * lng_ref[...] + lnb_ref[...]
    va_ref[...] = va[tm - va_rows:, :]

    vab = va.astype(BF16)
    lane = lax.broadcasted_iota(jnp.int32, (chunk, 128), 1)
    for c in range(tm // chunk):
        r0 = c * chunk
        for pair in range(A_GROUPS // 2):
            c0 = pair * 128
            vp = vab[r0:r0 + chunk, c0:c0 + 128]
            lo_half = jnp.where(lane < A_HEAD, vp, jnp.zeros_like(vp))
            hi_half = jnp.where(lane >= A_HEAD, vp, jnp.zeros_like(vp))
            s = (jnp.dot(ws_ref[2 * pair], lo_half, preferred_element_type=F32)
                 + jnp.dot(ws_ref[2 * pair + 1], hi_half, preferred_element_type=F32))
            a = u[r0:r0 + chunk, c0:c0 + 128] * (s + bs_ref[:, c0:c0 + 128])
            a_ref[r0:r0 + chunk, c0:c0 + 128] = a.astype(a_ref.dtype)

    q = jnp.dot(h, w_ref[:, 2 * A_WIDTH:2 * A_WIDTH + B_WIDTH], preferred_element_type=F32)
    q_ref[...] = (q * q_scale).astype(q_ref.dtype)
    k = jnp.dot(h, w_ref[:, 2 * A_WIDTH + B_WIDTH:2 * A_WIDTH + 2 * B_WIDTH], preferred_element_type=F32)
    kb_ref[...] = k.astype(BF16)
    v = jnp.dot(h, w_ref[:, 2 * A_WIDTH + 2 * B_WIDTH:IN_WIDTH], preferred_element_type=F32)
    for hh in range(B_HEADS):
        c0 = hh * B_VDIM
        k_ref[pl.ds(hh, tm, stride=B_HEADS), :] = k[:, c0:c0 + B_VDIM]
        v_ref[pl.ds(hh, tm, stride=B_HEADS), :] = v[:, c0:c0 + B_VDIM]
        vt_ref[0, hh, 0] = v[:, c0:c0 + B_VDIM].T.astype(BF16)


def _inproj(x2, sh, sc, ng, w_in_b, lng, lnb, avg, ws, bs, *, tiles_per_mod, chunk, q_dtype, q_scale, va_rows):
    rows = x2.shape[0]
    tm = ROW_TILE
    n_tiles = rows // tm
    mod_rows = sh.shape[1]
    row = lambda i: (i, 0)
    const2 = lambda i: (0, 0)
    mod_map = lambda i: (i // tiles_per_mod, 0, 0)
    out_w = lambda w, dt: jax.ShapeDtypeStruct((rows, w), dt)
    head_rows = jax.ShapeDtypeStruct((rows * B_HEADS, B_VDIM), F32)
    head_rows_spec = pl.BlockSpec((tm * B_HEADS, B_VDIM), row)
    vt_shape = jax.ShapeDtypeStruct((n_tiles // tiles_per_mod, B_HEADS, tiles_per_mod, B_VDIM, tm), BF16)
    vt_spec = pl.BlockSpec((1, B_HEADS, 1, B_VDIM, tm), lambda i: (i // tiles_per_mod, 0, i % tiles_per_mod, 0, 0))
    return pl.pallas_call(
        functools.partial(_inproj_kernel, chunk=chunk, q_scale=q_scale),
        grid=(rows // tm,),
        in_specs=[pl.BlockSpec((tm, D_MODEL), row),
                  pl.BlockSpec((1, mod_rows, D_MODEL), mod_map),
                  pl.BlockSpec((1, mod_rows, D_MODEL), mod_map),
                  pl.BlockSpec((1, D_MODEL), const2),
                  pl.BlockSpec((D_MODEL, IN_WIDTH), const2),
                  pl.BlockSpec((1, A_WIDTH), const2),
                  pl.BlockSpec((1, A_WIDTH), const2),
                  pl.BlockSpec((A_WIDTH, A_WIDTH), const2),
                  pl.BlockSpec((A_GROUPS, chunk, chunk), lambda i: (0, 0, 0)),
                  pl.BlockSpec((chunk, A_WIDTH), const2)],
        out_specs=[pl.BlockSpec((tm, A_WIDTH), row),
                   pl.BlockSpec((va_rows, A_WIDTH), lambda i: (i // tiles_per_mod, 0)),
                   pl.BlockSpec((tm, B_WIDTH), row), head_rows_spec, head_rows_spec,
                   pl.BlockSpec((tm, B_WIDTH), row), vt_spec],
        out_shape=[out_w(A_WIDTH, BF16),
                   jax.ShapeDtypeStruct((n_tiles // tiles_per_mod * va_rows, A_WIDTH), F32),
                   out_w(B_WIDTH, q_dtype), head_rows, head_rows, out_w(B_WIDTH, BF16), vt_shape],
        compiler_params=_cparams("arbitrary"),
        name="inproj",
    )(x2, sh, sc, ng, w_in_b, lng, lnb, avg, ws, bs)


def _stack_halves(q):
    lane = lax.broadcasted_iota(jnp.int32, q.shape, 1)
    zero = jnp.zeros_like(q)
    return jnp.concatenate([jnp.where(lane < B_HEAD, q, zero), jnp.where(lane >= B_HEAD, q, zero)], axis=0)


def _lambda(lam_ref):
    lp = lam_ref[...]
    return (jnp.exp(jnp.sum(lp[0:1] * lp[1:2], axis=-1, keepdims=True))
            - jnp.exp(jnp.sum(lp[2:3] * lp[3:4], axis=-1, keepdims=True)) + LAM_INIT)


def _diff_finish(acc, l, n, lam, g):
    o = acc[:n] / l[:n] - lam * (acc[n:] / l[n:])
    return _rms(o, g) * (1.0 - LAM_INIT)


def _attn_p_kernel(q_ref, k_ref, vt_ref, diag_ref, prev_ref, lam_ref, g_ref, o_ref,
                   acc_ref, m_ref, l_ref, s_ref, tmax_ref, p_ref, alpha_ref, *, tq, tk):
    qi = pl.program_id(2)
    nbb = q_ref.shape[0]
    streams = [(bb, c) for bb in range(nbb) for c in range(2)]
    q_half = []
    for bb in range(nbb):
        q = q_ref[bb]
        lane = lax.broadcasted_iota(jnp.int32, q.shape, 1)
        zero = jnp.zeros_like(q)
        q_half += [jnp.where(lane < B_HEAD, q, zero), jnp.where(lane >= B_HEAD, q, zero)]
    m_ref[...] = jnp.full(m_ref.shape, -jnp.inf, F32)
    l_ref[...] = jnp.zeros(l_ref.shape, F32)
    acc_ref[...] = jnp.zeros(acc_ref.shape, F32)
    p_ref[1] = jnp.zeros(p_ref.shape[1:], BF16)
    alpha_ref[1] = jnp.ones(alpha_ref.shape[1:], F32)

    def scores(j):
        out = []
        for bb in range(nbb):
            k = k_ref[bb, pl.ds(pl.multiple_of(j * tk, tk), tk), :]
            out += [lax.dot_general(k, q_half[2 * bb + c], (((1,), (1,)), ((), ())), preferred_element_type=F32)
                    for c in range(2)]
        return out

    near_keys = prev_ref.shape[1]

    def keep_scores(j, kind, ss):
        slot = j % 2
        for u in range(len(streams)):
            s_ref[slot, u] = ss[u]
            tmax_ref[slot, u] = jnp.max(ss[u], axis=0, keepdims=True)
        if kind is None:
            return

        @pl.when(kind == 0)
        def _():
            for u, (bb, c) in enumerate(streams):
                s = s_ref[slot, u] + diag_ref[0, :, c * tq:(c + 1) * tq]
                s_ref[slot, u] = s
                tmax_ref[slot, u] = jnp.max(s, axis=0, keepdims=True)

        @pl.when(kind == 1)
        def _():
            for u, (bb, c) in enumerate(streams):
                tail = s_ref[slot, u, tk - near_keys:, :] + prev_ref[0, :, c * tq:(c + 1) * tq]
                s_ref[slot, u, tk - near_keys:, :] = tail
                tmax_ref[slot, u] = jnp.maximum(tmax_ref[slot, u], jnp.max(tail, axis=0, keepdims=True))

    def softmax(j):
        slot = j % 2
        for u in range(len(streams)):
            m_old = m_ref[u]
            m_new = jnp.maximum(m_old, tmax_ref[slot, u])
            alpha = jnp.exp2(m_old - m_new)
            p = jnp.exp2(s_ref[slot, u] - m_new)
            l_ref[u] = alpha * l_ref[u] + jnp.sum(p, axis=0, keepdims=True)
            m_ref[u] = m_new
            p_ref[slot, u] = p.astype(BF16)
            alpha_ref[slot, u] = alpha

    def pv(j):
        slot = j % 2
        pieces = tk // vt_ref.shape[-1]
        first = jnp.maximum(j, 0) * pieces
        for u, (bb, c) in enumerate(streams):
            add = None
            for r in range(pieces):
                w = vt_ref.shape[-1]
                t = jnp.dot(vt_ref[bb, 0, first + r], p_ref[slot, u, r * w:(r + 1) * w, :],
                            preferred_element_type=F32)
                add = t if add is None else add + t
            acc_ref[u] = alpha_ref[slot, u] * acc_ref[u] + add

    def far_body(j, carry):
        ss = scores(j + 1)
        pv(j - 1)
        softmax(j)
        keep_scores(j + 1, None, ss)
        return carry

    def near_body(j, carry):
        ss = scores(j + 1)
        pv(j - 1)
        softmax(j)
        keep_scores(j + 1, qi - (j + 1), ss)
        return carry

    keep_scores(0, jnp.minimum(qi, 2), scores(0))
    n_far_fetch = jnp.maximum(qi - 2, 0)
    lax.fori_loop(0, n_far_fetch, far_body, 0)
    lax.fori_loop(n_far_fetch, qi, near_body, 0)
    pv(qi - 1)
    softmax(qi)
    pv(qi)

    lam = _lambda(lam_ref)
    for bb in range(nbb):
        o = acc_ref[2 * bb] / l_ref[2 * bb] - lam * (acc_ref[2 * bb + 1] / l_ref[2 * bb + 1])
        y = o * lax.rsqrt(jnp.mean(o * o, axis=0, keepdims=True) + NORM_EPS) * (1.0 - LAM_INIT)
        o_ref[bb] = (y.T * g_ref[...]).astype(o_ref.dtype)


def _attn_p(qb, kb, vt, bias_diag, bias_prev, lam_params, subln_g, *, batch, seq):
    tq = tk = ATTN_TILE
    nq = seq // tq
    nbb = ATTN_BATCHES
    ns = 2 * nbb
    assert batch % nbb == 0
    out = pl.pallas_call(
        functools.partial(_attn_p_kernel, tq=tq, tk=tk),
        grid=(batch // nbb, B_HEADS, nq),
        in_specs=[pl.BlockSpec((nbb, tq, B_VDIM), lambda b, h, i: (b, i, h)),
                  pl.BlockSpec((nbb, seq, B_VDIM), lambda b, h, i: (b, 0, h)),
                  pl.BlockSpec((nbb, 1, seq // ROW_TILE, B_VDIM, ROW_TILE), lambda b, h, i: (b, h, 0, 0, 0)),
                  pl.BlockSpec((1, tk, 2 * tq), lambda b, h, i: (h, 0, 0)),
                  pl.BlockSpec((1, bias_prev.shape[1], 2 * tq), lambda b, h, i: (h, 0, 0)),
                  pl.BlockSpec((4, B_HEAD), lambda b, h, i: (0, 0)),
                  pl.BlockSpec((1, B_VDIM), lambda b, h, i: (0, 0))],
        out_specs=pl.BlockSpec((nbb, tq, B_VDIM), lambda b, h, i: (b, i, h)),
        out_shape=jax.ShapeDtypeStruct((batch, seq, B_WIDTH), BF16),
        scratch_shapes=[pltpu.VMEM((ns, B_VDIM, tq), F32),
                        pltpu.VMEM((ns, 1, tq), F32),
                        pltpu.VMEM((ns, 1, tq), F32),
                        pltpu.VMEM((2, ns, tk, tq), F32),
                        pltpu.VMEM((2, ns, 1, tq), F32),
                        pltpu.VMEM((2, ns, tk, tq), BF16),
                        pltpu.VMEM((2, ns, 1, tq), F32)],
        compiler_params=_cparams("arbitrary", "arbitrary", "arbitrary"),
        name="attn_prompt",
    )(qb.reshape(batch, seq, B_WIDTH), kb.reshape(batch, seq, B_WIDTH), vt, bias_diag, bias_prev, lam_params,
      subln_g)
    return out.reshape(batch * seq, B_WIDTH)


def _attn_s_kernel(pt_ref, q_ref, kn_ref, vn_ref, maskb_ref, biasl_ref, biasn_ref, lam_ref, g_ref, *rest,
                   npages, ds):
    del pt_ref
    k_refs = rest[:npages]
    v_refs = rest[npages:2 * npages]
    o_ref = rest[2 * npages]
    acc_ref, m_ref, l_ref = rest[2 * npages + 1:]
    g = pl.program_id(1)
    last = g == pl.num_programs(1) - 1

    @pl.when(g == 0)
    def _():
        m_ref[...] = jnp.full(m_ref.shape, -jnp.inf, F32)
        l_ref[...] = jnp.zeros(l_ref.shape, F32)
        acc_ref[...] = jnp.zeros(acc_ref.shape, F32)

    q = q_ref[...] * (1.0 / math.sqrt(B_HEAD))
    qall = jnp.concatenate([_stack_halves(q[:, h * B_VDIM:(h + 1) * B_VDIM]) for h in range(B_HEADS)],
                           axis=0).astype(BF16)

    def process(k_blocks, v_blocks, biases):
        s = jnp.concatenate(
            [lax.dot_general(qall, kb.astype(BF16), (((1,), (1,)), ((), ())), preferred_element_type=F32) + bb
             for kb, bb in zip(k_blocks, biases)], axis=1)
        m_old = m_ref[...]
        m_new = jnp.maximum(m_old, jnp.max(s, axis=-1, keepdims=True))
        alpha = jnp.exp(m_old - m_new)
        pr = jnp.exp(s - m_new)
        l_ref[...] = alpha * l_ref[...] + jnp.sum(pr, axis=-1, keepdims=True)
        prb = pr.astype(BF16)
        pv = None
        off = 0
        for vb in v_blocks:
            n = vb.shape[0]
            t = jnp.dot(prb[:, off:off + n], vb.astype(BF16), preferred_element_type=F32)
            pv = t if pv is None else pv + t
            off += n
        acc_ref[...] = alpha * acc_ref[...] + pv
        m_ref[...] = m_new

    maskb = maskb_ref[...]
    newest = jnp.where(last, biasl_ref[...], maskb)
    process([k_refs[p][...] for p in range(npages)], [v_refs[p][...] for p in range(npages)],
            [maskb] * (npages - 1) + [newest])

    @pl.when(last)
    def _():
        process([kn_ref[0]], [vn_ref[0]], [biasn_ref[...]])
        lam = _lambda(lam_ref)
        acc = acc_ref[...]
        l = l_ref[...]
        for h in range(B_HEADS):
            r0 = h * 2 * ds
            o_ref[:, h * B_VDIM:(h + 1) * B_VDIM] = _diff_finish(
                acc[r0:r0 + 2 * ds], l[r0:r0 + 2 * ds], ds, lam, g_ref[...])


def _attn_s(page_table, qs, kn, vn, maskb, bias_last, bias_new, lam_params, subln_g, cache_k2, cache_v2, *, ds):
    nb, n_pages = page_table.shape
    npg = PAGES_PER_STEP
    steps = n_pages // npg
    page_rows = PAGE_SIZE * B_HEADS
    nrow = B_HEADS * 2 * ds

    def page_spec(p):
        return pl.BlockSpec((page_rows, B_VDIM), lambda b, g, pt, p=p: (pt[b, g * npg + p], 0))

    const2 = lambda b, g, pt: (0, 0)
    grid_spec = pltpu.PrefetchScalarGridSpec(
        num_scalar_prefetch=1,
        grid=(nb, steps),
        in_specs=[pl.BlockSpec((ds, B_WIDTH), lambda b, g, pt: (b, 0)),
                  pl.BlockSpec((1, PAGE_SIZE, B_VDIM), lambda b, g, pt: (b, 0, 0)),
                  pl.BlockSpec((1, PAGE_SIZE, B_VDIM), lambda b, g, pt: (b, 0, 0)),
                  pl.BlockSpec((nrow, page_rows), const2),
                  pl.BlockSpec((nrow, page_rows), const2),
                  pl.BlockSpec((nrow, PAGE_SIZE), const2),
                  pl.BlockSpec((4, B_HEAD), const2),
                  pl.BlockSpec((1, B_VDIM), const2)]
                 + [page_spec(p) for p in range(npg)] + [page_spec(p) for p in range(npg)],
        out_specs=pl.BlockSpec((ds, B_WIDTH), lambda b, g, pt: (b, 0)),
        scratch_shapes=[pltpu.VMEM((nrow, B_VDIM), F32),
                        pltpu.VMEM((nrow, 1), F32),
                        pltpu.VMEM((nrow, 1), F32)])
    return pl.pallas_call(
        functools.partial(_attn_s_kernel, npages=npg, ds=ds),
        grid_spec=grid_spec,
        out_shape=jax.ShapeDtypeStruct((nb * ds, B_WIDTH), F32),
        compiler_params=_cparams("arbitrary", "arbitrary"),
        name="attn_sample",
    )(page_table, qs, kn, vn, maskb, bias_last, bias_new, lam_params, subln_g,
      *([cache_k2] * npg), *([cache_v2] * npg))


def _outproj_kernel(ap_ref, as_ref, bp_ref, bs_ref, xp_ref, xs_ref, mp_ref, ms_ref, ng_ref, wo_ref,
                    wr_ref, br_ref, x1_ref, h2_ref, ti_ref, tr_ref, cnt_ref, base_ref, run_ref,
                    *, n_prompt_tiles):
    i = pl.program_id(0)
    tm = xp_ref.shape[0]
    is_p = i < n_prompt_tiles

    @pl.when(i == 0)
    def _():
        run_ref[...] = jnp.zeros(run_ref.shape, F32)

    a = jnp.where(is_p, ap_ref[...], as_ref[...])
    b = jnp.where(is_p, bp_ref[...], bs_ref[...].astype(BF16))
    x = jnp.where(is_p, xp_ref[...], xs_ref[...])
    g1 = jnp.where(is_p, mp_ref[0, 2:3, :], ms_ref[2])
    sh2 = jnp.where(is_p, mp_ref[0, 3:4, :], ms_ref[3])
    sc2 = jnp.where(is_p, mp_ref[0, 4:5, :], ms_ref[4])

    mix = (jnp.dot(a, wo_ref[0:A_WIDTH, :], preferred_element_type=F32)
           + jnp.dot(b, wo_ref[A_WIDTH:, :], preferred_element_type=F32))
    x1 = x + g1 * _rms(mix, ng_ref[1:2, :])
    x1_ref[...] = x1
    h2 = _rms(x1, ng_ref[2:3, :]) * (1.0 + sc2) + sh2
    low = lax.bitcast_convert_type(h2[:, 0:PACKED].astype(BF16).astype(F32), jnp.uint32)
    high = lax.bitcast_convert_type(h2[:, PACKED:].astype(BF16).astype(F32), jnp.uint32)
    words = jnp.bitwise_or(jnp.bitwise_and(high, jnp.uint32(0xFFFF0000)), lax.shift_right_logical(low, jnp.uint32(16)))
    h2_ref[:, 0:PACKED] = lax.bitcast_convert_type(words, F32)

    logits = jnp.dot(h2.astype(BF16), wr_ref[...], preferred_element_type=F32) + br_ref[...]
    work = logits.T[0:N_EXPERTS, :]
    sub = lax.broadcasted_iota(jnp.int32, work.shape, 0)
    vals, idxs = [], []
    for _ in range(TOP_K):
        mx = jnp.max(work, axis=0, keepdims=True)
        ix = jnp.min(jnp.where(work == mx, sub, N_EXPERTS), axis=0, keepdims=True)
        vals.append(mx)
        idxs.append(ix)
        work = jnp.where(sub == ix, -jnp.inf, work)
    exps = [jnp.exp(v - vals[0]) for v in vals]
    den = exps[0] + exps[1] + exps[2] + exps[3]

    sel = jnp.where(work == -jnp.inf, 1.0, 0.0)
    r_i = lax.broadcasted_iota(jnp.int32, (tm, tm), 0)
    c_i = lax.broadcasted_iota(jnp.int32, (tm, tm), 1)
    earlier = jnp.where(r_i < c_i, 1.0, 0.0).astype(BF16)
    base_ref[0] = run_ref[...]
    before = jnp.dot(sel.astype(BF16), earlier, preferred_element_type=F32) + run_ref[...]
    run_ref[...] = run_ref[...] + jnp.sum(sel, axis=1, keepdims=True)
    cnt_ref[...] = run_ref[...]

    ranks = [jnp.sum(jnp.where(sub == ix, before, 0.0), axis=0, keepdims=True) for ix in idxs]
    weights = [e / den for e in exps]
    ti_ref[...] = jnp.concatenate(idxs, axis=0)
    tr_ref[...] = jnp.concatenate(ranks, axis=0).astype(jnp.int32)
    meta = jnp.concatenate(weights + [ix.astype(F32) for ix in idxs]
                           + [jnp.zeros((LANES - 2 * TOP_K, tm), F32)], axis=0)
    h2_ref[:, PACKED:] = meta.T


def _outproj(a_p, a_s, b_p, b_s, x_p, x_s, mod_p, mod_s, norm_g, w_out_b, w_router, b_router):
    tm = ROW_TILE
    npt = x_p.shape[0] // tm
    nst = x_s.shape[0] // tm
    assert nst == 1
    n_tiles = npt + nst
    rows = n_tiles * tm
    tiles_per_batch = npt // mod_p.shape[0]
    last_p = npt - 1
    prow = lambda i: (jnp.minimum(i, last_p), 0)
    srow = lambda i: (0, 0)
    row = lambda i: (i, 0)
    col = lambda i: (0, i)
    const2 = lambda i: (0, 0)
    return pl.pallas_call(
        functools.partial(_outproj_kernel, n_prompt_tiles=npt),
        grid=(n_tiles,),
        in_specs=[pl.BlockSpec((tm, A_WIDTH), prow), pl.BlockSpec((tm, A_WIDTH), srow),
                  pl.BlockSpec((tm, B_WIDTH), prow), pl.BlockSpec((tm, B_WIDTH), srow),
                  pl.BlockSpec((tm, D_MODEL), prow), pl.BlockSpec((tm, D_MODEL), srow),
                  pl.BlockSpec((1, 6, D_MODEL), lambda i: (jnp.minimum(i, last_p) // tiles_per_batch, 0, 0)),
                  pl.BlockSpec((6, tm, D_MODEL), lambda i: (0, 0, 0)),
                  pl.BlockSpec((4, D_MODEL), const2),
                  pl.BlockSpec((D_MODEL, D_MODEL), const2),
                  pl.BlockSpec((D_MODEL, LANES), const2),
                  pl.BlockSpec((1, LANES), const2)],
        out_specs=[pl.BlockSpec((tm, D_MODEL), row), pl.BlockSpec((tm, ROW_WIDTH), row),
                   pl.BlockSpec((TOP_K, tm), col), pl.BlockSpec((TOP_K, tm), col),
                   pl.BlockSpec((N_EXPERTS, 1), const2),
                   pl.BlockSpec((1, N_EXPERTS, 1), lambda i: (i, 0, 0))],
        out_shape=[jax.ShapeDtypeStruct((rows, D_MODEL), F32), jax.ShapeDtypeStruct((rows, ROW_WIDTH), F32),
                   jax.ShapeDtypeStruct((TOP_K, rows), jnp.int32), jax.ShapeDtypeStruct((TOP_K, rows), jnp.int32),
                   jax.ShapeDtypeStruct((N_EXPERTS, 1), F32),
                   jax.ShapeDtypeStruct((n_tiles, N_EXPERTS, 1), F32)],
        scratch_shapes=[pltpu.VMEM((N_EXPERTS, 1), F32)],
        compiler_params=_cparams("arbitrary"),
        name="outproj_router",
    )(a_p, a_s, b_p, b_s, x_p, x_s, mod_p, mod_s, norm_g, w_out_b,
      jnp.pad(w_router, ((0, 0), (0, LANES - N_EXPERTS))).astype(BF16),
      jnp.pad(b_router.reshape(1, -1), ((0, 0), (0, LANES - N_EXPERTS))))


def _dispatch_kernel(pe_ref, cnt_ref, dest_ref, h_ref, xg_ref, zero_ref, sem, zsem, *, bm):
    tm = h_ref.shape[0]
    n_blocks = xg_ref.shape[0] // bm

    def zero_block(row0):
        return pltpu.make_async_copy(zero_ref, xg_ref.at[pl.ds(pl.multiple_of(row0, bm), bm), :], zsem)

    @pl.when(pl.program_id(0) == 0)
    def _():
        zero_ref[...] = jnp.zeros(zero_ref.shape, F32)
        first_unused = pe_ref[N_EXPERTS - 1] // bm
        for e in range(N_EXPERTS):
            @pl.when(cnt_ref[e] > 0)
            def _():
                zero_block(pe_ref[e] - bm).start()

        def start_unused(b, carry):
            zero_block(b * bm).start()
            return carry

        lax.fori_loop(first_unused, n_blocks, start_unused, 0)
        for e in range(N_EXPERTS):
            @pl.when(cnt_ref[e] > 0)
            def _():
                zero_block(pe_ref[e] - bm).wait()

        def wait_unused(b, carry):
            zero_block(b * bm).wait()
            return carry

        lax.fori_loop(first_unused, n_blocks, wait_unused, 0)

    for sub in range(tm // ROW_TILE):
        def body(t, carry, sub=sub):
            for j in range(TOP_K):
                d = dest_ref[(sub * TOP_K + j) * ROW_TILE + t]
                pltpu.make_async_copy(h_ref.at[pl.ds(sub * ROW_TILE + t, 1), :], xg_ref.at[pl.ds(d, 1), :],
                                      sem).start()
            return carry

        lax.fori_loop(0, ROW_TILE, body, 0)
    n = tm * TOP_K
    pltpu.make_async_copy(xg_ref.at[pl.ds(0, n), :], xg_ref.at[pl.ds(0, n), :], sem).wait()


def _dispatch(pad_end, cnt, dest_flat, h2, n_blocks):
    tm = ROW_TILE * DISPATCH_TILES
    bm = EXPERT_BLOCK
    rows = h2.shape[0]
    assert rows % tm == 0
    grid_spec = pltpu.PrefetchScalarGridSpec(
        num_scalar_prefetch=2,
        grid=(rows // tm,),
        in_specs=[pl.BlockSpec((tm * TOP_K,), lambda i, pe, cn: (i,), memory_space=pltpu.SMEM),
                  pl.BlockSpec((tm, ROW_WIDTH), lambda i, pe, cn: (i, 0))],
        out_specs=pl.BlockSpec(memory_space=pl.ANY),
        scratch_shapes=[pltpu.VMEM((bm, ROW_WIDTH), F32), pltpu.SemaphoreType.DMA(()), pltpu.SemaphoreType.DMA(())])
    return pl.pallas_call(
        functools.partial(_dispatch_kernel, bm=bm),
        grid_spec=grid_spec,
        out_shape=jax.ShapeDtypeStruct((n_blocks * bm, ROW_WIDTH), F32),
        compiler_params=_cparams("arbitrary"),
        name="moe_dispatch",
    )(pad_end, cnt, dest_flat, h2)


def _expert_kernel(be_ref, nu_ref, x_ref, wgu_ref, bgu_ref, wd_ref, bd_ref, y_ref, wgu_b, wd_b):
    i = pl.program_id(0)
    used = i < nu_ref[0]
    fresh = jnp.logical_or(i == 0, be_ref[i] != be_ref[jnp.maximum(i - 1, 0)])

    @pl.when(jnp.logical_and(used, fresh))
    def _():
        wgu_b[...] = wgu_ref[0].astype(BF16)
        wd_b[...] = wd_ref[0].astype(BF16)

    @pl.when(jnp.logical_not(used))
    def _():
        y_ref[...] = jnp.zeros(y_ref.shape, y_ref.dtype)

    @pl.when(used)
    def _():
        words = lax.bitcast_convert_type(x_ref[:, 0:PACKED], jnp.uint32)
        low = lax.bitcast_convert_type(lax.shift_left(words, jnp.uint32(16)), F32)
        high = lax.bitcast_convert_type(jnp.bitwise_and(words, jnp.uint32(0xFFFF0000)), F32)
        x = jnp.concatenate([low.astype(BF16), high.astype(BF16)], axis=1)
        gu = jnp.dot(x, wgu_b[...], preferred_element_type=F32) + bgu_ref[0]
        glu = jnp.minimum(gu[:, :D_EXPERT], SWIGLU_LIMIT)
        lin = jnp.clip(gu[:, D_EXPERT:], -SWIGLU_LIMIT, SWIGLU_LIMIT)
        hid = glu * jax.nn.sigmoid(SWIGLU_ALPHA * glu) * (lin + 1.0)
        y = jnp.dot(hid.astype(BF16), wd_b[...], preferred_element_type=F32) + bd_ref[0]
        meta = x_ref[:, PACKED:]
        me = be_ref[i].astype(F32)
        w = jnp.zeros((x_ref.shape[0], 1), F32)
        for j in range(TOP_K):
            w = w + jnp.where(meta[:, TOP_K + j:TOP_K + j + 1] == me, meta[:, j:j + 1], 0.0)
        y_ref[...] = (y * w).astype(y_ref.dtype)


def _experts(block_e, n_used, xg, w_gu_b, b_gu, w_down_b, b_down):
    bm = EXPERT_BLOCK
    n_blocks = xg.shape[0] // bm
    grid_spec = pltpu.PrefetchScalarGridSpec(
        num_scalar_prefetch=2,
        grid=(n_blocks,),
        in_specs=[pl.BlockSpec((bm, ROW_WIDTH), lambda i, be, nu: (jnp.minimum(i, nu[0] - 1), 0)),
                  pl.BlockSpec((1, D_MODEL, 2 * D_EXPERT), lambda i, be, nu: (be[i], 0, 0)),
                  pl.BlockSpec((1, 1, 2 * D_EXPERT), lambda i, be, nu: (be[i], 0, 0)),
                  pl.BlockSpec((1, D_EXPERT, D_MODEL), lambda i, be, nu: (be[i], 0, 0)),
                  pl.BlockSpec((1, 1, D_MODEL), lambda i, be, nu: (be[i], 0, 0))],
        out_specs=pl.BlockSpec((bm, D_MODEL), lambda i, be, nu: (i, 0)),
        scratch_shapes=[pltpu.VMEM((D_MODEL, 2 * D_EXPERT), BF16), pltpu.VMEM((D_EXPERT, D_MODEL), BF16)])
    return pl.pallas_call(
        _expert_kernel,
        grid_spec=grid_spec,
        out_shape=jax.ShapeDtypeStruct((xg.shape[0], D_MODEL), BF16),
        compiler_params=_cparams("arbitrary"),
        name="moe_experts",
    )(block_e, n_used, xg, w_gu_b, b_gu.reshape(N_EXPERTS, 1, -1), w_down_b, b_down.reshape(N_EXPERTS, 1, -1))


def _combine_kernel(row0_ref, col_ref, x1_ref, mp_ref, ms_ref, ng_ref, yg_ref,
                    yp_ref, ys_ref, rows_ref, g_ref, sem, *, n_prompt_tiles):
    i = pl.program_id(0)
    n_steps = pl.num_programs(0)
    tm = x1_ref.shape[0]
    is_p = i < n_prompt_tiles
    win = COMBINE_WINDOW
    n_win_rows = COMBINE_SLOTS * win
    slot = i % 2

    def fetch_windows(tile, buf):
        for s in range(COMBINE_SLOTS):
            row0 = pl.multiple_of(row0_ref[tile * COMBINE_SLOTS + s], COMBINE_ALIGN)
            pltpu.make_async_copy(yg_ref.at[pl.ds(row0, win), :], rows_ref.at[buf, pl.ds(s * win, win), :],
                                  sem.at[buf]).start()

    @pl.when(i == 0)
    def _():
        fetch_windows(0, 0)

    @pl.when(i + 1 < n_steps)
    def _():
        fetch_windows(i + 1, 1 - slot)

    pltpu.make_async_copy(yg_ref.at[pl.ds(0, n_win_rows), :], rows_ref.at[slot], sem.at[slot]).wait()
    col = col_ref[...]
    for c in range(n_win_rows // LANES):
        lane = lax.broadcasted_iota(jnp.int32, (tm, LANES), 1) + c * LANES
        g = jnp.zeros((tm, LANES), F32)
        for j in range(TOP_K):
            g = jnp.where(lane == col[:, j:j + 1], 1.0, g)
        g_ref[:, c * LANES:(c + 1) * LANES] = g.astype(BF16)
    f = jnp.dot(g_ref[...], rows_ref[slot], preferred_element_type=F32)

    g2 = jnp.where(is_p, mp_ref[0, 5:6, :], ms_ref[5])
    y = x1_ref[...] + g2 * _rms(f, ng_ref[3:4, :])

    @pl.when(is_p)
    def _():
        yp_ref[...] = y

    @pl.when(jnp.logical_not(is_p))
    def _():
        ys_ref[...] = y


def _combine(row0, col, x1, mod_p, mod_s, norm_g, yg, *, n_prompt_rows, n_sample_rows):
    tm = ROW_TILE
    npt = n_prompt_rows // tm
    assert n_sample_rows == tm and (COMBINE_SLOTS * COMBINE_WINDOW) % LANES == 0
    n_tiles = npt + 1
    tiles_per_batch = npt // mod_p.shape[0]
    last_p = npt - 1
    grid_spec = pltpu.PrefetchScalarGridSpec(
        num_scalar_prefetch=1,
        grid=(n_tiles,),
        in_specs=[pl.BlockSpec((tm, TOP_K), lambda i, r0: (i, 0)),
                  pl.BlockSpec((tm, D_MODEL), lambda i, r0: (i, 0)),
                  pl.BlockSpec((1, 6, D_MODEL), lambda i, r0: (jnp.minimum(i, last_p) // tiles_per_batch, 0, 0)),
                  pl.BlockSpec((6, tm, D_MODEL), lambda i, r0: (0, 0, 0)),
                  pl.BlockSpec((4, D_MODEL), lambda i, r0: (0, 0)),
                  pl.BlockSpec(memory_space=pl.ANY)],
        out_specs=[pl.BlockSpec((tm, D_MODEL), lambda i, r0: (jnp.minimum(i, last_p), 0)),
                   pl.BlockSpec((tm, D_MODEL), lambda i, r0: (0, 0))],
        scratch_shapes=[pltpu.VMEM((2, COMBINE_SLOTS * COMBINE_WINDOW, D_MODEL), BF16),
                        pltpu.VMEM((tm, COMBINE_SLOTS * COMBINE_WINDOW), BF16),
                        pltpu.SemaphoreType.DMA((2,))])
    return pl.pallas_call(
        functools.partial(_combine_kernel, n_prompt_tiles=npt),
        grid_spec=grid_spec,
        out_shape=[jax.ShapeDtypeStruct((n_prompt_rows, D_MODEL), F32),
                   jax.ShapeDtypeStruct((n_sample_rows, D_MODEL), F32)],
        compiler_params=_cparams("arbitrary"),
        name="moe_combine",
    )(row0, col, x1, mod_p, mod_s, norm_g, yg)


def _t5_bucket_np(dist):
    n = np.maximum(dist, 0)
    max_exact = NUM_BUCKETS // 2
    nf = np.maximum(n, 1).astype(np.float64)
    large = max_exact + (np.log(nf / max_exact) / math.log(MAX_DISTANCE / max_exact)
                         * (NUM_BUCKETS - max_exact)).astype(np.int32)
    large = np.minimum(large, NUM_BUCKETS - 1)
    return np.where(n < max_exact, n, large).astype(np.int32)


def _bias_from_dist(rel_bias, dist):
    onehot = jax.nn.one_hot(_t5_bucket_np(dist).reshape(-1), NUM_BUCKETS, dtype=F32)
    shifted = (rel_bias - rel_bias[NUM_BUCKETS - 1]).reshape(NUM_BUCKETS, -1)
    out = jnp.dot(onehot, shifted, precision=lax.Precision.HIGHEST).reshape(dist.shape + rel_bias.shape[1:])
    return jnp.where(jnp.asarray(dist >= 0)[..., None, None], out, -jnp.inf).astype(F32)


def _head_masked(t):
    h = t.shape[0]
    same = np.eye(h, dtype=bool)[:, None, None, None, :]
    full = jnp.where(same, t[..., None], -jnp.inf)
    return full.reshape(h * t.shape[1] * t.shape[2], t.shape[3] * h)


def kernel(x_prompt, x_sample, cache_k, cache_v, page_table, c_prompt, c_sample, w_ada, b_ada, norm_g, w_in,
           w_out, ln_v_g, ln_v_b, w_spatial, b_spatial, lam_params, subln_g, rel_bias, w_router, b_router,
           w_gu, b_gu, w_down, b_down):
    batch, seq, d = x_prompt.shape
    nb, ds = x_sample.shape[:2]
    n_pages = page_table.shape[1]
    past = n_pages * PAGE_SIZE
    rows_p = batch * seq
    rows_s = nb * ds
    assert rows_s == ROW_TILE and d == D_MODEL

    mod = _ada(jnp.concatenate([c_prompt, c_sample], axis=0), w_ada[0], b_ada[0])
    mod_p = mod[:batch].reshape(batch, 6, D_MODEL)
    mod_s = jnp.transpose(jnp.repeat(mod[batch:], ds, axis=0).reshape(rows_s, 6, D_MODEL), (1, 0, 2))

    ng = norm_g[0]
    w_in_b = w_in[0].astype(BF16)
    w_out_b = w_out[0].astype(BF16)
    lng = ln_v_g[0].reshape(1, A_WIDTH)
    lnb = ln_v_b[0].reshape(1, A_WIDTH)
    grp = np.arange(A_WIDTH) // A_HEAD
    avg = jnp.asarray((grp[:, None] == grp[None, :]).astype(np.float32) / A_HEAD, BF16)
    ws_p = jnp.tril(w_spatial[0]).astype(BF16)
    bs_p = jnp.repeat(b_spatial[0].T, A_HEAD, axis=1)
    w_small = jnp.tril(w_spatial[0][:, :ds, :ds])
    same_batch = np.kron(np.eye(nb, dtype=np.float32), np.ones((ds, ds), np.float32))
    spread = np.tile(np.eye(ds, dtype=np.float32), (nb, 1))
    ws_s = (jnp.einsum('rt,gts,cs->grc', spread, w_small, spread, precision=lax.Precision.HIGHEST)
            * same_batch).astype(BF16)
    bs_s = jnp.tile(jnp.repeat(b_spatial[0][:, :ds].T, A_HEAD, axis=1), (nb, 1))

    xp2 = x_prompt.reshape(rows_p, D_MODEL)
    xs2 = x_sample.reshape(rows_s, D_MODEL)
    assert ATTN_TILE % ROW_TILE == 0 and seq % ATTN_TILE == 0
    a_p, va_p, q_p, k_p, v_p, kb_p, vt_p = _inproj(
        xp2, mod_p[:, 0:1], mod_p[:, 1:2], ng[0:1], w_in_b, lng, lnb, avg, ws_p, bs_p,
        tiles_per_mod=seq // ROW_TILE, chunk=CHUNK, q_dtype=BF16, q_scale=LOG2E / math.sqrt(B_HEAD), va_rows=CHUNK)
    a_s, va_s, q_s, k_s, v_s, _, _ = _inproj(
        xs2, mod_s[0:1], mod_s[1:2], ng[0:1], w_in_b, lng, lnb, avg, ws_s, bs_s,
        tiles_per_mod=1, chunk=rows_s, q_dtype=F32, q_scale=1.0, va_rows=rows_s)

    ti = np.arange(ATTN_TILE)
    dist_diag = ti[:, None] - ti[None, :]
    dist_prev = ti[:, None] - ti[None, ATTN_TILE - MAX_DISTANCE:] + ATTN_TILE
    assert dist_prev.min() >= 1 and (ti[:, None] - ti[None, :ATTN_TILE - MAX_DISTANCE] + ATTN_TILE).min() >= MAX_DISTANCE
    to_cols = lambda t: jnp.transpose(t, (2, 1, 3, 0)).reshape(B_HEADS, t.shape[1], 2 * ATTN_TILE) * LOG2E
    b_p = _attn_p(q_p, kb_p, vt_p, to_cols(_bias_from_dist(rel_bias, dist_diag)),
                  to_cols(_bias_from_dist(rel_bias, dist_prev)), lam_params[0], subln_g, batch=batch, seq=seq)

    qi = np.arange(ds)
    ki = np.arange(PAGE_SIZE)
    dist_l = PAGE_SIZE + qi[:, None] - ki[None, :]
    kn_i = np.arange(PAGE_SIZE // B_HEADS)
    dist_n = np.where(kn_i[None, :] < ds, qi[:, None] - kn_i[None, :], -1)
    to_rows = lambda t: jnp.transpose(t, (2, 3, 0, 1))
    bias_l = _head_masked(to_rows(_bias_from_dist(rel_bias, dist_l)))
    bias_n = _head_masked(to_rows(_bias_from_dist(rel_bias, dist_n)))
    maskb = _head_masked(jnp.zeros((B_HEADS, 2, ds, PAGE_SIZE), F32))
    pad_keys = lambda t: jnp.pad(t.reshape(nb, ds * B_HEADS, B_VDIM), ((0, 0), (0, PAGE_SIZE - ds * B_HEADS), (0, 0)))
    n_phys = cache_k.shape[0]
    b_s = _attn_s(page_table, q_s, pad_keys(k_s), pad_keys(v_s), maskb, bias_l, bias_n, lam_params[0], subln_g,
                  cache_k.reshape(n_phys * PAGE_SIZE * B_HEADS, B_VDIM),
                  cache_v.reshape(n_phys * PAGE_SIZE * B_HEADS, B_VDIM), ds=ds)

    x1, h2, top_i, top_r, counts, tile_base = _outproj(
        a_p, a_s, b_p, b_s, xp2, xs2, mod_p, mod_s, ng, w_out_b, w_router[0], b_router[0])

    bm = EXPERT_BLOCK
    rows = rows_p + rows_s
    n_blocks = rows * TOP_K // bm + N_EXPERTS + 1
    cnt = counts[:, 0].astype(jnp.int32)
    padded = (cnt + bm - 1) // bm * bm
    pad_end = jnp.cumsum(padded)
    pad_start = pad_end - padded
    experts = jnp.arange(N_EXPERTS, dtype=jnp.int32)
    chosen = top_i[..., None] == experts
    dest = jnp.sum(jnp.where(chosen, pad_start, 0), axis=-1) + top_r
    n_tiles = rows // ROW_TILE
    win = COMBINE_WINDOW
    assert win % COMBINE_ALIGN == 0
    assert (ROW_TILE * TOP_K + N_EXPERTS * (COMBINE_ALIGN - 1)) // win + N_EXPERTS <= COMBINE_SLOTS
    base = tile_base[:, :, 0].astype(jnp.int32)
    first = pad_start[None, :] + base
    win0 = first // COMBINE_ALIGN * COMBINE_ALIGN
    sent = jnp.concatenate([base[1:], cnt[None, :]], axis=0) - base
    n_win = jnp.where(sent > 0, (first - win0 + sent + win - 1) // win, 0)
    slot_end = jnp.cumsum(n_win, axis=1)
    slot_start = slot_end - n_win
    slots = jnp.arange(COMBINE_SLOTS, dtype=jnp.int32)
    slot_e = jnp.minimum(jnp.sum((slot_end[:, None, :] <= slots[None, :, None]).astype(jnp.int32), axis=2),
                         N_EXPERTS - 1)
    pick = slot_e[..., None] == experts
    slot_row0 = (jnp.sum(jnp.where(pick, win0[:, None, :], 0), axis=2)
                 + win * (slots[None, :] - jnp.sum(jnp.where(pick, slot_start[:, None, :], 0), axis=2)))
    slot_row0 = jnp.where(slots[None, :] < slot_end[:, -1:], slot_row0, 0)
    local = dest - jnp.sum(jnp.where(chosen, jnp.repeat(win0, ROW_TILE, axis=0), 0), axis=-1)
    col = jnp.sum(jnp.where(chosen, jnp.repeat(slot_start, ROW_TILE, axis=0), 0), axis=-1) * win + local
    dest = jnp.transpose(dest.reshape(TOP_K, n_tiles, ROW_TILE), (1, 0, 2)).reshape(-1)
    blk_start = jnp.arange(n_blocks, dtype=jnp.int32) * bm
    block_e = jnp.minimum(jnp.sum((pad_end[None, :] <= blk_start[:, None]).astype(jnp.int32), axis=1),
                          N_EXPERTS - 1)
    n_used = (pad_end[-1:] // bm).astype(jnp.int32)

    xg = _dispatch(pad_end.astype(jnp.int32), cnt, dest, h2, n_blocks)
    yg = _experts(block_e, n_used, xg, w_gu[0], b_gu[0], w_down[0], b_down[0])
    y_p, y_s = _combine(slot_row0.reshape(-1), col.T, x1, mod_p, mod_s, ng, yg,
                        n_prompt_rows=rows_p, n_sample_rows=rows_s)

    return (y_p.reshape(batch, seq, D_MODEL),
            y_s.reshape(nb, ds, D_MODEL),
            k_p.reshape(batch, seq, 1, B_HEADS, B_VDIM),
            v_p.reshape(batch, seq, 1, B_HEADS, B_VDIM),
            k_s.reshape(nb, ds, 1, B_HEADS, B_VDIM),
            v_s.reshape(nb, ds, 1, B_HEADS, B_VDIM),
            va_p.reshape(batch, CHUNK, 1, A_WIDTH),
            va_s.reshape(nb, ds, 1, A_WIDTH))
```

```python
import functools
import math

import numpy as np
import jax
import jax.numpy as jnp
from jax import lax
from jax.experimental import pallas as pl
from jax.experimental.pallas import tpu as pltpu

F32 = jnp.float32
BF16 = jnp.bfloat16

D_MODEL = 1024
A_WIDTH = 512
A_HEAD = 64
A_GROUPS = 8
CHUNK = 128
B_WIDTH = 512
B_HEAD = 64
B_VDIM = 128
B_HEADS = 4
IN_WIDTH = 2 * A_WIDTH + 3 * B_WIDTH
NUM_BUCKETS = 32
MAX_DISTANCE = 128
PAGE_SIZE = 128
N_EXPERTS = 32
TOP_K = 4
D_EXPERT = 1024
SWIGLU_LIMIT = 7.0
SWIGLU_ALPHA = 1.702
NORM_EPS = 1e-6
LAM_INIT = 0.8 - 0.6 * math.exp(-0.3 * 0)

ROW_TILE = 256
ATTN_TILE = 512
ATTN_BATCHES = 1
PAGES_PER_STEP = 32
LOG2E = math.log2(math.e)
LANES = 128
MXU_WIDTH = 256
PACKED = D_MODEL // 2
ROW_WIDTH = PACKED + LANES
EXPERT_BLOCK = 512
DISPATCH_TILES = 5
COMBINE_ALIGN = 16
COMBINE_WINDOW = 16
COMBINE_SLOTS = 128
VMEM_LIMIT = 56 * 1024 * 1024


def _cparams(*sem):
    return pltpu.CompilerParams(dimension_semantics=sem, vmem_limit_bytes=VMEM_LIMIT)


def _rms(x, g):
    return x * lax.rsqrt(jnp.mean(x * x, axis=-1, keepdims=True) + NORM_EPS) * g


def _ada_kernel(c_ref, w_ref, b_ref, o_ref):
    s = jax.nn.silu(c_ref[...]).astype(BF16)
    o_ref[...] = jnp.dot(s, w_ref[...].astype(BF16), preferred_element_type=F32) + b_ref[...]


def _ada(c_all, w_ada, b_ada):
    n = c_all.shape[0]
    tn = 1024
    return pl.pallas_call(
        _ada_kernel,
        grid=(6 * D_MODEL // tn,),
        in_specs=[pl.BlockSpec((n, D_MODEL), lambda j: (0, 0)),
                  pl.BlockSpec((D_MODEL, tn), lambda j: (0, j)),
                  pl.BlockSpec((1, tn), lambda j: (0, j))],
        out_specs=pl.BlockSpec((n, tn), lambda j: (0, j)),
        out_shape=jax.ShapeDtypeStruct((n, 6 * D_MODEL), F32),
        compiler_params=_cparams("arbitrary"),
        name="ada",
    )(c_all, w_ada, b_ada.reshape(1, -1))


def _inproj_kernel(x_ref, sh_ref, sc_ref, ng_ref, w_ref, lng_ref, lnb_ref, avg_ref, ws_ref, bs_ref,
                   a_ref, va_ref, q_ref, k_ref, v_ref, kb_ref, vt_ref, *, chunk, q_scale):
    tm = x_ref.shape[0]
    va_rows = va_ref.shape[0]
    h = (_rms(x_ref[...], ng_ref[...]) * (1.0 + sc_ref[0]) + sh_ref[0]).astype(BF16)

    u = jax.nn.gelu(jnp.dot(h, w_ref[:, 0:A_WIDTH], preferred_element_type=F32))
    gv = jax.nn.gelu(jnp.dot(h, w_ref[:, A_WIDTH:2 * A_WIDTH], preferred_element_type=F32))

    avg = avg_ref[...]

    def group_mean(t):
        return jnp.dot(t.astype(BF16), avg, preferred_element_type=F32)

    xc = gv - group_mean(gv)
    va = xc * lax.rsqrt(group_mean(xc * xc) + NORM_EPS) * lng_ref[...] + lnb_ref[...]
    va_ref[...] = va[tm - va_rows:, :]

    vab = va.astype(BF16)
    lane = lax.broadcasted_iota(jnp.int32, (chunk, 128), 1)
    for c in range(tm // chunk):
        r0 = c * chunk
        for pair in range(A_GROUPS // 2):
            c0 = pair * 128
            vp = vab[r0:r0 + chunk, c0:c0 + 128]
            lo_half = jnp.where(lane < A_HEAD, vp, jnp.zeros_like(vp))
            hi_half = jnp.where(lane >= A_HEAD, vp, jnp.zeros_like(vp))
            s = (jnp.dot(ws_ref[2 * pair], lo_half, preferred_element_type=F32)
                 + jnp.dot(ws_ref[2 * pair + 1], hi_half, preferred_element_type=F32))
            a = u[r0:r0 + chunk, c0:c0 + 128] * (s + bs_ref[:, c0:c0 + 128])
            a_ref[r0:r0 + chunk, c0:c0 + 128] = a.astype(a_ref.dtype)

    q = jnp.dot(h, w_ref[:, 2 * A_WIDTH:2 * A_WIDTH + B_WIDTH], preferred_element_type=F32)
    q_ref[...] = (q * q_scale).astype(q_ref.dtype)
    k = jnp.dot(h, w_ref[:, 2 * A_WIDTH + B_WIDTH:2 * A_WIDTH + 2 * B_WIDTH], preferred_element_type=F32)
    kb_ref[...] = k.astype(BF16)
    v = jnp.dot(h, w_ref[:, 2 * A_WIDTH + 2 * B_WIDTH:IN_WIDTH], preferred_element_type=F32)
    for hh in range(B_HEADS):
        c0 = hh * B_VDIM
        k_ref[pl.ds(hh, tm, stride=B_HEADS), :] = k[:, c0:c0 + B_VDIM]
        v_ref[pl.ds(hh, tm, stride=B_HEADS), :] = v[:, c0:c0 + B_VDIM]
        vt_ref[0, hh, 0] = v[:, c0:c0 + B_VDIM].T.astype(BF16)


def _inproj(x2, sh, sc, ng, w_in_b, lng, lnb, avg, ws, bs, *, tiles_per_mod, chunk, q_dtype, q_scale, va_rows):
    rows = x2.shape[0]
    tm = ROW_TILE
    n_tiles = rows // tm
    mod_rows = sh.shape[1]
    row = lambda i: (i, 0)
    const2 = lambda i: (0, 0)
    mod_map = lambda i: (i // tiles_per_mod, 0, 0)
    out_w = lambda w, dt: jax.ShapeDtypeStruct((rows, w), dt)
    head_rows = jax.ShapeDtypeStruct((rows * B_HEADS, B_VDIM), F32)
    head_rows_spec = pl.BlockSpec((tm * B_HEADS, B_VDIM), row)
    vt_shape = jax.ShapeDtypeStruct((n_tiles // tiles_per_mod, B_HEADS, tiles_per_mod, B_VDIM, tm), BF16)
    vt_spec = pl.BlockSpec((1, B_HEADS, 1, B_VDIM, tm), lambda i: (i // tiles_per_mod, 0, i % tiles_per_mod, 0, 0))
    return pl.pallas_call(
        functools.partial(_inproj_kernel, chunk=chunk, q_scale=q_scale),
        grid=(rows // tm,),
        in_specs=[pl.BlockSpec((tm, D_MODEL), row),
                  pl.BlockSpec((1, mod_rows, D_MODEL), mod_map),
                  pl.BlockSpec((1, mod_rows, D_MODEL), mod_map),
                  pl.BlockSpec((1, D_MODEL), const2),
                  pl.BlockSpec((D_MODEL, IN_WIDTH), const2),
                  pl.BlockSpec((1, A_WIDTH), const2),
                  pl.BlockSpec((1, A_WIDTH), const2),
                  pl.BlockSpec((A_WIDTH, A_WIDTH), const2),
                  pl.BlockSpec((A_GROUPS, chunk, chunk), lambda i: (0, 0, 0)),
                  pl.BlockSpec((chunk, A_WIDTH), const2)],
        out_specs=[pl.BlockSpec((tm, A_WIDTH), row),
                   pl.BlockSpec((va_rows, A_WIDTH), lambda i: (i // tiles_per_mod, 0)),
                   pl.BlockSpec((tm, B_WIDTH), row), head_rows_spec, head_rows_spec,
                   pl.BlockSpec((tm, B_WIDTH), row), vt_spec],
        out_shape=[out_w(A_WIDTH, BF16),
                   jax.ShapeDtypeStruct((n_tiles // tiles_per_mod * va_rows, A_WIDTH), F32),
                   out_w(B_WIDTH, q_dtype), head_rows, head_rows, out_w(B_WIDTH, BF16), vt_shape],
        compiler_params=_cparams("arbitrary"),
        name="inproj",
    )(x2, sh, sc, ng, w_in_b, lng, lnb, avg, ws, bs)


def _stack_halves(q):
    lane = lax.broadcasted_iota(jnp.int32, q.shape, 1)
    zero = jnp.zeros_like(q)
    return jnp.concatenate([jnp.where(lane < B_HEAD, q, zero), jnp.where(lane >= B_HEAD, q, zero)], axis=0)


def _lambda(lam_ref):
    lp = lam_ref[...]
    return (jnp.exp(jnp.sum(lp[0:1] * lp[1:2], axis=-1, keepdims=True))
            - jnp.exp(jnp.sum(lp[2:3] * lp[3:4], axis=-1, keepdims=True)) + LAM_INIT)


def _diff_finish(acc, l, n, lam, g):
    o = acc[:n] / l[:n] - lam * (acc[n:] / l[n:])
    return _rms(o, g) * (1.0 - LAM_INIT)


def _attn_p_kernel(q_ref, k_ref, vt_ref, diag_ref, prev_ref, lam_ref, g_ref, o_ref,
                   acc_ref, m_ref, l_ref, s_ref, tmax_ref, p_ref, alpha_ref, *, tq, tk):
    qi = pl.program_id(2)
    nbb = q_ref.shape[0]
    streams = [(bb, c) for bb in range(nbb) for c in range(2)]
    q_half = []
    for bb in range(nbb):
        q = q_ref[bb]
        lane = lax.broadcasted_iota(jnp.int32, q.shape, 1)
        zero = jnp.zeros_like(q)
        q_half += [jnp.where(lane < B_HEAD, q, zero), jnp.where(lane >= B_HEAD, q, zero)]
    m_ref[...] = jnp.full(m_ref.shape, -jnp.inf, F32)
    l_ref[...] = jnp.zeros(l_ref.shape, F32)
    acc_ref[...] = jnp.zeros(acc_ref.shape, F32)
    p_ref[1] = jnp.zeros(p_ref.shape[1:], BF16)
    alpha_ref[1] = jnp.ones(alpha_ref.shape[1:], F32)

    def scores(j):
        out = []
        for bb in range(nbb):
            k = k_ref[bb, pl.ds(pl.multiple_of(j * tk, tk), tk), :]
            out += [lax.dot_general(k, q_half[2 * bb + c], (((1,), (1,)), ((), ())), preferred_element_type=F32)
                    for c in range(2)]
        return out

    near_keys = prev_ref.shape[1]

    def keep_scores(j, kind, ss):
        slot = j % 2
        for u, (bb, c) in enumerate(streams):
            s = ss[u]
            if kind == 0:
                s = s + diag_ref[0, :, c * tq:(c + 1) * tq]
            elif kind == 1:
                s = jnp.concatenate([s[:tk - near_keys], s[tk - near_keys:] + prev_ref[0, :, c * tq:(c + 1) * tq]],
                                    axis=0)
            s_ref[slot, u] = s
            tmax_ref[slot, u] = jnp.max(s, axis=0, keepdims=True)

    def softmax(j):
        slot = j % 2
        for u in range(len(streams)):
            m_old = m_ref[u]
            m_new = jnp.maximum(m_old, tmax_ref[slot, u])
            alpha = jnp.exp2(m_old - m_new)
            p = jnp.exp2(s_ref[slot, u] - m_new)
            l_ref[u] = alpha * l_ref[u] + jnp.sum(p, axis=0, keepdims=True)
            m_ref[u] = m_new
            p_ref[slot, u] = p.astype(BF16)
            alpha_ref[slot, u] = alpha

    def pv(j):
        slot = j % 2
        pieces = tk // vt_ref.shape[-1]
        first = jnp.maximum(j, 0) * pieces
        for u, (bb, c) in enumerate(streams):
            add = None
            for r in range(pieces):
                w = vt_ref.shape[-1]
                t = jnp.dot(vt_ref[bb, 0, first + r], p_ref[slot, u, r * w:(r + 1) * w, :],
                            preferred_element_type=F32)
                add = t if add is None else add + t
            acc_ref[u] = alpha_ref[slot, u] * acc_ref[u] + add

    def iteration(j, next_kind):
        ss = scores(j + 1)
        pv(j - 1)
        softmax(j)
        keep_scores(j + 1, next_kind, ss)

    def far_body(j, carry):
        iteration(j, None)
        return carry

    for kind, cond in ((0, qi == 0), (1, qi == 1), (None, qi >= 2)):
        @pl.when(cond)
        def _(kind=kind):
            keep_scores(0, kind, scores(0))

    lax.fori_loop(0, jnp.maximum(qi - 2, 0), far_body, 0)

    @pl.when(qi >= 2)
    def _():
        iteration(qi - 2, 1)

    @pl.when(qi >= 1)
    def _():
        iteration(qi - 1, 0)

    pv(qi - 1)
    softmax(qi)
    pv(qi)

    lam = _lambda(lam_ref)
    for bb in range(nbb):
        o = acc_ref[2 * bb] / l_ref[2 * bb] - lam * (acc_ref[2 * bb + 1] / l_ref[2 * bb + 1])
        y = o * lax.rsqrt(jnp.mean(o * o, axis=0, keepdims=True) + NORM_EPS) * (1.0 - LAM_INIT)
        o_ref[bb] = (y.T * g_ref[...]).astype(o_ref.dtype)


def _attn_p(qb, kb, vt, bias_diag, bias_prev, lam_params, subln_g, *, batch, seq):
    tq = tk = ATTN_TILE
    nq = seq // tq
    nbb = ATTN_BATCHES
    ns = 2 * nbb
    assert batch % nbb == 0
    out = pl.pallas_call(
        functools.partial(_attn_p_kernel, tq=tq, tk=tk),
        grid=(batch // nbb, B_HEADS, nq),
        in_specs=[pl.BlockSpec((nbb, tq, B_VDIM), lambda b, h, i: (b, i, h)),
                  pl.BlockSpec((nbb, seq, B_VDIM), lambda b, h, i: (b, 0, h)),
                  pl.BlockSpec((nbb, 1, seq // ROW_TILE, B_VDIM, ROW_TILE), lambda b, h, i: (b, h, 0, 0, 0)),
                  pl.BlockSpec((1, tk, 2 * tq), lambda b, h, i: (h, 0, 0)),
                  pl.BlockSpec((1, bias_prev.shape[1], 2 * tq), lambda b, h, i: (h, 0, 0)),
                  pl.BlockSpec((4, B_HEAD), lambda b, h, i: (0, 0)),
                  pl.BlockSpec((1, B_VDIM), lambda b, h, i: (0, 0))],
        out_specs=pl.BlockSpec((nbb, tq, B_VDIM), lambda b, h, i: (b, i, h)),
        out_shape=jax.ShapeDtypeStruct((batch, seq, B_WIDTH), BF16),
        scratch_shapes=[pltpu.VMEM((ns, B_VDIM, tq), F32),
                        pltpu.VMEM((ns, 1, tq), F32),
                        pltpu.VMEM((ns, 1, tq), F32),
                        pltpu.VMEM((2, ns, tk, tq), F32),
                        pltpu.VMEM((2, ns, 1, tq), F32),
                        pltpu.VMEM((2, ns, tk, tq), BF16),
                        pltpu.VMEM((2, ns, 1, tq), F32)],
        compiler_params=_cparams("arbitrary", "arbitrary", "arbitrary"),
        name="attn_prompt",
    )(qb.reshape(batch, seq, B_WIDTH), kb.reshape(batch, seq, B_WIDTH), vt, bias_diag, bias_prev, lam_params,
      subln_g)
    return out.reshape(batch * seq, B_WIDTH)


def _attn_s_kernel(pt_ref, q_ref, kn_ref, vn_ref, maskb_ref, biasl_ref, biasn_ref, lam_ref, g_ref, *rest,
                   npages, ds):
    del pt_ref
    k_refs = rest[:npages]
    v_refs = rest[npages:2 * npages]
    o_ref = rest[2 * npages]
    acc_ref, m_ref, l_ref = rest[2 * npages + 1:]
    g = pl.program_id(1)
    last = g == pl.num_programs(1) - 1

    @pl.when(g == 0)
    def _():
        m_ref[...] = jnp.full(m_ref.shape, -jnp.inf, F32)
        l_ref[...] = jnp.zeros(l_ref.shape, F32)
        acc_ref[...] = jnp.zeros(acc_ref.shape, F32)

    q = q_ref[...] * (1.0 / math.sqrt(B_HEAD))
    qall = jnp.concatenate([_stack_halves(q[:, h * B_VDIM:(h + 1) * B_VDIM]) for h in range(B_HEADS)],
                           axis=0).astype(BF16)

    def process(k_blocks, v_blocks, biases):
        s = jnp.concatenate(
            [lax.dot_general(qall, kb.astype(BF16), (((1,), (1,)), ((), ())), preferred_element_type=F32) + bb
             for kb, bb in zip(k_blocks, biases)], axis=1)
        m_old = m_ref[...]
        m_new = jnp.maximum(m_old, jnp.max(s, axis=-1, keepdims=True))
        alpha = jnp.exp(m_old - m_new)
        pr = jnp.exp(s - m_new)
        l_ref[...] = alpha * l_ref[...] + jnp.sum(pr, axis=-1, keepdims=True)
        prb = pr.astype(BF16)
        pv = None
        off = 0
        for vb in v_blocks:
            n = vb.shape[0]
            t = jnp.dot(prb[:, off:off + n], vb.astype(BF16), preferred_element_type=F32)
            pv = t if pv is None else pv + t
            off += n
        acc_ref[...] = alpha * acc_ref[...] + pv
        m_ref[...] = m_new

    maskb = maskb_ref[...]
    newest = jnp.where(last, biasl_ref[...], maskb)
    process([k_refs[p][...] for p in range(npages)], [v_refs[p][...] for p in range(npages)],
            [maskb] * (npages - 1) + [newest])

    @pl.when(last)
    def _():
        process([kn_ref[0]], [vn_ref[0]], [biasn_ref[...]])
        lam = _lambda(lam_ref)
        acc = acc_ref[...]
        l = l_ref[...]
        for h in range(B_HEADS):
            r0 = h * 2 * ds
            o_ref[:, h * B_VDIM:(h + 1) * B_VDIM] = _diff_finish(
                acc[r0:r0 + 2 * ds], l[r0:r0 + 2 * ds], ds, lam, g_ref[...])


def _attn_s(page_table, qs, kn, vn, maskb, bias_last, bias_new, lam_params, subln_g, cache_k2, cache_v2, *, ds):
    nb, n_pages = page_table.shape
    npg = PAGES_PER_STEP
    steps = n_pages // npg
    page_rows = PAGE_SIZE * B_HEADS
    nrow = B_HEADS * 2 * ds

    def page_spec(p):
        return pl.BlockSpec((page_rows, B_VDIM), lambda b, g, pt, p=p: (pt[b, g * npg + p], 0))

    const2 = lambda b, g, pt: (0, 0)
    grid_spec = pltpu.PrefetchScalarGridSpec(
        num_scalar_prefetch=1,
        grid=(nb, steps),
        in_specs=[pl.BlockSpec((ds, B_WIDTH), lambda b, g, pt: (b, 0)),
                  pl.BlockSpec((1, PAGE_SIZE, B_VDIM), lambda b, g, pt: (b, 0, 0)),
                  pl.BlockSpec((1, PAGE_SIZE, B_VDIM), lambda b, g, pt: (b, 0, 0)),
                  pl.BlockSpec((nrow, page_rows), const2),
                  pl.BlockSpec((nrow, page_rows), const2),
                  pl.BlockSpec((nrow, PAGE_SIZE), const2),
                  pl.BlockSpec((4, B_HEAD), const2),
                  pl.BlockSpec((1, B_VDIM), const2)]
                 + [page_spec(p) for p in range(npg)] + [page_spec(p) for p in range(npg)],
        out_specs=pl.BlockSpec((ds, B_WIDTH), lambda b, g, pt: (b, 0)),
        scratch_shapes=[pltpu.VMEM((nrow, B_VDIM), F32),
                        pltpu.VMEM((nrow, 1), F32),
                        pltpu.VMEM((nrow, 1), F32)])
    return pl.pallas_call(
        functools.partial(_attn_s_kernel, npages=npg, ds=ds),
        grid_spec=grid_spec,
        out_shape=jax.ShapeDtypeStruct((nb * ds, B_WIDTH), F32),
        compiler_params=_cparams("arbitrary", "arbitrary"),
        name="attn_sample",
    )(page_table, qs, kn, vn, maskb, bias_last, bias_new, lam_params, subln_g,
      *([cache_k2] * npg), *([cache_v2] * npg))


def _outproj_kernel(ap_ref, as_ref, bp_ref, bs_ref, xp_ref, xs_ref, mp_ref, ms_ref, ng_ref, wo_ref,
                    wr_ref, br_ref, x1_ref, h2_ref, ti_ref, tr_ref, cnt_ref, base_ref, run_ref,
                    *, n_prompt_tiles):
    i = pl.program_id(0)
    tm = xp_ref.shape[0]
    is_p = i < n_prompt_tiles

    @pl.when(i == 0)
    def _():
        run_ref[...] = jnp.zeros(run_ref.shape, F32)

    a = jnp.where(is_p, ap_ref[...], as_ref[...])
    b = jnp.where(is_p, bp_ref[...], bs_ref[...].astype(BF16))
    x = jnp.where(is_p, xp_ref[...], xs_ref[...])
    g1 = jnp.where(is_p, mp_ref[0, 2:3, :], ms_ref[2])
    sh2 = jnp.where(is_p, mp_ref[0, 3:4, :], ms_ref[3])
    sc2 = jnp.where(is_p, mp_ref[0, 4:5, :], ms_ref[4])

    mix = (jnp.dot(a, wo_ref[0:A_WIDTH, :], preferred_element_type=F32)
           + jnp.dot(b, wo_ref[A_WIDTH:, :], preferred_element_type=F32))
    x1 = x + g1 * _rms(mix, ng_ref[1:2, :])
    x1_ref[...] = x1
    h2 = _rms(x1, ng_ref[2:3, :]) * (1.0 + sc2) + sh2
    low = lax.bitcast_convert_type(h2[:, 0:PACKED].astype(BF16).astype(F32), jnp.uint32)
    high = lax.bitcast_convert_type(h2[:, PACKED:].astype(BF16).astype(F32), jnp.uint32)
    words = jnp.bitwise_or(jnp.bitwise_and(high, jnp.uint32(0xFFFF0000)), lax.shift_right_logical(low, jnp.uint32(16)))
    h2_ref[:, 0:PACKED] = lax.bitcast_convert_type(words, F32)

    logits = jnp.dot(h2.astype(BF16), wr_ref[...], preferred_element_type=F32) + br_ref[...]
    work = logits.T[0:N_EXPERTS, :]
    sub = lax.broadcasted_iota(jnp.int32, work.shape, 0)
    vals, idxs = [], []
    for _ in range(TOP_K):
        mx = jnp.max(work, axis=0, keepdims=True)
        ix = jnp.min(jnp.where(work == mx, sub, N_EXPERTS), axis=0, keepdims=True)
        vals.append(mx)
        idxs.append(ix)
        work = jnp.where(sub == ix, -jnp.inf, work)
    exps = [jnp.exp(v - vals[0]) for v in vals]
    den = exps[0] + exps[1] + exps[2] + exps[3]

    sel = jnp.where(work == -jnp.inf, 1.0, 0.0)
    r_i = lax.broadcasted_iota(jnp.int32, (tm, tm), 0)
    c_i = lax.broadcasted_iota(jnp.int32, (tm, tm), 1)
    earlier = jnp.where(r_i < c_i, 1.0, 0.0).astype(BF16)
    base_ref[0] = run_ref[...]
    before = jnp.dot(sel.astype(BF16), earlier, preferred_element_type=F32) + run_ref[...]
    run_ref[...] = run_ref[...] + jnp.sum(sel, axis=1, keepdims=True)
    cnt_ref[...] = run_ref[...]

    ranks = [jnp.sum(jnp.where(sub == ix, before, 0.0), axis=0, keepdims=True) for ix in idxs]
    weights = [e / den for e in exps]
    ti_ref[...] = jnp.concatenate(idxs, axis=0)
    tr_ref[...] = jnp.concatenate(ranks, axis=0).astype(jnp.int32)
    meta = jnp.concatenate(weights + [ix.astype(F32) for ix in idxs]
                           + [jnp.zeros((LANES - 2 * TOP_K, tm), F32)], axis=0)
    h2_ref[:, PACKED:] = meta.T


def _outproj(a_p, a_s, b_p, b_s, x_p, x_s, mod_p, mod_s, norm_g, w_out_b, w_router, b_router):
    tm = ROW_TILE
    npt = x_p.shape[0] // tm
    nst = x_s.shape[0] // tm
    assert nst == 1
    n_tiles = npt + nst
    rows = n_tiles * tm
    tiles_per_batch = npt // mod_p.shape[0]
    last_p = npt - 1
    prow = lambda i: (jnp.minimum(i, last_p), 0)
    srow = lambda i: (0, 0)
    row = lambda i: (i, 0)
    col = lambda i: (0, i)
    const2 = lambda i: (0, 0)
    return pl.pallas_call(
        functools.partial(_outproj_kernel, n_prompt_tiles=npt),
        grid=(n_tiles,),
        in_specs=[pl.BlockSpec((tm, A_WIDTH), prow), pl.BlockSpec((tm, A_WIDTH), srow),
                  pl.BlockSpec((tm, B_WIDTH), prow), pl.BlockSpec((tm, B_WIDTH), srow),
                  pl.BlockSpec((tm, D_MODEL), prow), pl.BlockSpec((tm, D_MODEL), srow),
                  pl.BlockSpec((1, 6, D_MODEL), lambda i: (jnp.minimum(i, last_p) // tiles_per_batch, 0, 0)),
                  pl.BlockSpec((6, tm, D_MODEL), lambda i: (0, 0, 0)),
                  pl.BlockSpec((4, D_MODEL), const2),
                  pl.BlockSpec((D_MODEL, D_MODEL), const2),
                  pl.BlockSpec((D_MODEL, LANES), const2),
                  pl.BlockSpec((1, LANES), const2)],
        out_specs=[pl.BlockSpec((tm, D_MODEL), row), pl.BlockSpec((tm, ROW_WIDTH), row),
                   pl.BlockSpec((TOP_K, tm), col), pl.BlockSpec((TOP_K, tm), col),
                   pl.BlockSpec((N_EXPERTS, 1), const2),
                   pl.BlockSpec((1, N_EXPERTS, 1), lambda i: (i, 0, 0))],
        out_shape=[jax.ShapeDtypeStruct((rows, D_MODEL), F32), jax.ShapeDtypeStruct((rows, ROW_WIDTH), F32),
                   jax.ShapeDtypeStruct((TOP_K, rows), jnp.int32), jax.ShapeDtypeStruct((TOP_K, rows), jnp.int32),
                   jax.ShapeDtypeStruct((N_EXPERTS, 1), F32),
                   jax.ShapeDtypeStruct((n_tiles, N_EXPERTS, 1), F32)],
        scratch_shapes=[pltpu.VMEM((N_EXPERTS, 1), F32)],
        compiler_params=_cparams("arbitrary"),
        name="outproj_router",
    )(a_p, a_s, b_p, b_s, x_p, x_s, mod_p, mod_s, norm_g, w_out_b,
      jnp.pad(w_router, ((0, 0), (0, LANES - N_EXPERTS))).astype(BF16),
      jnp.pad(b_router.reshape(1, -1), ((0, 0), (0, LANES - N_EXPERTS))))


def _dispatch_kernel(pe_ref, cnt_ref, dest_ref, h_ref, xg_ref, zero_ref, sem, zsem, *, bm):
    tm = h_ref.shape[0]
    n_blocks = xg_ref.shape[0] // bm

    def zero_block(row0):
        return pltpu.make_async_copy(zero_ref, xg_ref.at[pl.ds(pl.multiple_of(row0, bm), bm), :], zsem)

    @pl.when(pl.program_id(0) == 0)
    def _():
        zero_ref[...] = jnp.zeros(zero_ref.shape, F32)
        first_unused = pe_ref[N_EXPERTS - 1] // bm
        for e in range(N_EXPERTS):
            @pl.when(cnt_ref[e] > 0)
            def _():
                zero_block(pe_ref[e] - bm).start()

        def start_unused(b, carry):
            zero_block(b * bm).start()
            return carry

        lax.fori_loop(first_unused, n_blocks, start_unused, 0)
        for e in range(N_EXPERTS):
            @pl.when(cnt_ref[e] > 0)
            def _():
                zero_block(pe_ref[e] - bm).wait()

        def wait_unused(b, carry):
            zero_block(b * bm).wait()
            return carry

        lax.fori_loop(first_unused, n_blocks, wait_unused, 0)

    for sub in range(tm // ROW_TILE):
        def body(t, carry, sub=sub):
            for j in range(TOP_K):
                d = dest_ref[(sub * TOP_K + j) * ROW_TILE + t]
                pltpu.make_async_copy(h_ref.at[pl.ds(sub * ROW_TILE + t, 1), :], xg_ref.at[pl.ds(d, 1), :],
                                      sem).start()
            return carry

        lax.fori_loop(0, ROW_TILE, body, 0)
    n = tm * TOP_K
    pltpu.make_async_copy(xg_ref.at[pl.ds(0, n), :], xg_ref.at[pl.ds(0, n), :], sem).wait()


def _dispatch(pad_end, cnt, dest_flat, h2, n_blocks):
    tm = ROW_TILE * DISPATCH_TILES
    bm = EXPERT_BLOCK
    rows = h2.shape[0]
    assert rows % tm == 0
    grid_spec = pltpu.PrefetchScalarGridSpec(
        num_scalar_prefetch=2,
        grid=(rows // tm,),
        in_specs=[pl.BlockSpec((tm * TOP_K,), lambda i, pe, cn: (i,), memory_space=pltpu.SMEM),
                  pl.BlockSpec((tm, ROW_WIDTH), lambda i, pe, cn: (i, 0))],
        out_specs=pl.BlockSpec(memory_space=pl.ANY),
        scratch_shapes=[pltpu.VMEM((bm, ROW_WIDTH), F32), pltpu.SemaphoreType.DMA(()), pltpu.SemaphoreType.DMA(())])
    return pl.pallas_call(
        functools.partial(_dispatch_kernel, bm=bm),
        grid_spec=grid_spec,
        out_shape=jax.ShapeDtypeStruct((n_blocks * bm, ROW_WIDTH), F32),
        compiler_params=_cparams("arbitrary"),
        name="moe_dispatch",
    )(pad_end, cnt, dest_flat, h2)


def _expert_kernel(be_ref, nu_ref, x_ref, wgu_ref, bgu_ref, wd_ref, bd_ref, y_ref, wgu_b, wd_b):
    i = pl.program_id(0)
    used = i < nu_ref[0]
    fresh = jnp.logical_or(i == 0, be_ref[i] != be_ref[jnp.maximum(i - 1, 0)])

    @pl.when(jnp.logical_and(used, fresh))
    def _():
        wgu_b[...] = wgu_ref[0].astype(BF16)
        wd_b[...] = wd_ref[0].astype(BF16)

    @pl.when(jnp.logical_not(used))
    def _():
        y_ref[...] = jnp.zeros(y_ref.shape, y_ref.dtype)

    @pl.when(used)
    def _():
        words = lax.bitcast_convert_type(x_ref[:, 0:PACKED], jnp.uint32)
        low = lax.bitcast_convert_type(lax.shift_left(words, jnp.uint32(16)), F32)
        high = lax.bitcast_convert_type(jnp.bitwise_and(words, jnp.uint32(0xFFFF0000)), F32)
        x = jnp.concatenate([low.astype(BF16), high.astype(BF16)], axis=1)
        gu = jnp.dot(x, wgu_b[...], preferred_element_type=F32) + bgu_ref[0]
        glu = jnp.minimum(gu[:, :D_EXPERT], SWIGLU_LIMIT)
        lin = jnp.clip(gu[:, D_EXPERT:], -SWIGLU_LIMIT, SWIGLU_LIMIT)
        hid = glu * jax.nn.sigmoid(SWIGLU_ALPHA * glu) * (lin + 1.0)
        y = jnp.dot(hid.astype(BF16), wd_b[...], preferred_element_type=F32) + bd_ref[0]
        meta = x_ref[:, PACKED:]
        me = be_ref[i].astype(F32)
        w = jnp.zeros((x_ref.shape[0], 1), F32)
        for j in range(TOP_K):
            w = w + jnp.where(meta[:, TOP_K + j:TOP_K + j + 1] == me, meta[:, j:j + 1], 0.0)
        y_ref[...] = (y * w).astype(y_ref.dtype)


def _experts(block_e, n_used, xg, w_gu_b, b_gu, w_down_b, b_down):
    bm = EXPERT_BLOCK
    n_blocks = xg.shape[0] // bm
    grid_spec = pltpu.PrefetchScalarGridSpec(
        num_scalar_prefetch=2,
        grid=(n_blocks,),
        in_specs=[pl.BlockSpec((bm, ROW_WIDTH), lambda i, be, nu: (jnp.minimum(i, nu[0] - 1), 0)),
                  pl.BlockSpec((1, D_MODEL, 2 * D_EXPERT), lambda i, be, nu: (be[i], 0, 0)),
                  pl.BlockSpec((1, 1, 2 * D_EXPERT), lambda i, be, nu: (be[i], 0, 0)),
                  pl.BlockSpec((1, D_EXPERT, D_MODEL), lambda i, be, nu: (be[i], 0, 0)),
                  pl.BlockSpec((1, 1, D_MODEL), lambda i, be, nu: (be[i], 0, 0))],
        out_specs=pl.BlockSpec((bm, D_MODEL), lambda i, be, nu: (i, 0)),
        scratch_shapes=[pltpu.VMEM((D_MODEL, 2 * D_EXPERT), BF16), pltpu.VMEM((D_EXPERT, D_MODEL), BF16)])
    return pl.pallas_call(
        _expert_kernel,
        grid_spec=grid_spec,
        out_shape=jax.ShapeDtypeStruct((xg.shape[0], D_MODEL), BF16),
        compiler_params=_cparams("arbitrary"),
        name="moe_experts",
    )(block_e, n_used, xg, w_gu_b, b_gu.reshape(N_EXPERTS, 1, -1), w_down_b, b_down.reshape(N_EXPERTS, 1, -1))


def _combine_kernel(row0_ref, col_ref, x1_ref, mp_ref, ms_ref, ng_ref, yg_ref,
                    yp_ref, ys_ref, rows_ref, g_ref, sem, *, n_prompt_tiles):
    i = pl.program_id(0)
    n_steps = pl.num_programs(0)
    tm = x1_ref.shape[0]
    is_p = i < n_prompt_tiles
    win = COMBINE_WINDOW
    n_win_rows = COMBINE_SLOTS * win
    slot = i % 2

    def fetch_windows(tile, buf):
        for s in range(COMBINE_SLOTS):
            row0 = pl.multiple_of(row0_ref[tile * COMBINE_SLOTS + s], COMBINE_ALIGN)
            pltpu.make_async_copy(yg_ref.at[pl.ds(row0, win), :], rows_ref.at[buf, pl.ds(s * win, win), :],
                                  sem.at[buf]).start()

    @pl.when(i == 0)
    def _():
        fetch_windows(0, 0)

    @pl.when(i + 1 < n_steps)
    def _():
        fetch_windows(i + 1, 1 - slot)

    pltpu.make_async_copy(yg_ref.at[pl.ds(0, n_win_rows), :], rows_ref.at[slot], sem.at[slot]).wait()
    col = col_ref[...]
    for c in range(n_win_rows // LANES):
        lane = lax.broadcasted_iota(jnp.int32, (tm, LANES), 1) + c * LANES
        g = jnp.zeros((tm, LANES), F32)
        for j in range(TOP_K):
            g = jnp.where(lane == col[:, j:j + 1], 1.0, g)
        g_ref[:, c * LANES:(c + 1) * LANES] = g.astype(BF16)
    f = jnp.dot(g_ref[...], rows_ref[slot], preferred_element_type=F32)

    g2 = jnp.where(is_p, mp_ref[0, 5:6, :], ms_ref[5])
    y = x1_ref[...] + g2 * _rms(f, ng_ref[3:4, :])

    @pl.when(is_p)
    def _():
        yp_ref[...] = y

    @pl.when(jnp.logical_not(is_p))
    def _():
        ys_ref[...] = y


def _combine(row0, col, x1, mod_p, mod_s, norm_g, yg, *, n_prompt_rows, n_sample_rows):
    tm = ROW_TILE
    npt = n_prompt_rows // tm
    assert n_sample_rows == tm and (COMBINE_SLOTS * COMBINE_WINDOW) % LANES == 0
    n_tiles = npt + 1
    tiles_per_batch = npt // mod_p.shape[0]
    last_p = npt - 1
    grid_spec = pltpu.PrefetchScalarGridSpec(
        num_scalar_prefetch=1,
        grid=(n_tiles,),
        in_specs=[pl.BlockSpec((tm, TOP_K), lambda i, r0: (i, 0)),
                  pl.BlockSpec((tm, D_MODEL), lambda i, r0: (i, 0)),
                  pl.BlockSpec((1, 6, D_MODEL), lambda i, r0: (jnp.minimum(i, last_p) // tiles_per_batch, 0, 0)),
                  pl.BlockSpec((6, tm, D_MODEL), lambda i, r0: (0, 0, 0)),
                  pl.BlockSpec((4, D_MODEL), lambda i, r0: (0, 0)),
                  pl.BlockSpec(memory_space=pl.ANY)],
        out_specs=[pl.BlockSpec((tm, D_MODEL), lambda i, r0: (jnp.minimum(i, last_p), 0)),
                   pl.BlockSpec((tm, D_MODEL), lambda i, r0: (0, 0))],
        scratch_shapes=[pltpu.VMEM((2, COMBINE_SLOTS * COMBINE_WINDOW, D_MODEL), BF16),
                        pltpu.VMEM((tm, COMBINE_SLOTS * COMBINE_WINDOW), BF16),
                        pltpu.SemaphoreType.DMA((2,))])
    return pl.pallas_call(
        functools.partial(_combine_kernel, n_prompt_tiles=npt),
        grid_spec=grid_spec,
        out_shape=[jax.ShapeDtypeStruct((n_prompt_rows, D_MODEL), F32),
                   jax.ShapeDtypeStruct((n_sample_rows, D_MODEL), F32)],
        compiler_params=_cparams("arbitrary"),
        name="moe_combine",
    )(row0, col, x1, mod_p, mod_s, norm_g, yg)


def _t5_bucket_np(dist):
    n = np.maximum(dist, 0)
    max_exact = NUM_BUCKETS // 2
    nf = np.maximum(n, 1).astype(np.float64)
    large = max_exact + (np.log(nf / max_exact) / math.log(MAX_DISTANCE / max_exact)
                         * (NUM_BUCKETS - max_exact)).astype(np.int32)
    large = np.minimum(large, NUM_BUCKETS - 1)
    return np.where(n < max_exact, n, large).astype(np.int32)


def _bias_from_dist(rel_bias, dist):
    onehot = jax.nn.one_hot(_t5_bucket_np(dist).reshape(-1), NUM_BUCKETS, dtype=F32)
    shifted = (rel_bias - rel_bias[NUM_BUCKETS - 1]).reshape(NUM_BUCKETS, -1)
    out = jnp.dot(onehot, shifted, precision=lax.Precision.HIGHEST).reshape(dist.shape + rel_bias.shape[1:])
    return jnp.where(jnp.asarray(dist >= 0)[..., None, None], out, -jnp.inf).astype(F32)


def _head_masked(t):
    h = t.shape[0]
    same = np.eye(h, dtype=bool)[:, None, None, None, :]
    full = jnp.where(same, t[..., None], -jnp.inf)
    return full.reshape(h * t.shape[1] * t.shape[2], t.shape[3] * h)


def kernel(x_prompt, x_sample, cache_k, cache_v, page_table, c_prompt, c_sample, w_ada, b_ada, norm_g, w_in,
           w_out, ln_v_g, ln_v_b, w_spatial, b_spatial, lam_params, subln_g, rel_bias, w_router, b_router,
           w_gu, b_gu, w_down, b_down):
    batch, seq, d = x_prompt.shape
    nb, ds = x_sample.shape[:2]
    n_pages = page_table.shape[1]
    past = n_pages * PAGE_SIZE
    rows_p = batch * seq
    rows_s = nb * ds
    assert rows_s == ROW_TILE and d == D_MODEL

    mod = _ada(jnp.concatenate([c_prompt, c_sample], axis=0), w_ada[0], b_ada[0])
    mod_p = mod[:batch].reshape(batch, 6, D_MODEL)
    mod_s = jnp.transpose(jnp.repeat(mod[batch:], ds, axis=0).reshape(rows_s, 6, D_MODEL), (1, 0, 2))

    ng = norm_g[0]
    w_in_b = w_in[0].astype(BF16)
    w_out_b = w_out[0].astype(BF16)
    lng = ln_v_g[0].reshape(1, A_WIDTH)
    lnb = ln_v_b[0].reshape(1, A_WIDTH)
    grp = np.arange(A_WIDTH) // A_HEAD
    avg = jnp.asarray((grp[:, None] == grp[None, :]).astype(np.float32) / A_HEAD, BF16)
    ws_p = jnp.tril(w_spatial[0]).astype(BF16)
    bs_p = jnp.repeat(b_spatial[0].T, A_HEAD, axis=1)
    w_small = jnp.tril(w_spatial[0][:, :ds, :ds])
    same_batch = np.kron(np.eye(nb, dtype=np.float32), np.ones((ds, ds), np.float32))
    spread = np.tile(np.eye(ds, dtype=np.float32), (nb, 1))
    ws_s = (jnp.einsum('rt,gts,cs->grc', spread, w_small, spread, precision=lax.Precision.HIGHEST)
            * same_batch).astype(BF16)
    bs_s = jnp.tile(jnp.repeat(b_spatial[0][:, :ds].T, A_HEAD, axis=1), (nb, 1))

    xp2 = x_prompt.reshape(rows_p, D_MODEL)
    xs2 = x_sample.reshape(rows_s, D_MODEL)
    assert ATTN_TILE % ROW_TILE == 0 and seq % ATTN_TILE == 0
    a_p, va_p, q_p, k_p, v_p, kb_p, vt_p = _inproj(
        xp2, mod_p[:, 0:1], mod_p[:, 1:2], ng[0:1], w_in_b, lng, lnb, avg, ws_p, bs_p,
        tiles_per_mod=seq // ROW_TILE, chunk=CHUNK, q_dtype=BF16, q_scale=LOG2E / math.sqrt(B_HEAD), va_rows=CHUNK)
    a_s, va_s, q_s, k_s, v_s, _, _ = _inproj(
        xs2, mod_s[0:1], mod_s[1:2], ng[0:1], w_in_b, lng, lnb, avg, ws_s, bs_s,
        tiles_per_mod=1, chunk=rows_s, q_dtype=F32, q_scale=1.0, va_rows=rows_s)

    ti = np.arange(ATTN_TILE)
    dist_diag = ti[:, None] - ti[None, :]
    dist_prev = ti[:, None] - ti[None, ATTN_TILE - MAX_DISTANCE:] + ATTN_TILE
    assert dist_prev.min() >= 1 and (ti[:, None] - ti[None, :ATTN_TILE - MAX_DISTANCE] + ATTN_TILE).min() >= MAX_DISTANCE
    to_cols = lambda t: jnp.transpose(t, (2, 1, 3, 0)).reshape(B_HEADS, t.shape[1], 2 * ATTN_TILE) * LOG2E
    b_p = _attn_p(q_p, kb_p, vt_p, to_cols(_bias_from_dist(rel_bias, dist_diag)),
                  to_cols(_bias_from_dist(rel_bias, dist_prev)), lam_params[0], subln_g, batch=batch, seq=seq)

    qi = np.arange(ds)
    ki = np.arange(PAGE_SIZE)
    dist_l = PAGE_SIZE + qi[:, None] - ki[None, :]
    kn_i = np.arange(PAGE_SIZE // B_HEADS)
    dist_n = np.where(kn_i[None, :] < ds, qi[:, None] - kn_i[None, :], -1)
    to_rows = lambda t: jnp.transpose(t, (2, 3, 0, 1))
    bias_l = _head_masked(to_rows(_bias_from_dist(rel_bias, dist_l)))
    bias_n = _head_masked(to_rows(_bias_from_dist(rel_bias, dist_n)))
    maskb = _head_masked(jnp.zeros((B_HEADS, 2, ds, PAGE_SIZE), F32))
    pad_keys = lambda t: jnp.pad(t.reshape(nb, ds * B_HEADS, B_VDIM), ((0, 0), (0, PAGE_SIZE - ds * B_HEADS), (0, 0)))
    n_phys = cache_k.shape[0]
    b_s = _attn_s(page_table, q_s, pad_keys(k_s), pad_keys(v_s), maskb, bias_l, bias_n, lam_params[0], subln_g,
                  cache_k.reshape(n_phys * PAGE_SIZE * B_HEADS, B_VDIM),
                  cache_v.reshape(n_phys * PAGE_SIZE * B_HEADS, B_VDIM), ds=ds)

    x1, h2, top_i, top_r, counts, tile_base = _outproj(
        a_p, a_s, b_p, b_s, xp2, xs2, mod_p, mod_s, ng, w_out_b, w_router[0], b_router[0])

    bm = EXPERT_BLOCK
    rows = rows_p + rows_s
    n_blocks = rows * TOP_K // bm + N_EXPERTS + 1
    cnt = counts[:, 0].astype(jnp.int32)
    padded = (cnt + bm - 1) // bm * bm
    pad_end = jnp.cumsum(padded)
    pad_start = pad_end - padded
    experts = jnp.arange(N_EXPERTS, dtype=jnp.int32)
    chosen = top_i[..., None] == experts
    dest = jnp.sum(jnp.where(chosen, pad_start, 0), axis=-1) + top_r
    n_tiles = rows // ROW_TILE
    win = COMBINE_WINDOW
    assert win % COMBINE_ALIGN == 0
    assert (ROW_TILE * TOP_K + N_EXPERTS * (COMBINE_ALIGN - 1)) // win + N_EXPERTS <= COMBINE_SLOTS
    base = tile_base[:, :, 0].astype(jnp.int32)
    first = pad_start[None, :] + base
    win0 = first // COMBINE_ALIGN * COMBINE_ALIGN
    sent = jnp.concatenate([base[1:], cnt[None, :]], axis=0) - base
    n_win = jnp.where(sent > 0, (first - win0 + sent + win - 1) // win, 0)
    slot_end = jnp.cumsum(n_win, axis=1)
    slot_start = slot_end - n_win
    slots = jnp.arange(COMBINE_SLOTS, dtype=jnp.int32)
    slot_e = jnp.minimum(jnp.sum((slot_end[:, None, :] <= slots[None, :, None]).astype(jnp.int32), axis=2),
                         N_EXPERTS - 1)
    pick = slot_e[..., None] == experts
    slot_row0 = (jnp.sum(jnp.where(pick, win0[:, None, :], 0), axis=2)
                 + win * (slots[None, :] - jnp.sum(jnp.where(pick, slot_start[:, None, :], 0), axis=2)))
    slot_row0 = jnp.where(slots[None, :] < slot_end[:, -1:], slot_row0, 0)
    local = dest - jnp.sum(jnp.where(chosen, jnp.repeat(win0, ROW_TILE, axis=0), 0), axis=-1)
    col = jnp.sum(jnp.where(chosen, jnp.repeat(slot_start, ROW_TILE, axis=0), 0), axis=-1) * win + local
    dest = jnp.transpose(dest.reshape(TOP_K, n_tiles, ROW_TILE), (1, 0, 2)).reshape(-1)
    blk_start = jnp.arange(n_blocks, dtype=jnp.int32) * bm
    block_e = jnp.minimum(jnp.sum((pad_end[None, :] <= blk_start[:, None]).astype(jnp.int32), axis=1),
                          N_EXPERTS - 1)
    n_used = (pad_end[-1:] // bm).astype(jnp.int32)

    xg = _dispatch(pad_end.astype(jnp.int32), cnt, dest, h2, n_blocks)
    yg = _experts(block_e, n_used, xg, w_gu[0], b_gu[0], w_down[0], b_down[0])
    y_p, y_s = _combine(slot_row0.reshape(-1), col.T, x1, mod_p, mod_s, ng, yg,
                        n_prompt_rows=rows_p, n_sample_rows=rows_s)

    return (y_p.reshape(batch, seq, D_MODEL),
            y_s.reshape(nb, ds, D_MODEL),
            k_p.reshape(batch, seq, 1, B_HEADS, B_VDIM),
            v_p.reshape(batch, seq, 1, B_HEADS, B_VDIM),
            k_s.reshape(nb, ds, 1, B_HEADS, B_VDIM),
            v_s.reshape(nb, ds, 1, B_HEADS, B_VDIM),
            va_p.reshape(batch, CHUNK, 1, A_WIDTH),
            va_s.reshape(nb, ds, 1, A_WIDTH))
```

```python
import functools
import math

import numpy as np
import jax
import jax.numpy as jnp
from jax import lax
from jax.experimental import pallas as pl
from jax.experimental.pallas import tpu as pltpu

F32 = jnp.float32
BF16 = jnp.bfloat16

D_MODEL = 1024
A_WIDTH = 512
A_HEAD = 64
A_GROUPS = 8
CHUNK = 128
B_WIDTH = 512
B_HEAD = 64
B_VDIM = 128
B_HEADS = 4
IN_WIDTH = 2 * A_WIDTH + 3 * B_WIDTH
NUM_BUCKETS = 32
MAX_DISTANCE = 128
PAGE_SIZE = 128
N_EXPERTS = 32
TOP_K = 4
D_EXPERT = 1024
SWIGLU_LIMIT = 7.0
SWIGLU_ALPHA = 1.702
NORM_EPS = 1e-6
LAM_INIT = 0.8 - 0.6 * math.exp(-0.3 * 0)

ROW_TILE = 256
ATTN_TILE = 512
ATTN_BATCHES = 1
PAGES_PER_STEP = 32
LOG2E = math.log2(math.e)
LANES = 128
PACKED = D_MODEL // 2
ROW_WIDTH = PACKED + LANES
EXPERT_BLOCK = 512
DISPATCH_TILES = 13
COMBINE_ALIGN = 16
COMBINE_WINDOW = 16
COMBINE_SLOTS = 128
VMEM_LIMIT = 56 * 1024 * 1024


def _cparams(*sem):
    return pltpu.CompilerParams(dimension_semantics=sem, vmem_limit_bytes=VMEM_LIMIT)


def _rms(x, g):
    return x * lax.rsqrt(jnp.mean(x * x, axis=-1, keepdims=True) + NORM_EPS) * g


def _ada_kernel(c_ref, w_ref, b_ref, o_ref):
    s = jax.nn.silu(c_ref[...]).astype(BF16)
    o_ref[...] = jnp.dot(s, w_ref[...].astype(BF16), preferred_element_type=F32) + b_ref[...]


def _ada(c_all, w_ada, b_ada):
    n = c_all.shape[0]
    tn = 1024
    return pl.pallas_call(
        _ada_kernel,
        grid=(6 * D_MODEL // tn,),
        in_specs=[pl.BlockSpec((n, D_MODEL), lambda j: (0, 0)),
                  pl.BlockSpec((D_MODEL, tn), lambda j: (0, j)),
                  pl.BlockSpec((1, tn), lambda j: (0, j))],
        out_specs=pl.BlockSpec((n, tn), lambda j: (0, j)),
        out_shape=jax.ShapeDtypeStruct((n, 6 * D_MODEL), F32),
        compiler_params=_cparams("arbitrary"),
        name="ada",
    )(c_all, w_ada, b_ada.reshape(1, -1))


def _inproj_kernel(x_ref, sh_ref, sc_ref, ng_ref, w_ref, lng_ref, lnb_ref, avg_ref, ws_ref, bs_ref,
                   a_ref, va_ref, q_ref, k_ref, v_ref, kb_ref, vt_ref, *, chunk, q_scale):
    tm = x_ref.shape[0]
    va_rows = va_ref.shape[0]
    h = (_rms(x_ref[...], ng_ref[...]) * (1.0 + sc_ref[0]) + sh_ref[0]).astype(BF16)

    u = jax.nn.gelu(jnp.dot(h, w_ref[:, 0:A_WIDTH], preferred_element_type=F32))
    gv = jax.nn.gelu(jnp.dot(h, w_ref[:, A_WIDTH:2 * A_WIDTH], preferred_element_type=F32))

    avg = avg_ref[...]

    def group_mean(t):
        return jnp.dot(t.astype(BF16), avg, preferred_element_type=F32)

    xc = gv - group_mean(gv)
    va = xc * lax.rsqrt(group_mean(xc * xc) + NORM_EPS) * lng_ref[...] + lnb_ref[...]
    va_ref[...] = va[tm - va_rows:, :]

    vab = va.astype(BF16)
    n_chunks = tm // chunk
    lane = lax.broadcasted_iota(jnp.int32, (chunk, n_chunks * 128), 1) % 128
    for pair in range(A_GROUPS // 2):
        c0 = pair * 128
        vp = jnp.concatenate([vab[c * chunk:(c + 1) * chunk, c0:c0 + 128] for c in range(n_chunks)], axis=1)
        lo_half = jnp.where(lane < A_HEAD, vp, jnp.zeros_like(vp))
        hi_half = jnp.where(lane >= A_HEAD, vp, jnp.zeros_like(vp))
        s = (jnp.dot(ws_ref[2 * pair], lo_half, preferred_element_type=F32)
             + jnp.dot(ws_ref[2 * pair + 1], hi_half, preferred_element_type=F32))
        for c in range(n_chunks):
            r0 = c * chunk
            a = u[r0:r0 + chunk, c0:c0 + 128] * (s[:, c * 128:(c + 1) * 128] + bs_ref[:, c0:c0 + 128])
            a_ref[r0:r0 + chunk, c0:c0 + 128] = a.astype(a_ref.dtype)

    q = jnp.dot(h, w_ref[:, 2 * A_WIDTH:2 * A_WIDTH + B_WIDTH], preferred_element_type=F32)
    q_ref[...] = (q * q_scale).astype(q_ref.dtype)
    k = jnp.dot(h, w_ref[:, 2 * A_WIDTH + B_WIDTH:2 * A_WIDTH + 2 * B_WIDTH], preferred_element_type=F32)
    kb_ref[...] = k.astype(BF16)
    v = jnp.dot(h, w_ref[:, 2 * A_WIDTH + 2 * B_WIDTH:IN_WIDTH], preferred_element_type=F32)
    for hh in range(B_HEADS):
        c0 = hh * B_VDIM
        k_ref[pl.ds(hh, tm, stride=B_HEADS), :] = k[:, c0:c0 + B_VDIM]
        v_ref[pl.ds(hh, tm, stride=B_HEADS), :] = v[:, c0:c0 + B_VDIM]
        vt_ref[0, hh, 0] = v[:, c0:c0 + B_VDIM].T.astype(BF16)


def _inproj(x2, sh, sc, ng, w_in_b, lng, lnb, avg, ws, bs, *, tiles_per_mod, chunk, q_dtype, q_scale, va_rows):
    rows = x2.shape[0]
    tm = ROW_TILE
    n_tiles = rows // tm
    mod_rows = sh.shape[1]
    row = lambda i: (i, 0)
    const2 = lambda i: (0, 0)
    mod_map = lambda i: (i // tiles_per_mod, 0, 0)
    out_w = lambda w, dt: jax.ShapeDtypeStruct((rows, w), dt)
    head_rows = jax.ShapeDtypeStruct((rows * B_HEADS, B_VDIM), F32)
    head_rows_spec = pl.BlockSpec((tm * B_HEADS, B_VDIM), row)
    vt_shape = jax.ShapeDtypeStruct((n_tiles // tiles_per_mod, B_HEADS, tiles_per_mod, B_VDIM, tm), BF16)
    vt_spec = pl.BlockSpec((1, B_HEADS, 1, B_VDIM, tm), lambda i: (i // tiles_per_mod, 0, i % tiles_per_mod, 0, 0))
    return pl.pallas_call(
        functools.partial(_inproj_kernel, chunk=chunk, q_scale=q_scale),
        grid=(rows // tm,),
        in_specs=[pl.BlockSpec((tm, D_MODEL), row),
                  pl.BlockSpec((1, mod_rows, D_MODEL), mod_map),
                  pl.BlockSpec((1, mod_rows, D_MODEL), mod_map),
                  pl.BlockSpec((1, D_MODEL), const2),
                  pl.BlockSpec((D_MODEL, IN_WIDTH), const2),
                  pl.BlockSpec((1, A_WIDTH), const2),
                  pl.BlockSpec((1, A_WIDTH), const2),
                  pl.BlockSpec((A_WIDTH, A_WIDTH), const2),
                  pl.BlockSpec((A_GROUPS, chunk, chunk), lambda i: (0, 0, 0)),
                  pl.BlockSpec((chunk, A_WIDTH), const2)],
        out_specs=[pl.BlockSpec((tm, A_WIDTH), row),
                   pl.BlockSpec((va_rows, A_WIDTH), lambda i: (i // tiles_per_mod, 0)),
                   pl.BlockSpec((tm, B_WIDTH), row), head_rows_spec, head_rows_spec,
                   pl.BlockSpec((tm, B_WIDTH), row), vt_spec],
        out_shape=[out_w(A_WIDTH, BF16),
                   jax.ShapeDtypeStruct((n_tiles // tiles_per_mod * va_rows, A_WIDTH), F32),
                   out_w(B_WIDTH, q_dtype), head_rows, head_rows, out_w(B_WIDTH, BF16), vt_shape],
        compiler_params=_cparams("arbitrary"),
        name="inproj",
    )(x2, sh, sc, ng, w_in_b, lng, lnb, avg, ws, bs)


def _stack_halves(q):
    lane = lax.broadcasted_iota(jnp.int32, q.shape, 1)
    zero = jnp.zeros_like(q)
    return jnp.concatenate([jnp.where(lane < B_HEAD, q, zero), jnp.where(lane >= B_HEAD, q, zero)], axis=0)


def _lambda(lam_ref):
    lp = lam_ref[...]
    return (jnp.exp(jnp.sum(lp[0:1] * lp[1:2], axis=-1, keepdims=True))
            - jnp.exp(jnp.sum(lp[2:3] * lp[3:4], axis=-1, keepdims=True)) + LAM_INIT)


def _diff_finish(acc, l, n, lam, g):
    o = acc[:n] / l[:n] - lam * (acc[n:] / l[n:])
    return _rms(o, g) * (1.0 - LAM_INIT)


def _attn_p_kernel(q_ref, k_ref, vt_ref, diag_ref, prev_ref, lam_ref, g_ref, o_ref,
                   acc_ref, m_ref, l_ref, s_ref, tmax_ref, p_ref, alpha_ref, *, tq, tk):
    qi = pl.program_id(2)
    nbb = q_ref.shape[0]
    streams = [(bb, c) for bb in range(nbb) for c in range(2)]
    q_half = []
    for bb in range(nbb):
        q = q_ref[bb]
        lane = lax.broadcasted_iota(jnp.int32, q.shape, 1)
        zero = jnp.zeros_like(q)
        q_half += [jnp.where(lane < B_HEAD, q, zero), jnp.where(lane >= B_HEAD, q, zero)]
    m_ref[...] = jnp.full(m_ref.shape, -jnp.inf, F32)
    l_ref[...] = jnp.zeros(l_ref.shape, F32)
    acc_ref[...] = jnp.zeros(acc_ref.shape, F32)
    p_ref[1] = jnp.zeros(p_ref.shape[1:], BF16)
    alpha_ref[1] = jnp.ones(alpha_ref.shape[1:], F32)

    def scores(j):
        out = []
        for bb in range(nbb):
            k = k_ref[bb, pl.ds(pl.multiple_of(j * tk, tk), tk), :]
            out += [lax.dot_general(k, q_half[2 * bb + c], (((1,), (1,)), ((), ())), preferred_element_type=F32)
                    for c in range(2)]
        return out

    near_keys = prev_ref.shape[1]

    def keep_scores(j, kind, ss):
        slot = j % 2
        for u, (bb, c) in enumerate(streams):
            s = ss[u]
            if kind == 0:
                s = s + diag_ref[0, :, c * tq:(c + 1) * tq]
            elif kind == 1:
                s = jnp.concatenate([s[:tk - near_keys], s[tk - near_keys:] + prev_ref[0, :, c * tq:(c + 1) * tq]],
                                    axis=0)
            s_ref[slot, u] = s
            tmax_ref[slot, u] = jnp.max(s, axis=0, keepdims=True)

    def softmax(j):
        slot = j % 2
        for u in range(len(streams)):
            m_old = m_ref[u]
            m_new = jnp.maximum(m_old, tmax_ref[slot, u])
            alpha = jnp.exp2(m_old - m_new)
            p = jnp.exp2(s_ref[slot, u] - m_new)
            l_ref[u] = alpha * l_ref[u] + jnp.sum(p, axis=0, keepdims=True)
            m_ref[u] = m_new
            p_ref[slot, u] = p.astype(BF16)
            alpha_ref[slot, u] = alpha

    def pv(j):
        slot = j % 2
        pieces = tk // vt_ref.shape[-1]
        first = jnp.maximum(j, 0) * pieces
        for u, (bb, c) in enumerate(streams):
            add = None
            for r in range(pieces):
                w = vt_ref.shape[-1]
                t = jnp.dot(vt_ref[bb, 0, first + r], p_ref[slot, u, r * w:(r + 1) * w, :],
                            preferred_element_type=F32)
                add = t if add is None else add + t
            acc_ref[u] = alpha_ref[slot, u] * acc_ref[u] + add

    def iteration(j, next_kind):
        ss = scores(j + 1)
        pv(j - 1)
        softmax(j)
        keep_scores(j + 1, next_kind, ss)

    def far_pair(jj, carry):
        iteration(2 * jj, None)
        iteration(2 * jj + 1, None)
        return carry

    for kind, cond in ((0, qi == 0), (1, qi == 1), (None, qi >= 2)):
        @pl.when(cond)
        def _(kind=kind):
            keep_scores(0, kind, scores(0))

    n_far = jnp.maximum(qi - 2, 0)
    lax.fori_loop(0, n_far // 2, far_pair, 0)

    @pl.when(n_far % 2 == 1)
    def _():
        iteration(n_far - 1, None)

    @pl.when(qi >= 2)
    def _():
        iteration(qi - 2, 1)

    @pl.when(qi >= 1)
    def _():
        iteration(qi - 1, 0)

    pv(qi - 1)
    softmax(qi)
    pv(qi)

    lam = _lambda(lam_ref)
    for bb in range(nbb):
        o = acc_ref[2 * bb] / l_ref[2 * bb] - lam * (acc_ref[2 * bb + 1] / l_ref[2 * bb + 1])
        y = o * lax.rsqrt(jnp.mean(o * o, axis=0, keepdims=True) + NORM_EPS) * (1.0 - LAM_INIT)
        o_ref[bb] = (y.T * g_ref[...]).astype(o_ref.dtype)


def _attn_p(qb, kb, vt, bias_diag, bias_prev, lam_params, subln_g, *, batch, seq):
    tq = tk = ATTN_TILE
    nq = seq // tq
    nbb = ATTN_BATCHES
    ns = 2 * nbb
    assert batch % nbb == 0
    out = pl.pallas_call(
        functools.partial(_attn_p_kernel, tq=tq, tk=tk),
        grid=(batch // nbb, B_HEADS, nq),
        in_specs=[pl.BlockSpec((nbb, tq, B_VDIM), lambda b, h, i: (b, i, h)),
                  pl.BlockSpec((nbb, seq, B_VDIM), lambda b, h, i: (b, 0, h)),
                  pl.BlockSpec((nbb, 1, seq // ROW_TILE, B_VDIM, ROW_TILE), lambda b, h, i: (b, h, 0, 0, 0)),
                  pl.BlockSpec((1, tk, 2 * tq), lambda b, h, i: (h, 0, 0)),
                  pl.BlockSpec((1, bias_prev.shape[1], 2 * tq), lambda b, h, i: (h, 0, 0)),
                  pl.BlockSpec((4, B_HEAD), lambda b, h, i: (0, 0)),
                  pl.BlockSpec((1, B_VDIM), lambda b, h, i: (0, 0))],
        out_specs=pl.BlockSpec((nbb, tq, B_VDIM), lambda b, h, i: (b, i, h)),
        out_shape=jax.ShapeDtypeStruct((batch, seq, B_WIDTH), BF16),
        scratch_shapes=[pltpu.VMEM((ns, B_VDIM, tq), F32),
                        pltpu.VMEM((ns, 1, tq), F32),
                        pltpu.VMEM((ns, 1, tq), F32),
                        pltpu.VMEM((2, ns, tk, tq), F32),
                        pltpu.VMEM((2, ns, 1, tq), F32),
                        pltpu.VMEM((2, ns, tk, tq), BF16),
                        pltpu.VMEM((2, ns, 1, tq), F32)],
        compiler_params=_cparams("arbitrary", "arbitrary", "arbitrary"),
        name="attn_prompt",
    )(qb.reshape(batch, seq, B_WIDTH), kb.reshape(batch, seq, B_WIDTH), vt, bias_diag, bias_prev, lam_params,
      subln_g)
    return out.reshape(batch * seq, B_WIDTH)


def _attn_s_kernel(pt_ref, q_ref, kn_ref, vn_ref, maskb_ref, biasl_ref, biasn_ref, lam_ref, g_ref, *rest,
                   npages, ds):
    del pt_ref
    k_refs = rest[:npages]
    v_refs = rest[npages:2 * npages]
    o_ref = rest[2 * npages]
    acc_ref, m_ref, l_ref = rest[2 * npages + 1:]
    g = pl.program_id(1)
    last = g == pl.num_programs(1) - 1

    @pl.when(g == 0)
    def _():
        m_ref[...] = jnp.full(m_ref.shape, -jnp.inf, F32)
        l_ref[...] = jnp.zeros(l_ref.shape, F32)
        acc_ref[...] = jnp.zeros(acc_ref.shape, F32)

    q = q_ref[...] * (1.0 / math.sqrt(B_HEAD))
    qall = jnp.concatenate([_stack_halves(q[:, h * B_VDIM:(h + 1) * B_VDIM]) for h in range(B_HEADS)],
                           axis=0).astype(BF16)

    def process(k_blocks, v_blocks, biases):
        s = jnp.concatenate(
            [lax.dot_general(qall, kb.astype(BF16), (((1,), (1,)), ((), ())), preferred_element_type=F32) + bb
             for kb, bb in zip(k_blocks, biases)], axis=1)
        m_old = m_ref[...]
        m_new = jnp.maximum(m_old, jnp.max(s, axis=-1, keepdims=True))
        alpha = jnp.exp(m_old - m_new)
        pr = jnp.exp(s - m_new)
        l_ref[...] = alpha * l_ref[...] + jnp.sum(pr, axis=-1, keepdims=True)
        prb = pr.astype(BF16)
        pv = None
        off = 0
        for vb in v_blocks:
            n = vb.shape[0]
            t = jnp.dot(prb[:, off:off + n], vb.astype(BF16), preferred_element_type=F32)
            pv = t if pv is None else pv + t
            off += n
        acc_ref[...] = alpha * acc_ref[...] + pv
        m_ref[...] = m_new

    maskb = maskb_ref[...]
    newest = jnp.where(last, biasl_ref[...], maskb)
    process([k_refs[p][...] for p in range(npages)], [v_refs[p][...] for p in range(npages)],
            [maskb] * (npages - 1) + [newest])

    @pl.when(last)
    def _():
        process([kn_ref[0]], [vn_ref[0]], [biasn_ref[...]])
        lam = _lambda(lam_ref)
        acc = acc_ref[...]
        l = l_ref[...]
        for h in range(B_HEADS):
            r0 = h * 2 * ds
            o_ref[:, h * B_VDIM:(h + 1) * B_VDIM] = _diff_finish(
                acc[r0:r0 + 2 * ds], l[r0:r0 + 2 * ds], ds, lam, g_ref[...])


def _attn_s(page_table, qs, kn, vn, maskb, bias_last, bias_new, lam_params, subln_g, cache_k2, cache_v2, *, ds):
    nb, n_pages = page_table.shape
    npg = PAGES_PER_STEP
    steps = n_pages // npg
    page_rows = PAGE_SIZE * B_HEADS
    nrow = B_HEADS * 2 * ds

    def page_spec(p):
        return pl.BlockSpec((page_rows, B_VDIM), lambda b, g, pt, p=p: (pt[b, g * npg + p], 0))

    const2 = lambda b, g, pt: (0, 0)
    grid_spec = pltpu.PrefetchScalarGridSpec(
        num_scalar_prefetch=1,
        grid=(nb, steps),
        in_specs=[pl.BlockSpec((ds, B_WIDTH), lambda b, g, pt: (b, 0)),
                  pl.BlockSpec((1, PAGE_SIZE, B_VDIM), lambda b, g, pt: (b, 0, 0)),
                  pl.BlockSpec((1, PAGE_SIZE, B_VDIM), lambda b, g, pt: (b, 0, 0)),
                  pl.BlockSpec((nrow, page_rows), const2),
                  pl.BlockSpec((nrow, page_rows), const2),
                  pl.BlockSpec((nrow, PAGE_SIZE), const2),
                  pl.BlockSpec((4, B_HEAD), const2),
                  pl.BlockSpec((1, B_VDIM), const2)]
                 + [page_spec(p) for p in range(npg)] + [page_spec(p) for p in range(npg)],
        out_specs=pl.BlockSpec((ds, B_WIDTH), lambda b, g, pt: (b, 0)),
        scratch_shapes=[pltpu.VMEM((nrow, B_VDIM), F32),
                        pltpu.VMEM((nrow, 1), F32),
                        pltpu.VMEM((nrow, 1), F32)])
    return pl.pallas_call(
        functools.partial(_attn_s_kernel, npages=npg, ds=ds),
        grid_spec=grid_spec,
        out_shape=jax.ShapeDtypeStruct((nb * ds, B_WIDTH), F32),
        compiler_params=_cparams("arbitrary", "arbitrary"),
        name="attn_sample",
    )(page_table, qs, kn, vn, maskb, bias_last, bias_new, lam_params, subln_g,
      *([cache_k2] * npg), *([cache_v2] * npg))


def _outproj_kernel(ap_ref, as_ref, bp_ref, bs_ref, xp_ref, xs_ref, mp_ref, ms_ref, ng_ref, wo_ref,
                    wr_ref, br_ref, x1_ref, h2_ref, ti_ref, tr_ref, cnt_ref, base_ref, run_ref,
                    *, n_prompt_tiles):
    i = pl.program_id(0)
    tm = xp_ref.shape[0]
    is_p = i < n_prompt_tiles

    @pl.when(i == 0)
    def _():
        run_ref[...] = jnp.zeros(run_ref.shape, F32)

    a = jnp.where(is_p, ap_ref[...], as_ref[...])
    b = jnp.where(is_p, bp_ref[...], bs_ref[...].astype(BF16))
    x = jnp.where(is_p, xp_ref[...], xs_ref[...])
    g1 = jnp.where(is_p, mp_ref[0, 2:3, :], ms_ref[2])
    sh2 = jnp.where(is_p, mp_ref[0, 3:4, :], ms_ref[3])
    sc2 = jnp.where(is_p, mp_ref[0, 4:5, :], ms_ref[4])

    mix = (jnp.dot(a, wo_ref[0:A_WIDTH, :], preferred_element_type=F32)
           + jnp.dot(b, wo_ref[A_WIDTH:, :], preferred_element_type=F32))
    x1 = x + g1 * _rms(mix, ng_ref[1:2, :])
    x1_ref[...] = x1
    h2 = _rms(x1, ng_ref[2:3, :]) * (1.0 + sc2) + sh2
    low = lax.bitcast_convert_type(h2[:, 0:PACKED].astype(BF16).astype(F32), jnp.uint32)
    high = lax.bitcast_convert_type(h2[:, PACKED:].astype(BF16).astype(F32), jnp.uint32)
    words = jnp.bitwise_or(jnp.bitwise_and(high, jnp.uint32(0xFFFF0000)), lax.shift_right_logical(low, jnp.uint32(16)))
    h2_ref[:, 0:PACKED] = lax.bitcast_convert_type(words, F32)

    logits = jnp.dot(h2.astype(BF16), wr_ref[...], preferred_element_type=F32) + br_ref[...]
    work = logits.T[0:N_EXPERTS, :]
    sub = lax.broadcasted_iota(jnp.int32, work.shape, 0)
    vals, idxs = [], []
    for _ in range(TOP_K):
        mx = jnp.max(work, axis=0, keepdims=True)
        ix = jnp.min(jnp.where(work == mx, sub, N_EXPERTS), axis=0, keepdims=True)
        vals.append(mx)
        idxs.append(ix)
        work = jnp.where(sub == ix, -jnp.inf, work)
    exps = [jnp.exp(v - vals[0]) for v in vals]
    den = exps[0] + exps[1] + exps[2] + exps[3]

    sel = jnp.where(work == -jnp.inf, 1.0, 0.0)
    r_i = lax.broadcasted_iota(jnp.int32, (tm, tm), 0)
    c_i = lax.broadcasted_iota(jnp.int32, (tm, tm), 1)
    earlier = jnp.where(r_i < c_i, 1.0, 0.0).astype(BF16)
    base_ref[0] = run_ref[...]
    before = jnp.dot(sel.astype(BF16), earlier, preferred_element_type=F32) + run_ref[...]
    run_ref[...] = run_ref[...] + jnp.sum(sel, axis=1, keepdims=True)
    cnt_ref[...] = run_ref[...]

    ranks = [jnp.sum(jnp.where(sub == ix, before, 0.0), axis=0, keepdims=True) for ix in idxs]
    weights = [e / den for e in exps]
    ti_ref[...] = jnp.concatenate(idxs, axis=0)
    tr_ref[...] = jnp.concatenate(ranks, axis=0).astype(jnp.int32)
    meta = jnp.concatenate(weights + [ix.astype(F32) for ix in idxs]
                           + [jnp.zeros((LANES - 2 * TOP_K, tm), F32)], axis=0)
    h2_ref[:, PACKED:] = meta.T


def _outproj(a_p, a_s, b_p, b_s, x_p, x_s, mod_p, mod_s, norm_g, w_out_b, w_router, b_router):
    tm = ROW_TILE
    npt = x_p.shape[0] // tm
    nst = x_s.shape[0] // tm
    assert nst == 1
    n_tiles = npt + nst
    rows = n_tiles * tm
    tiles_per_batch = npt // mod_p.shape[0]
    last_p = npt - 1
    prow = lambda i: (jnp.minimum(i, last_p), 0)
    srow = lambda i: (0, 0)
    row = lambda i: (i, 0)
    col = lambda i: (0, i)
    const2 = lambda i: (0, 0)
    return pl.pallas_call(
        functools.partial(_outproj_kernel, n_prompt_tiles=npt),
        grid=(n_tiles,),
        in_specs=[pl.BlockSpec((tm, A_WIDTH), prow), pl.BlockSpec((tm, A_WIDTH), srow),
                  pl.BlockSpec((tm, B_WIDTH), prow), pl.BlockSpec((tm, B_WIDTH), srow),
                  pl.BlockSpec((tm, D_MODEL), prow), pl.BlockSpec((tm, D_MODEL), srow),
                  pl.BlockSpec((1, 6, D_MODEL), lambda i: (jnp.minimum(i, last_p) // tiles_per_batch, 0, 0)),
                  pl.BlockSpec((6, tm, D_MODEL), lambda i: (0, 0, 0)),
                  pl.BlockSpec((4, D_MODEL), const2),
                  pl.BlockSpec((D_MODEL, D_MODEL), const2),
                  pl.BlockSpec((D_MODEL, LANES), const2),
                  pl.BlockSpec((1, LANES), const2)],
        out_specs=[pl.BlockSpec((tm, D_MODEL), row), pl.BlockSpec((tm, ROW_WIDTH), row),
                   pl.BlockSpec((TOP_K, tm), col), pl.BlockSpec((TOP_K, tm), col),
                   pl.BlockSpec((N_EXPERTS, 1), const2),
                   pl.BlockSpec((1, N_EXPERTS, 1), lambda i: (i, 0, 0))],
        out_shape=[jax.ShapeDtypeStruct((rows, D_MODEL), F32), jax.ShapeDtypeStruct((rows, ROW_WIDTH), F32),
                   jax.ShapeDtypeStruct((TOP_K, rows), jnp.int32), jax.ShapeDtypeStruct((TOP_K, rows), jnp.int32),
                   jax.ShapeDtypeStruct((N_EXPERTS, 1), F32),
                   jax.ShapeDtypeStruct((n_tiles, N_EXPERTS, 1), F32)],
        scratch_shapes=[pltpu.VMEM((N_EXPERTS, 1), F32)],
        compiler_params=_cparams("arbitrary"),
        name="outproj_router",
    )(a_p, a_s, b_p, b_s, x_p, x_s, mod_p, mod_s, norm_g, w_out_b,
      jnp.pad(w_router, ((0, 0), (0, LANES - N_EXPERTS))).astype(BF16),
      jnp.pad(b_router.reshape(1, -1), ((0, 0), (0, LANES - N_EXPERTS))))


def _dispatch_kernel(pe_ref, cnt_ref, dest_ref, h_ref, xg_ref, zero_ref, sem, zsem, *, bm):
    tm = h_ref.shape[0]
    n_blocks = xg_ref.shape[0] // bm

    def zero_block(row0):
        return pltpu.make_async_copy(zero_ref, xg_ref.at[pl.ds(pl.multiple_of(row0, bm), bm), :], zsem)

    @pl.when(pl.program_id(0) == 0)
    def _():
        zero_ref[...] = jnp.zeros(zero_ref.shape, F32)
        first_unused = pe_ref[N_EXPERTS - 1] // bm
        for e in range(N_EXPERTS):
            @pl.when(cnt_ref[e] > 0)
            def _():
                zero_block(pe_ref[e] - bm).start()

        def start_unused(b, carry):
            zero_block(b * bm).start()
            return carry

        lax.fori_loop(first_unused, n_blocks, start_unused, 0)
        for e in range(N_EXPERTS):
            @pl.when(cnt_ref[e] > 0)
            def _():
                zero_block(pe_ref[e] - bm).wait()

        def wait_unused(b, carry):
            zero_block(b * bm).wait()
            return carry

        lax.fori_loop(first_unused, n_blocks, wait_unused, 0)

    for sub in range(tm // ROW_TILE):
        def body(t, carry, sub=sub):
            for j in range(TOP_K):
                d = dest_ref[(sub * TOP_K + j) * ROW_TILE + t]
                pltpu.make_async_copy(h_ref.at[pl.ds(sub * ROW_TILE + t, 1), :], xg_ref.at[pl.ds(d, 1), :],
                                      sem).start()
            return carry

        lax.fori_loop(0, ROW_TILE, body, 0)
    n = tm * TOP_K
    pltpu.make_async_copy(xg_ref.at[pl.ds(0, n), :], xg_ref.at[pl.ds(0, n), :], sem).wait()


def _dispatch(pad_end, cnt, dest_flat, h2, n_blocks):
    tm = ROW_TILE * DISPATCH_TILES
    bm = EXPERT_BLOCK
    rows = h2.shape[0]
    assert rows % tm == 0
    grid_spec = pltpu.PrefetchScalarGridSpec(
        num_scalar_prefetch=2,
        grid=(rows // tm,),
        in_specs=[pl.BlockSpec((tm * TOP_K,), lambda i, pe, cn: (i,), memory_space=pltpu.SMEM),
                  pl.BlockSpec((tm, ROW_WIDTH), lambda i, pe, cn: (i, 0))],
        out_specs=pl.BlockSpec(memory_space=pl.ANY),
        scratch_shapes=[pltpu.VMEM((bm, ROW_WIDTH), F32), pltpu.SemaphoreType.DMA(()), pltpu.SemaphoreType.DMA(())])
    return pl.pallas_call(
        functools.partial(_dispatch_kernel, bm=bm),
        grid_spec=grid_spec,
        out_shape=jax.ShapeDtypeStruct((n_blocks * bm, ROW_WIDTH), F32),
        compiler_params=_cparams("arbitrary"),
        name="moe_dispatch",
    )(pad_end, cnt, dest_flat, h2)


def _expert_kernel(be_ref, nu_ref, x_ref, wgu_ref, bgu_ref, wd_ref, bd_ref, y_ref, wgu_b, wd_b):
    i = pl.program_id(0)
    used = i < nu_ref[0]
    fresh = jnp.logical_or(i == 0, be_ref[i] != be_ref[jnp.maximum(i - 1, 0)])

    @pl.when(jnp.logical_and(used, fresh))
    def _():
        wgu_b[...] = wgu_ref[0].astype(BF16)
        wd_b[...] = wd_ref[0].astype(BF16)

    @pl.when(jnp.logical_not(used))
    def _():
        y_ref[...] = jnp.zeros(y_ref.shape, y_ref.dtype)

    @pl.when(used)
    def _():
        words = lax.bitcast_convert_type(x_ref[:, 0:PACKED], jnp.uint32)
        low = lax.bitcast_convert_type(lax.shift_left(words, jnp.uint32(16)), F32)
        high = lax.bitcast_convert_type(jnp.bitwise_and(words, jnp.uint32(0xFFFF0000)), F32)
        x = jnp.concatenate([low.astype(BF16), high.astype(BF16)], axis=1)
        gu = jnp.dot(x, wgu_b[...], preferred_element_type=F32) + bgu_ref[0]
        glu = jnp.minimum(gu[:, :D_EXPERT], SWIGLU_LIMIT)
        lin = jnp.clip(gu[:, D_EXPERT:], -SWIGLU_LIMIT, SWIGLU_LIMIT)
        hid = glu * jax.nn.sigmoid(SWIGLU_ALPHA * glu) * (lin + 1.0)
        y = jnp.dot(hid.astype(BF16), wd_b[...], preferred_element_type=F32) + bd_ref[0]
        meta = x_ref[:, PACKED:]
        me = be_ref[i].astype(F32)
        w = jnp.zeros((x_ref.shape[0], 1), F32)
        for j in range(TOP_K):
            w = w + jnp.where(meta[:, TOP_K + j:TOP_K + j + 1] == me, meta[:, j:j + 1], 0.0)
        y_ref[...] = (y * w).astype(y_ref.dtype)


def _experts(block_e, n_used, xg, w_gu_b, b_gu, w_down_b, b_down):
    bm = EXPERT_BLOCK
    n_blocks = xg.shape[0] // bm
    grid_spec = pltpu.PrefetchScalarGridSpec(
        num_scalar_prefetch=2,
        grid=(n_blocks,),
        in_specs=[pl.BlockSpec((bm, ROW_WIDTH), lambda i, be, nu: (jnp.minimum(i, nu[0] - 1), 0)),
                  pl.BlockSpec((1, D_MODEL, 2 * D_EXPERT), lambda i, be, nu: (be[i], 0, 0)),
                  pl.BlockSpec((1, 1, 2 * D_EXPERT), lambda i, be, nu: (be[i], 0, 0)),
                  pl.BlockSpec((1, D_EXPERT, D_MODEL), lambda i, be, nu: (be[i], 0, 0)),
                  pl.BlockSpec((1, 1, D_MODEL), lambda i, be, nu: (be[i], 0, 0))],
        out_specs=pl.BlockSpec((bm, D_MODEL), lambda i, be, nu: (i, 0)),
        scratch_shapes=[pltpu.VMEM((D_MODEL, 2 * D_EXPERT), BF16), pltpu.VMEM((D_EXPERT, D_MODEL), BF16)])
    return pl.pallas_call(
        _expert_kernel,
        grid_spec=grid_spec,
        out_shape=jax.ShapeDtypeStruct((xg.shape[0], D_MODEL), BF16),
        compiler_params=_cparams("arbitrary"),
        name="moe_experts",
    )(block_e, n_used, xg, w_gu_b, b_gu.reshape(N_EXPERTS, 1, -1), w_down_b, b_down.reshape(N_EXPERTS, 1, -1))


def _combine_kernel(row0_ref, col_ref, x1_ref, mp_ref, ms_ref, ng_ref, yg_ref,
                    yp_ref, ys_ref, rows_ref, g_ref, sem, *, n_prompt_tiles):
    i = pl.program_id(0)
    n_steps = pl.num_programs(0)
    tm = x1_ref.shape[0]
    is_p = i < n_prompt_tiles
    win = COMBINE_WINDOW
    n_win_rows = COMBINE_SLOTS * win
    slot = i % 2

    def fetch_windows(tile, buf):
        for s in range(COMBINE_SLOTS):
            row0 = pl.multiple_of(row0_ref[tile * COMBINE_SLOTS + s], COMBINE_ALIGN)
            pltpu.make_async_copy(yg_ref.at[pl.ds(row0, win), :], rows_ref.at[buf, pl.ds(s * win, win), :],
                                  sem.at[buf]).start()

    @pl.when(i == 0)
    def _():
        fetch_windows(0, 0)

    @pl.when(i + 1 < n_steps)
    def _():
        fetch_windows(i + 1, 1 - slot)

    pltpu.make_async_copy(yg_ref.at[pl.ds(0, n_win_rows), :], rows_ref.at[slot], sem.at[slot]).wait()
    col = col_ref[...]
    for c in range(n_win_rows // LANES):
        lane = lax.broadcasted_iota(jnp.int32, (tm, LANES), 1) + c * LANES
        g = jnp.zeros((tm, LANES), F32)
        for j in range(TOP_K):
            g = jnp.where(lane == col[:, j:j + 1], 1.0, g)
        g_ref[:, c * LANES:(c + 1) * LANES] = g.astype(BF16)
    f = jnp.dot(g_ref[...], rows_ref[slot], preferred_element_type=F32)

    g2 = jnp.where(is_p, mp_ref[0, 5:6, :], ms_ref[5])
    y = x1_ref[...] + g2 * _rms(f, ng_ref[3:4, :])

    @pl.when(is_p)
    def _():
        yp_ref[...] = y

    @pl.when(jnp.logical_not(is_p))
    def _():
        ys_ref[...] = y


def _combine(row0, col, x1, mod_p, mod_s, norm_g, yg, *, n_prompt_rows, n_sample_rows):
    tm = ROW_TILE
    npt = n_prompt_rows // tm
    assert n_sample_rows == tm and (COMBINE_SLOTS * COMBINE_WINDOW) % LANES == 0
    n_tiles = npt + 1
    tiles_per_batch = npt // mod_p.shape[0]
    last_p = npt - 1
    grid_spec = pltpu.PrefetchScalarGridSpec(
        num_scalar_prefetch=1,
        grid=(n_tiles,),
        in_specs=[pl.BlockSpec((tm, TOP_K), lambda i, r0: (i, 0)),
                  pl.BlockSpec((tm, D_MODEL), lambda i, r0: (i, 0)),
                  pl.BlockSpec((1, 6, D_MODEL), lambda i, r0: (jnp.minimum(i, last_p) // tiles_per_batch, 0, 0)),
                  pl.BlockSpec((6, tm, D_MODEL), lambda i, r0: (0, 0, 0)),
                  pl.BlockSpec((4, D_MODEL), lambda i, r0: (0, 0)),
                  pl.BlockSpec(memory_space=pl.ANY)],
        out_specs=[pl.BlockSpec((tm, D_MODEL), lambda i, r0: (jnp.minimum(i, last_p), 0)),
                   pl.BlockSpec((tm, D_MODEL), lambda i, r0: (0, 0))],
        scratch_shapes=[pltpu.VMEM((2, COMBINE_SLOTS * COMBINE_WINDOW, D_MODEL), BF16),
                        pltpu.VMEM((tm, COMBINE_SLOTS * COMBINE_WINDOW), BF16),
                        pltpu.SemaphoreType.DMA((2,))])
    return pl.pallas_call(
        functools.partial(_combine_kernel, n_prompt_tiles=npt),
        grid_spec=grid_spec,
        out_shape=[jax.ShapeDtypeStruct((n_prompt_rows, D_MODEL), F32),
                   jax.ShapeDtypeStruct((n_sample_rows, D_MODEL), F32)],
        compiler_params=_cparams("arbitrary"),
        name="moe_combine",
    )(row0, col, x1, mod_p, mod_s, norm_g, yg)


def _t5_bucket_np(dist):
    n = np.maximum(dist, 0)
    max_exact = NUM_BUCKETS // 2
    nf = np.maximum(n, 1).astype(np.float64)
    large = max_exact + (np.log(nf / max_exact) / math.log(MAX_DISTANCE / max_exact)
                         * (NUM_BUCKETS - max_exact)).astype(np.int32)
    large = np.minimum(large, NUM_BUCKETS - 1)
    return np.where(n < max_exact, n, large).astype(np.int32)


def _bias_from_dist(rel_bias, dist):
    onehot = jax.nn.one_hot(_t5_bucket_np(dist).reshape(-1), NUM_BUCKETS, dtype=F32)
    shifted = (rel_bias - rel_bias[NUM_BUCKETS - 1]).reshape(NUM_BUCKETS, -1)
    out = jnp.dot(onehot, shifted, precision=lax.Precision.HIGHEST).reshape(dist.shape + rel_bias.shape[1:])
    return jnp.where(jnp.asarray(dist >= 0)[..., None, None], out, -jnp.inf).astype(F32)


def _head_masked(t):
    h = t.shape[0]
    same = np.eye(h, dtype=bool)[:, None, None, None, :]
    full = jnp.where(same, t[..., None], -jnp.inf)
    return full.reshape(h * t.shape[1] * t.shape[2], t.shape[3] * h)


def kernel(x_prompt, x_sample, cache_k, cache_v, page_table, c_prompt, c_sample, w_ada, b_ada, norm_g, w_in,
           w_out, ln_v_g, ln_v_b, w_spatial, b_spatial, lam_params, subln_g, rel_bias, w_router, b_router,
           w_gu, b_gu, w_down, b_down):
    batch, seq, d = x_prompt.shape
    nb, ds = x_sample.shape[:2]
    n_pages = page_table.shape[1]
    past = n_pages * PAGE_SIZE
    rows_p = batch * seq
    rows_s = nb * ds
    assert rows_s == ROW_TILE and d == D_MODEL

    mod = _ada(jnp.concatenate([c_prompt, c_sample], axis=0), w_ada[0], b_ada[0])
    mod_p = mod[:batch].reshape(batch, 6, D_MODEL)
    mod_s = jnp.repeat(jnp.transpose(mod[batch:].reshape(nb, 6, D_MODEL), (1, 0, 2)), ds, axis=1)

    ng = norm_g[0]
    w_in_b = w_in[0].astype(BF16)
    w_out_b = w_out[0].astype(BF16)
    lng = ln_v_g[0].reshape(1, A_WIDTH)
    lnb = ln_v_b[0].reshape(1, A_WIDTH)
    grp = np.arange(A_WIDTH) // A_HEAD
    avg = jnp.asarray((grp[:, None] == grp[None, :]).astype(np.float32) / A_HEAD, BF16)
    ws_p = jnp.tril(w_spatial[0]).astype(BF16)
    bs_p = jnp.repeat(b_spatial[0].T, A_HEAD, axis=1)
    w_small = jnp.tril(w_spatial[0][:, :ds, :ds])
    same_batch = np.kron(np.eye(nb, dtype=np.float32), np.ones((ds, ds), np.float32))
    spread = np.tile(np.eye(ds, dtype=np.float32), (nb, 1))
    ws_s = (jnp.einsum('rt,gts,cs->grc', spread, w_small, spread, precision=lax.Precision.HIGHEST)
            * same_batch).astype(BF16)
    bs_s = jnp.tile(jnp.repeat(b_spatial[0][:, :ds].T, A_HEAD, axis=1), (nb, 1))

    xp2 = x_prompt.reshape(rows_p, D_MODEL)
    xs2 = x_sample.reshape(rows_s, D_MODEL)
    assert ATTN_TILE % ROW_TILE == 0 and seq % ATTN_TILE == 0
    a_p, va_p, q_p, k_p, v_p, kb_p, vt_p = _inproj(
        xp2, mod_p[:, 0:1], mod_p[:, 1:2], ng[0:1], w_in_b, lng, lnb, avg, ws_p, bs_p,
        tiles_per_mod=seq // ROW_TILE, chunk=CHUNK, q_dtype=BF16, q_scale=LOG2E / math.sqrt(B_HEAD), va_rows=CHUNK)
    a_s, va_s, q_s, k_s, v_s, _, _ = _inproj(
        xs2, mod_s[0:1], mod_s[1:2], ng[0:1], w_in_b, lng, lnb, avg, ws_s, bs_s,
        tiles_per_mod=1, chunk=rows_s, q_dtype=F32, q_scale=1.0, va_rows=rows_s)

    ti = np.arange(ATTN_TILE)
    dist_diag = ti[:, None] - ti[None, :]
    dist_prev = ti[:, None] - ti[None, ATTN_TILE - MAX_DISTANCE:] + ATTN_TILE
    assert dist_prev.min() >= 1 and (ti[:, None] - ti[None, :ATTN_TILE - MAX_DISTANCE] + ATTN_TILE).min() >= MAX_DISTANCE
    to_cols = lambda t: jnp.transpose(t, (2, 1, 3, 0)).reshape(B_HEADS, t.shape[1], 2 * ATTN_TILE) * LOG2E
    b_p = _attn_p(q_p, kb_p, vt_p, to_cols(_bias_from_dist(rel_bias, dist_diag)),
                  to_cols(_bias_from_dist(rel_bias, dist_prev)), lam_params[0], subln_g, batch=batch, seq=seq)

    qi = np.arange(ds)
    ki = np.arange(PAGE_SIZE)
    dist_l = PAGE_SIZE + qi[:, None] - ki[None, :]
    kn_i = np.arange(PAGE_SIZE // B_HEADS)
    dist_n = np.where(kn_i[None, :] < ds, qi[:, None] - kn_i[None, :], -1)
    to_rows = lambda t: jnp.transpose(t, (2, 3, 0, 1))
    bias_l = _head_masked(to_rows(_bias_from_dist(rel_bias, dist_l)))
    bias_n = _head_masked(to_rows(_bias_from_dist(rel_bias, dist_n)))
    maskb = _head_masked(jnp.zeros((B_HEADS, 2, ds, PAGE_SIZE), F32))
    pad_keys = lambda t: jnp.pad(t.reshape(nb, ds * B_HEADS, B_VDIM), ((0, 0), (0, PAGE_SIZE - ds * B_HEADS), (0, 0)))
    n_phys = cache_k.shape[0]
    b_s = _attn_s(page_table, q_s, pad_keys(k_s), pad_keys(v_s), maskb, bias_l, bias_n, lam_params[0], subln_g,
                  cache_k.reshape(n_phys * PAGE_SIZE * B_HEADS, B_VDIM),
                  cache_v.reshape(n_phys * PAGE_SIZE * B_HEADS, B_VDIM), ds=ds)

    x1, h2, top_i, top_r, counts, tile_base = _outproj(
        a_p, a_s, b_p, b_s, xp2, xs2, mod_p, mod_s, ng, w_out_b, w_router[0], b_router[0])

    bm = EXPERT_BLOCK
    rows = rows_p + rows_s
    n_blocks = rows * TOP_K // bm + N_EXPERTS + 1
    cnt = counts[:, 0].astype(jnp.int32)
    padded = (cnt + bm - 1) // bm * bm
    pad_end = jnp.cumsum(padded)
    pad_start = pad_end - padded
    experts = jnp.arange(N_EXPERTS, dtype=jnp.int32)
    chosen = top_i[..., None] == experts
    dest = jnp.sum(jnp.where(chosen, pad_start, 0), axis=-1) + top_r
    n_tiles = rows // ROW_TILE
    win = COMBINE_WINDOW
    assert win % COMBINE_ALIGN == 0
    assert (ROW_TILE * TOP_K + N_EXPERTS * (COMBINE_ALIGN - 1)) // win + N_EXPERTS <= COMBINE_SLOTS
    base = tile_base[:, :, 0].astype(jnp.int32)
    first = pad_start[None, :] + base
    win0 = first // COMBINE_ALIGN * COMBINE_ALIGN
    sent = jnp.concatenate([base[1:], cnt[None, :]], axis=0) - base
    n_win = jnp.where(sent > 0, (first - win0 + sent + win - 1) // win, 0)
    slot_end = jnp.cumsum(n_win, axis=1)
    slot_start = slot_end - n_win
    slots = jnp.arange(COMBINE_SLOTS, dtype=jnp.int32)
    slot_e = jnp.minimum(jnp.sum((slot_end[:, None, :] <= slots[None, :, None]).astype(jnp.int32), axis=2),
                         N_EXPERTS - 1)
    pick = slot_e[..., None] == experts
    slot_row0 = (jnp.sum(jnp.where(pick, win0[:, None, :], 0), axis=2)
                 + win * (slots[None, :] - jnp.sum(jnp.where(pick, slot_start[:, None, :], 0), axis=2)))
    slot_row0 = jnp.where(slots[None, :] < slot_end[:, -1:], slot_row0, 0)
    local = dest - jnp.sum(jnp.where(chosen, jnp.repeat(win0, ROW_TILE, axis=0), 0), axis=-1)
    col = jnp.sum(jnp.where(chosen, jnp.repeat(slot_start, ROW_TILE, axis=0), 0), axis=-1) * win + local
    dest = jnp.transpose(dest.reshape(TOP_K, n_tiles, ROW_TILE), (1, 0, 2)).reshape(-1)
    blk_start = jnp.arange(n_blocks, dtype=jnp.int32) * bm
    block_e = jnp.minimum(jnp.sum((pad_end[None, :] <= blk_start[:, None]).astype(jnp.int32), axis=1),
                          N_EXPERTS - 1)
    n_used = (pad_end[-1:] // bm).astype(jnp.int32)

    xg = _dispatch(pad_end.astype(jnp.int32), cnt, dest, h2, n_blocks)
    yg = _experts(block_e, n_used, xg, w_gu[0], b_gu[0], w_down[0], b_down[0])
    y_p, y_s = _combine(slot_row0.reshape(-1), col.T, x1, mod_p, mod_s, ng, yg,
                        n_prompt_rows=rows_p, n_sample_rows=rows_s)

    return (y_p.reshape(batch, seq, D_MODEL),
            y_s.reshape(nb, ds, D_MODEL),
            k_p.reshape(batch, seq, 1, B_HEADS, B_VDIM),
            v_p.reshape(batch, seq, 1, B_HEADS, B_VDIM),
            k_s.reshape(nb, ds, 1, B_HEADS, B_VDIM),
            v_s.reshape(nb, ds, 1, B_HEADS, B_VDIM),
            va_p.reshape(batch, CHUNK, 1, A_WIDTH),
            va_s.reshape(nb, ds, 1, A_WIDTH))
```

```python
import functools
import math

import numpy as np
import jax
import jax.numpy as jnp
from jax import lax
from jax.experimental import pallas as pl
from jax.experimental.pallas import tpu as pltpu

F32 = jnp.float32
BF16 = jnp.bfloat16

D_MODEL = 1024
A_WIDTH = 512
A_HEAD = 64
A_GROUPS = 8
CHUNK = 128
B_WIDTH = 512
B_HEAD = 64
B_VDIM = 128
B_HEADS = 4
IN_WIDTH = 2 * A_WIDTH + 3 * B_WIDTH
NUM_BUCKETS = 32
MAX_DISTANCE = 128
PAGE_SIZE = 128
N_EXPERTS = 32
TOP_K = 4
D_EXPERT = 1024
SWIGLU_LIMIT = 7.0
SWIGLU_ALPHA = 1.702
NORM_EPS = 1e-6
LAM_INIT = 0.8 - 0.6 * math.exp(-0.3 * 0)

ROW_TILE = 256
ATTN_TILE = 512
ATTN_BATCHES = 1
PAGES_PER_STEP = 32
LOG2E = math.log2(math.e)
LANES = 128
PACKED = D_MODEL // 2
ROW_WIDTH = PACKED + LANES
EXPERT_BLOCK = 512
EXPERT_HIDDEN_SLABS = 2
DISPATCH_TILES = 13
COMBINE_ALIGN = 16
COMBINE_WINDOW = 16
COMBINE_SLOTS = 128
VMEM_LIMIT = 56 * 1024 * 1024


def _cparams(*sem):
    return pltpu.CompilerParams(dimension_semantics=sem, vmem_limit_bytes=VMEM_LIMIT)


def _rms(x, g):
    return x * lax.rsqrt(jnp.mean(x * x, axis=-1, keepdims=True) + NORM_EPS) * g


def _ada_kernel(c_ref, w_ref, b_ref, o_ref):
    s = jax.nn.silu(c_ref[...]).astype(BF16)
    o_ref[...] = jnp.dot(s, w_ref[...].astype(BF16), preferred_element_type=F32) + b_ref[...]


def _ada(c_all, w_ada, b_ada):
    n = c_all.shape[0]
    tn = 1024
    return pl.pallas_call(
        _ada_kernel,
        grid=(6 * D_MODEL // tn,),
        in_specs=[pl.BlockSpec((n, D_MODEL), lambda j: (0, 0)),
                  pl.BlockSpec((D_MODEL, tn), lambda j: (0, j)),
                  pl.BlockSpec((1, tn), lambda j: (0, j))],
        out_specs=pl.BlockSpec((n, tn), lambda j: (0, j)),
        out_shape=jax.ShapeDtypeStruct((n, 6 * D_MODEL), F32),
        compiler_params=_cparams("arbitrary"),
        name="ada",
    )(c_all, w_ada, b_ada.reshape(1, -1))


def _inproj_kernel(x_ref, sh_ref, sc_ref, ng_ref, w_ref, lng_ref, lnb_ref, avg_ref, ws_ref, bs_ref,
                   a_ref, va_ref, q_ref, k_ref, v_ref, kb_ref, vt_ref, *, chunk, q_scale):
    tm = x_ref.shape[0]
    va_rows = va_ref.shape[0]
    h = (_rms(x_ref[...], ng_ref[...]) * (1.0 + sc_ref[0]) + sh_ref[0]).astype(BF16)

    u = jax.nn.gelu(jnp.dot(h, w_ref[:, 0:A_WIDTH], preferred_element_type=F32))
    gv = jax.nn.gelu(jnp.dot(h, w_ref[:, A_WIDTH:2 * A_WIDTH], preferred_element_type=F32))

    avg = avg_ref[...]

    def group_mean(t):
        return jnp.dot(t.astype(BF16), avg, preferred_element_type=F32)

    xc = gv - group_mean(gv)
    va = xc * lax.rsqrt(group_mean(xc * xc) + NORM_EPS) * lng_ref[...] + lnb_ref[...]
    va_ref[...] = va[tm - va_rows:, :]

    vab = va.astype(BF16)
    n_chunks = tm // chunk
    lane = lax.broadcasted_iota(jnp.int32, (chunk, n_chunks * 128), 1) % 128
    for pair in range(A_GROUPS // 2):
        c0 = pair * 128
        vp = jnp.concatenate([vab[c * chunk:(c + 1) * chunk, c0:c0 + 128] for c in range(n_chunks)], axis=1)
        lo_half = jnp.where(lane < A_HEAD, vp, jnp.zeros_like(vp))
        hi_half = jnp.where(lane >= A_HEAD, vp, jnp.zeros_like(vp))
        s = (jnp.dot(ws_ref[2 * pair], lo_half, preferred_element_type=F32)
             + jnp.dot(ws_ref[2 * pair + 1], hi_half, preferred_element_type=F32))
        for c in range(n_chunks):
            r0 = c * chunk
            a = u[r0:r0 + chunk, c0:c0 + 128] * (s[:, c * 128:(c + 1) * 128] + bs_ref[:, c0:c0 + 128])
            a_ref[r0:r0 + chunk, c0:c0 + 128] = a.astype(a_ref.dtype)

    q = jnp.dot(h, w_ref[:, 2 * A_WIDTH:2 * A_WIDTH + B_WIDTH], preferred_element_type=F32)
    q_ref[...] = (q * q_scale).astype(q_ref.dtype)
    k = jnp.dot(h, w_ref[:, 2 * A_WIDTH + B_WIDTH:2 * A_WIDTH + 2 * B_WIDTH], preferred_element_type=F32)
    kb_ref[...] = k.astype(BF16)
    v = jnp.dot(h, w_ref[:, 2 * A_WIDTH + 2 * B_WIDTH:IN_WIDTH], preferred_element_type=F32)
    for hh in range(B_HEADS):
        c0 = hh * B_VDIM
        k_ref[pl.ds(hh, tm, stride=B_HEADS), :] = k[:, c0:c0 + B_VDIM]
        v_ref[pl.ds(hh, tm, stride=B_HEADS), :] = v[:, c0:c0 + B_VDIM]
        vt_ref[0, hh, 0] = v[:, c0:c0 + B_VDIM].T.astype(BF16)


def _inproj(x2, sh, sc, ng, w_in_b, lng, lnb, avg, ws, bs, *, tiles_per_mod, chunk, q_dtype, q_scale, va_rows):
    rows = x2.shape[0]
    tm = ROW_TILE
    n_tiles = rows // tm
    mod_rows = sh.shape[1]
    row = lambda i: (i, 0)
    const2 = lambda i: (0, 0)
    mod_map = lambda i: (i // tiles_per_mod, 0, 0)
    out_w = lambda w, dt: jax.ShapeDtypeStruct((rows, w), dt)
    head_rows = jax.ShapeDtypeStruct((rows * B_HEADS, B_VDIM), F32)
    head_rows_spec = pl.BlockSpec((tm * B_HEADS, B_VDIM), row)
    vt_shape = jax.ShapeDtypeStruct((n_tiles // tiles_per_mod, B_HEADS, tiles_per_mod, B_VDIM, tm), BF16)
    vt_spec = pl.BlockSpec((1, B_HEADS, 1, B_VDIM, tm), lambda i: (i // tiles_per_mod, 0, i % tiles_per_mod, 0, 0))
    return pl.pallas_call(
        functools.partial(_inproj_kernel, chunk=chunk, q_scale=q_scale),
        grid=(rows // tm,),
        in_specs=[pl.BlockSpec((tm, D_MODEL), row),
                  pl.BlockSpec((1, mod_rows, D_MODEL), mod_map),
                  pl.BlockSpec((1, mod_rows, D_MODEL), mod_map),
                  pl.BlockSpec((1, D_MODEL), const2),
                  pl.BlockSpec((D_MODEL, IN_WIDTH), const2),
                  pl.BlockSpec((1, A_WIDTH), const2),
                  pl.BlockSpec((1, A_WIDTH), const2),
                  pl.BlockSpec((A_WIDTH, A_WIDTH), const2),
                  pl.BlockSpec((A_GROUPS, chunk, chunk), lambda i: (0, 0, 0)),
                  pl.BlockSpec((chunk, A_WIDTH), const2)],
        out_specs=[pl.BlockSpec((tm, A_WIDTH), row),
                   pl.BlockSpec((va_rows, A_WIDTH), lambda i: (i // tiles_per_mod, 0)),
                   pl.BlockSpec((tm, B_WIDTH), row), head_rows_spec, head_rows_spec,
                   pl.BlockSpec((tm, B_WIDTH), row), vt_spec],
        out_shape=[out_w(A_WIDTH, BF16),
                   jax.ShapeDtypeStruct((n_tiles // tiles_per_mod * va_rows, A_WIDTH), F32),
                   out_w(B_WIDTH, q_dtype), head_rows, head_rows, out_w(B_WIDTH, BF16), vt_shape],
        compiler_params=_cparams("arbitrary"),
        name="inproj",
    )(x2, sh, sc, ng, w_in_b, lng, lnb, avg, ws, bs)


def _stack_halves(q):
    lane = lax.broadcasted_iota(jnp.int32, q.shape, 1)
    zero = jnp.zeros_like(q)
    return jnp.concatenate([jnp.where(lane < B_HEAD, q, zero), jnp.where(lane >= B_HEAD, q, zero)], axis=0)


def _lambda(lam_ref):
    lp = lam_ref[...]
    return (jnp.exp(jnp.sum(lp[0:1] * lp[1:2], axis=-1, keepdims=True))
            - jnp.exp(jnp.sum(lp[2:3] * lp[3:4], axis=-1, keepdims=True)) + LAM_INIT)


def _diff_finish(acc, l, n, lam, g):
    o = acc[:n] / l[:n] - lam * (acc[n:] / l[n:])
    return _rms(o, g) * (1.0 - LAM_INIT)


def _attn_p_kernel(q_ref, k_ref, vt_ref, diag_ref, prev_ref, lam_ref, g_ref, o_ref,
                   acc_ref, m_ref, l_ref, s_ref, tmax_ref, p_ref, alpha_ref, *, tq, tk):
    qi = pl.program_id(2)
    nbb = q_ref.shape[0]
    streams = [(bb, c) for bb in range(nbb) for c in range(2)]
    q_half = []
    for bb in range(nbb):
        q = q_ref[bb]
        lane = lax.broadcasted_iota(jnp.int32, q.shape, 1)
        zero = jnp.zeros_like(q)
        q_half += [jnp.where(lane < B_HEAD, q, zero), jnp.where(lane >= B_HEAD, q, zero)]
    m_ref[...] = jnp.full(m_ref.shape, -jnp.inf, F32)
    l_ref[...] = jnp.zeros(l_ref.shape, F32)
    acc_ref[...] = jnp.zeros(acc_ref.shape, F32)
    p_ref[1] = jnp.zeros(p_ref.shape[1:], BF16)
    alpha_ref[1] = jnp.ones(alpha_ref.shape[1:], F32)

    def scores(j):
        out = []
        for bb in range(nbb):
            k = k_ref[bb, pl.ds(pl.multiple_of(j * tk, tk), tk), :]
            out += [lax.dot_general(k, q_half[2 * bb + c], (((1,), (1,)), ((), ())), preferred_element_type=F32)
                    for c in range(2)]
        return out

    near_keys = prev_ref.shape[1]

    def keep_scores(j, kind, ss):
        slot = j % 2
        for u, (bb, c) in enumerate(streams):
            s = ss[u]
            if kind == 0:
                s = s + diag_ref[0, :, c * tq:(c + 1) * tq]
            elif kind == 1:
                s = jnp.concatenate([s[:tk - near_keys], s[tk - near_keys:] + prev_ref[0, :, c * tq:(c + 1) * tq]],
                                    axis=0)
            s_ref[slot, u] = s
            tmax_ref[slot, u] = jnp.max(s, axis=0, keepdims=True)

    def softmax(j):
        slot = j % 2
        for u in range(len(streams)):
            m_old = m_ref[u]
            m_new = jnp.maximum(m_old, tmax_ref[slot, u])
            alpha = jnp.exp2(m_old - m_new)
            p = jnp.exp2(s_ref[slot, u] - m_new)
            l_ref[u] = alpha * l_ref[u] + jnp.sum(p, axis=0, keepdims=True)
            m_ref[u] = m_new
            p_ref[slot, u] = p.astype(BF16)
            alpha_ref[slot, u] = alpha

    def pv(j):
        slot = j % 2
        pieces = tk // vt_ref.shape[-1]
        first = jnp.maximum(j, 0) * pieces
        for u, (bb, c) in enumerate(streams):
            add = None
            for r in range(pieces):
                w = vt_ref.shape[-1]
                t = jnp.dot(vt_ref[bb, 0, first + r], p_ref[slot, u, r * w:(r + 1) * w, :],
                            preferred_element_type=F32)
                add = t if add is None else add + t
            acc_ref[u] = alpha_ref[slot, u] * acc_ref[u] + add

    def iteration(j, next_kind):
        ss = scores(j + 1)
        pv(j - 1)
        softmax(j)
        keep_scores(j + 1, next_kind, ss)

    def far_pair(jj, carry):
        iteration(2 * jj, None)
        iteration(2 * jj + 1, None)
        return carry

    for kind, cond in ((0, qi == 0), (1, qi == 1), (None, qi >= 2)):
        @pl.when(cond)
        def _(kind=kind):
            keep_scores(0, kind, scores(0))

    n_far = jnp.maximum(qi - 2, 0)
    lax.fori_loop(0, n_far // 2, far_pair, 0)

    @pl.when(n_far % 2 == 1)
    def _():
        iteration(n_far - 1, None)

    @pl.when(qi >= 2)
    def _():
        iteration(qi - 2, 1)

    @pl.when(qi >= 1)
    def _():
        iteration(qi - 1, 0)

    pv(qi - 1)
    softmax(qi)
    pv(qi)

    lam = _lambda(lam_ref)
    for bb in range(nbb):
        o = acc_ref[2 * bb] / l_ref[2 * bb] - lam * (acc_ref[2 * bb + 1] / l_ref[2 * bb + 1])
        y = o * lax.rsqrt(jnp.mean(o * o, axis=0, keepdims=True) + NORM_EPS) * (1.0 - LAM_INIT)
        o_ref[bb] = (y.T * g_ref[...]).astype(o_ref.dtype)


def _attn_p(qb, kb, vt, bias_diag, bias_prev, lam_params, subln_g, *, batch, seq):
    tq = tk = ATTN_TILE
    nq = seq // tq
    nbb = ATTN_BATCHES
    ns = 2 * nbb
    assert batch % nbb == 0
    out = pl.pallas_call(
        functools.partial(_attn_p_kernel, tq=tq, tk=tk),
        grid=(batch // nbb, B_HEADS, nq),
        in_specs=[pl.BlockSpec((nbb, tq, B_VDIM), lambda b, h, i: (b, i, h)),
                  pl.BlockSpec((nbb, seq, B_VDIM), lambda b, h, i: (b, 0, h)),
                  pl.BlockSpec((nbb, 1, seq // ROW_TILE, B_VDIM, ROW_TILE), lambda b, h, i: (b, h, 0, 0, 0)),
                  pl.BlockSpec((1, tk, 2 * tq), lambda b, h, i: (h, 0, 0)),
                  pl.BlockSpec((1, bias_prev.shape[1], 2 * tq), lambda b, h, i: (h, 0, 0)),
                  pl.BlockSpec((4, B_HEAD), lambda b, h, i: (0, 0)),
                  pl.BlockSpec((1, B_VDIM), lambda b, h, i: (0, 0))],
        out_specs=pl.BlockSpec((nbb, tq, B_VDIM), lambda b, h, i: (b, i, h)),
        out_shape=jax.ShapeDtypeStruct((batch, seq, B_WIDTH), BF16),
        scratch_shapes=[pltpu.VMEM((ns, B_VDIM, tq), F32),
                        pltpu.VMEM((ns, 1, tq), F32),
                        pltpu.VMEM((ns, 1, tq), F32),
                        pltpu.VMEM((2, ns, tk, tq), F32),
                        pltpu.VMEM((2, ns, 1, tq), F32),
                        pltpu.VMEM((2, ns, tk, tq), BF16),
                        pltpu.VMEM((2, ns, 1, tq), F32)],
        compiler_params=_cparams("arbitrary", "arbitrary", "arbitrary"),
        name="attn_prompt",
    )(qb.reshape(batch, seq, B_WIDTH), kb.reshape(batch, seq, B_WIDTH), vt, bias_diag, bias_prev, lam_params,
      subln_g)
    return out.reshape(batch * seq, B_WIDTH)


def _attn_s_kernel(pt_ref, q_ref, kn_ref, vn_ref, maskb_ref, biasl_ref, biasn_ref, lam_ref, g_ref, *rest,
                   npages, ds):
    del pt_ref
    k_refs = rest[:npages]
    v_refs = rest[npages:2 * npages]
    o_ref = rest[2 * npages]
    acc_ref, m_ref, l_ref = rest[2 * npages + 1:]
    g = pl.program_id(1)
    last = g == pl.num_programs(1) - 1

    @pl.when(g == 0)
    def _():
        m_ref[...] = jnp.full(m_ref.shape, -jnp.inf, F32)
        l_ref[...] = jnp.zeros(l_ref.shape, F32)
        acc_ref[...] = jnp.zeros(acc_ref.shape, F32)

    q = q_ref[...] * (1.0 / math.sqrt(B_HEAD))
    qall = jnp.concatenate([_stack_halves(q[:, h * B_VDIM:(h + 1) * B_VDIM]) for h in range(B_HEADS)],
                           axis=0).astype(BF16)

    def process(k_blocks, v_blocks, biases):
        s = jnp.concatenate(
            [lax.dot_general(qall, kb.astype(BF16), (((1,), (1,)), ((), ())), preferred_element_type=F32) + bb
             for kb, bb in zip(k_blocks, biases)], axis=1)
        m_old = m_ref[...]
        m_new = jnp.maximum(m_old, jnp.max(s, axis=-1, keepdims=True))
        alpha = jnp.exp(m_old - m_new)
        pr = jnp.exp(s - m_new)
        l_ref[...] = alpha * l_ref[...] + jnp.sum(pr, axis=-1, keepdims=True)
        prb = pr.astype(BF16)
        pv = None
        off = 0
        for vb in v_blocks:
            n = vb.shape[0]
            t = jnp.dot(prb[:, off:off + n], vb.astype(BF16), preferred_element_type=F32)
            pv = t if pv is None else pv + t
            off += n
        acc_ref[...] = alpha * acc_ref[...] + pv
        m_ref[...] = m_new

    maskb = maskb_ref[...]
    newest = jnp.where(last, biasl_ref[...], maskb)
    process([k_refs[p][...] for p in range(npages)], [v_refs[p][...] for p in range(npages)],
            [maskb] * (npages - 1) + [newest])

    @pl.when(last)
    def _():
        process([kn_ref[0]], [vn_ref[0]], [biasn_ref[...]])
        lam = _lambda(lam_ref)
        acc = acc_ref[...]
        l = l_ref[...]
        for h in range(B_HEADS):
            r0 = h * 2 * ds
            o_ref[:, h * B_VDIM:(h + 1) * B_VDIM] = _diff_finish(
                acc[r0:r0 + 2 * ds], l[r0:r0 + 2 * ds], ds, lam, g_ref[...])


def _attn_s(page_table, qs, kn, vn, maskb, bias_last, bias_new, lam_params, subln_g, cache_k2, cache_v2, *, ds):
    nb, n_pages = page_table.shape
    npg = PAGES_PER_STEP
    steps = n_pages // npg
    page_rows = PAGE_SIZE * B_HEADS
    nrow = B_HEADS * 2 * ds

    def page_spec(p):
        return pl.BlockSpec((page_rows, B_VDIM), lambda b, g, pt, p=p: (pt[b, g * npg + p], 0))

    const2 = lambda b, g, pt: (0, 0)
    grid_spec = pltpu.PrefetchScalarGridSpec(
        num_scalar_prefetch=1,
        grid=(nb, steps),
        in_specs=[pl.BlockSpec((ds, B_WIDTH), lambda b, g, pt: (b, 0)),
                  pl.BlockSpec((1, PAGE_SIZE, B_VDIM), lambda b, g, pt: (b, 0, 0)),
                  pl.BlockSpec((1, PAGE_SIZE, B_VDIM), lambda b, g, pt: (b, 0, 0)),
                  pl.BlockSpec((nrow, page_rows), const2),
                  pl.BlockSpec((nrow, page_rows), const2),
                  pl.BlockSpec((nrow, PAGE_SIZE), const2),
                  pl.BlockSpec((4, B_HEAD), const2),
                  pl.BlockSpec((1, B_VDIM), const2)]
                 + [page_spec(p) for p in range(npg)] + [page_spec(p) for p in range(npg)],
        out_specs=pl.BlockSpec((ds, B_WIDTH), lambda b, g, pt: (b, 0)),
        scratch_shapes=[pltpu.VMEM((nrow, B_VDIM), F32),
                        pltpu.VMEM((nrow, 1), F32),
                        pltpu.VMEM((nrow, 1), F32)])
    return pl.pallas_call(
        functools.partial(_attn_s_kernel, npages=npg, ds=ds),
        grid_spec=grid_spec,
        out_shape=jax.ShapeDtypeStruct((nb * ds, B_WIDTH), F32),
        compiler_params=_cparams("arbitrary", "arbitrary"),
        name="attn_sample",
    )(page_table, qs, kn, vn, maskb, bias_last, bias_new, lam_params, subln_g,
      *([cache_k2] * npg), *([cache_v2] * npg))


def _outproj_kernel(ap_ref, as_ref, bp_ref, bs_ref, xp_ref, xs_ref, mp_ref, ms_ref, ng_ref, wo_ref,
                    wr_ref, br_ref, x1_ref, h2_ref, ti_ref, tr_ref, cnt_ref, base_ref, run_ref,
                    *, n_prompt_tiles):
    i = pl.program_id(0)
    tm = xp_ref.shape[0]
    is_p = i < n_prompt_tiles

    @pl.when(i == 0)
    def _():
        run_ref[...] = jnp.zeros(run_ref.shape, F32)

    a = jnp.where(is_p, ap_ref[...], as_ref[...])
    b = jnp.where(is_p, bp_ref[...], bs_ref[...].astype(BF16))
    x = jnp.where(is_p, xp_ref[...], xs_ref[...])
    g1 = jnp.where(is_p, mp_ref[0, 2:3, :], ms_ref[2])
    sh2 = jnp.where(is_p, mp_ref[0, 3:4, :], ms_ref[3])
    sc2 = jnp.where(is_p, mp_ref[0, 4:5, :], ms_ref[4])

    mix = (jnp.dot(a, wo_ref[0:A_WIDTH, :], preferred_element_type=F32)
           + jnp.dot(b, wo_ref[A_WIDTH:, :], preferred_element_type=F32))
    x1 = x + g1 * _rms(mix, ng_ref[1:2, :])
    x1_ref[...] = x1
    h2 = _rms(x1, ng_ref[2:3, :]) * (1.0 + sc2) + sh2
    low = lax.bitcast_convert_type(h2[:, 0:PACKED].astype(BF16).astype(F32), jnp.uint32)
    high = lax.bitcast_convert_type(h2[:, PACKED:].astype(BF16).astype(F32), jnp.uint32)
    words = jnp.bitwise_or(jnp.bitwise_and(high, jnp.uint32(0xFFFF0000)), lax.shift_right_logical(low, jnp.uint32(16)))
    h2_ref[:, 0:PACKED] = lax.bitcast_convert_type(words, F32)

    logits = jnp.dot(h2.astype(BF16), wr_ref[...], preferred_element_type=F32) + br_ref[...]
    work = logits.T[0:N_EXPERTS, :]
    sub = lax.broadcasted_iota(jnp.int32, work.shape, 0)
    vals, idxs = [], []
    for _ in range(TOP_K):
        mx = jnp.max(work, axis=0, keepdims=True)
        ix = jnp.min(jnp.where(work == mx, sub, N_EXPERTS), axis=0, keepdims=True)
        vals.append(mx)
        idxs.append(ix)
        work = jnp.where(sub == ix, -jnp.inf, work)
    exps = [jnp.exp(v - vals[0]) for v in vals]
    den = exps[0] + exps[1] + exps[2] + exps[3]

    sel = jnp.where(work == -jnp.inf, 1.0, 0.0)
    r_i = lax.broadcasted_iota(jnp.int32, (tm, tm), 0)
    c_i = lax.broadcasted_iota(jnp.int32, (tm, tm), 1)
    earlier = jnp.where(r_i < c_i, 1.0, 0.0).astype(BF16)
    base_ref[0] = run_ref[...]
    before = jnp.dot(sel.astype(BF16), earlier, preferred_element_type=F32) + run_ref[...]
    run_ref[...] = run_ref[...] + jnp.sum(sel, axis=1, keepdims=True)
    cnt_ref[...] = run_ref[...]

    ranks = [jnp.sum(jnp.where(sub == ix, before, 0.0), axis=0, keepdims=True) for ix in idxs]
    weights = [e / den for e in exps]
    ti_ref[...] = jnp.concatenate(idxs, axis=0)
    tr_ref[...] = jnp.concatenate(ranks, axis=0).astype(jnp.int32)
    meta = jnp.concatenate(weights + [ix.astype(F32) for ix in idxs]
                           + [jnp.zeros((LANES - 2 * TOP_K, tm), F32)], axis=0)
    h2_ref[:, PACKED:] = meta.T


def _outproj(a_p, a_s, b_p, b_s, x_p, x_s, mod_p, mod_s, norm_g, w_out_b, w_router, b_router):
    tm = ROW_TILE
    npt = x_p.shape[0] // tm
    nst = x_s.shape[0] // tm
    assert nst == 1
    n_tiles = npt + nst
    rows = n_tiles * tm
    tiles_per_batch = npt // mod_p.shape[0]
    last_p = npt - 1
    prow = lambda i: (jnp.minimum(i, last_p), 0)
    srow = lambda i: (0, 0)
    row = lambda i: (i, 0)
    col = lambda i: (0, i)
    const2 = lambda i: (0, 0)
    return pl.pallas_call(
        functools.partial(_outproj_kernel, n_prompt_tiles=npt),
        grid=(n_tiles,),
        in_specs=[pl.BlockSpec((tm, A_WIDTH), prow), pl.BlockSpec((tm, A_WIDTH), srow),
                  pl.BlockSpec((tm, B_WIDTH), prow), pl.BlockSpec((tm, B_WIDTH), srow),
                  pl.BlockSpec((tm, D_MODEL), prow), pl.BlockSpec((tm, D_MODEL), srow),
                  pl.BlockSpec((1, 6, D_MODEL), lambda i: (jnp.minimum(i, last_p) // tiles_per_batch, 0, 0)),
                  pl.BlockSpec((6, tm, D_MODEL), lambda i: (0, 0, 0)),
                  pl.BlockSpec((4, D_MODEL), const2),
                  pl.BlockSpec((D_MODEL, D_MODEL), const2),
                  pl.BlockSpec((D_MODEL, LANES), const2),
                  pl.BlockSpec((1, LANES), const2)],
        out_specs=[pl.BlockSpec((tm, D_MODEL), row), pl.BlockSpec((tm, ROW_WIDTH), row),
                   pl.BlockSpec((TOP_K, tm), col), pl.BlockSpec((TOP_K, tm), col),
                   pl.BlockSpec((N_EXPERTS, 1), const2),
                   pl.BlockSpec((1, N_EXPERTS, 1), lambda i: (i, 0, 0))],
        out_shape=[jax.ShapeDtypeStruct((rows, D_MODEL), F32), jax.ShapeDtypeStruct((rows, ROW_WIDTH), F32),
                   jax.ShapeDtypeStruct((TOP_K, rows), jnp.int32), jax.ShapeDtypeStruct((TOP_K, rows), jnp.int32),
                   jax.ShapeDtypeStruct((N_EXPERTS, 1), F32),
                   jax.ShapeDtypeStruct((n_tiles, N_EXPERTS, 1), F32)],
        scratch_shapes=[pltpu.VMEM((N_EXPERTS, 1), F32)],
        compiler_params=_cparams("arbitrary"),
        name="outproj_router",
    )(a_p, a_s, b_p, b_s, x_p, x_s, mod_p, mod_s, norm_g, w_out_b,
      jnp.pad(w_router, ((0, 0), (0, LANES - N_EXPERTS))).astype(BF16),
      jnp.pad(b_router.reshape(1, -1), ((0, 0), (0, LANES - N_EXPERTS))))


def _dispatch_kernel(pe_ref, cnt_ref, dest_ref, h_ref, xg_ref, zero_ref, sem, zsem, *, bm):
    tm = h_ref.shape[0]
    n_blocks = xg_ref.shape[0] // bm

    def zero_block(row0):
        return pltpu.make_async_copy(zero_ref, xg_ref.at[pl.ds(pl.multiple_of(row0, bm), bm), :], zsem)

    @pl.when(pl.program_id(0) == 0)
    def _():
        zero_ref[...] = jnp.zeros(zero_ref.shape, F32)
        first_unused = pe_ref[N_EXPERTS - 1] // bm
        for e in range(N_EXPERTS):
            @pl.when(cnt_ref[e] > 0)
            def _():
                zero_block(pe_ref[e] - bm).start()

        def start_unused(b, carry):
            zero_block(b * bm).start()
            return carry

        lax.fori_loop(first_unused, n_blocks, start_unused, 0)
        for e in range(N_EXPERTS):
            @pl.when(cnt_ref[e] > 0)
            def _():
                zero_block(pe_ref[e] - bm).wait()

        def wait_unused(b, carry):
            zero_block(b * bm).wait()
            return carry

        lax.fori_loop(first_unused, n_blocks, wait_unused, 0)

    for sub in range(tm // ROW_TILE):
        def body(t, carry, sub=sub):
            for j in range(TOP_K):
                d = dest_ref[(sub * TOP_K + j) * ROW_TILE + t]
                pltpu.make_async_copy(h_ref.at[pl.ds(sub * ROW_TILE + t, 1), :], xg_ref.at[pl.ds(d, 1), :],
                                      sem).start()
            return carry

        lax.fori_loop(0, ROW_TILE, body, 0)
    n = tm * TOP_K
    pltpu.make_async_copy(xg_ref.at[pl.ds(0, n), :], xg_ref.at[pl.ds(0, n), :], sem).wait()


def _dispatch(pad_end, cnt, dest_flat, h2, n_blocks):
    tm = ROW_TILE * DISPATCH_TILES
    bm = EXPERT_BLOCK
    rows = h2.shape[0]
    assert rows % tm == 0
    grid_spec = pltpu.PrefetchScalarGridSpec(
        num_scalar_prefetch=2,
        grid=(rows // tm,),
        in_specs=[pl.BlockSpec((tm * TOP_K,), lambda i, pe, cn: (i,), memory_space=pltpu.SMEM),
                  pl.BlockSpec((tm, ROW_WIDTH), lambda i, pe, cn: (i, 0))],
        out_specs=pl.BlockSpec(memory_space=pl.ANY),
        scratch_shapes=[pltpu.VMEM((bm, ROW_WIDTH), F32), pltpu.SemaphoreType.DMA(()), pltpu.SemaphoreType.DMA(())])
    return pl.pallas_call(
        functools.partial(_dispatch_kernel, bm=bm),
        grid_spec=grid_spec,
        out_shape=jax.ShapeDtypeStruct((n_blocks * bm, ROW_WIDTH), F32),
        compiler_params=_cparams("arbitrary"),
        name="moe_dispatch",
    )(pad_end, cnt, dest_flat, h2)


def _expert_kernel(be_ref, nu_ref, x_ref, wgu_ref, bgu_ref, wd_ref, bd_ref, y_ref, wgu_b, wd_b):
    i = pl.program_id(0)
    used = i < nu_ref[0]
    fresh = jnp.logical_or(i == 0, be_ref[i] != be_ref[jnp.maximum(i - 1, 0)])

    @pl.when(jnp.logical_and(used, fresh))
    def _():
        wgu_b[...] = wgu_ref[0].astype(BF16)
        wd_b[...] = wd_ref[0].astype(BF16)

    @pl.when(jnp.logical_not(used))
    def _():
        y_ref[...] = jnp.zeros(y_ref.shape, y_ref.dtype)

    @pl.when(used)
    def _():
        words = lax.bitcast_convert_type(x_ref[:, 0:PACKED], jnp.uint32)
        low = lax.bitcast_convert_type(lax.shift_left(words, jnp.uint32(16)), F32)
        high = lax.bitcast_convert_type(jnp.bitwise_and(words, jnp.uint32(0xFFFF0000)), F32)
        x = jnp.concatenate([low.astype(BF16), high.astype(BF16)], axis=1)
        y = bd_ref[0]
        slab = D_EXPERT // EXPERT_HIDDEN_SLABS
        for c in range(EXPERT_HIDDEN_SLABS):
            g0, l0 = c * slab, D_EXPERT + c * slab
            glu = jnp.dot(x, wgu_b[:, g0:g0 + slab], preferred_element_type=F32) + bgu_ref[0, :, g0:g0 + slab]
            lin = jnp.dot(x, wgu_b[:, l0:l0 + slab], preferred_element_type=F32) + bgu_ref[0, :, l0:l0 + slab]
            glu = jnp.minimum(glu, SWIGLU_LIMIT)
            lin = jnp.clip(lin, -SWIGLU_LIMIT, SWIGLU_LIMIT)
            hid = glu * jax.nn.sigmoid(SWIGLU_ALPHA * glu) * (lin + 1.0)
            y = y + jnp.dot(hid.astype(BF16), wd_b[g0:g0 + slab, :], preferred_element_type=F32)
        meta = x_ref[:, PACKED:]
        me = be_ref[i].astype(F32)
        w = jnp.zeros((x_ref.shape[0], 1), F32)
        for j in range(TOP_K):
            w = w + jnp.where(meta[:, TOP_K + j:TOP_K + j + 1] == me, meta[:, j:j + 1], 0.0)
        y_ref[...] = (y * w).astype(y_ref.dtype)


def _experts(block_e, n_used, xg, w_gu_b, b_gu, w_down_b, b_down):
    bm = EXPERT_BLOCK
    n_blocks = xg.shape[0] // bm
    grid_spec = pltpu.PrefetchScalarGridSpec(
        num_scalar_prefetch=2,
        grid=(n_blocks,),
        in_specs=[pl.BlockSpec((bm, ROW_WIDTH), lambda i, be, nu: (jnp.minimum(i, nu[0] - 1), 0)),
                  pl.BlockSpec((1, D_MODEL, 2 * D_EXPERT), lambda i, be, nu: (be[i], 0, 0)),
                  pl.BlockSpec((1, 1, 2 * D_EXPERT), lambda i, be, nu: (be[i], 0, 0)),
                  pl.BlockSpec((1, D_EXPERT, D_MODEL), lambda i, be, nu: (be[i], 0, 0)),
                  pl.BlockSpec((1, 1, D_MODEL), lambda i, be, nu: (be[i], 0, 0))],
        out_specs=pl.BlockSpec((bm, D_MODEL), lambda i, be, nu: (i, 0)),
        scratch_shapes=[pltpu.VMEM((D_MODEL, 2 * D_EXPERT), BF16), pltpu.VMEM((D_EXPERT, D_MODEL), BF16)])
    return pl.pallas_call(
        _expert_kernel,
        grid_spec=grid_spec,
        out_shape=jax.ShapeDtypeStruct((xg.shape[0], D_MODEL), BF16),
        compiler_params=_cparams("arbitrary"),
        name="moe_experts",
    )(block_e, n_used, xg, w_gu_b, b_gu.reshape(N_EXPERTS, 1, -1), w_down_b, b_down.reshape(N_EXPERTS, 1, -1))


def _combine_kernel(row0_ref, col_ref, x1_ref, mp_ref, ms_ref, ng_ref, yg_ref,
                    yp_ref, ys_ref, rows_ref, g_ref, sem, *, n_prompt_tiles):
    i = pl.program_id(0)
    n_steps = pl.num_programs(0)
    tm = x1_ref.shape[0]
    is_p = i < n_prompt_tiles
    win = COMBINE_WINDOW
    n_win_rows = COMBINE_SLOTS * win
    slot = i % 2

    def fetch_windows(tile, buf):
        for s in range(COMBINE_SLOTS):
            row0 = pl.multiple_of(row0_ref[tile * COMBINE_SLOTS + s], COMBINE_ALIGN)
            pltpu.make_async_copy(yg_ref.at[pl.ds(row0, win), :], rows_ref.at[buf, pl.ds(s * win, win), :],
                                  sem.at[buf]).start()

    @pl.when(i == 0)
    def _():
        fetch_windows(0, 0)

    @pl.when(i + 1 < n_steps)
    def _():
        fetch_windows(i + 1, 1 - slot)

    pltpu.make_async_copy(yg_ref.at[pl.ds(0, n_win_rows), :], rows_ref.at[slot], sem.at[slot]).wait()
    col = col_ref[...]
    for c in range(n_win_rows // LANES):
        lane = lax.broadcasted_iota(jnp.int32, (tm, LANES), 1) + c * LANES
        g = jnp.zeros((tm, LANES), F32)
        for j in range(TOP_K):
            g = jnp.where(lane == col[:, j:j + 1], 1.0, g)
        g_ref[:, c * LANES:(c + 1) * LANES] = g.astype(BF16)
    f = jnp.dot(g_ref[...], rows_ref[slot], preferred_element_type=F32)

    g2 = jnp.where(is_p, mp_ref[0, 5:6, :], ms_ref[5])
    y = x1_ref[...] + g2 * _rms(f, ng_ref[3:4, :])

    @pl.when(is_p)
    def _():
        yp_ref[...] = y

    @pl.when(jnp.logical_not(is_p))
    def _():
        ys_ref[...] = y


def _combine(row0, col, x1, mod_p, mod_s, norm_g, yg, *, n_prompt_rows, n_sample_rows):
    tm = ROW_TILE
    npt = n_prompt_rows // tm
    assert n_sample_rows == tm and (COMBINE_SLOTS * COMBINE_WINDOW) % LANES == 0
    n_tiles = npt + 1
    tiles_per_batch = npt // mod_p.shape[0]
    last_p = npt - 1
    grid_spec = pltpu.PrefetchScalarGridSpec(
        num_scalar_prefetch=1,
        grid=(n_tiles,),
        in_specs=[pl.BlockSpec((tm, TOP_K), lambda i, r0: (i, 0)),
                  pl.BlockSpec((tm, D_MODEL), lambda i, r0: (i, 0)),
                  pl.BlockSpec((1, 6, D_MODEL), lambda i, r0: (jnp.minimum(i, last_p) // tiles_per_batch, 0, 0)),
                  pl.BlockSpec((6, tm, D_MODEL), lambda i, r0: (0, 0, 0)),
                  pl.BlockSpec((4, D_MODEL), lambda i, r0: (0, 0)),
                  pl.BlockSpec(memory_space=pl.ANY)],
        out_specs=[pl.BlockSpec((tm, D_MODEL), lambda i, r0: (jnp.minimum(i, last_p), 0)),
                   pl.BlockSpec((tm, D_MODEL), lambda i, r0: (0, 0))],
        scratch_shapes=[pltpu.VMEM((2, COMBINE_SLOTS * COMBINE_WINDOW, D_MODEL), BF16),
                        pltpu.VMEM((tm, COMBINE_SLOTS * COMBINE_WINDOW), BF16),
                        pltpu.SemaphoreType.DMA((2,))])
    return pl.pallas_call(
        functools.partial(_combine_kernel, n_prompt_tiles=npt),
        grid_spec=grid_spec,
        out_shape=[jax.ShapeDtypeStruct((n_prompt_rows, D_MODEL), F32),
                   jax.ShapeDtypeStruct((n_sample_rows, D_MODEL), F32)],
        compiler_params=_cparams("arbitrary"),
        name="moe_combine",
    )(row0, col, x1, mod_p, mod_s, norm_g, yg)


def _t5_bucket_np(dist):
    n = np.maximum(dist, 0)
    max_exact = NUM_BUCKETS // 2
    nf = np.maximum(n, 1).astype(np.float64)
    large = max_exact + (np.log(nf / max_exact) / math.log(MAX_DISTANCE / max_exact)
                         * (NUM_BUCKETS - max_exact)).astype(np.int32)
    large = np.minimum(large, NUM_BUCKETS - 1)
    return np.where(n < max_exact, n, large).astype(np.int32)


def _bias_from_dist(rel_bias, dist):
    onehot = jax.nn.one_hot(_t5_bucket_np(dist).reshape(-1), NUM_BUCKETS, dtype=F32)
    shifted = (rel_bias - rel_bias[NUM_BUCKETS - 1]).reshape(NUM_BUCKETS, -1)
    out = jnp.dot(onehot, shifted, precision=lax.Precision.HIGHEST).reshape(dist.shape + rel_bias.shape[1:])
    return jnp.where(jnp.asarray(dist >= 0)[..., None, None], out, -jnp.inf).astype(F32)


def _head_masked(t):
    h = t.shape[0]
    same = np.eye(h, dtype=bool)[:, None, None, None, :]
    full = jnp.where(same, t[..., None], -jnp.inf)
    return full.reshape(h * t.shape[1] * t.shape[2], t.shape[3] * h)


def kernel(x_prompt, x_sample, cache_k, cache_v, page_table, c_prompt, c_sample, w_ada, b_ada, norm_g, w_in,
           w_out, ln_v_g, ln_v_b, w_spatial, b_spatial, lam_params, subln_g, rel_bias, w_router, b_router,
           w_gu, b_gu, w_down, b_down):
    batch, seq, d = x_prompt.shape
    nb, ds = x_sample.shape[:2]
    n_pages = page_table.shape[1]
    past = n_pages * PAGE_SIZE
    rows_p = batch * seq
    rows_s = nb * ds
    assert rows_s == ROW_TILE and d == D_MODEL

    mod = _ada(jnp.concatenate([c_prompt, c_sample], axis=0), w_ada[0], b_ada[0])
    mod_p = mod[:batch].reshape(batch, 6, D_MODEL)
    mod_s = jnp.repeat(jnp.transpose(mod[batch:].reshape(nb, 6, D_MODEL), (1, 0, 2)), ds, axis=1)

    ng = norm_g[0]
    w_in_b = w_in[0].astype(BF16)
    w_out_b = w_out[0].astype(BF16)
    lng = ln_v_g[0].reshape(1, A_WIDTH)
    lnb = ln_v_b[0].reshape(1, A_WIDTH)
    grp = np.arange(A_WIDTH) // A_HEAD
    avg = jnp.asarray((grp[:, None] == grp[None, :]).astype(np.float32) / A_HEAD, BF16)
    ws_p = jnp.tril(w_spatial[0]).astype(BF16)
    bs_p = jnp.repeat(b_spatial[0].T, A_HEAD, axis=1)
    w_small = jnp.tril(w_spatial[0][:, :ds, :ds])
    same_batch = np.kron(np.eye(nb, dtype=np.float32), np.ones((ds, ds), np.float32))
    spread = np.tile(np.eye(ds, dtype=np.float32), (nb, 1))
    ws_s = (jnp.einsum('rt,gts,cs->grc', spread, w_small, spread, precision=lax.Precision.HIGHEST)
            * same_batch).astype(BF16)
    bs_s = jnp.tile(jnp.repeat(b_spatial[0][:, :ds].T, A_HEAD, axis=1), (nb, 1))

    xp2 = x_prompt.reshape(rows_p, D_MODEL)
    xs2 = x_sample.reshape(rows_s, D_MODEL)
    assert ATTN_TILE % ROW_TILE == 0 and seq % ATTN_TILE == 0
    a_p, va_p, q_p, k_p, v_p, kb_p, vt_p = _inproj(
        xp2, mod_p[:, 0:1], mod_p[:, 1:2], ng[0:1], w_in_b, lng, lnb, avg, ws_p, bs_p,
        tiles_per_mod=seq // ROW_TILE, chunk=CHUNK, q_dtype=BF16, q_scale=LOG2E / math.sqrt(B_HEAD), va_rows=CHUNK)
    a_s, va_s, q_s, k_s, v_s, _, _ = _inproj(
        xs2, mod_s[0:1], mod_s[1:2], ng[0:1], w_in_b, lng, lnb, avg, ws_s, bs_s,
        tiles_per_mod=1, chunk=rows_s, q_dtype=F32, q_scale=1.0, va_rows=rows_s)

    ti = np.arange(ATTN_TILE)
    dist_diag = ti[:, None] - ti[None, :]
    dist_prev = ti[:, None] - ti[None, ATTN_TILE - MAX_DISTANCE:] + ATTN_TILE
    assert dist_prev.min() >= 1 and (ti[:, None] - ti[None, :ATTN_TILE - MAX_DISTANCE] + ATTN_TILE).min() >= MAX_DISTANCE
    to_cols = lambda t: jnp.transpose(t, (2, 1, 3, 0)).reshape(B_HEADS, t.shape[1], 2 * ATTN_TILE) * LOG2E
    b_p = _attn_p(q_p, kb_p, vt_p, to_cols(_bias_from_dist(rel_bias, dist_diag)),
                  to_cols(_bias_from_dist(rel_bias, dist_prev)), lam_params[0], subln_g, batch=batch, seq=seq)

    qi = np.arange(ds)
    ki = np.arange(PAGE_SIZE)
    dist_l = PAGE_SIZE + qi[:, None] - ki[None, :]
    kn_i = np.arange(PAGE_SIZE // B_HEADS)
    dist_n = np.where(kn_i[None, :] < ds, qi[:, None] - kn_i[None, :], -1)
    to_rows = lambda t: jnp.transpose(t, (2, 3, 0, 1))
    bias_l = _head_masked(to_rows(_bias_from_dist(rel_bias, dist_l)))
    bias_n = _head_masked(to_rows(_bias_from_dist(rel_bias, dist_n)))
    maskb = _head_masked(jnp.zeros((B_HEADS, 2, ds, PAGE_SIZE), F32))
    pad_keys = lambda t: jnp.pad(t.reshape(nb, ds * B_HEADS, B_VDIM), ((0, 0), (0, PAGE_SIZE - ds * B_HEADS), (0, 0)))
    n_phys = cache_k.shape[0]
    b_s = _attn_s(page_table, q_s, pad_keys(k_s), pad_keys(v_s), maskb, bias_l, bias_n, lam_params[0], subln_g,
                  cache_k.reshape(n_phys * PAGE_SIZE * B_HEADS, B_VDIM),
                  cache_v.reshape(n_phys * PAGE_SIZE * B_HEADS, B_VDIM), ds=ds)

    x1, h2, top_i, top_r, counts, tile_base = _outproj(
        a_p, a_s, b_p, b_s, xp2, xs2, mod_p, mod_s, ng, w_out_b, w_router[0], b_router[0])

    bm = EXPERT_BLOCK
    rows = rows_p + rows_s
    n_blocks = rows * TOP_K // bm + N_EXPERTS + 1
    cnt = counts[:, 0].astype(jnp.int32)
    padded = (cnt + bm - 1) // bm * bm
    pad_end = jnp.cumsum(padded)
    pad_start = pad_end - padded
    experts = jnp.arange(N_EXPERTS, dtype=jnp.int32)
    chosen = top_i[..., None] == experts
    dest = jnp.sum(jnp.where(chosen, pad_start, 0), axis=-1) + top_r
    n_tiles = rows // ROW_TILE
    win = COMBINE_WINDOW
    assert win % COMBINE_ALIGN == 0
    assert (ROW_TILE * TOP_K + N_EXPERTS * (COMBINE_ALIGN - 1)) // win + N_EXPERTS <= COMBINE_SLOTS
    base = tile_base[:, :, 0].astype(jnp.int32)
    first = pad_start[None, :] + base
    win0 = first // COMBINE_ALIGN * COMBINE_ALIGN
    sent = jnp.concatenate([base[1:], cnt[None, :]], axis=0) - base
    n_win = jnp.where(sent > 0, (first - win0 + sent + win - 1) // win, 0)
    slot_end = jnp.cumsum(n_win, axis=1)
    slot_start = slot_end - n_win
    slots = jnp.arange(COMBINE_SLOTS, dtype=jnp.int32)
    slot_e = jnp.minimum(jnp.sum((slot_end[:, None, :] <= slots[None, :, None]).astype(jnp.int32), axis=2),
                         N_EXPERTS - 1)
    pick = slot_e[..., None] == experts
    slot_row0 = (jnp.sum(jnp.where(pick, win0[:, None, :], 0), axis=2)
                 + win * (slots[None, :] - jnp.sum(jnp.where(pick, slot_start[:, None, :], 0), axis=2)))
    slot_row0 = jnp.where(slots[None, :] < slot_end[:, -1:], slot_row0, 0)
    local = dest - jnp.sum(jnp.where(chosen, jnp.repeat(win0, ROW_TILE, axis=0), 0), axis=-1)
    col = jnp.sum(jnp.where(chosen, jnp.repeat(slot_start, ROW_TILE, axis=0), 0), axis=-1) * win + local
    dest = jnp.transpose(dest.reshape(TOP_K, n_tiles, ROW_TILE), (1, 0, 2)).reshape(-1)
    blk_start = jnp.arange(n_blocks, dtype=jnp.int32) * bm
    block_e = jnp.minimum(jnp.sum((pad_end[None, :] <= blk_start[:, None]).astype(jnp.int32), axis=1),
                          N_EXPERTS - 1)
    n_used = (pad_end[-1:] // bm).astype(jnp.int32)

    xg = _dispatch(pad_end.astype(jnp.int32), cnt, dest, h2, n_blocks)
    yg = _experts(block_e, n_used, xg, w_gu[0], b_gu[0], w_down[0], b_down[0])
    y_p, y_s = _combine(slot_row0.reshape(-1), col.T, x1, mod_p, mod_s, ng, yg,
                        n_prompt_rows=rows_p, n_sample_rows=rows_s)

    return (y_p.reshape(batch, seq, D_MODEL),
            y_s.reshape(nb, ds, D_MODEL),
            k_p.reshape(batch, seq, 1, B_HEADS, B_VDIM),
            v_p.reshape(batch, seq, 1, B_HEADS, B_VDIM),
            k_s.reshape(nb, ds, 1, B_HEADS, B_VDIM),
            v_s.reshape(nb, ds, 1, B_HEADS, B_VDIM),
            va_p.reshape(batch, CHUNK, 1, A_WIDTH),
            va_s.reshape(nb, ds, 1, A_WIDTH))
```
